```python
import math
import jax
import jax.numpy as jnp
from jax import lax
import numpy as np

D_MODEL = 1024
BATCH = 1
SEQ = 16384
DEPTH = 1

GRID_W = 64
CTX_LEN = 256
NA_HEADS = 8
NA_DH = 64
NA_W = NA_HEADS * NA_DH
NA_KH = 8
NA_KW = 16
GDN_HEADS = 4
GDN_DK = 128
GDN_DV = 128
GDN_W = GDN_HEADS * GDN_DV
GDN_QKV_W = 2 * GDN_HEADS * GDN_DK + GDN_W
GDN_CHUNK = 64
CONV_K = 5
ROPE_AXIS_DIM = GDN_DK // 2
ROPE_BASE = 10000.0
EPS = 1e-6
MIX_W = NA_W + GDN_W
IN_WIDTHS = (NA_W, NA_W, NA_W, NA_W, GDN_QKV_W, GDN_W, 2 * GDN_HEADS, 2 * GDN_HEADS)
IN_DIM = sum(IN_WIDTHS)
IN_SPLITS = tuple(int(s) for s in np.cumsum(IN_WIDTHS)[:-1])

kernel_name = 'hybrid_natten_gdn_diffusion_block'


def rmsnorm(x, w):
    xf = x.astype(jnp.float32)
    xf = xf * lax.rsqrt(jnp.mean(xf * xf, axis=-1, keepdims=True) + EPS)
    return xf.astype(x.dtype) * w


def l2norm(x):
    xf = x.astype(jnp.float32)
    return xf * lax.rsqrt(jnp.sum(xf * xf, axis=-1, keepdims=True) + EPS)


def rotate_half_axis(x, cos, sin):
    x1, x2 = jnp.split(x, 2, axis=-1)
    c, s = cos[:, None, :], sin[:, None, :]
    return jnp.concatenate([x1 * c - x2 * s, x2 * c + x1 * s], axis=-1)


def axial_rope(x, rope):
    cos_r, sin_r, cos_c, sin_c = rope
    xr, xc = x[..., :ROPE_AXIS_DIM], x[..., ROPE_AXIS_DIM:]
    return jnp.concatenate([rotate_half_axis(xr, cos_r, sin_r),
                            rotate_half_axis(xc, cos_c, sin_c)], axis=-1)


def short_conv(x, w):
    out = lax.conv_general_dilated(
        x, w[:, None, :], window_strides=(1,), padding=[(CONV_K // 2, CONV_K // 2)],
        dimension_numbers=('NWC', 'WIO', 'NWC'), feature_group_count=x.shape[-1])
    return jax.nn.silu(out)


def neighborhood_attention(q, k, v, k_ctx, v_ctx, rpb):
    B, L, H, dh = q.shape
    rows = L // GRID_W
    kh = min(NA_KH, rows)
    scale = dh ** -0.5
    qg = (q * scale).reshape(B, rows, GRID_W, H, dh)
    kg = k.reshape(B, rows, GRID_W, H, dh)
    vg = v.reshape(B, rows, GRID_W, H, dh)
    col = jnp.arange(GRID_W)
    col_start = jnp.clip(col - NA_KW // 2, 0, GRID_W - NA_KW)
    col_idx = col_start[:, None] + jnp.arange(NA_KW)[None, :]
    col_off = col_idx - col[:, None] + NA_KW - 1

    def row_block(r):
        rs = jnp.clip(r - kh // 2, 0, rows - kh)
        k_rows = lax.dynamic_slice_in_dim(kg, rs, kh, axis=1)
        v_rows = lax.dynamic_slice_in_dim(vg, rs, kh, axis=1)
        k_win = k_rows[:, :, col_idx]
        v_win = v_rows[:, :, col_idx]
        q_r = lax.dynamic_index_in_dim(qg, r, axis=1, keepdims=False)
        row_off = rs + jnp.arange(kh) - r + NA_KH - 1
        bias = rpb[:, row_off[:, None, None], col_off[None, :, :]]
        bias = bias.transpose(0, 2, 1, 3).astype(jnp.float32)
        s_loc = jnp.einsum('bchd,bicjhd->bhcij', q_r, k_win).astype(jnp.float32) + bias
        s_ctx = jnp.einsum('bchd,bnhd->bhcn', q_r, k_ctx).astype(jnp.float32)
        s = jnp.concatenate([s_loc.reshape(B, H, GRID_W, kh * NA_KW), s_ctx], axis=-1)
        p = jax.nn.softmax(s, axis=-1).astype(v.dtype)
        p_loc = p[..., :kh * NA_KW].reshape(B, H, GRID_W, kh, NA_KW)
        p_ctx = p[..., kh * NA_KW:]
        return (jnp.einsum('bhcij,bicjhd->bchd', p_loc, v_win)
                + jnp.einsum('bhcn,bnhd->bchd', p_ctx, v_ctx))

    out = lax.map(row_block, jnp.arange(rows))
    return out.transpose(1, 0, 2, 3, 4).reshape(B, L, H * dh)


def context_attention(q, k, v):
    B, Lc, H, dh = q.shape
    s = jnp.einsum('bqhd,bkhd->bhqk', q, k).astype(jnp.float32) * dh ** -0.5
    p = jax.nn.softmax(s, axis=-1).astype(v.dtype)
    return jnp.einsum('bhqk,bkhd->bqhd', p, v).reshape(B, Lc, H * dh)


def gated_delta_chunked(q, k, v, g, beta, state0):
    B, L, H, dk = q.shape
    dv = v.shape[-1]
    n = L // GDN_CHUNK

    def blocks(t):
        return t.reshape(B, n, GDN_CHUNK, H, t.shape[-1]).transpose(1, 0, 3, 2, 4)

    q = blocks(q * dk ** -0.5)
    k = blocks(k)
    v = blocks(v)
    g = g.reshape(B, n, GDN_CHUNK, H).transpose(1, 0, 3, 2)
    beta = beta.reshape(B, n, GDN_CHUNK, H).transpose(1, 0, 3, 2)
    gc = jnp.cumsum(g, axis=-1)
    pos = jnp.arange(GDN_CHUNK)
    incl = pos[:, None] >= pos[None, :]
    decay = jnp.exp(jnp.where(incl, gc[..., :, None] - gc[..., None, :], -jnp.inf))
    kb = k * beta[..., None]
    a = jnp.where(pos[:, None] > pos[None, :],
                  jnp.einsum('nbhid,nbhjd->nbhij', kb, k) * decay, 0.0)
    eye = jnp.eye(GDN_CHUNK, dtype=a.dtype)
    t_inv = lax.linalg.triangular_solve(eye + a, jnp.broadcast_to(eye, a.shape),
                                        left_side=True, lower=True, unit_diagonal=True)
    u = jnp.einsum('nbhij,nbhje->nbhie', t_inv, v * beta[..., None])
    w = jnp.einsum('nbhij,nbhjd->nbhid', t_inv, kb * jnp.exp(gc)[..., None])
    qk = jnp.einsum('nbhid,nbhjd->nbhij', q, k) * decay
    q_dec = q * jnp.exp(gc)[..., None]
    k_dec = k * jnp.exp(gc[..., -1:] - gc)[..., None]
    g_last = jnp.exp(gc[..., -1])

    def step(state, xs):
        u_n, w_n, qk_n, q_n, k_n, gl_n = xs
        v_new = u_n - jnp.einsum('bhcd,bhde->bhce', w_n, state)
        o_n = (jnp.einsum('bhcd,bhde->bhce', q_n, state)
               + jnp.einsum('bhij,bhje->bhie', qk_n, v_new))
        state = state * gl_n[..., None, None] + jnp.einsum('bhcd,bhce->bhde', k_n, v_new)
        return state, o_n

    state, o = lax.scan(step, state0, (u, w, qk, q_dec, k_dec, g_last))
    o = o.transpose(1, 0, 3, 2, 4).reshape(B, L, H, dv)
    return o, state


def gdn_qkv(qkv, conv_w):
    B, L, _ = qkv.shape
    qkv = short_conv(qkv, conv_w).astype(jnp.float32)
    q, k, v = jnp.split(qkv, (GDN_HEADS * GDN_DK, 2 * GDN_HEADS * GDN_DK), axis=-1)
    q = l2norm(q.reshape(B, L, GDN_HEADS, GDN_DK))
    k = l2norm(k.reshape(B, L, GDN_HEADS, GDN_DK))
    v = v.reshape(B, L, GDN_HEADS, GDN_DV)
    return q, k, v


def gdn_gates(b, a, A_log, dt_bias):
    B, L, _ = b.shape
    b = b.reshape(B, L, 2, GDN_HEADS).astype(jnp.float32)
    a = a.reshape(B, L, 2, GDN_HEADS).astype(jnp.float32)
    beta = jax.nn.sigmoid(b)
    g = -jnp.exp(A_log.astype(jnp.float32)) * jax.nn.softplus(a + dt_bias.astype(jnp.float32))
    return beta, g


def bidirectional_gdn(q, k, v, g, beta, qc, kc, vc, gc, betac):
    B = q.shape[0]
    zero = jnp.zeros((B, GDN_HEADS, GDN_DK, GDN_DV), jnp.float32)
    flip = lambda t: jnp.flip(t, axis=1)
    oc_f, s_f = gated_delta_chunked(qc, kc, vc, gc[:, :, 0], betac[:, :, 0], zero)
    oc_b, s_b = gated_delta_chunked(flip(qc), flip(kc), flip(vc),
                                    flip(gc[:, :, 1]), flip(betac[:, :, 1]), zero)
    o_f, _ = gated_delta_chunked(q, k, v, g[:, :, 0], beta[:, :, 0], s_f)
    o_b, _ = gated_delta_chunked(flip(q), flip(k), flip(v),
                                 flip(g[:, :, 1]), flip(beta[:, :, 1]), s_b)
    return o_f + flip(o_b), oc_f + flip(oc_b)


def hybrid_layer(x, xc, c, c_ctx, w_ada, b_ada, g_pre, g_post, w_in, conv_w, rpb,
                 A_log, dt_bias, gdn_norm_w, w_out, rope, update_ctx):
    B, L, _ = x.shape
    Lc = xc.shape[1]
    shift, scale, gate = jnp.split(jax.nn.silu(c) @ w_ada + b_ada, 3, axis=-1)
    shift_c, scale_c, gate_c = jnp.split(jax.nn.silu(c_ctx) @ w_ada + b_ada, 3, axis=-1)
    h = rmsnorm(x, g_pre) * (1.0 + scale[:, None]) + shift[:, None]
    hc = rmsnorm(xc, g_pre) * (1.0 + scale_c) + shift_c
    na_q, na_k, na_v, na_z, g_qkv, g_z, g_b, g_a = jnp.split(h @ w_in, IN_SPLITS, axis=-1)
    na_qc, na_kc, na_vc, na_zc, g_qkvc, g_zc, g_bc, g_ac = jnp.split(hc @ w_in, IN_SPLITS, axis=-1)
    heads = lambda t: t.reshape(t.shape[0], t.shape[1], NA_HEADS, NA_DH)

    o_na = neighborhood_attention(heads(na_q), heads(na_k), heads(na_v),
                                  heads(na_kc), heads(na_vc), rpb)

    q, k, v = gdn_qkv(g_qkv, conv_w)
    q, k = axial_rope(q, rope), axial_rope(k, rope)
    qc, kc, vc = gdn_qkv(g_qkvc, conv_w)
    beta, g = gdn_gates(g_b, g_a, A_log, dt_bias)
    betac, gc = gdn_gates(g_bc, g_ac, A_log, dt_bias)
    o_g, o_gc = bidirectional_gdn(q, k, v, g, beta, qc, kc, vc, gc, betac)
    o_g = rmsnorm(o_g, gdn_norm_w).reshape(B, L, GDN_W).astype(x.dtype)

    y = jnp.concatenate([o_na * jax.nn.silu(na_z), o_g * jax.nn.silu(g_z)], axis=-1) @ w_out
    x = x + gate[:, None] * rmsnorm(y, g_post)

    if update_ctx:
        o_nac = context_attention(heads(na_qc), heads(na_kc), heads(na_vc))
        o_gc = rmsnorm(o_gc, gdn_norm_w).reshape(B, Lc, GDN_W).astype(xc.dtype)
        yc = jnp.concatenate([o_nac * jax.nn.silu(na_zc), o_gc * jax.nn.silu(g_zc)], axis=-1) @ w_out
        xc = xc + gate_c * rmsnorm(yc, g_post)
    return x, xc


def setup_inputs(seed: int = 0) -> dict:
    key = jax.random.key(seed)
    ks = jax.random.split(key, 16)
    f32 = jnp.float32
    nrm = lambda kk, shape, s: jax.random.normal(kk, shape, f32) * s
    x = nrm(ks[0], (BATCH, SEQ, D_MODEL), 1.0)
    c = nrm(ks[1], (BATCH, D_MODEL), 1.0)
    ctx = nrm(ks[2], (BATCH, CTX_LEN, D_MODEL), 1.0)
    c_ctx = nrm(ks[3], (D_MODEL,), 1.0)
    w_ada = nrm(ks[4], (DEPTH, D_MODEL, 3 * D_MODEL), 0.5 * D_MODEL ** -0.5)
    b_ada = nrm(ks[5], (DEPTH, 3 * D_MODEL), 0.01)
    g_pre = 1.0 + nrm(ks[6], (DEPTH, D_MODEL), 0.02)
    g_post = 1.0 + nrm(ks[7], (DEPTH, D_MODEL), 0.02)
    w_in = nrm(ks[8], (DEPTH, D_MODEL, IN_DIM), D_MODEL ** -0.5)
    conv_w = nrm(ks[9], (DEPTH, CONV_K, GDN_QKV_W), CONV_K ** -0.5)
    rpb = nrm(ks[10], (DEPTH, NA_HEADS, 2 * NA_KH - 1, 2 * NA_KW - 1), 0.1)
    A_log = jnp.log(jax.random.uniform(ks[11], (DEPTH, 2, GDN_HEADS), f32, 1.0, 16.0))
    dt = jnp.exp(jax.random.uniform(ks[12], (DEPTH, 2, GDN_HEADS), f32,
                                    math.log(1e-3), math.log(1e-1)))
    dt_bias = dt + jnp.log(-jnp.expm1(-dt))
    gdn_norm_w = 1.0 + nrm(ks[13], (DEPTH, GDN_DV), 0.02)
    w_out = nrm(ks[14], (DEPTH, MIX_W, D_MODEL), MIX_W ** -0.5)
    return {'x': x, 'c': c, 'ctx': ctx, 'c_ctx': c_ctx, 'w_ada': w_ada, 'b_ada': b_ada,
            'g_pre': g_pre, 'g_post': g_post, 'w_in': w_in, 'conv_w': conv_w, 'rpb': rpb,
            'A_log': A_log, 'dt_bias': dt_bias, 'gdn_norm_w': gdn_norm_w, 'w_out': w_out}


def reference(x, c, ctx, c_ctx, w_ada, b_ada, g_pre, g_post, w_in, conv_w, rpb,
              A_log, dt_bias, gdn_norm_w, w_out):
    L = x.shape[1]
    t = jnp.arange(L)
    row = (t // GRID_W).astype(jnp.float32)
    col = (t % GRID_W).astype(jnp.float32)
    inv_freq = ROPE_BASE ** (-jnp.arange(0, ROPE_AXIS_DIM, 2, dtype=jnp.float32) / ROPE_AXIS_DIM)
    ang_r = row[:, None] * inv_freq[None, :]
    ang_c = col[:, None] * inv_freq[None, :]
    rope = (jnp.cos(ang_r), jnp.sin(ang_r), jnp.cos(ang_c), jnp.sin(ang_c))
    xc = ctx
    for layer in range(DEPTH):
        x, xc = hybrid_layer(x, xc, c, c_ctx, w_ada[layer], b_ada[layer], g_pre[layer],
                             g_post[layer], w_in[layer], conv_w[layer], rpb[layer],
                             A_log[layer], dt_bias[layer], gdn_norm_w[layer], w_out[layer],
                             rope, update_ctx=(layer < DEPTH - 1))
    return x
```

```python
import functools
import math

import jax
import jax.numpy as jnp
from jax import lax
from jax.experimental import pallas as pl
from jax.experimental.pallas import tpu as pltpu

F32 = jnp.float32
BF16 = jnp.bfloat16

D_MODEL = 1024
GRID_W = 64
NA_HEADS = 8
NA_DH = 64
NA_W = NA_HEADS * NA_DH
NA_KH = 8
NA_KW = 16
GDN_HEADS = 4
GDN_DK = 128
GDN_DV = 128
GDN_W = GDN_HEADS * GDN_DV
CHUNK = 64
CONV_K = 5
ROPE_AXIS_DIM = GDN_DK // 2
ROPE_BASE = 10000.0
EPS = 1e-6
PROJ_W = 4 * NA_W + 3 * GDN_W + GDN_W
GATE_W = 2 * 2 * GDN_HEADS
LANES = 128
NEG = -1e30

VMEM_LIMIT = 56 * 1024 * 1024


def _silu(x):
    return x * jax.nn.sigmoid(x)


def _dot(a, b):
    return jnp.dot(a, b, preferred_element_type=F32)


def _dot_nt(a, b):
    return lax.dot_general(a, b, (((1,), (1,)), ((), ())), preferred_element_type=F32)


def _dot_tn(a, b):
    return lax.dot_general(a, b, (((0,), (0,)), ((), ())), preferred_element_type=F32)


def _split2(x):
    hi = x.astype(BF16)
    lo = (x - hi.astype(F32)).astype(BF16)
    return hi, lo


def _split3(x):
    hi = x.astype(BF16)
    r = x - hi.astype(F32)
    mid = r.astype(BF16)
    lo = (r - mid.astype(F32)).astype(BF16)
    return hi, mid, lo


def _mm3(a, b):
    ah, al = _split2(a)
    bh, bl = _split2(b)
    return _dot(ah, bh) + (_dot(al, bh) + _dot(ah, bl))


def _ada_kernel(c_ref, w_ref, b_ref, o_ref):
    s = _silu(c_ref[...])
    o_ref[...] = _mm3(s, w_ref[...]) + b_ref[...]


def _ada(cc, w_ada, b_ada):
    tn = 512
    n = w_ada.shape[1]
    return pl.pallas_call(
        _ada_kernel,
        grid=(n // tn,),
        in_specs=[pl.BlockSpec((8, D_MODEL), lambda j: (0, 0)),
                  pl.BlockSpec((D_MODEL, tn), lambda j: (0, j)),
                  pl.BlockSpec((1, tn), lambda j: (0, j))],
        out_specs=pl.BlockSpec((8, tn), lambda j: (0, j)),
        out_shape=jax.ShapeDtypeStruct((8, n), F32),
        name="ada",
    )(cc, w_ada, b_ada)


def _inproj_kernel(x_ref, mod_ref, gpre_ref, w_ref, wg_ref, proj_ref, gates_ref, *, row):
    x = x_ref[...]
    xn = x * lax.rsqrt(jnp.mean(x * x, axis=-1, keepdims=True) + EPS)
    shift = mod_ref[row:row + 1, 0:D_MODEL]
    scale = mod_ref[row:row + 1, D_MODEL:2 * D_MODEL]
    h = (xn * gpre_ref[...]) * (1.0 + scale) + shift
    hb = h.astype(BF16)
    nb = 512
    for j in range(PROJ_W // nb):
        proj_ref[:, j * nb:(j + 1) * nb] = _dot(hb, w_ref[:, j * nb:(j + 1) * nb]).astype(BF16)
    gates_ref[...] = _dot(hb, wg_ref[...])


def _inproj(x2, mod, gpre, w_main, w_gate, row, tm):
    n = x2.shape[0]
    return pl.pallas_call(
        functools.partial(_inproj_kernel, row=row),
        grid=(n // tm,),
        in_specs=[pl.BlockSpec((tm, D_MODEL), lambda i: (i, 0)),
                  pl.BlockSpec((8, 3 * D_MODEL), lambda i: (0, 0)),
                  pl.BlockSpec((1, D_MODEL), lambda i: (0, 0)),
                  pl.BlockSpec((D_MODEL, PROJ_W), lambda i: (0, 0)),
                  pl.BlockSpec((D_MODEL, LANES), lambda i: (0, 0))],
        out_specs=[pl.BlockSpec((tm, PROJ_W), lambda i: (i, 0)),
                   pl.BlockSpec((tm, LANES), lambda i: (i, 0))],
        out_shape=[jax.ShapeDtypeStruct((n, PROJ_W), BF16),
                   jax.ShapeDtypeStruct((n, LANES), F32)],
        compiler_params=pltpu.CompilerParams(
            dimension_semantics=("arbitrary",), vmem_limit_bytes=VMEM_LIMIT),
        name="inproj",
    )(x2, mod, gpre, w_main, w_gate)


def _na_bias_tables(rpb):
    qc = jnp.arange(GRID_W)[:, None]
    kc = jnp.arange(GRID_W)[None, :]
    col_start = jnp.clip(qc - NA_KW // 2, 0, GRID_W - NA_KW)
    inwin = (kc >= col_start) & (kc < col_start + NA_KW)
    off = jnp.clip(kc - qc + NA_KW - 1, 0, 2 * NA_KW - 2)
    t = jnp.where(inwin[None, None], rpb[:, :, off], NEG)
    variants = []
    for d in range(NA_KH):
        v = t[:, NA_KH - 1 - d:2 * NA_KH - 1 - d]
        variants.append(v.transpose(0, 2, 1, 3).reshape(NA_HEADS, GRID_W, NA_KH * GRID_W))
    return jnp.stack(variants, axis=1).astype(F32)


def _na_kernel(q_ref, k_ref, v_ref, z_ref, kc_ref, vc_ref, bias_ref, o_ref, *, rb_rows, rows):
    rb = pl.program_id(1)
    lane = lax.broadcasted_iota(jnp.int32, (1, LANES), 1)
    first = lane < NA_DH
    kc = kc_ref[...]
    vc = vc_ref[...]
    win = NA_KH * GRID_W

    def body(i, carry):
        r = rb * rb_rows + i
        rs = jnp.clip(r - NA_KH // 2, 0, rows - NA_KH)
        d = r - rs
        t0 = pl.multiple_of(i * GRID_W, GRID_W)
        q = q_ref[pl.ds(t0, GRID_W), :] * jnp.asarray(NA_DH ** -0.5, BF16)
        k0 = pl.multiple_of(rs * GRID_W, GRID_W)
        kw = k_ref[pl.ds(k0, win), :]
        vw = v_ref[pl.ds(k0, win), :]
        outs = []
        for h in range(2):
            sel = first if h == 0 else jnp.logical_not(first)
            qh = jnp.where(sel, q, jnp.zeros_like(q))
            s_loc = _dot_nt(qh, kw) + bias_ref[h, d]
            s_ctx = _dot_nt(qh, kc)
            m = jnp.maximum(jnp.max(s_loc, axis=-1, keepdims=True),
                            jnp.max(s_ctx, axis=-1, keepdims=True))
            p_loc = jnp.exp(s_loc - m)
            p_ctx = jnp.exp(s_ctx - m)
            l = jnp.sum(p_loc, axis=-1, keepdims=True) + jnp.sum(p_ctx, axis=-1, keepdims=True)
            o = _dot(p_loc.astype(BF16), vw) + _dot(p_ctx.astype(BF16), vc)
            outs.append(o / l)
        o = jnp.where(first, outs[0], outs[1])
        z = z_ref[pl.ds(t0, GRID_W), :].astype(F32)
        o_ref[pl.ds(t0, GRID_W), :] = (o * _silu(z)).astype(BF16)
        return carry

    lax.fori_loop(0, rb_rows, body, 0)


def _natten(proj, projc, bias, rb_rows=8):
    n = proj.shape[0]
    nc = projc.shape[0]
    rows = n // GRID_W
    tq = rb_rows * GRID_W
    kcol = NA_W // LANES
    return pl.pallas_call(
        functools.partial(_na_kernel, rb_rows=rb_rows, rows=rows),
        grid=(NA_W // LANES, rows // rb_rows),
        in_specs=[pl.BlockSpec((tq, LANES), lambda hp, rb: (rb, hp)),
                  pl.BlockSpec((n, LANES), lambda hp, rb: (0, kcol + hp)),
                  pl.BlockSpec((n, LANES), lambda hp, rb: (0, 2 * kcol + hp)),
                  pl.BlockSpec((tq, LANES), lambda hp, rb: (rb, 3 * kcol + hp)),
                  pl.BlockSpec((nc, LANES), lambda hp, rb: (0, kcol + hp)),
                  pl.BlockSpec((nc, LANES), lambda hp, rb: (0, 2 * kcol + hp)),
                  pl.BlockSpec((2, NA_KH, GRID_W, NA_KH * GRID_W), lambda hp, rb: (hp, 0, 0, 0))],
        out_specs=pl.BlockSpec((tq, LANES), lambda hp, rb: (rb, hp)),
        out_shape=jax.ShapeDtypeStruct((n, NA_W), BF16),
        compiler_params=pltpu.CompilerParams(
            dimension_semantics=("arbitrary", "arbitrary"), vmem_limit_bytes=VMEM_LIMIT),
        name="natten",
    )(proj, proj, proj, proj, projc, projc, bias)


GP_TOK = 256
GP_CH = GP_TOK // CHUNK
HALO = 16
XE_OFF = 8


def _gdnprep_kernel(main_ref, left_ref, right_ref, gates_ref, cw_ref, alog_ref, dtb_ref,
                    cos_ref, sina_ref, sinb_ref,
                    u_ref, w_ref, qd_ref, kd_ref, qk_ref, gl_ref,
                    xe_ref, qn_ref, kn_ref, vv_ref, beta_ref, gc_ref, gct_ref):
    i = pl.program_id(0)
    last = pl.num_programs(0) - 1
    qkv_w = 3 * GDN_W

    lh = left_ref[...].astype(F32)[HALO - 8:HALO, 0:qkv_w]
    rh = right_ref[...].astype(F32)[0:8, 0:qkv_w]
    xe_ref[0:XE_OFF, :] = jnp.where(i > 0, lh, 0.0)
    xe_ref[XE_OFF:XE_OFF + GP_TOK, :] = main_ref[:, 0:qkv_w].astype(F32)
    xe_ref[XE_OFF + GP_TOK:XE_OFF + GP_TOK + 8, :] = jnp.where(i < last, rh, 0.0)

    cos = cos_ref[...]
    sina = sina_ref[...]
    sinb = sinb_ref[...]
    for cb in range(qkv_w // LANES):
        cols = slice(cb * LANES, (cb + 1) * LANES)
        acc = None
        for j in range(CONV_K):
            term = cw_ref[j:j + 1, cols] * xe_ref[pl.ds(XE_OFF - CONV_K // 2 + j, GP_TOK), cols]
            acc = term if acc is None else acc + term
        y = _silu(acc)
        if cb < 2 * GDN_HEADS:
            y = y * lax.rsqrt(jnp.sum(y * y, axis=-1, keepdims=True) + EPS)
            y = (y * cos + pltpu.roll(y, LANES - ROPE_AXIS_DIM // 2, 1) * sina
                 + pltpu.roll(y, ROPE_AXIS_DIM // 2, 1) * sinb)
            if cb < GDN_HEADS:
                qn_ref[:, cols] = y * (GDN_DK ** -0.5)
            else:
                kn_ref[:, (cb - GDN_HEADS) * LANES:(cb - GDN_HEADS + 1) * LANES] = y
        else:
            vv_ref[:, (cb - 2 * GDN_HEADS) * LANES:(cb - 2 * GDN_HEADS + 1) * LANES] = y

    gates = gates_ref[...]
    beta_ref[...] = jax.nn.sigmoid(gates)
    xa = gates + dtb_ref[...]
    softplus = jnp.maximum(xa, 0.0) + jnp.log1p(jnp.exp(-jnp.abs(xa)))
    g = -jnp.exp(alog_ref[...]) * softplus

    ti = lax.broadcasted_iota(jnp.int32, (GP_TOK, GP_TOK), 0)
    tj = lax.broadcasted_iota(jnp.int32, (GP_TOK, GP_TOK), 1)
    same = (ti // CHUNK) == (tj // CHUNK)
    lower = jnp.where(same & (ti >= tj), 1.0, 0.0).astype(BF16)
    upper = jnp.where(same & (ti <= tj), 1.0, 0.0).astype(BF16)
    g3 = _split3(g)
    gc_f = _dot(lower, g3[0]) + (_dot(lower, g3[1]) + _dot(lower, g3[2]))
    gc_b = _dot(upper, g3[0]) + (_dot(upper, g3[1]) + _dot(upper, g3[2]))
    gc_ref[0] = gc_f
    gc_ref[1] = gc_b
    gl_ref[...] = jnp.zeros_like(gl_ref)
    for c in range(GP_CH):
        rows = slice(c * CHUNK, (c + 1) * CHUNK)
        gct_ref[c, 0] = gc_f[rows].T
        gct_ref[c, 1] = gc_b[rows].T
        gl_ref[c, 0:1, :] = jnp.exp(gc_f[(c + 1) * CHUNK - 1:(c + 1) * CHUNK, :])
        gl_ref[c, 1:2, :] = jnp.exp(gc_b[c * CHUNK:c * CHUNK + 1, :])

    ii = lax.broadcasted_iota(jnp.int32, (CHUNK, CHUNK), 0)
    jj = lax.broadcasted_iota(jnp.int32, (CHUNK, CHUNK), 1)
    eye = jnp.where(ii == jj, 1.0, 0.0).astype(F32)

    def chunk_body(c, carry):
        c0 = pl.multiple_of(c * CHUNK, CHUNK)
        tok = pl.ds(c0, CHUNK)
        for d in range(2):
            incl = (ii >= jj) if d == 0 else (ii <= jj)
            strict = (ii > jj) if d == 0 else (ii < jj)
            for h in range(GDN_HEADS):
                hl = slice(h * LANES, (h + 1) * LANES)
                lb = d * GDN_HEADS + h
                lg = 2 * GDN_HEADS + lb
                q = qn_ref[tok, hl]
                k = kn_ref[tok, hl]
                v = vv_ref[tok, hl]
                beta = beta_ref[tok, lb:lb + 1]
                gcol = gc_ref[d, tok, lg:lg + 1]
                grow = gct_ref[c, d, lg:lg + 1, :]
                if d == 0:
                    gend = gc_ref[0, pl.ds(c0 + CHUNK - 1, 1), lg:lg + 1]
                else:
                    gend = gc_ref[1, pl.ds(c0, 1), lg:lg + 1]
                dec = jnp.exp(jnp.where(incl, gcol - grow, NEG))
                kb = k * beta
                k16 = k.astype(BF16)
                a = jnp.where(strict, _dot_nt(kb.astype(BF16), k16) * dec, 0.0)
                p = a
                t = eye - a
                for _ in range(5):
                    p = _mm3(p, p)
                    t = t + _mm3(t, p)
                t16 = t.astype(BF16)
                eg = jnp.exp(gcol)
                u_ref[d, tok, hl] = _dot(t16, (v * beta).astype(BF16))
                w_ref[d, tok, hl] = _dot(t16, (kb * eg).astype(BF16)).astype(BF16)
                qk = _dot_nt(q.astype(BF16), k16) * dec
                qk_ref[d, tok, h * CHUNK:(h + 1) * CHUNK] = qk.astype(BF16)
                qd_ref[d, tok, hl] = (q * eg).astype(BF16)
                kd_ref[d, tok, hl] = (k * jnp.exp(gend - gcol)).astype(BF16)
        return carry

    lax.fori_loop(0, GP_CH, chunk_body, 0)


def _gdnprep(proj, gates, conv_w, alog, dtb, cos, sina, sinb):
    n = proj.shape[0]
    nt = n // GP_TOK
    hb = GP_TOK // HALO
    nhalo = n // HALO
    qkv_w = 3 * GDN_W
    tok_spec = lambda w: pl.BlockSpec((GP_TOK, w), lambda i: (i, 0))
    dir_spec = lambda w: pl.BlockSpec((2, GP_TOK, w), lambda i: (0, i, 0))
    return pl.pallas_call(
        _gdnprep_kernel,
        grid=(nt,),
        in_specs=[pl.BlockSpec((GP_TOK, PROJ_W // 2), lambda i: (i, 1)),
                  pl.BlockSpec((HALO, PROJ_W // 2), lambda i: (jnp.maximum(i * hb - 1, 0), 1)),
                  pl.BlockSpec((HALO, PROJ_W // 2),
                               lambda i: (jnp.minimum((i + 1) * hb, nhalo - 1), 1)),
                  tok_spec(LANES),
                  pl.BlockSpec((8, qkv_w), lambda i: (0, 0)),
                  pl.BlockSpec((1, LANES), lambda i: (0, 0)),
                  pl.BlockSpec((1, LANES), lambda i: (0, 0)),
                  tok_spec(LANES), tok_spec(LANES), tok_spec(LANES)],
        out_specs=[dir_spec(GDN_W), dir_spec(GDN_W), dir_spec(GDN_W), dir_spec(GDN_W),
                   dir_spec(GDN_HEADS * CHUNK),
                   pl.BlockSpec((GP_CH, 8, LANES), lambda i: (i, 0, 0))],
        out_shape=[jax.ShapeDtypeStruct((2, n, GDN_W), F32),
                   jax.ShapeDtypeStruct((2, n, GDN_W), BF16),
                   jax.ShapeDtypeStruct((2, n, GDN_W), BF16),
                   jax.ShapeDtypeStruct((2, n, GDN_W), BF16),
                   jax.ShapeDtypeStruct((2, n, GDN_HEADS * CHUNK), BF16),
                   jax.ShapeDtypeStruct((n // CHUNK, 8, LANES), F32)],
        scratch_shapes=[pltpu.VMEM((GP_TOK + 2 * XE_OFF, qkv_w), F32),
                        pltpu.VMEM((GP_TOK, GDN_W), F32),
                        pltpu.VMEM((GP_TOK, GDN_W), F32),
                        pltpu.VMEM((GP_TOK, GDN_W), F32),
                        pltpu.VMEM((GP_TOK, LANES), F32),
                        pltpu.VMEM((2, GP_TOK, LANES), F32),
                        pltpu.VMEM((GP_CH, 2, LANES, CHUNK), F32)],
        compiler_params=pltpu.CompilerParams(
            dimension_semantics=("arbitrary",), vmem_limit_bytes=VMEM_LIMIT),
        name="gdnprep",
    )(proj, proj, proj, gates, conv_w, alog, dtb, cos, sina, sinb)


def _scan_kernel(gl_ref, s0_ref, uf_ref, wf_ref, qdf_ref, kdf_ref, qkf_ref,
                 ub_ref, wb_ref, qdb_ref, kdb_ref, qkb_ref,
                 of_ref, ob_ref, sfin_ref, s_ref):
    n = pl.program_id(0)
    nch = pl.num_programs(0)

    @pl.when(n == 0)
    def _():
        s_ref[...] = s0_ref[...]

    streams = ((uf_ref, wf_ref, qdf_ref, kdf_ref, qkf_ref, of_ref, n),
               (ub_ref, wb_ref, qdb_ref, kdb_ref, qkb_ref, ob_ref, nch - 1 - n))
    for d, (u_ref, w_ref, qd_ref, kd_ref, qk_ref, o_ref, ch) in enumerate(streams):
        for h in range(GDN_HEADS):
            hl = slice(h * LANES, (h + 1) * LANES)
            s = s_ref[d, h]
            s16 = s.astype(BF16)
            vnew = u_ref[:, hl] - _dot(w_ref[:, hl], s16)
            v16 = vnew.astype(BF16)
            o_ref[:, hl] = _dot(qd_ref[:, hl], s16) + _dot(qk_ref[:, h * CHUNK:(h + 1) * CHUNK], v16)
            decay = gl_ref[(d * nch + ch) * GDN_HEADS + h]
            s_ref[d, h] = s * decay + _dot_tn(kd_ref[:, hl], v16)

    @pl.when(n == nch - 1)
    def _():
        sfin_ref[...] = s_ref[...]


def _scan(gl, s0, u, w, qd, kd, qk):
    n = u.shape[1]
    nch = n // CHUNK
    fwd = lambda wd: pl.BlockSpec((None, CHUNK, wd), lambda i: (0, i, 0))
    bwd = lambda wd: pl.BlockSpec((None, CHUNK, wd), lambda i: (1, nch - 1 - i, 0))
    qkw = GDN_HEADS * CHUNK
    state_spec = pl.BlockSpec((2, GDN_HEADS, GDN_DK, GDN_DV), lambda i: (0, 0, 0, 0))
    return pl.pallas_call(
        _scan_kernel,
        grid=(nch,),
        in_specs=[pl.BlockSpec(memory_space=pltpu.SMEM), state_spec,
                  fwd(GDN_W), fwd(GDN_W), fwd(GDN_W), fwd(GDN_W), fwd(qkw),
                  bwd(GDN_W), bwd(GDN_W), bwd(GDN_W), bwd(GDN_W), bwd(qkw)],
        out_specs=[pl.BlockSpec((CHUNK, GDN_W), lambda i: (i, 0)),
                   pl.BlockSpec((CHUNK, GDN_W), lambda i: (nch - 1 - i, 0)),
                   state_spec],
        out_shape=[jax.ShapeDtypeStruct((n, GDN_W), F32),
                   jax.ShapeDtypeStruct((n, GDN_W), F32),
                   jax.ShapeDtypeStruct((2, GDN_HEADS, GDN_DK, GDN_DV), F32)],
        scratch_shapes=[pltpu.VMEM((2, GDN_HEADS, GDN_DK, GDN_DV), F32)],
        compiler_params=pltpu.CompilerParams(dimension_semantics=("arbitrary",)),
        name="scan",
    )(gl, s0, u, w, qd, kd, qk, u, w, qd, kd, qk)


def _outproj_kernel(x_ref, na_ref, of_ref, ob_ref, gz_ref, gnw_ref, wout_ref, gpost_ref, mod_ref,
                    o_ref):
    og = of_ref[...] + ob_ref[...]
    gz = gz_ref[...].astype(F32)
    gnw = gnw_ref[...]
    parts = []
    for h in range(GDN_HEADS):
        seg = og[:, h * LANES:(h + 1) * LANES]
        seg = seg * lax.rsqrt(jnp.mean(seg * seg, axis=-1, keepdims=True) + EPS)
        parts.append(seg * gnw)
    gd = (jnp.concatenate(parts, axis=-1) * _silu(gz)).astype(BF16)
    y = _dot(na_ref[...], wout_ref[0:NA_W, :]) + _dot(gd, wout_ref[NA_W:NA_W + GDN_W, :])
    yn = y * lax.rsqrt(jnp.mean(y * y, axis=-1, keepdims=True) + EPS)
    gate = mod_ref[0:1, 2 * D_MODEL:3 * D_MODEL]
    o_ref[...] = x_ref[...] + gate * (yn * gpost_ref[...])


def _outproj(x2, na, o_f, o_b, proj, gnw, w_out, gpost, mod, tm=512):
    n = x2.shape[0]
    gz_col = (PROJ_W - GDN_W) // GDN_W
    return pl.pallas_call(
        _outproj_kernel,
        grid=(n // tm,),
        in_specs=[pl.BlockSpec((tm, D_MODEL), lambda i: (i, 0)),
                  pl.BlockSpec((tm, NA_W), lambda i: (i, 0)),
                  pl.BlockSpec((tm, GDN_W), lambda i: (i, 0)),
                  pl.BlockSpec((tm, GDN_W), lambda i: (i, 0)),
                  pl.BlockSpec((tm, GDN_W), lambda i: (i, gz_col)),
                  pl.BlockSpec((1, LANES), lambda i: (0, 0)),
                  pl.BlockSpec((NA_W + GDN_W, D_MODEL), lambda i: (0, 0)),
                  pl.BlockSpec((1, D_MODEL), lambda i: (0, 0)),
                  pl.BlockSpec((8, 3 * D_MODEL), lambda i: (0, 0))],
        out_specs=pl.BlockSpec((tm, D_MODEL), lambda i: (i, 0)),
        out_shape=jax.ShapeDtypeStruct((n, D_MODEL), F32),
        compiler_params=pltpu.CompilerParams(
            dimension_semantics=("arbitrary",), vmem_limit_bytes=VMEM_LIMIT),
        name="outproj",
    )(x2, na, o_f, o_b, proj, gnw, w_out, gpost, mod)


def _rope_tables(n):
    t = jnp.arange(n)
    row = (t // GRID_W).astype(F32)
    col = (t % GRID_W).astype(F32)
    inv_freq = ROPE_BASE ** (-jnp.arange(0, ROPE_AXIS_DIM, 2, dtype=F32) / ROPE_AXIS_DIM)
    ang_r = row[:, None] * inv_freq[None, :]
    ang_c = col[:, None] * inv_freq[None, :]
    cr, sr, cc, sc = jnp.cos(ang_r), jnp.sin(ang_r), jnp.cos(ang_c), jnp.sin(ang_c)
    zero = jnp.zeros_like(sr)
    cos = jnp.concatenate([cr, cr, cc, cc], axis=-1)
    sina = jnp.concatenate([-sr, zero, -sc, zero], axis=-1)
    sinb = jnp.concatenate([zero, sr, zero, sc], axis=-1)
    return cos, sina, sinb


def _lane_row(vals, offset):
    return jnp.zeros((1, LANES), F32).at[0, offset:offset + vals.shape[0]].set(vals)


def kernel(x, c, ctx, c_ctx, w_ada, b_ada, g_pre, g_post, w_in, conv_w, rpb, A_log, dt_bias,
           gdn_norm_w, w_out):
    n = x.shape[1]
    nc = ctx.shape[1]
    x2 = x[0]
    xc2 = ctx[0]

    cc = jnp.zeros((8, D_MODEL), F32).at[0].set(c[0]).at[1].set(c_ctx)
    mod = _ada(cc, w_ada[0], b_ada[0][None, :])

    w_main = w_in[0][:, :PROJ_W].astype(BF16)
    w_gate = jnp.pad(w_in[0][:, PROJ_W:], ((0, 0), (0, LANES - GATE_W))).astype(BF16)
    gpre = g_pre[0][None, :]
    proj, gates = _inproj(x2, mod, gpre, w_main, w_gate, row=0, tm=512)
    projc, gatesc = _inproj(xc2, mod, gpre, w_main, w_gate, row=1, tm=nc)

    na = _natten(proj, projc, _na_bias_tables(rpb[0]))

    cw = jnp.pad(conv_w[0], ((0, 8 - CONV_K), (0, 0)))
    alog = _lane_row(A_log[0].reshape(-1), 2 * GDN_HEADS)
    dtb = _lane_row(dt_bias[0].reshape(-1), 2 * GDN_HEADS)
    cos, sina, sinb = _rope_tables(n)
    ones = jnp.ones((nc, LANES), F32)
    zeros = jnp.zeros((nc, LANES), F32)

    def gl_rows(gl):
        return jnp.stack([gl[:, 0, 2 * GDN_HEADS:3 * GDN_HEADS],
                          gl[:, 1, 3 * GDN_HEADS:4 * GDN_HEADS]], axis=0).reshape(-1)

    uc, wc, qdc, kdc, qkc, glc = _gdnprep(projc, gatesc, cw, alog, dtb, ones, zeros, zeros)
    s0 = jnp.zeros((2, GDN_HEADS, GDN_DK, GDN_DV), F32)
    _, _, s_ctx = _scan(gl_rows(glc), s0, uc, wc, qdc, kdc, qkc)

    u, w, qd, kd, qk, gl = _gdnprep(proj, gates, cw, alog, dtb, cos, sina, sinb)
    o_f, o_b, _ = _scan(gl_rows(gl), s_ctx, u, w, qd, kd, qk)

    gnw = gdn_norm_w[0][None, :]
    out = _outproj(x2, na, o_f, o_b, proj, gnw, w_out[0].astype(BF16), g_post[0][None, :], mod)
    return out[None]
```

```python
import functools
import math

import jax
import jax.numpy as jnp
from jax import lax
from jax.experimental import pallas as pl
from jax.experimental.pallas import tpu as pltpu

F32 = jnp.float32
BF16 = jnp.bfloat16

D_MODEL = 1024
GRID_W = 64
NA_HEADS = 8
NA_DH = 64
NA_W = NA_HEADS * NA_DH
NA_KH = 8
NA_KW = 16
GDN_HEADS = 4
GDN_DK = 128
GDN_DV = 128
GDN_W = GDN_HEADS * GDN_DV
CHUNK = 64
CONV_K = 5
ROPE_AXIS_DIM = GDN_DK // 2
ROPE_BASE = 10000.0
EPS = 1e-6
PROJ_W = 4 * NA_W + 3 * GDN_W + GDN_W
GATE_W = 2 * 2 * GDN_HEADS
LANES = 128
NEG = -1e30

VMEM_LIMIT = 56 * 1024 * 1024


def _silu(x):
    return x * jax.nn.sigmoid(x)


def _dot(a, b):
    return jnp.dot(a, b, preferred_element_type=F32)


def _dot_nt(a, b):
    return lax.dot_general(a, b, (((1,), (1,)), ((), ())), preferred_element_type=F32)


def _dot_tn(a, b):
    return lax.dot_general(a, b, (((0,), (0,)), ((), ())), preferred_element_type=F32)


def _split2(x):
    hi = x.astype(BF16)
    lo = (x - hi.astype(F32)).astype(BF16)
    return hi, lo


def _split3(x):
    hi = x.astype(BF16)
    r = x - hi.astype(F32)
    mid = r.astype(BF16)
    lo = (r - mid.astype(F32)).astype(BF16)
    return hi, mid, lo


def _mm3(a, b):
    ah, al = _split2(a)
    bh, bl = _split2(b)
    return _dot(ah, bh) + (_dot(al, bh) + _dot(ah, bl))


def _ada_kernel(c_ref, w_ref, b_ref, o_ref):
    s = _silu(c_ref[...])
    o_ref[...] = _mm3(s, w_ref[...]) + b_ref[...]


def _ada(cc, w_ada, b_ada):
    tn = 512
    n = w_ada.shape[1]
    return pl.pallas_call(
        _ada_kernel,
        grid=(n // tn,),
        in_specs=[pl.BlockSpec((8, D_MODEL), lambda j: (0, 0)),
                  pl.BlockSpec((D_MODEL, tn), lambda j: (0, j)),
                  pl.BlockSpec((1, tn), lambda j: (0, j))],
        out_specs=pl.BlockSpec((8, tn), lambda j: (0, j)),
        out_shape=jax.ShapeDtypeStruct((8, n), F32),
        name="ada",
    )(cc, w_ada, b_ada)


def _inproj_kernel(x_ref, mod_ref, gpre_ref, w_ref, wg_ref, proj_ref, gates_ref, *, row):
    x = x_ref[...]
    xn = x * lax.rsqrt(jnp.mean(x * x, axis=-1, keepdims=True) + EPS)
    shift = mod_ref[row:row + 1, 0:D_MODEL]
    scale = mod_ref[row:row + 1, D_MODEL:2 * D_MODEL]
    h = (xn * gpre_ref[...]) * (1.0 + scale) + shift
    hb = h.astype(BF16)
    nb = 512
    for j in range(PROJ_W // nb):
        proj_ref[:, j * nb:(j + 1) * nb] = _dot(hb, w_ref[:, j * nb:(j + 1) * nb]).astype(BF16)
    gates_ref[...] = _dot(hb, wg_ref[...])


def _inproj(x2, mod, gpre, w_main, w_gate, row, tm):
    n = x2.shape[0]
    return pl.pallas_call(
        functools.partial(_inproj_kernel, row=row),
        grid=(n // tm,),
        in_specs=[pl.BlockSpec((tm, D_MODEL), lambda i: (i, 0)),
                  pl.BlockSpec((8, 3 * D_MODEL), lambda i: (0, 0)),
                  pl.BlockSpec((1, D_MODEL), lambda i: (0, 0)),
                  pl.BlockSpec((D_MODEL, PROJ_W), lambda i: (0, 0)),
                  pl.BlockSpec((D_MODEL, LANES), lambda i: (0, 0))],
        out_specs=[pl.BlockSpec((tm, PROJ_W), lambda i: (i, 0)),
                   pl.BlockSpec((tm, LANES), lambda i: (i, 0))],
        out_shape=[jax.ShapeDtypeStruct((n, PROJ_W), BF16),
                   jax.ShapeDtypeStruct((n, LANES), F32)],
        compiler_params=pltpu.CompilerParams(
            dimension_semantics=("arbitrary",), vmem_limit_bytes=VMEM_LIMIT),
        name="inproj",
    )(x2, mod, gpre, w_main, w_gate)


def _na_bias_tables(rpb):
    qc = jnp.arange(GRID_W)[:, None]
    kc = jnp.arange(GRID_W)[None, :]
    col_start = jnp.clip(qc - NA_KW // 2, 0, GRID_W - NA_KW)
    inwin = (kc >= col_start) & (kc < col_start + NA_KW)
    off = jnp.clip(kc - qc + NA_KW - 1, 0, 2 * NA_KW - 2)
    t = jnp.where(inwin[None, None], rpb[:, :, off], NEG)
    variants = []
    for d in range(NA_KH):
        v = t[:, NA_KH - 1 - d:2 * NA_KH - 1 - d]
        variants.append(v.transpose(0, 2, 1, 3).reshape(NA_HEADS, GRID_W, NA_KH * GRID_W))
    return jnp.stack(variants, axis=1).astype(F32)


def _na_kernel(q_ref, k_ref, v_ref, z_ref, kc_ref, vc_ref, bias_ref, o_ref, *, rb_rows, rows):
    rb = pl.program_id(1)
    lane = lax.broadcasted_iota(jnp.int32, (1, LANES), 1)
    first = lane < NA_DH
    kc = kc_ref[...]
    vc = vc_ref[...]
    win = NA_KH * GRID_W

    def body(i, carry):
        r = rb * rb_rows + i
        rs = jnp.clip(r - NA_KH // 2, 0, rows - NA_KH)
        d = r - rs
        t0 = pl.multiple_of(i * GRID_W, GRID_W)
        q = q_ref[pl.ds(t0, GRID_W), :] * jnp.asarray(NA_DH ** -0.5, BF16)
        k0 = pl.multiple_of(rs * GRID_W, GRID_W)
        kw = k_ref[pl.ds(k0, win), :]
        vw = v_ref[pl.ds(k0, win), :]
        outs = []
        for h in range(2):
            sel = first if h == 0 else jnp.logical_not(first)
            qh = jnp.where(sel, q, jnp.zeros_like(q))
            s_loc = _dot_nt(qh, kw) + bias_ref[h, d]
            s_ctx = _dot_nt(qh, kc)
            m = jnp.maximum(jnp.max(s_loc, axis=-1, keepdims=True),
                            jnp.max(s_ctx, axis=-1, keepdims=True))
            p_loc = jnp.exp(s_loc - m)
            p_ctx = jnp.exp(s_ctx - m)
            l = jnp.sum(p_loc, axis=-1, keepdims=True) + jnp.sum(p_ctx, axis=-1, keepdims=True)
            o = _dot(p_loc.astype(BF16), vw) + _dot(p_ctx.astype(BF16), vc)
            outs.append(o / l)
        o = jnp.where(first, outs[0], outs[1])
        z = z_ref[pl.ds(t0, GRID_W), :].astype(F32)
        o_ref[pl.ds(t0, GRID_W), :] = (o * _silu(z)).astype(BF16)
        return carry

    lax.fori_loop(0, rb_rows, body, 0)


def _natten(proj, projc, bias, rb_rows=8):
    n = proj.shape[0]
    nc = projc.shape[0]
    rows = n // GRID_W
    tq = rb_rows * GRID_W
    kcol = NA_W // LANES
    return pl.pallas_call(
        functools.partial(_na_kernel, rb_rows=rb_rows, rows=rows),
        grid=(NA_W // LANES, rows // rb_rows),
        in_specs=[pl.BlockSpec((tq, LANES), lambda hp, rb: (rb, hp)),
                  pl.BlockSpec((n, LANES), lambda hp, rb: (0, kcol + hp)),
                  pl.BlockSpec((n, LANES), lambda hp, rb: (0, 2 * kcol + hp)),
                  pl.BlockSpec((tq, LANES), lambda hp, rb: (rb, 3 * kcol + hp)),
                  pl.BlockSpec((nc, LANES), lambda hp, rb: (0, kcol + hp)),
                  pl.BlockSpec((nc, LANES), lambda hp, rb: (0, 2 * kcol + hp)),
                  pl.BlockSpec((2, NA_KH, GRID_W, NA_KH * GRID_W), lambda hp, rb: (hp, 0, 0, 0))],
        out_specs=pl.BlockSpec((tq, LANES), lambda hp, rb: (rb, hp)),
        out_shape=jax.ShapeDtypeStruct((n, NA_W), BF16),
        compiler_params=pltpu.CompilerParams(
            dimension_semantics=("arbitrary", "arbitrary"), vmem_limit_bytes=VMEM_LIMIT),
        name="natten",
    )(proj, proj, proj, proj, projc, projc, bias)


GP_TOK = 256
GP_CH = GP_TOK // CHUNK
HALO = 16
XE_OFF = 8


def _gdnprep_kernel(main_ref, left_ref, right_ref, gates_ref, cw_ref, alog_ref, dtb_ref,
                    cos_ref, sina_ref, sinb_ref,
                    u_ref, w_ref, qd_ref, kd_ref, qk_ref, gl_ref,
                    xe_ref, qn_ref, kn_ref, vv_ref, beta_ref, gc_ref, gct_ref):
    i = pl.program_id(0)
    last = pl.num_programs(0) - 1
    qkv_w = 3 * GDN_W

    lh = left_ref[...].astype(F32)[HALO - 8:HALO, 0:qkv_w]
    rh = right_ref[...].astype(F32)[0:8, 0:qkv_w]
    xe_ref[0:XE_OFF, :] = jnp.where(i > 0, lh, 0.0)
    xe_ref[XE_OFF:XE_OFF + GP_TOK, :] = main_ref[:, 0:qkv_w].astype(F32)
    xe_ref[XE_OFF + GP_TOK:XE_OFF + GP_TOK + 8, :] = jnp.where(i < last, rh, 0.0)

    cos = cos_ref[...]
    sina = sina_ref[...]
    sinb = sinb_ref[...]
    for cb in range(qkv_w // LANES):
        cols = slice(cb * LANES, (cb + 1) * LANES)
        acc = None
        for j in range(CONV_K):
            term = cw_ref[j:j + 1, cols] * xe_ref[pl.ds(XE_OFF - CONV_K // 2 + j, GP_TOK), cols]
            acc = term if acc is None else acc + term
        y = _silu(acc)
        if cb < 2 * GDN_HEADS:
            y = y * lax.rsqrt(jnp.sum(y * y, axis=-1, keepdims=True) + EPS)
            y = (y * cos + pltpu.roll(y, LANES - ROPE_AXIS_DIM // 2, 1) * sina
                 + pltpu.roll(y, ROPE_AXIS_DIM // 2, 1) * sinb)
            if cb < GDN_HEADS:
                qn_ref[:, cols] = y * (GDN_DK ** -0.5)
            else:
                kn_ref[:, (cb - GDN_HEADS) * LANES:(cb - GDN_HEADS + 1) * LANES] = y
        else:
            vv_ref[:, (cb - 2 * GDN_HEADS) * LANES:(cb - 2 * GDN_HEADS + 1) * LANES] = y

    gates = gates_ref[...]
    beta_ref[...] = jax.nn.sigmoid(gates)
    xa = gates + dtb_ref[...]
    softplus = jnp.maximum(xa, 0.0) + jnp.log1p(jnp.exp(-jnp.abs(xa)))
    g = -jnp.exp(alog_ref[...]) * softplus

    ti = lax.broadcasted_iota(jnp.int32, (GP_TOK, GP_TOK), 0)
    tj = lax.broadcasted_iota(jnp.int32, (GP_TOK, GP_TOK), 1)
    same = (ti // CHUNK) == (tj // CHUNK)
    lower = jnp.where(same & (ti >= tj), 1.0, 0.0).astype(BF16)
    upper = jnp.where(same & (ti <= tj), 1.0, 0.0).astype(BF16)
    g3 = _split3(g)
    gc_f = _dot(lower, g3[0]) + (_dot(lower, g3[1]) + _dot(lower, g3[2]))
    gc_b = _dot(upper, g3[0]) + (_dot(upper, g3[1]) + _dot(upper, g3[2]))
    gc_ref[0] = gc_f
    gc_ref[1] = gc_b
    gl_ref[...] = jnp.zeros_like(gl_ref)
    for c in range(GP_CH):
        rows = slice(c * CHUNK, (c + 1) * CHUNK)
        for d, gc in enumerate((gc_f, gc_b)):
            blk = gc[rows]
            gct_ref[c, d] = jnp.concatenate([blk, pltpu.roll(blk, LANES - 1, 1)], axis=0).T
        gl_ref[c, 0:1, :] = jnp.exp(gc_f[(c + 1) * CHUNK - 1:(c + 1) * CHUNK, :])
        gl_ref[c, 1:2, :] = jnp.exp(gc_b[c * CHUNK:c * CHUNK + 1, :])

    ii = lax.broadcasted_iota(jnp.int32, (CHUNK, LANES), 0)
    jj = lax.broadcasted_iota(jnp.int32, (CHUNK, LANES), 1) % CHUNK
    lo = lax.broadcasted_iota(jnp.int32, (1, LANES), 1) < CHUNK
    lo_wide = lax.broadcasted_iota(jnp.int32, (1, 2 * LANES), 1) < LANES
    eye = jnp.where(ii == jj, 1.0, 0.0).astype(F32)
    bodies = [(d, p) for d in range(2) for p in range(GDN_HEADS // 2)]

    def blockdiag(y, first):
        z = jnp.zeros_like(y)
        return jnp.concatenate([jnp.where(first, y, z), jnp.where(first, z, y)], axis=0)

    def pair_mm(x16, ybd16):
        return _dot(x16, ybd16)

    def chunk_body(c, carry):
        c0 = pl.multiple_of(c * CHUNK, CHUNK)
        tok = pl.ds(c0, CHUNK)
        a_l, dec_l, t_l, p_l, kbeg_l, vb_l = [], [], [], [], [], []
        for d, p in bodies:
            incl = (ii >= jj) if d == 0 else (ii <= jj)
            strict = (ii > jj) if d == 0 else (ii < jj)
            pl2 = slice(2 * p * LANES, (2 * p + 2) * LANES)
            lb = d * GDN_HEADS + 2 * p
            lg = 2 * GDN_HEADS + lb
            q = qn_ref[tok, pl2]
            k = kn_ref[tok, pl2]
            v = vv_ref[tok, pl2]
            beta = jnp.where(lo_wide, beta_ref[tok, lb:lb + 1], beta_ref[tok, lb + 1:lb + 2])
            g0 = gc_ref[d, tok, lg:lg + 1]
            g1 = gc_ref[d, tok, lg + 1:lg + 2]
            gcol = jnp.where(lo, g0, g1)
            gcol_wide = jnp.where(lo_wide, g0, g1)
            grow = gct_ref[c, d, lg:lg + 1, :]
            e0 = c0 + CHUNK - 1 if d == 0 else c0
            gend = jnp.where(lo_wide, gc_ref[d, pl.ds(e0, 1), lg:lg + 1],
                             gc_ref[d, pl.ds(e0, 1), lg + 1:lg + 2])
            dec = jnp.exp(jnp.where(incl, gcol - grow, NEG))
            kb = k * beta
            k_nt = blockdiag(k.astype(BF16), lo_wide)
            a = jnp.where(strict, _dot_nt(kb.astype(BF16), k_nt) * dec, 0.0)
            eg = jnp.exp(gcol_wide)
            qk = _dot_nt(q.astype(BF16), k_nt) * dec
            qk_ref[d, tok, p * LANES:(p + 1) * LANES] = qk.astype(BF16)
            qd_ref[d, tok, pl2] = (q * eg).astype(BF16)
            kd_ref[d, tok, pl2] = (k * jnp.exp(gend - gcol_wide)).astype(BF16)
            a_l.append(a)
            p_l.append(blockdiag(a.astype(BF16), lo))
            t_l.append(eye - a)
            kbeg_l.append((kb * eg).astype(BF16))
            vb_l.append((v * beta).astype(BF16))
        for _ in range(5):
            p_l = [blockdiag(pair_mm(pb[0:CHUNK] + pb[CHUNK:], pb).astype(BF16), lo) for pb in p_l]
            t_l = [t + pair_mm(t.astype(BF16), pb) for t, pb in zip(t_l, p_l)]
        res_l = []
        for a, t in zip(a_l, t_l):
            ah, al = _split2(a)
            th, tl = _split2(t)
            thb = blockdiag(th, lo)
            at = pair_mm(ah, thb) + (pair_mm(al, thb) + pair_mm(ah, blockdiag(tl, lo)))
            res_l.append((eye - t) - at)
        t_l = [t + pair_mm(t.astype(BF16), blockdiag(r.astype(BF16), lo))
               for t, r in zip(t_l, res_l)]
        for (d, p), t, vb, kbeg in zip(bodies, t_l, vb_l, kbeg_l):
            pl2 = slice(2 * p * LANES, (2 * p + 2) * LANES)
            t16 = t.astype(BF16)
            u_ref[d, tok, pl2] = _dot(t16, blockdiag(vb, lo_wide))
            w_ref[d, tok, pl2] = _dot(t16, blockdiag(kbeg, lo_wide)).astype(BF16)
        return carry

    lax.fori_loop(0, GP_CH, chunk_body, 0)


def _gdnprep(proj, gates, conv_w, alog, dtb, cos, sina, sinb):
    n = proj.shape[0]
    nt = n // GP_TOK
    hb = GP_TOK // HALO
    nhalo = n // HALO
    qkv_w = 3 * GDN_W
    tok_spec = lambda w: pl.BlockSpec((GP_TOK, w), lambda i: (i, 0))
    dir_spec = lambda w: pl.BlockSpec((2, GP_TOK, w), lambda i: (0, i, 0))
    return pl.pallas_call(
        _gdnprep_kernel,
        grid=(nt,),
        in_specs=[pl.BlockSpec((GP_TOK, PROJ_W // 2), lambda i: (i, 1)),
                  pl.BlockSpec((HALO, PROJ_W // 2), lambda i: (jnp.maximum(i * hb - 1, 0), 1)),
                  pl.BlockSpec((HALO, PROJ_W // 2),
                               lambda i: (jnp.minimum((i + 1) * hb, nhalo - 1), 1)),
                  tok_spec(LANES),
                  pl.BlockSpec((8, qkv_w), lambda i: (0, 0)),
                  pl.BlockSpec((1, LANES), lambda i: (0, 0)),
                  pl.BlockSpec((1, LANES), lambda i: (0, 0)),
                  tok_spec(LANES), tok_spec(LANES), tok_spec(LANES)],
        out_specs=[dir_spec(GDN_W), dir_spec(GDN_W), dir_spec(GDN_W), dir_spec(GDN_W),
                   dir_spec(GDN_HEADS * CHUNK),
                   pl.BlockSpec((GP_CH, 8, LANES), lambda i: (i, 0, 0))],
        out_shape=[jax.ShapeDtypeStruct((2, n, GDN_W), F32),
                   jax.ShapeDtypeStruct((2, n, GDN_W), BF16),
                   jax.ShapeDtypeStruct((2, n, GDN_W), BF16),
                   jax.ShapeDtypeStruct((2, n, GDN_W), BF16),
                   jax.ShapeDtypeStruct((2, n, GDN_HEADS * CHUNK), BF16),
                   jax.ShapeDtypeStruct((n // CHUNK, 8, LANES), F32)],
        scratch_shapes=[pltpu.VMEM((GP_TOK + 2 * XE_OFF, qkv_w), F32),
                        pltpu.VMEM((GP_TOK, GDN_W), F32),
                        pltpu.VMEM((GP_TOK, GDN_W), F32),
                        pltpu.VMEM((GP_TOK, GDN_W), F32),
                        pltpu.VMEM((GP_TOK, LANES), F32),
                        pltpu.VMEM((2, GP_TOK, LANES), F32),
                        pltpu.VMEM((GP_CH, 2, LANES, LANES), F32)],
        compiler_params=pltpu.CompilerParams(
            dimension_semantics=("arbitrary",), vmem_limit_bytes=VMEM_LIMIT),
        name="gdnprep",
    )(proj, proj, proj, gates, conv_w, alog, dtb, cos, sina, sinb)


def _scan_kernel(gl_ref, s0_ref, uf_ref, wf_ref, qdf_ref, kdf_ref, qkf_ref,
                 ub_ref, wb_ref, qdb_ref, kdb_ref, qkb_ref,
                 of_ref, ob_ref, sfin_ref, s_ref):
    n = pl.program_id(0)
    nch = pl.num_programs(0)

    @pl.when(n == 0)
    def _():
        s_ref[...] = s0_ref[...]

    streams = ((uf_ref, wf_ref, qdf_ref, kdf_ref, qkf_ref, of_ref, n),
               (ub_ref, wb_ref, qdb_ref, kdb_ref, qkb_ref, ob_ref, nch - 1 - n))
    lo_wide = lax.broadcasted_iota(jnp.int32, (1, 2 * LANES), 1) < LANES
    for d, (u_ref, w_ref, qd_ref, kd_ref, qk_ref, o_ref, ch) in enumerate(streams):
        for p in range(GDN_HEADS // 2):
            pl2 = slice(2 * p * LANES, (2 * p + 2) * LANES)
            s16, v16 = [], []
            for h in (2 * p, 2 * p + 1):
                hl = slice(h * LANES, (h + 1) * LANES)
                s16.append(s_ref[d, h].astype(BF16))
                v16.append((u_ref[:, hl] - _dot(w_ref[:, hl], s16[-1])).astype(BF16))
            v_pair = jnp.concatenate(v16, axis=1)
            zero = jnp.zeros_like(v_pair)
            v_bd = jnp.concatenate([jnp.where(lo_wide, v_pair, zero),
                                    jnp.where(lo_wide, zero, v_pair)], axis=0)
            o_intra = _dot(qk_ref[:, p * LANES:(p + 1) * LANES], v_bd)
            for j, h in enumerate((2 * p, 2 * p + 1)):
                hl = slice(h * LANES, (h + 1) * LANES)
                o_ref[:, hl] = _dot(qd_ref[:, hl], s16[j]) + o_intra[:, j * LANES:(j + 1) * LANES]
                decay = gl_ref[(d * nch + ch) * GDN_HEADS + h]
                s_ref[d, h] = s_ref[d, h] * decay + _dot_tn(kd_ref[:, hl], v16[j])

    @pl.when(n == nch - 1)
    def _():
        sfin_ref[...] = s_ref[...]


def _scan(gl, s0, u, w, qd, kd, qk):
    n = u.shape[1]
    nch = n // CHUNK
    fwd = lambda wd: pl.BlockSpec((None, CHUNK, wd), lambda i: (0, i, 0))
    bwd = lambda wd: pl.BlockSpec((None, CHUNK, wd), lambda i: (1, nch - 1 - i, 0))
    qkw = GDN_HEADS * CHUNK
    state_spec = pl.BlockSpec((2, GDN_HEADS, GDN_DK, GDN_DV), lambda i: (0, 0, 0, 0))
    return pl.pallas_call(
        _scan_kernel,
        grid=(nch,),
        in_specs=[pl.BlockSpec(memory_space=pltpu.SMEM), state_spec,
                  fwd(GDN_W), fwd(GDN_W), fwd(GDN_W), fwd(GDN_W), fwd(qkw),
                  bwd(GDN_W), bwd(GDN_W), bwd(GDN_W), bwd(GDN_W), bwd(qkw)],
        out_specs=[pl.BlockSpec((CHUNK, GDN_W), lambda i: (i, 0)),
                   pl.BlockSpec((CHUNK, GDN_W), lambda i: (nch - 1 - i, 0)),
                   state_spec],
        out_shape=[jax.ShapeDtypeStruct((n, GDN_W), F32),
                   jax.ShapeDtypeStruct((n, GDN_W), F32),
                   jax.ShapeDtypeStruct((2, GDN_HEADS, GDN_DK, GDN_DV), F32)],
        scratch_shapes=[pltpu.VMEM((2, GDN_HEADS, GDN_DK, GDN_DV), F32)],
        compiler_params=pltpu.CompilerParams(dimension_semantics=("arbitrary",)),
        name="scan",
    )(gl, s0, u, w, qd, kd, qk, u, w, qd, kd, qk)


def _outproj_kernel(x_ref, na_ref, of_ref, ob_ref, gz_ref, gnw_ref, wout_ref, gpost_ref, mod_ref,
                    o_ref):
    og = of_ref[...] + ob_ref[...]
    gz = gz_ref[...].astype(F32)
    gnw = gnw_ref[...]
    parts = []
    for h in range(GDN_HEADS):
        seg = og[:, h * LANES:(h + 1) * LANES]
        seg = seg * lax.rsqrt(jnp.mean(seg * seg, axis=-1, keepdims=True) + EPS)
        parts.append(seg * gnw)
    gd = (jnp.concatenate(parts, axis=-1) * _silu(gz)).astype(BF16)
    y = _dot(na_ref[...], wout_ref[0:NA_W, :]) + _dot(gd, wout_ref[NA_W:NA_W + GDN_W, :])
    yn = y * lax.rsqrt(jnp.mean(y * y, axis=-1, keepdims=True) + EPS)
    gate = mod_ref[0:1, 2 * D_MODEL:3 * D_MODEL]
    o_ref[...] = x_ref[...] + gate * (yn * gpost_ref[...])


def _outproj(x2, na, o_f, o_b, proj, gnw, w_out, gpost, mod, tm=512):
    n = x2.shape[0]
    gz_col = (PROJ_W - GDN_W) // GDN_W
    return pl.pallas_call(
        _outproj_kernel,
        grid=(n // tm,),
        in_specs=[pl.BlockSpec((tm, D_MODEL), lambda i: (i, 0)),
                  pl.BlockSpec((tm, NA_W), lambda i: (i, 0)),
                  pl.BlockSpec((tm, GDN_W), lambda i: (i, 0)),
                  pl.BlockSpec((tm, GDN_W), lambda i: (i, 0)),
                  pl.BlockSpec((tm, GDN_W), lambda i: (i, gz_col)),
                  pl.BlockSpec((1, LANES), lambda i: (0, 0)),
                  pl.BlockSpec((NA_W + GDN_W, D_MODEL), lambda i: (0, 0)),
                  pl.BlockSpec((1, D_MODEL), lambda i: (0, 0)),
                  pl.BlockSpec((8, 3 * D_MODEL), lambda i: (0, 0))],
        out_specs=pl.BlockSpec((tm, D_MODEL), lambda i: (i, 0)),
        out_shape=jax.ShapeDtypeStruct((n, D_MODEL), F32),
        compiler_params=pltpu.CompilerParams(
            dimension_semantics=("arbitrary",), vmem_limit_bytes=VMEM_LIMIT),
        name="outproj",
    )(x2, na, o_f, o_b, proj, gnw, w_out, gpost, mod)


def _rope_tables(n):
    t = jnp.arange(n)
    row = (t // GRID_W).astype(F32)
    col = (t % GRID_W).astype(F32)
    inv_freq = ROPE_BASE ** (-jnp.arange(0, ROPE_AXIS_DIM, 2, dtype=F32) / ROPE_AXIS_DIM)
    ang_r = row[:, None] * inv_freq[None, :]
    ang_c = col[:, None] * inv_freq[None, :]
    cr, sr, cc, sc = jnp.cos(ang_r), jnp.sin(ang_r), jnp.cos(ang_c), jnp.sin(ang_c)
    zero = jnp.zeros_like(sr)
    cos = jnp.concatenate([cr, cr, cc, cc], axis=-1)
    sina = jnp.concatenate([-sr, zero, -sc, zero], axis=-1)
    sinb = jnp.concatenate([zero, sr, zero, sc], axis=-1)
    return cos, sina, sinb


def _lane_row(vals, offset):
    return jnp.zeros((1, LANES), F32).at[0, offset:offset + vals.shape[0]].set(vals)


def kernel(x, c, ctx, c_ctx, w_ada, b_ada, g_pre, g_post, w_in, conv_w, rpb, A_log, dt_bias,
           gdn_norm_w, w_out):
    n = x.shape[1]
    nc = ctx.shape[1]
    x2 = x[0]
    xc2 = ctx[0]

    cc = jnp.zeros((8, D_MODEL), F32).at[0].set(c[0]).at[1].set(c_ctx)
    mod = _ada(cc, w_ada[0], b_ada[0][None, :])

    w_main = w_in[0][:, :PROJ_W].astype(BF16)
    w_gate = jnp.pad(w_in[0][:, PROJ_W:], ((0, 0), (0, LANES - GATE_W))).astype(BF16)
    gpre = g_pre[0][None, :]
    proj, gates = _inproj(x2, mod, gpre, w_main, w_gate, row=0, tm=512)
    projc, gatesc = _inproj(xc2, mod, gpre, w_main, w_gate, row=1, tm=nc)

    na = _natten(proj, projc, _na_bias_tables(rpb[0]))

    cw = jnp.pad(conv_w[0], ((0, 8 - CONV_K), (0, 0)))
    alog = _lane_row(A_log[0].reshape(-1), 2 * GDN_HEADS)
    dtb = _lane_row(dt_bias[0].reshape(-1), 2 * GDN_HEADS)
    cos, sina, sinb = _rope_tables(n)
    ones = jnp.ones((nc, LANES), F32)
    zeros = jnp.zeros((nc, LANES), F32)

    def gl_rows(gl):
        return jnp.stack([gl[:, 0, 2 * GDN_HEADS:3 * GDN_HEADS],
                          gl[:, 1, 3 * GDN_HEADS:4 * GDN_HEADS]], axis=0).reshape(-1)

    uc, wc, qdc, kdc, qkc, glc = _gdnprep(projc, gatesc, cw, alog, dtb, ones, zeros, zeros)
    s0 = jnp.zeros((2, GDN_HEADS, GDN_DK, GDN_DV), F32)
    _, _, s_ctx = _scan(gl_rows(glc), s0, uc, wc, qdc, kdc, qkc)

    u, w, qd, kd, qk, gl = _gdnprep(proj, gates, cw, alog, dtb, cos, sina, sinb)
    o_f, o_b, _ = _scan(gl_rows(gl), s_ctx, u, w, qd, kd, qk)

    gnw = gdn_norm_w[0][None, :]
    out = _outproj(x2, na, o_f, o_b, proj, gnw, w_out[0].astype(BF16), g_post[0][None, :], mod)
    return out[None]
```

```python
import functools
import math

import jax
import jax.numpy as jnp
from jax import lax
from jax.experimental import pallas as pl
from jax.experimental.pallas import tpu as pltpu

F32 = jnp.float32
BF16 = jnp.bfloat16

D_MODEL = 1024
GRID_W = 64
NA_HEADS = 8
NA_DH = 64
NA_W = NA_HEADS * NA_DH
NA_KH = 8
NA_KW = 16
GDN_HEADS = 4
GDN_DK = 128
GDN_DV = 128
GDN_W = GDN_HEADS * GDN_DV
CHUNK = 64
CONV_K = 5
ROPE_AXIS_DIM = GDN_DK // 2
ROPE_BASE = 10000.0
EPS = 1e-6
PROJ_W = 4 * NA_W + 3 * GDN_W + GDN_W
GATE_W = 2 * 2 * GDN_HEADS
LANES = 128
NEG = -1e30

VMEM_LIMIT = 56 * 1024 * 1024


def _silu(x):
    return x * jax.nn.sigmoid(x)


def _dot(a, b):
    return jnp.dot(a, b, preferred_element_type=F32)


def _dot_nt(a, b):
    return lax.dot_general(a, b, (((1,), (1,)), ((), ())), preferred_element_type=F32)


def _dot_tn(a, b):
    return lax.dot_general(a, b, (((0,), (0,)), ((), ())), preferred_element_type=F32)


def _split2(x):
    hi = x.astype(BF16)
    lo = (x - hi.astype(F32)).astype(BF16)
    return hi, lo


def _split3(x):
    hi = x.astype(BF16)
    r = x - hi.astype(F32)
    mid = r.astype(BF16)
    lo = (r - mid.astype(F32)).astype(BF16)
    return hi, mid, lo


def _mm3(a, b):
    ah, al = _split2(a)
    bh, bl = _split2(b)
    return _dot(ah, bh) + (_dot(al, bh) + _dot(ah, bl))


def _ada_kernel(c_ref, w_ref, b_ref, o_ref):
    s = _silu(c_ref[...])
    o_ref[...] = _mm3(s, w_ref[...]) + b_ref[...]


def _ada(cc, w_ada, b_ada):
    tn = 512
    n = w_ada.shape[1]
    return pl.pallas_call(
        _ada_kernel,
        grid=(n // tn,),
        in_specs=[pl.BlockSpec((8, D_MODEL), lambda j: (0, 0)),
                  pl.BlockSpec((D_MODEL, tn), lambda j: (0, j)),
                  pl.BlockSpec((1, tn), lambda j: (0, j))],
        out_specs=pl.BlockSpec((8, tn), lambda j: (0, j)),
        out_shape=jax.ShapeDtypeStruct((8, n), F32),
        name="ada",
    )(cc, w_ada, b_ada)


def _inproj_kernel(x_ref, mod_ref, gpre_ref, w_ref, wg_ref, proj_ref, gates_ref, *, row):
    x = x_ref[...]
    xn = x * lax.rsqrt(jnp.mean(x * x, axis=-1, keepdims=True) + EPS)
    shift = mod_ref[row:row + 1, 0:D_MODEL]
    scale = mod_ref[row:row + 1, D_MODEL:2 * D_MODEL]
    h = (xn * gpre_ref[...]) * (1.0 + scale) + shift
    hb = h.astype(BF16)
    nb = 512
    for j in range(PROJ_W // nb):
        proj_ref[:, j * nb:(j + 1) * nb] = _dot(hb, w_ref[:, j * nb:(j + 1) * nb]).astype(BF16)
    gates_ref[...] = _dot(hb, wg_ref[...])


def _inproj(x2, mod, gpre, w_main, w_gate, row, tm):
    n = x2.shape[0]
    return pl.pallas_call(
        functools.partial(_inproj_kernel, row=row),
        grid=(n // tm,),
        in_specs=[pl.BlockSpec((tm, D_MODEL), lambda i: (i, 0)),
                  pl.BlockSpec((8, 3 * D_MODEL), lambda i: (0, 0)),
                  pl.BlockSpec((1, D_MODEL), lambda i: (0, 0)),
                  pl.BlockSpec((D_MODEL, PROJ_W), lambda i: (0, 0)),
                  pl.BlockSpec((D_MODEL, LANES), lambda i: (0, 0))],
        out_specs=[pl.BlockSpec((tm, PROJ_W), lambda i: (i, 0)),
                   pl.BlockSpec((tm, LANES), lambda i: (i, 0))],
        out_shape=[jax.ShapeDtypeStruct((n, PROJ_W), BF16),
                   jax.ShapeDtypeStruct((n, LANES), F32)],
        compiler_params=pltpu.CompilerParams(
            dimension_semantics=("arbitrary",), vmem_limit_bytes=VMEM_LIMIT),
        name="inproj",
    )(x2, mod, gpre, w_main, w_gate)


NA_RO = 2 * NA_KH - 1
NA_CO = 2 * NA_KW - 1
NA_ROWS_PER_ITER = 4


def _na_build_tables(rpb_ref, tab_ref, hp):
    qc = lax.broadcasted_iota(jnp.int32, (GRID_W, LANES), 0)
    kcol = lax.broadcasted_iota(jnp.int32, (GRID_W, LANES), 1) % GRID_W
    lo = lax.broadcasted_iota(jnp.int32, (1, LANES), 1) < GRID_W
    diff = kcol - qc + (NA_KW - 1)
    col_start = jnp.clip(qc - NA_KW // 2, 0, GRID_W - NA_KW)
    inwin = (kcol >= col_start) & (kcol < col_start + NA_KW)
    for hh in range(2):
        def ro_body(ro, carry):
            base = ((hp * 2 + hh) * NA_RO + ro) * NA_CO
            acc = jnp.zeros((GRID_W, LANES), F32)
            for j in range(NA_CO):
                val = jnp.where(lo, rpb_ref[base + j], rpb_ref[base + NA_CO + j])
                acc = jnp.where(diff == j, val, acc)
            tab_ref[hh, ro] = jnp.where(inwin, acc, NEG)
            return carry
        lax.fori_loop(0, NA_RO - 1, ro_body, 0)


def _na_kernel(rpb_ref, q_ref, k_ref, v_ref, z_ref, kc_ref, vc_ref, o_ref, tab_ref,
               *, rb_rows, rows):
    hp = pl.program_id(0)
    rb = pl.program_id(1)

    @pl.when(rb == 0)
    def _():
        _na_build_tables(rpb_ref, tab_ref, hp)

    lane = lax.broadcasted_iota(jnp.int32, (1, LANES), 1)
    first = lane < NA_DH
    kc = kc_ref[...]
    vc = vc_ref[...]
    win = NA_KH * GRID_W

    def body(it, carry):
        items = []
        for rr in range(NA_ROWS_PER_ITER):
            i = it * NA_ROWS_PER_ITER + rr
            r = rb * rb_rows + i
            rs = jnp.clip(r - NA_KH // 2, 0, rows - NA_KH)
            ro0 = NA_KH - 1 - (r - rs)
            t0 = pl.multiple_of(i * GRID_W, GRID_W)
            q = q_ref[pl.ds(t0, GRID_W), :] * jnp.asarray(NA_DH ** -0.5, BF16)
            k0 = pl.multiple_of(rs * GRID_W, GRID_W)
            kw = k_ref[pl.ds(k0, win), :]
            vw = v_ref[pl.ds(k0, win), :]
            zq = jnp.zeros_like(q)
            qs = jnp.concatenate([jnp.where(first, q, zq), jnp.where(first, zq, q)], axis=0)
            items.append((t0, ro0, qs, kw, vw))
        s_loc = [_dot_nt(qs, kw)
                 + jnp.concatenate(
                     [jnp.concatenate([tab_ref[hh, ro0 + 2 * m] for m in range(NA_KH // 2)], axis=1)
                      for hh in range(2)], axis=0)
                 for (_, ro0, qs, kw, _) in items]
        s_ctx = [_dot_nt(qs, kc) for (_, _, qs, _, _) in items]
        mx = [jnp.maximum(jnp.max(a, axis=-1, keepdims=True), jnp.max(b, axis=-1, keepdims=True))
              for a, b in zip(s_loc, s_ctx)]
        p_loc = [jnp.exp(a - m) for a, m in zip(s_loc, mx)]
        p_ctx = [jnp.exp(b - m) for b, m in zip(s_ctx, mx)]
        inv = [1.0 / (jnp.sum(a, axis=-1, keepdims=True) + jnp.sum(b, axis=-1, keepdims=True))
               for a, b in zip(p_loc, p_ctx)]
        outs = [(_dot(a.astype(BF16), it_[4]) + _dot(b.astype(BF16), vc)) * il
                for a, b, il, it_ in zip(p_loc, p_ctx, inv, items)]
        for (t0, _, _, _, _), o2 in zip(items, outs):
            o = jnp.where(first, o2[0:GRID_W], o2[GRID_W:2 * GRID_W])
            z = z_ref[pl.ds(t0, GRID_W), :].astype(F32)
            o_ref[pl.ds(t0, GRID_W), :] = (o * _silu(z)).astype(BF16)
        return carry

    lax.fori_loop(0, rb_rows // NA_ROWS_PER_ITER, body, 0)


def _natten(proj, projc, rpb_flat, rb_rows=16):
    n = proj.shape[0]
    nc = projc.shape[0]
    rows = n // GRID_W
    tq = rb_rows * GRID_W
    kcol = NA_W // LANES
    return pl.pallas_call(
        functools.partial(_na_kernel, rb_rows=rb_rows, rows=rows),
        grid=(NA_W // LANES, rows // rb_rows),
        in_specs=[pl.BlockSpec(memory_space=pltpu.SMEM),
                  pl.BlockSpec((tq, LANES), lambda hp, rb: (rb, hp)),
                  pl.BlockSpec((n, LANES), lambda hp, rb: (0, kcol + hp)),
                  pl.BlockSpec((n, LANES), lambda hp, rb: (0, 2 * kcol + hp)),
                  pl.BlockSpec((tq, LANES), lambda hp, rb: (rb, 3 * kcol + hp)),
                  pl.BlockSpec((nc, LANES), lambda hp, rb: (0, kcol + hp)),
                  pl.BlockSpec((nc, LANES), lambda hp, rb: (0, 2 * kcol + hp))],
        out_specs=pl.BlockSpec((tq, LANES), lambda hp, rb: (rb, hp)),
        out_shape=jax.ShapeDtypeStruct((n, NA_W), BF16),
        scratch_shapes=[pltpu.VMEM((2, NA_RO - 1, GRID_W, LANES), F32)],
        compiler_params=pltpu.CompilerParams(
            dimension_semantics=("arbitrary", "arbitrary"), vmem_limit_bytes=VMEM_LIMIT),
        name="natten",
    )(rpb_flat, proj, proj, proj, proj, projc, projc)


GP_TOK = 256
GP_CH = GP_TOK // CHUNK
HALO = 16
XE_OFF = 8


def _gdnprep_kernel(main_ref, left_ref, right_ref, gates_ref, cw_ref, alog_ref, dtb_ref,
                    rowtab_ref, coltab_ref,
                    u_ref, w_ref, qd_ref, kd_ref, qk_ref, gl_ref,
                    xe_ref, qn_ref, kn_ref, vv_ref, beta_ref, gc_ref, gct_ref):
    i = pl.program_id(0)
    last = pl.num_programs(0) - 1
    qkv_w = 3 * GDN_W

    lh = left_ref[...].astype(F32)[HALO - 8:HALO, 0:qkv_w]
    rh = right_ref[...].astype(F32)[0:8, 0:qkv_w]
    xe_ref[0:XE_OFF, :] = jnp.where(i > 0, lh, 0.0)
    xe_ref[XE_OFF:XE_OFF + GP_TOK, :] = main_ref[:, 0:qkv_w].astype(F32)
    xe_ref[XE_OFF + GP_TOK:XE_OFF + GP_TOK + 8, :] = jnp.where(i < last, rh, 0.0)

    row_lanes = lax.broadcasted_iota(jnp.int32, (1, LANES), 1) < ROPE_AXIS_DIM

    def rope_table(kind):
        return jnp.concatenate(
            [jnp.where(row_lanes, rowtab_ref[kind * GP_CH + c:kind * GP_CH + c + 1, :],
                       coltab_ref[kind]) for c in range(GP_CH)], axis=0)

    cos, sina, sinb = rope_table(0), rope_table(1), rope_table(2)
    for cb in range(qkv_w // LANES):
        cols = slice(cb * LANES, (cb + 1) * LANES)
        acc = None
        for j in range(CONV_K):
            term = cw_ref[j:j + 1, cols] * xe_ref[pl.ds(XE_OFF - CONV_K // 2 + j, GP_TOK), cols]
            acc = term if acc is None else acc + term
        y = _silu(acc)
        if cb < 2 * GDN_HEADS:
            y = y * lax.rsqrt(jnp.sum(y * y, axis=-1, keepdims=True) + EPS)
            y = (y * cos + pltpu.roll(y, LANES - ROPE_AXIS_DIM // 2, 1) * sina
                 + pltpu.roll(y, ROPE_AXIS_DIM // 2, 1) * sinb)
            if cb < GDN_HEADS:
                qn_ref[:, cols] = y * (GDN_DK ** -0.5)
            else:
                kn_ref[:, (cb - GDN_HEADS) * LANES:(cb - GDN_HEADS + 1) * LANES] = y
        else:
            vv_ref[:, (cb - 2 * GDN_HEADS) * LANES:(cb - 2 * GDN_HEADS + 1) * LANES] = y

    gates = gates_ref[...]
    beta_ref[...] = jax.nn.sigmoid(gates)
    xa = gates + dtb_ref[...]
    softplus = jnp.maximum(xa, 0.0) + jnp.log1p(jnp.exp(-jnp.abs(xa)))
    g = -jnp.exp(alog_ref[...]) * softplus

    ti = lax.broadcasted_iota(jnp.int32, (GP_TOK, GP_TOK), 0)
    tj = lax.broadcasted_iota(jnp.int32, (GP_TOK, GP_TOK), 1)
    same = (ti // CHUNK) == (tj // CHUNK)
    lower = jnp.where(same & (ti >= tj), 1.0, 0.0).astype(BF16)
    upper = jnp.where(same & (ti <= tj), 1.0, 0.0).astype(BF16)
    g3 = _split3(g)
    gc_f = _dot(lower, g3[0]) + (_dot(lower, g3[1]) + _dot(lower, g3[2]))
    gc_b = _dot(upper, g3[0]) + (_dot(upper, g3[1]) + _dot(upper, g3[2]))
    gc_ref[0] = gc_f
    gc_ref[1] = gc_b
    gl_ref[...] = jnp.zeros_like(gl_ref)
    for c in range(GP_CH):
        rows = slice(c * CHUNK, (c + 1) * CHUNK)
        for d, gc in enumerate((gc_f, gc_b)):
            blk = gc[rows]
            gct_ref[c, d] = jnp.concatenate([blk, pltpu.roll(blk, LANES - 1, 1)], axis=0).T
        gl_ref[c, 0:1, :] = jnp.exp(gc_f[(c + 1) * CHUNK - 1:(c + 1) * CHUNK, :])
        gl_ref[c, 1:2, :] = jnp.exp(gc_b[c * CHUNK:c * CHUNK + 1, :])

    ii = lax.broadcasted_iota(jnp.int32, (CHUNK, LANES), 0)
    jj = lax.broadcasted_iota(jnp.int32, (CHUNK, LANES), 1) % CHUNK
    lo = lax.broadcasted_iota(jnp.int32, (1, LANES), 1) < CHUNK
    lo_wide = lax.broadcasted_iota(jnp.int32, (1, 2 * LANES), 1) < LANES
    eye = jnp.where(ii == jj, 1.0, 0.0).astype(F32)
    bodies = [(d, p) for d in range(2) for p in range(GDN_HEADS // 2)]

    def blockdiag(y, first):
        z = jnp.zeros_like(y)
        return jnp.concatenate([jnp.where(first, y, z), jnp.where(first, z, y)], axis=0)

    def pair_mm(x16, ybd16):
        return _dot(x16, ybd16)

    def chunk_body(c, carry):
        c0 = pl.multiple_of(c * CHUNK, CHUNK)
        tok = pl.ds(c0, CHUNK)
        a_l, dec_l, t_l, p_l, kbeg_l, vb_l = [], [], [], [], [], []
        for d, p in bodies:
            incl = (ii >= jj) if d == 0 else (ii <= jj)
            strict = (ii > jj) if d == 0 else (ii < jj)
            pl2 = slice(2 * p * LANES, (2 * p + 2) * LANES)
            lb = d * GDN_HEADS + 2 * p
            lg = 2 * GDN_HEADS + lb
            q = qn_ref[tok, pl2]
            k = kn_ref[tok, pl2]
            v = vv_ref[tok, pl2]
            beta = jnp.where(lo_wide, beta_ref[tok, lb:lb + 1], beta_ref[tok, lb + 1:lb + 2])
            g0 = gc_ref[d, tok, lg:lg + 1]
            g1 = gc_ref[d, tok, lg + 1:lg + 2]
            gcol = jnp.where(lo, g0, g1)
            gcol_wide = jnp.where(lo_wide, g0, g1)
            grow = gct_ref[c, d, lg:lg + 1, :]
            e0 = c0 + CHUNK - 1 if d == 0 else c0
            gend = jnp.where(lo_wide, gc_ref[d, pl.ds(e0, 1), lg:lg + 1],
                             gc_ref[d, pl.ds(e0, 1), lg + 1:lg + 2])
            dec = jnp.exp(jnp.where(incl, gcol - grow, NEG))
            kb = k * beta
            k_nt = blockdiag(k.astype(BF16), lo_wide)
            a = jnp.where(strict, _dot_nt(kb.astype(BF16), k_nt) * dec, 0.0)
            eg = jnp.exp(gcol_wide)
            qk = _dot_nt(q.astype(BF16), k_nt) * dec
            qk_ref[d, tok, p * LANES:(p + 1) * LANES] = qk.astype(BF16)
            qd_ref[d, tok, pl2] = (q * eg).astype(BF16)
            kd_ref[d, tok, pl2] = (k * jnp.exp(gend - gcol_wide)).astype(BF16)
            a_l.append(a)
            p_l.append(blockdiag(a.astype(BF16), lo))
            t_l.append(eye - a)
            kbeg_l.append((kb * eg).astype(BF16))
            vb_l.append((v * beta).astype(BF16))
        for _ in range(5):
            p_l = [blockdiag(pair_mm(pb[0:CHUNK] + pb[CHUNK:], pb).astype(BF16), lo) for pb in p_l]
            t_l = [t + pair_mm(t.astype(BF16), pb) for t, pb in zip(t_l, p_l)]
        res_l = []
        for a, t in zip(a_l, t_l):
            ah, al = _split2(a)
            th, tl = _split2(t)
            thb = blockdiag(th, lo)
            at = pair_mm(ah, thb) + (pair_mm(al, thb) + pair_mm(ah, blockdiag(tl, lo)))
            res_l.append((eye - t) - at)
        t_l = [t + pair_mm(t.astype(BF16), blockdiag(r.astype(BF16), lo))
               for t, r in zip(t_l, res_l)]
        for (d, p), t, vb, kbeg in zip(bodies, t_l, vb_l, kbeg_l):
            pl2 = slice(2 * p * LANES, (2 * p + 2) * LANES)
            t16 = t.astype(BF16)
            u_ref[d, tok, pl2] = _dot(t16, blockdiag(vb, lo_wide))
            w_ref[d, tok, pl2] = _dot(t16, blockdiag(kbeg, lo_wide)).astype(BF16)
        return carry

    lax.fori_loop(0, GP_CH, chunk_body, 0)


def _gdnprep(proj, gates, conv_w, alog, dtb, rowtab, coltab):
    n = proj.shape[0]
    nt = n // GP_TOK
    hb = GP_TOK // HALO
    nhalo = n // HALO
    qkv_w = 3 * GDN_W
    tok_spec = lambda w: pl.BlockSpec((GP_TOK, w), lambda i: (i, 0))
    dir_spec = lambda w: pl.BlockSpec((2, GP_TOK, w), lambda i: (0, i, 0))
    return pl.pallas_call(
        _gdnprep_kernel,
        grid=(nt,),
        in_specs=[pl.BlockSpec((GP_TOK, PROJ_W // 2), lambda i: (i, 1)),
                  pl.BlockSpec((HALO, PROJ_W // 2), lambda i: (jnp.maximum(i * hb - 1, 0), 1)),
                  pl.BlockSpec((HALO, PROJ_W // 2),
                               lambda i: (jnp.minimum((i + 1) * hb, nhalo - 1), 1)),
                  tok_spec(LANES),
                  pl.BlockSpec((8, qkv_w), lambda i: (0, 0)),
                  pl.BlockSpec((1, LANES), lambda i: (0, 0)),
                  pl.BlockSpec((1, LANES), lambda i: (0, 0)),
                  pl.BlockSpec((None, 16, LANES), lambda i: (i, 0, 0)),
                  pl.BlockSpec((3, GRID_W, LANES), lambda i: (0, 0, 0))],
        out_specs=[dir_spec(GDN_W), dir_spec(GDN_W), dir_spec(GDN_W), dir_spec(GDN_W),
                   dir_spec(GDN_HEADS * CHUNK),
                   pl.BlockSpec((GP_CH, 8, LANES), lambda i: (i, 0, 0))],
        out_shape=[jax.ShapeDtypeStruct((2, n, GDN_W), F32),
                   jax.ShapeDtypeStruct((2, n, GDN_W), BF16),
                   jax.ShapeDtypeStruct((2, n, GDN_W), BF16),
                   jax.ShapeDtypeStruct((2, n, GDN_W), BF16),
                   jax.ShapeDtypeStruct((2, n, GDN_HEADS * CHUNK), BF16),
                   jax.ShapeDtypeStruct((n // CHUNK, 8, LANES), F32)],
        scratch_shapes=[pltpu.VMEM((GP_TOK + 2 * XE_OFF, qkv_w), F32),
                        pltpu.VMEM((GP_TOK, GDN_W), F32),
                        pltpu.VMEM((GP_TOK, GDN_W), F32),
                        pltpu.VMEM((GP_TOK, GDN_W), F32),
                        pltpu.VMEM((GP_TOK, LANES), F32),
                        pltpu.VMEM((2, GP_TOK, LANES), F32),
                        pltpu.VMEM((GP_CH, 2, LANES, LANES), F32)],
        compiler_params=pltpu.CompilerParams(
            dimension_semantics=("arbitrary",), vmem_limit_bytes=VMEM_LIMIT),
        name="gdnprep",
    )(proj, proj, proj, gates, conv_w, alog, dtb, rowtab, coltab)


def _scan_kernel(gl_ref, s0_ref, uf_ref, wf_ref, qdf_ref, kdf_ref, qkf_ref,
                 ub_ref, wb_ref, qdb_ref, kdb_ref, qkb_ref,
                 of_ref, ob_ref, sfin_ref, s_ref):
    n = pl.program_id(0)
    nch = pl.num_programs(0)

    @pl.when(n == 0)
    def _():
        s_ref[...] = s0_ref[...]

    streams = ((uf_ref, wf_ref, qdf_ref, kdf_ref, qkf_ref, of_ref, n),
               (ub_ref, wb_ref, qdb_ref, kdb_ref, qkb_ref, ob_ref, nch - 1 - n))
    lo_wide = lax.broadcasted_iota(jnp.int32, (1, 2 * LANES), 1) < LANES
    chains = [(d, h) for d in range(2) for h in range(GDN_HEADS)]
    hl = lambda h: slice(h * LANES, (h + 1) * LANES)
    s32 = [s_ref[d, h] for d, h in chains]
    s16 = [s.astype(BF16) for s in s32]
    ws = [_dot(streams[d][1][:, hl(h)], s) for (d, h), s in zip(chains, s16)]
    v16 = [(streams[d][0][:, hl(h)] - x).astype(BF16) for (d, h), x in zip(chains, ws)]
    upd = [_dot_tn(streams[d][3][:, hl(h)], v) for (d, h), v in zip(chains, v16)]
    for (d, h), s, x in zip(chains, s32, upd):
        decay = gl_ref[(d * nch + streams[d][6]) * GDN_HEADS + h]
        s_ref[d, h] = s * decay + x
    inter = [_dot(streams[d][2][:, hl(h)], s) for (d, h), s in zip(chains, s16)]
    for d in range(2):
        qk_ref, o_ref = streams[d][4], streams[d][5]
        for p in range(GDN_HEADS // 2):
            v_pair = jnp.concatenate(v16[d * GDN_HEADS + 2 * p:d * GDN_HEADS + 2 * p + 2], axis=1)
            zero = jnp.zeros_like(v_pair)
            v_bd = jnp.concatenate([jnp.where(lo_wide, v_pair, zero),
                                    jnp.where(lo_wide, zero, v_pair)], axis=0)
            o_intra = _dot(qk_ref[:, p * LANES:(p + 1) * LANES], v_bd)
            for j in range(2):
                h = 2 * p + j
                o_ref[:, hl(h)] = inter[d * GDN_HEADS + h] + o_intra[:, j * LANES:(j + 1) * LANES]

    @pl.when(n == nch - 1)
    def _():
        sfin_ref[...] = s_ref[...]


def _scan(gl, s0, u, w, qd, kd, qk):
    n = u.shape[1]
    nch = n // CHUNK
    fwd = lambda wd: pl.BlockSpec((None, CHUNK, wd), lambda i: (0, i, 0))
    bwd = lambda wd: pl.BlockSpec((None, CHUNK, wd), lambda i: (1, nch - 1 - i, 0))
    qkw = GDN_HEADS * CHUNK
    state_spec = pl.BlockSpec((2, GDN_HEADS, GDN_DK, GDN_DV), lambda i: (0, 0, 0, 0))
    return pl.pallas_call(
        _scan_kernel,
        grid=(nch,),
        in_specs=[pl.BlockSpec(memory_space=pltpu.SMEM), state_spec,
                  fwd(GDN_W), fwd(GDN_W), fwd(GDN_W), fwd(GDN_W), fwd(qkw),
                  bwd(GDN_W), bwd(GDN_W), bwd(GDN_W), bwd(GDN_W), bwd(qkw)],
        out_specs=[pl.BlockSpec((CHUNK, GDN_W), lambda i: (i, 0)),
                   pl.BlockSpec((CHUNK, GDN_W), lambda i: (nch - 1 - i, 0)),
                   state_spec],
        out_shape=[jax.ShapeDtypeStruct((n, GDN_W), F32),
                   jax.ShapeDtypeStruct((n, GDN_W), F32),
                   jax.ShapeDtypeStruct((2, GDN_HEADS, GDN_DK, GDN_DV), F32)],
        scratch_shapes=[pltpu.VMEM((2, GDN_HEADS, GDN_DK, GDN_DV), F32)],
        compiler_params=pltpu.CompilerParams(dimension_semantics=("arbitrary",)),
        name="scan",
    )(gl, s0, u, w, qd, kd, qk, u, w, qd, kd, qk)


def _outproj_kernel(x_ref, na_ref, of_ref, ob_ref, gz_ref, gnw_ref, wout_ref, gpost_ref, mod_ref,
                    o_ref):
    og = of_ref[...] + ob_ref[...]
    gz = gz_ref[...].astype(F32)
    gnw = gnw_ref[...]
    parts = []
    for h in range(GDN_HEADS):
        seg = og[:, h * LANES:(h + 1) * LANES]
        seg = seg * lax.rsqrt(jnp.mean(seg * seg, axis=-1, keepdims=True) + EPS)
        parts.append(seg * gnw)
    gd = (jnp.concatenate(parts, axis=-1) * _silu(gz)).astype(BF16)
    y = _dot(na_ref[...], wout_ref[0:NA_W, :]) + _dot(gd, wout_ref[NA_W:NA_W + GDN_W, :])
    yn = y * lax.rsqrt(jnp.mean(y * y, axis=-1, keepdims=True) + EPS)
    gate = mod_ref[0:1, 2 * D_MODEL:3 * D_MODEL]
    o_ref[...] = x_ref[...] + gate * (yn * gpost_ref[...])


def _outproj(x2, na, o_f, o_b, proj, gnw, w_out, gpost, mod, tm=512):
    n = x2.shape[0]
    gz_col = (PROJ_W - GDN_W) // GDN_W
    return pl.pallas_call(
        _outproj_kernel,
        grid=(n // tm,),
        in_specs=[pl.BlockSpec((tm, D_MODEL), lambda i: (i, 0)),
                  pl.BlockSpec((tm, NA_W), lambda i: (i, 0)),
                  pl.BlockSpec((tm, GDN_W), lambda i: (i, 0)),
                  pl.BlockSpec((tm, GDN_W), lambda i: (i, 0)),
                  pl.BlockSpec((tm, GDN_W), lambda i: (i, gz_col)),
                  pl.BlockSpec((1, LANES), lambda i: (0, 0)),
                  pl.BlockSpec((NA_W + GDN_W, D_MODEL), lambda i: (0, 0)),
                  pl.BlockSpec((1, D_MODEL), lambda i: (0, 0)),
                  pl.BlockSpec((8, 3 * D_MODEL), lambda i: (0, 0))],
        out_specs=pl.BlockSpec((tm, D_MODEL), lambda i: (i, 0)),
        out_shape=jax.ShapeDtypeStruct((n, D_MODEL), F32),
        compiler_params=pltpu.CompilerParams(
            dimension_semantics=("arbitrary",), vmem_limit_bytes=VMEM_LIMIT),
        name="outproj",
    )(x2, na, o_f, o_b, proj, gnw, w_out, gpost, mod)


def _rope_tables(rows, identity):
    inv_freq = ROPE_BASE ** (-jnp.arange(0, ROPE_AXIS_DIM, 2, dtype=F32) / ROPE_AXIS_DIM)

    def tables(count):
        ang = jnp.arange(count, dtype=F32)[:, None] * inv_freq[None, :]
        if identity:
            ang = jnp.zeros_like(ang)
        c, s = jnp.cos(ang), jnp.sin(ang)
        z = jnp.zeros_like(s)
        return (jnp.concatenate([c, c], -1), jnp.concatenate([-s, z], -1),
                jnp.concatenate([z, s], -1))

    half = jnp.zeros((rows, LANES // 2), F32)
    rowtab = jnp.stack([jnp.concatenate([t, half], -1) for t in tables(rows)], axis=0)
    rowtab = rowtab.reshape(3, rows // GP_CH, GP_CH, LANES).transpose(1, 0, 2, 3)
    rowtab = rowtab.reshape(rows // GP_CH, 3 * GP_CH, LANES)
    rowtab = jnp.pad(rowtab, ((0, 0), (0, 16 - 3 * GP_CH), (0, 0)))
    halfc = jnp.zeros((GRID_W, LANES // 2), F32)
    coltab = jnp.stack([jnp.concatenate([halfc, t], -1) for t in tables(GRID_W)], axis=0)
    return rowtab, coltab


def _lane_row(vals, offset):
    return jnp.zeros((1, LANES), F32).at[0, offset:offset + vals.shape[0]].set(vals)


def kernel(x, c, ctx, c_ctx, w_ada, b_ada, g_pre, g_post, w_in, conv_w, rpb, A_log, dt_bias,
           gdn_norm_w, w_out):
    n = x.shape[1]
    nc = ctx.shape[1]
    x2 = x[0]
    xc2 = ctx[0]

    cc = jnp.zeros((8, D_MODEL), F32).at[0].set(c[0]).at[1].set(c_ctx)
    mod = _ada(cc, w_ada[0], b_ada[0][None, :])

    w_main = w_in[0][:, :PROJ_W].astype(BF16)
    w_gate = jnp.pad(w_in[0][:, PROJ_W:], ((0, 0), (0, LANES - GATE_W))).astype(BF16)
    gpre = g_pre[0][None, :]
    proj, gates = _inproj(x2, mod, gpre, w_main, w_gate, row=0, tm=512)
    projc, gatesc = _inproj(xc2, mod, gpre, w_main, w_gate, row=1, tm=nc)

    na = _natten(proj, projc, rpb[0].reshape(-1))

    cw = jnp.pad(conv_w[0], ((0, 8 - CONV_K), (0, 0)))
    alog = _lane_row(A_log[0].reshape(-1), 2 * GDN_HEADS)
    dtb = _lane_row(dt_bias[0].reshape(-1), 2 * GDN_HEADS)
    rowtab, coltab = _rope_tables(n // GRID_W, identity=False)
    rowtab_c, coltab_c = _rope_tables(nc // GRID_W, identity=True)

    def gl_rows(gl):
        return jnp.stack([gl[:, 0, 2 * GDN_HEADS:3 * GDN_HEADS],
                          gl[:, 1, 3 * GDN_HEADS:4 * GDN_HEADS]], axis=0).reshape(-1)

    uc, wc, qdc, kdc, qkc, glc = _gdnprep(projc, gatesc, cw, alog, dtb, rowtab_c, coltab_c)
    s0 = jnp.zeros((2, GDN_HEADS, GDN_DK, GDN_DV), F32)
    _, _, s_ctx = _scan(gl_rows(glc), s0, uc, wc, qdc, kdc, qkc)

    u, w, qd, kd, qk, gl = _gdnprep(proj, gates, cw, alog, dtb, rowtab, coltab)
    o_f, o_b, _ = _scan(gl_rows(gl), s_ctx, u, w, qd, kd, qk)

    gnw = gdn_norm_w[0][None, :]
    out = _outproj(x2, na, o_f, o_b, proj, gnw, w_out[0].astype(BF16), g_post[0][None, :], mod)
    return out[None]
```

```python
import functools
import math

import jax
import jax.numpy as jnp
from jax import lax
from jax.experimental import pallas as pl
from jax.experimental.pallas import tpu as pltpu

F32 = jnp.float32
BF16 = jnp.bfloat16

D_MODEL = 1024
GRID_W = 64
NA_HEADS = 8
NA_DH = 64
NA_W = NA_HEADS * NA_DH
NA_KH = 8
NA_KW = 16
GDN_HEADS = 4
GDN_DK = 128
GDN_DV = 128
GDN_W = GDN_HEADS * GDN_DV
CHUNK = 64
CONV_K = 5
ROPE_AXIS_DIM = GDN_DK // 2
ROPE_BASE = 10000.0
EPS = 1e-6
PROJ_W = 4 * NA_W + 3 * GDN_W + GDN_W
GATE_W = 2 * 2 * GDN_HEADS
LANES = 128
NEG = -1e30

VMEM_LIMIT = 56 * 1024 * 1024


def _silu(x):
    return x * jax.nn.sigmoid(x)


def _dot(a, b):
    return jnp.dot(a, b, preferred_element_type=F32)


def _dot_nt(a, b):
    return lax.dot_general(a, b, (((1,), (1,)), ((), ())), preferred_element_type=F32)


def _dot_tn(a, b):
    return lax.dot_general(a, b, (((0,), (0,)), ((), ())), preferred_element_type=F32)


def _split2(x):
    hi = x.astype(BF16)
    lo = (x - hi.astype(F32)).astype(BF16)
    return hi, lo


def _split3(x):
    hi = x.astype(BF16)
    r = x - hi.astype(F32)
    mid = r.astype(BF16)
    lo = (r - mid.astype(F32)).astype(BF16)
    return hi, mid, lo


def _mm3(a, b):
    ah, al = _split2(a)
    bh, bl = _split2(b)
    return _dot(ah, bh) + (_dot(al, bh) + _dot(ah, bl))


def _ada_kernel(c_ref, w_ref, b_ref, o_ref):
    s = _silu(c_ref[...])
    o_ref[...] = _mm3(s, w_ref[...]) + b_ref[...]


def _ada(cc, w_ada, b_ada):
    tn = 512
    n = w_ada.shape[1]
    return pl.pallas_call(
        _ada_kernel,
        grid=(n // tn,),
        in_specs=[pl.BlockSpec((8, D_MODEL), lambda j: (0, 0)),
                  pl.BlockSpec((D_MODEL, tn), lambda j: (0, j)),
                  pl.BlockSpec((1, tn), lambda j: (0, j))],
        out_specs=pl.BlockSpec((8, tn), lambda j: (0, j)),
        out_shape=jax.ShapeDtypeStruct((8, n), F32),
        name="ada",
    )(cc, w_ada, b_ada)


def _inproj_kernel(x_ref, mod_ref, gpre_ref, w_ref, wg_ref, proj_ref, gates_ref, *, row):
    x = x_ref[...]
    xn = x * lax.rsqrt(jnp.mean(x * x, axis=-1, keepdims=True) + EPS)
    shift = mod_ref[row:row + 1, 0:D_MODEL]
    scale = mod_ref[row:row + 1, D_MODEL:2 * D_MODEL]
    h = (xn * gpre_ref[...]) * (1.0 + scale) + shift
    hb = h.astype(BF16)
    nb = 512
    for j in range(PROJ_W // nb):
        proj_ref[:, j * nb:(j + 1) * nb] = _dot(hb, w_ref[:, j * nb:(j + 1) * nb]).astype(BF16)
    gates_ref[...] = _dot(hb, wg_ref[...])


def _inproj(x2, mod, gpre, w_main, w_gate, row, tm):
    n = x2.shape[0]
    return pl.pallas_call(
        functools.partial(_inproj_kernel, row=row),
        grid=(n // tm,),
        in_specs=[pl.BlockSpec((tm, D_MODEL), lambda i: (i, 0)),
                  pl.BlockSpec((8, 3 * D_MODEL), lambda i: (0, 0)),
                  pl.BlockSpec((1, D_MODEL), lambda i: (0, 0)),
                  pl.BlockSpec((D_MODEL, PROJ_W), lambda i: (0, 0)),
                  pl.BlockSpec((D_MODEL, LANES), lambda i: (0, 0))],
        out_specs=[pl.BlockSpec((tm, PROJ_W), lambda i: (i, 0)),
                   pl.BlockSpec((tm, LANES), lambda i: (i, 0))],
        out_shape=[jax.ShapeDtypeStruct((n, PROJ_W), BF16),
                   jax.ShapeDtypeStruct((n, LANES), F32)],
        compiler_params=pltpu.CompilerParams(
            dimension_semantics=("arbitrary",), vmem_limit_bytes=VMEM_LIMIT),
        name="inproj",
    )(x2, mod, gpre, w_main, w_gate)


NA_RO = 2 * NA_KH - 1
NA_CO = 2 * NA_KW - 1
NA_ROWS_PER_ITER = 4


def _na_build_tables(rpb_ref, tab_ref, hp):
    qc = lax.broadcasted_iota(jnp.int32, (GRID_W, LANES), 0)
    kcol = lax.broadcasted_iota(jnp.int32, (GRID_W, LANES), 1) % GRID_W
    lo = lax.broadcasted_iota(jnp.int32, (1, LANES), 1) < GRID_W
    diff = kcol - qc + (NA_KW - 1)
    col_start = jnp.clip(qc - NA_KW // 2, 0, GRID_W - NA_KW)
    inwin = (kcol >= col_start) & (kcol < col_start + NA_KW)
    for hh in range(2):
        def ro_body(ro, carry):
            base = ((hp * 2 + hh) * NA_RO + ro) * NA_CO
            acc = jnp.zeros((GRID_W, LANES), F32)
            for j in range(NA_CO):
                val = jnp.where(lo, rpb_ref[base + j], rpb_ref[base + NA_CO + j])
                acc = jnp.where(diff == j, val, acc)
            tab_ref[hh, ro] = jnp.where(inwin, acc, NEG)
            return carry
        lax.fori_loop(0, NA_RO - 1, ro_body, 0)


def _na_kernel(rpb_ref, q_ref, k_ref, v_ref, z_ref, kc_ref, vc_ref, o_ref, tab_ref,
               *, rb_rows, rows):
    hp = pl.program_id(0)
    rb = pl.program_id(1)

    @pl.when(rb == 0)
    def _():
        _na_build_tables(rpb_ref, tab_ref, hp)

    lane = lax.broadcasted_iota(jnp.int32, (1, LANES), 1)
    first = lane < NA_DH
    kc = kc_ref[...]
    vc = vc_ref[...]
    win = NA_KH * GRID_W

    def body(it, carry):
        items = []
        for rr in range(NA_ROWS_PER_ITER):
            i = it * NA_ROWS_PER_ITER + rr
            r = rb * rb_rows + i
            rs = jnp.clip(r - NA_KH // 2, 0, rows - NA_KH)
            ro0 = NA_KH - 1 - (r - rs)
            t0 = pl.multiple_of(i * GRID_W, GRID_W)
            q = q_ref[pl.ds(t0, GRID_W), :] * jnp.asarray(NA_DH ** -0.5, BF16)
            k0 = pl.multiple_of(rs * GRID_W, GRID_W)
            kw = k_ref[pl.ds(k0, win), :]
            vw = v_ref[pl.ds(k0, win), :]
            zq = jnp.zeros_like(q)
            qs = jnp.concatenate([jnp.where(first, q, zq), jnp.where(first, zq, q)], axis=0)
            items.append((t0, ro0, qs, kw, vw))
        s_loc = [_dot_nt(qs, kw)
                 + jnp.concatenate(
                     [jnp.concatenate([tab_ref[hh, ro0 + 2 * m] for m in range(NA_KH // 2)], axis=1)
                      for hh in range(2)], axis=0)
                 for (_, ro0, qs, kw, _) in items]
        s_ctx = [_dot_nt(qs, kc) for (_, _, qs, _, _) in items]
        mx = [jnp.maximum(jnp.max(a, axis=-1, keepdims=True), jnp.max(b, axis=-1, keepdims=True))
              for a, b in zip(s_loc, s_ctx)]
        p_loc = [jnp.exp(a - m) for a, m in zip(s_loc, mx)]
        p_ctx = [jnp.exp(b - m) for b, m in zip(s_ctx, mx)]
        inv = [1.0 / (jnp.sum(a, axis=-1, keepdims=True) + jnp.sum(b, axis=-1, keepdims=True))
               for a, b in zip(p_loc, p_ctx)]
        outs = [(_dot(a.astype(BF16), it_[4]) + _dot(b.astype(BF16), vc)) * il
                for a, b, il, it_ in zip(p_loc, p_ctx, inv, items)]
        for (t0, _, _, _, _), o2 in zip(items, outs):
            o = jnp.where(first, o2[0:GRID_W], o2[GRID_W:2 * GRID_W])
            z = z_ref[pl.ds(t0, GRID_W), :].astype(F32)
            o_ref[pl.ds(t0, GRID_W), :] = (o * _silu(z)).astype(BF16)
        return carry

    lax.fori_loop(0, rb_rows // NA_ROWS_PER_ITER, body, 0)


def _natten(proj, projc, rpb_flat, rb_rows=16):
    n = proj.shape[0]
    nc = projc.shape[0]
    rows = n // GRID_W
    tq = rb_rows * GRID_W
    kcol = NA_W // LANES
    return pl.pallas_call(
        functools.partial(_na_kernel, rb_rows=rb_rows, rows=rows),
        grid=(NA_W // LANES, rows // rb_rows),
        in_specs=[pl.BlockSpec(memory_space=pltpu.SMEM),
                  pl.BlockSpec((tq, LANES), lambda hp, rb: (rb, hp)),
                  pl.BlockSpec((n, LANES), lambda hp, rb: (0, kcol + hp)),
                  pl.BlockSpec((n, LANES), lambda hp, rb: (0, 2 * kcol + hp)),
                  pl.BlockSpec((tq, LANES), lambda hp, rb: (rb, 3 * kcol + hp)),
                  pl.BlockSpec((nc, LANES), lambda hp, rb: (0, kcol + hp)),
                  pl.BlockSpec((nc, LANES), lambda hp, rb: (0, 2 * kcol + hp))],
        out_specs=pl.BlockSpec((tq, LANES), lambda hp, rb: (rb, hp)),
        out_shape=jax.ShapeDtypeStruct((n, NA_W), BF16),
        scratch_shapes=[pltpu.VMEM((2, NA_RO - 1, GRID_W, LANES), F32)],
        compiler_params=pltpu.CompilerParams(
            dimension_semantics=("arbitrary", "arbitrary"), vmem_limit_bytes=VMEM_LIMIT),
        name="natten",
    )(rpb_flat, proj, proj, proj, proj, projc, projc)


GP_TOK = 256
GP_CH = GP_TOK // CHUNK
GP_LOCK = 4
INV_BASE = 8
HALO = 16
XE_OFF = 8


def _gdnprep_kernel(main_ref, left_ref, right_ref, gates_ref, cw_ref, alog_ref, dtb_ref,
                    rowtab_ref, coltab_ref,
                    u_ref, w_ref, qd_ref, kd_ref, qk_ref, gl_ref,
                    xe_ref, qn_ref, kn_ref, vv_ref, beta_ref, gc_ref, gct_ref):
    i = pl.program_id(0)
    last = pl.num_programs(0) - 1
    qkv_w = 3 * GDN_W

    lh = left_ref[...].astype(F32)[HALO - 8:HALO, 0:qkv_w]
    rh = right_ref[...].astype(F32)[0:8, 0:qkv_w]
    xe_ref[0:XE_OFF, :] = jnp.where(i > 0, lh, 0.0)
    xe_ref[XE_OFF:XE_OFF + GP_TOK, :] = main_ref[:, 0:qkv_w].astype(F32)
    xe_ref[XE_OFF + GP_TOK:XE_OFF + GP_TOK + 8, :] = jnp.where(i < last, rh, 0.0)

    row_lanes = lax.broadcasted_iota(jnp.int32, (1, LANES), 1) < ROPE_AXIS_DIM

    def rope_table(kind):
        return jnp.concatenate(
            [jnp.where(row_lanes, rowtab_ref[kind * GP_CH + c:kind * GP_CH + c + 1, :],
                       coltab_ref[kind]) for c in range(GP_CH)], axis=0)

    cos, sina, sinb = rope_table(0), rope_table(1), rope_table(2)
    for cb in range(qkv_w // LANES):
        cols = slice(cb * LANES, (cb + 1) * LANES)
        acc = None
        for j in range(CONV_K):
            term = cw_ref[j:j + 1, cols] * xe_ref[pl.ds(XE_OFF - CONV_K // 2 + j, GP_TOK), cols]
            acc = term if acc is None else acc + term
        y = _silu(acc)
        if cb < 2 * GDN_HEADS:
            y = y * lax.rsqrt(jnp.sum(y * y, axis=-1, keepdims=True) + EPS)
            y = (y * cos + pltpu.roll(y, LANES - ROPE_AXIS_DIM // 2, 1) * sina
                 + pltpu.roll(y, ROPE_AXIS_DIM // 2, 1) * sinb)
            if cb < GDN_HEADS:
                qn_ref[:, cols] = y * (GDN_DK ** -0.5)
            else:
                kn_ref[:, (cb - GDN_HEADS) * LANES:(cb - GDN_HEADS + 1) * LANES] = y
        else:
            vv_ref[:, (cb - 2 * GDN_HEADS) * LANES:(cb - 2 * GDN_HEADS + 1) * LANES] = y

    gates = gates_ref[...]
    beta_ref[...] = jax.nn.sigmoid(gates)
    xa = gates + dtb_ref[...]
    softplus = jnp.maximum(xa, 0.0) + jnp.log1p(jnp.exp(-jnp.abs(xa)))
    g = -jnp.exp(alog_ref[...]) * softplus

    ti = lax.broadcasted_iota(jnp.int32, (GP_TOK, GP_TOK), 0)
    tj = lax.broadcasted_iota(jnp.int32, (GP_TOK, GP_TOK), 1)
    same = (ti // CHUNK) == (tj // CHUNK)
    lower = jnp.where(same & (ti >= tj), 1.0, 0.0).astype(BF16)
    upper = jnp.where(same & (ti <= tj), 1.0, 0.0).astype(BF16)
    g3 = _split3(g)
    gc_f = _dot(lower, g3[0]) + (_dot(lower, g3[1]) + _dot(lower, g3[2]))
    gc_b = _dot(upper, g3[0]) + (_dot(upper, g3[1]) + _dot(upper, g3[2]))
    gc_ref[0] = gc_f
    gc_ref[1] = gc_b
    gl_ref[...] = jnp.zeros_like(gl_ref)
    for c in range(GP_CH):
        rows = slice(c * CHUNK, (c + 1) * CHUNK)
        for d, gc in enumerate((gc_f, gc_b)):
            blk = gc[rows]
            gct_ref[c, d] = jnp.concatenate([blk, pltpu.roll(blk, LANES - 1, 1)], axis=0).T
        gl_ref[c, 0:1, :] = jnp.exp(gc_f[(c + 1) * CHUNK - 1:(c + 1) * CHUNK, :])
        gl_ref[c, 1:2, :] = jnp.exp(gc_b[c * CHUNK:c * CHUNK + 1, :])

    ii = lax.broadcasted_iota(jnp.int32, (CHUNK, LANES), 0)
    jj = lax.broadcasted_iota(jnp.int32, (CHUNK, LANES), 1) % CHUNK
    lo = lax.broadcasted_iota(jnp.int32, (1, LANES), 1) < CHUNK
    lo_wide = lax.broadcasted_iota(jnp.int32, (1, 2 * LANES), 1) < LANES
    eye = jnp.where(ii == jj, 1.0, 0.0).astype(F32)
    bodies = [(d, p) for d in range(2) for p in range(GDN_HEADS // 2)]
    same_block = lambda size: (ii // size) == (jj // size)
    base_blocks = same_block(INV_BASE)
    merge_masks = []
    size = INV_BASE
    while size < CHUNK:
        merge_masks.append(same_block(2 * size) & jnp.logical_not(same_block(size)))
        size *= 2

    def blockdiag(y, first):
        z = jnp.zeros_like(y)
        return jnp.concatenate([jnp.where(first, y, z), jnp.where(first, z, y)], axis=0)

    def pair_mm(x16, ybd16):
        return _dot(x16, ybd16)

    def chunk_body(cg, carry):
        a_l, kbeg_l, vb_l, where_l = [], [], [], []
        for cc, (d, p) in [(cc, b) for cc in range(GP_LOCK) for b in bodies]:
            c = cg * GP_LOCK + cc
            c0 = pl.multiple_of(c * CHUNK, CHUNK)
            tok = pl.ds(c0, CHUNK)
            incl = (ii >= jj) if d == 0 else (ii <= jj)
            strict = (ii > jj) if d == 0 else (ii < jj)
            pl2 = slice(2 * p * LANES, (2 * p + 2) * LANES)
            lb = d * GDN_HEADS + 2 * p
            lg = 2 * GDN_HEADS + lb
            q = qn_ref[tok, pl2]
            k = kn_ref[tok, pl2]
            v = vv_ref[tok, pl2]
            beta = jnp.where(lo_wide, beta_ref[tok, lb:lb + 1], beta_ref[tok, lb + 1:lb + 2])
            g0 = gc_ref[d, tok, lg:lg + 1]
            g1 = gc_ref[d, tok, lg + 1:lg + 2]
            gcol = jnp.where(lo, g0, g1)
            gcol_wide = jnp.where(lo_wide, g0, g1)
            grow = gct_ref[c, d, lg:lg + 1, :]
            e0 = c0 + CHUNK - 1 if d == 0 else c0
            gend = jnp.where(lo_wide, gc_ref[d, pl.ds(e0, 1), lg:lg + 1],
                             gc_ref[d, pl.ds(e0, 1), lg + 1:lg + 2])
            dec = jnp.exp(jnp.where(incl, gcol - grow, NEG))
            kb = k * beta
            k_nt = blockdiag(k.astype(BF16), lo_wide)
            a = jnp.where(strict, _dot_nt(kb.astype(BF16), k_nt) * dec, 0.0)
            eg = jnp.exp(gcol_wide)
            qk = _dot_nt(q.astype(BF16), k_nt) * dec
            qk_ref[d, tok, p * LANES:(p + 1) * LANES] = qk.astype(BF16)
            qd_ref[d, tok, pl2] = (q * eg).astype(BF16)
            kd_ref[d, tok, pl2] = (k * jnp.exp(gend - gcol_wide)).astype(BF16)
            a_l.append(a)
            kbeg_l.append((kb * eg).astype(BF16))
            vb_l.append((v * beta).astype(BF16))
            where_l.append((d, tok, pl2))
        x_l = [jnp.where(base_blocks, -a, 0.0) for a in a_l]
        t_l = [eye + x for x in x_l]
        x_l = [x.astype(BF16) for x in x_l]
        x_l = [pair_mm(x, blockdiag(x, lo)).astype(BF16) for x in x_l]
        r_l = [pair_mm(jnp.concatenate([x, t.astype(BF16)], axis=0), blockdiag(x, lo))
               for x, t in zip(x_l, t_l)]
        x_l = [r[0:CHUNK].astype(BF16) for r in r_l]
        t_l = [t + r[CHUNK:2 * CHUNK] for t, r in zip(t_l, r_l)]
        t_l = [t + pair_mm(t.astype(BF16), blockdiag(x, lo)) for t, x in zip(t_l, x_l)]
        for merged in merge_masks:
            t16_l = [t.astype(BF16) for t in t_l]
            te_l = [pair_mm(t16, blockdiag(jnp.where(merged, a, 0.0).astype(BF16), lo))
                    for t16, a in zip(t16_l, a_l)]
            t_l = [t - pair_mm(te.astype(BF16), blockdiag(t16, lo))
                   for t, te, t16 in zip(t_l, te_l, t16_l)]
        for (d, tok, pl2), t, vb, kbeg in zip(where_l, t_l, vb_l, kbeg_l):
            t16 = t.astype(BF16)
            u_ref[d, tok, pl2] = _dot(t16, blockdiag(vb, lo_wide))
            w_ref[d, tok, pl2] = _dot(t16, blockdiag(kbeg, lo_wide)).astype(BF16)
        return carry

    if GP_LOCK == GP_CH:
        chunk_body(0, 0)
    else:
        lax.fori_loop(0, GP_CH // GP_LOCK, chunk_body, 0)


def _gdnprep(proj, gates, conv_w, alog, dtb, rowtab, coltab):
    n = proj.shape[0]
    nt = n // GP_TOK
    hb = GP_TOK // HALO
    nhalo = n // HALO
    qkv_w = 3 * GDN_W
    tok_spec = lambda w: pl.BlockSpec((GP_TOK, w), lambda i: (i, 0))
    dir_spec = lambda w: pl.BlockSpec((2, GP_TOK, w), lambda i: (0, i, 0))
    return pl.pallas_call(
        _gdnprep_kernel,
        grid=(nt,),
        in_specs=[pl.BlockSpec((GP_TOK, PROJ_W // 2), lambda i: (i, 1)),
                  pl.BlockSpec((HALO, PROJ_W // 2), lambda i: (jnp.maximum(i * hb - 1, 0), 1)),
                  pl.BlockSpec((HALO, PROJ_W // 2),
                               lambda i: (jnp.minimum((i + 1) * hb, nhalo - 1), 1)),
                  tok_spec(LANES),
                  pl.BlockSpec((8, qkv_w), lambda i: (0, 0)),
                  pl.BlockSpec((1, LANES), lambda i: (0, 0)),
                  pl.BlockSpec((1, LANES), lambda i: (0, 0)),
                  pl.BlockSpec((None, 16, LANES), lambda i: (i, 0, 0)),
                  pl.BlockSpec((3, GRID_W, LANES), lambda i: (0, 0, 0))],
        out_specs=[dir_spec(GDN_W), dir_spec(GDN_W), dir_spec(GDN_W), dir_spec(GDN_W),
                   dir_spec(GDN_HEADS * CHUNK),
                   pl.BlockSpec((GP_CH, 8, LANES), lambda i: (i, 0, 0))],
        out_shape=[jax.ShapeDtypeStruct((2, n, GDN_W), F32),
                   jax.ShapeDtypeStruct((2, n, GDN_W), BF16),
                   jax.ShapeDtypeStruct((2, n, GDN_W), BF16),
                   jax.ShapeDtypeStruct((2, n, GDN_W), BF16),
                   jax.ShapeDtypeStruct((2, n, GDN_HEADS * CHUNK), BF16),
                   jax.ShapeDtypeStruct((n // CHUNK, 8, LANES), F32)],
        scratch_shapes=[pltpu.VMEM((GP_TOK + 2 * XE_OFF, qkv_w), F32),
                        pltpu.VMEM((GP_TOK, GDN_W), F32),
                        pltpu.VMEM((GP_TOK, GDN_W), F32),
                        pltpu.VMEM((GP_TOK, GDN_W), F32),
                        pltpu.VMEM((GP_TOK, LANES), F32),
                        pltpu.VMEM((2, GP_TOK, LANES), F32),
                        pltpu.VMEM((GP_CH, 2, LANES, LANES), F32)],
        compiler_params=pltpu.CompilerParams(
            dimension_semantics=("arbitrary",), vmem_limit_bytes=VMEM_LIMIT),
        name="gdnprep",
    )(proj, proj, proj, gates, conv_w, alog, dtb, rowtab, coltab)


SC_CH = 4


def _scan_kernel(gl_ref, s0_ref, uf_ref, wf_ref, qdf_ref, kdf_ref, qkf_ref,
                 ub_ref, wb_ref, qdb_ref, kdb_ref, qkb_ref,
                 of_ref, ob_ref, sfin_ref, s_ref):
    n = pl.program_id(0)
    nsteps = pl.num_programs(0)
    nch = nsteps * SC_CH

    @pl.when(n == 0)
    def _():
        s_ref[...] = s0_ref[...]

    streams = ((uf_ref, wf_ref, qdf_ref, kdf_ref, qkf_ref, of_ref),
               (ub_ref, wb_ref, qdb_ref, kdb_ref, qkb_ref, ob_ref))
    lo_wide = lax.broadcasted_iota(jnp.int32, (1, 2 * LANES), 1) < LANES
    chains = [(d, h) for d in range(2) for h in range(GDN_HEADS)]
    hl = lambda h: slice(h * LANES, (h + 1) * LANES)
    for j in range(SC_CH):
        local = (j, SC_CH - 1 - j)
        chunk = (n * SC_CH + j, (nsteps - 1 - n) * SC_CH + SC_CH - 1 - j)
        tok = [slice(c * CHUNK, (c + 1) * CHUNK) for c in local]
        s32 = [s_ref[d, h] for d, h in chains]
        s16 = [s.astype(BF16) for s in s32]
        ws = [_dot(streams[d][1][tok[d], hl(h)], s) for (d, h), s in zip(chains, s16)]
        v16 = [(streams[d][0][tok[d], hl(h)] - x).astype(BF16) for (d, h), x in zip(chains, ws)]
        upd = [_dot_tn(streams[d][3][tok[d], hl(h)], v) for (d, h), v in zip(chains, v16)]
        for (d, h), s, x in zip(chains, s32, upd):
            decay = gl_ref[(d * nch + chunk[d]) * GDN_HEADS + h]
            s_ref[d, h] = s * decay + x
        inter = [_dot(streams[d][2][tok[d], hl(h)], s) for (d, h), s in zip(chains, s16)]
        for d in range(2):
            qk_ref, o_ref = streams[d][4], streams[d][5]
            for p in range(GDN_HEADS // 2):
                v_pair = jnp.concatenate(
                    v16[d * GDN_HEADS + 2 * p:d * GDN_HEADS + 2 * p + 2], axis=1)
                zero = jnp.zeros_like(v_pair)
                v_bd = jnp.concatenate([jnp.where(lo_wide, v_pair, zero),
                                        jnp.where(lo_wide, zero, v_pair)], axis=0)
                o_intra = _dot(qk_ref[tok[d], p * LANES:(p + 1) * LANES], v_bd)
                for jh in range(2):
                    h = 2 * p + jh
                    o_ref[tok[d], hl(h)] = (inter[d * GDN_HEADS + h]
                                            + o_intra[:, jh * LANES:(jh + 1) * LANES])

    @pl.when(n == nsteps - 1)
    def _():
        sfin_ref[...] = s_ref[...]


def _scan(gl, s0, u, w, qd, kd, qk):
    n = u.shape[1]
    tok = SC_CH * CHUNK
    nsteps = n // tok
    fwd = lambda wd: pl.BlockSpec((None, tok, wd), lambda i: (0, i, 0))
    bwd = lambda wd: pl.BlockSpec((None, tok, wd), lambda i: (1, nsteps - 1 - i, 0))
    qkw = GDN_HEADS * CHUNK
    state_spec = pl.BlockSpec((2, GDN_HEADS, GDN_DK, GDN_DV), lambda i: (0, 0, 0, 0))
    return pl.pallas_call(
        _scan_kernel,
        grid=(nsteps,),
        in_specs=[pl.BlockSpec(memory_space=pltpu.SMEM), state_spec,
                  fwd(GDN_W), fwd(GDN_W), fwd(GDN_W), fwd(GDN_W), fwd(qkw),
                  bwd(GDN_W), bwd(GDN_W), bwd(GDN_W), bwd(GDN_W), bwd(qkw)],
        out_specs=[pl.BlockSpec((tok, GDN_W), lambda i: (i, 0)),
                   pl.BlockSpec((tok, GDN_W), lambda i: (nsteps - 1 - i, 0)),
                   state_spec],
        out_shape=[jax.ShapeDtypeStruct((n, GDN_W), F32),
                   jax.ShapeDtypeStruct((n, GDN_W), F32),
                   jax.ShapeDtypeStruct((2, GDN_HEADS, GDN_DK, GDN_DV), F32)],
        scratch_shapes=[pltpu.VMEM((2, GDN_HEADS, GDN_DK, GDN_DV), F32)],
        compiler_params=pltpu.CompilerParams(dimension_semantics=("arbitrary",)),
        name="scan",
    )(gl, s0, u, w, qd, kd, qk, u, w, qd, kd, qk)


def _outproj_kernel(x_ref, na_ref, of_ref, ob_ref, gz_ref, gnw_ref, wout_ref, gpost_ref, mod_ref,
                    o_ref):
    og = of_ref[...] + ob_ref[...]
    gz = gz_ref[...].astype(F32)
    gnw = gnw_ref[...]
    parts = []
    for h in range(GDN_HEADS):
        seg = og[:, h * LANES:(h + 1) * LANES]
        seg = seg * lax.rsqrt(jnp.mean(seg * seg, axis=-1, keepdims=True) + EPS)
        parts.append(seg * gnw)
    gd = (jnp.concatenate(parts, axis=-1) * _silu(gz)).astype(BF16)
    y = _dot(na_ref[...], wout_ref[0:NA_W, :]) + _dot(gd, wout_ref[NA_W:NA_W + GDN_W, :])
    yn = y * lax.rsqrt(jnp.mean(y * y, axis=-1, keepdims=True) + EPS)
    gate = mod_ref[0:1, 2 * D_MODEL:3 * D_MODEL]
    o_ref[...] = x_ref[...] + gate * (yn * gpost_ref[...])


def _outproj(x2, na, o_f, o_b, proj, gnw, w_out, gpost, mod, tm=512):
    n = x2.shape[0]
    gz_col = (PROJ_W - GDN_W) // GDN_W
    return pl.pallas_call(
        _outproj_kernel,
        grid=(n // tm,),
        in_specs=[pl.BlockSpec((tm, D_MODEL), lambda i: (i, 0)),
                  pl.BlockSpec((tm, NA_W), lambda i: (i, 0)),
                  pl.BlockSpec((tm, GDN_W), lambda i: (i, 0)),
                  pl.BlockSpec((tm, GDN_W), lambda i: (i, 0)),
                  pl.BlockSpec((tm, GDN_W), lambda i: (i, gz_col)),
                  pl.BlockSpec((1, LANES), lambda i: (0, 0)),
                  pl.BlockSpec((NA_W + GDN_W, D_MODEL), lambda i: (0, 0)),
                  pl.BlockSpec((1, D_MODEL), lambda i: (0, 0)),
                  pl.BlockSpec((8, 3 * D_MODEL), lambda i: (0, 0))],
        out_specs=pl.BlockSpec((tm, D_MODEL), lambda i: (i, 0)),
        out_shape=jax.ShapeDtypeStruct((n, D_MODEL), F32),
        compiler_params=pltpu.CompilerParams(
            dimension_semantics=("arbitrary",), vmem_limit_bytes=VMEM_LIMIT),
        name="outproj",
    )(x2, na, o_f, o_b, proj, gnw, w_out, gpost, mod)


def _rope_tables(rows, identity):
    inv_freq = ROPE_BASE ** (-jnp.arange(0, ROPE_AXIS_DIM, 2, dtype=F32) / ROPE_AXIS_DIM)

    def tables(count):
        ang = jnp.arange(count, dtype=F32)[:, None] * inv_freq[None, :]
        if identity:
            ang = jnp.zeros_like(ang)
        c, s = jnp.cos(ang), jnp.sin(ang)
        z = jnp.zeros_like(s)
        return (jnp.concatenate([c, c], -1), jnp.concatenate([-s, z], -1),
                jnp.concatenate([z, s], -1))

    half = jnp.zeros((rows, LANES // 2), F32)
    rowtab = jnp.stack([jnp.concatenate([t, half], -1) for t in tables(rows)], axis=0)
    rowtab = rowtab.reshape(3, rows // GP_CH, GP_CH, LANES).transpose(1, 0, 2, 3)
    rowtab = rowtab.reshape(rows // GP_CH, 3 * GP_CH, LANES)
    rowtab = jnp.pad(rowtab, ((0, 0), (0, 16 - 3 * GP_CH), (0, 0)))
    halfc = jnp.zeros((GRID_W, LANES // 2), F32)
    coltab = jnp.stack([jnp.concatenate([halfc, t], -1) for t in tables(GRID_W)], axis=0)
    return rowtab, coltab


def _lane_row(vals, offset):
    return jnp.zeros((1, LANES), F32).at[0, offset:offset + vals.shape[0]].set(vals)


def kernel(x, c, ctx, c_ctx, w_ada, b_ada, g_pre, g_post, w_in, conv_w, rpb, A_log, dt_bias,
           gdn_norm_w, w_out):
    n = x.shape[1]
    nc = ctx.shape[1]
    x2 = x[0]
    xc2 = ctx[0]

    cc = jnp.zeros((8, D_MODEL), F32).at[0].set(c[0]).at[1].set(c_ctx)
    mod = _ada(cc, w_ada[0], b_ada[0][None, :])

    w_main = w_in[0][:, :PROJ_W].astype(BF16)
    w_gate = jnp.pad(w_in[0][:, PROJ_W:], ((0, 0), (0, LANES - GATE_W))).astype(BF16)
    gpre = g_pre[0][None, :]
    proj, gates = _inproj(x2, mod, gpre, w_main, w_gate, row=0, tm=512)
    projc, gatesc = _inproj(xc2, mod, gpre, w_main, w_gate, row=1, tm=nc)

    na = _natten(proj, projc, rpb[0].reshape(-1))

    cw = jnp.pad(conv_w[0], ((0, 8 - CONV_K), (0, 0)))
    alog = _lane_row(A_log[0].reshape(-1), 2 * GDN_HEADS)
    dtb = _lane_row(dt_bias[0].reshape(-1), 2 * GDN_HEADS)
    rowtab, coltab = _rope_tables(n // GRID_W, identity=False)
    rowtab_c, coltab_c = _rope_tables(nc // GRID_W, identity=True)

    def gl_rows(gl):
        return jnp.stack([gl[:, 0, 2 * GDN_HEADS:3 * GDN_HEADS],
                          gl[:, 1, 3 * GDN_HEADS:4 * GDN_HEADS]], axis=0).reshape(-1)

    uc, wc, qdc, kdc, qkc, glc = _gdnprep(projc, gatesc, cw, alog, dtb, rowtab_c, coltab_c)
    s0 = jnp.zeros((2, GDN_HEADS, GDN_DK, GDN_DV), F32)
    _, _, s_ctx = _scan(gl_rows(glc), s0, uc, wc, qdc, kdc, qkc)

    u, w, qd, kd, qk, gl = _gdnprep(proj, gates, cw, alog, dtb, rowtab, coltab)
    o_f, o_b, _ = _scan(gl_rows(gl), s_ctx, u, w, qd, kd, qk)

    gnw = gdn_norm_w[0][None, :]
    out = _outproj(x2, na, o_f, o_b, proj, gnw, w_out[0].astype(BF16), g_post[0][None, :], mod)
    return out[None]
```

```python
import functools
import math

import jax
import jax.numpy as jnp
from jax import lax
from jax.experimental import pallas as pl
from jax.experimental.pallas import tpu as pltpu

F32 = jnp.float32
BF16 = jnp.bfloat16

D_MODEL = 1024
GRID_W = 64
NA_HEADS = 8
NA_DH = 64
NA_W = NA_HEADS * NA_DH
NA_KH = 8
NA_KW = 16
GDN_HEADS = 4
GDN_DK = 128
GDN_DV = 128
GDN_W = GDN_HEADS * GDN_DV
CHUNK = 64
CONV_K = 5
ROPE_AXIS_DIM = GDN_DK // 2
ROPE_BASE = 10000.0
EPS = 1e-6
PROJ_W = 4 * NA_W + 3 * GDN_W + GDN_W
GATE_W = 2 * 2 * GDN_HEADS
LANES = 128
NEG = -1e30

VMEM_LIMIT = 56 * 1024 * 1024


def _silu(x):
    h = 0.5 * x
    return h + h * jnp.tanh(h)


def _dot(a, b):
    return jnp.dot(a, b, preferred_element_type=F32)


def _dot_nt(a, b):
    return lax.dot_general(a, b, (((1,), (1,)), ((), ())), preferred_element_type=F32)


def _dot_tn(a, b):
    return lax.dot_general(a, b, (((0,), (0,)), ((), ())), preferred_element_type=F32)


def _split2(x):
    hi = x.astype(BF16)
    lo = (x - hi.astype(F32)).astype(BF16)
    return hi, lo


def _split3(x):
    hi = x.astype(BF16)
    r = x - hi.astype(F32)
    mid = r.astype(BF16)
    lo = (r - mid.astype(F32)).astype(BF16)
    return hi, mid, lo


def _mm3(a, b):
    ah, al = _split2(a)
    bh, bl = _split2(b)
    return _dot(ah, bh) + (_dot(al, bh) + _dot(ah, bl))


def _ada_kernel(c_ref, w_ref, b_ref, o_ref):
    s = _silu(c_ref[...])
    o_ref[...] = _mm3(s, w_ref[...]) + b_ref[...]


def _ada(cc, w_ada, b_ada):
    tn = 512
    n = w_ada.shape[1]
    return pl.pallas_call(
        _ada_kernel,
        grid=(n // tn,),
        in_specs=[pl.BlockSpec((8, D_MODEL), lambda j: (0, 0)),
                  pl.BlockSpec((D_MODEL, tn), lambda j: (0, j)),
                  pl.BlockSpec((1, tn), lambda j: (0, j))],
        out_specs=pl.BlockSpec((8, tn), lambda j: (0, j)),
        out_shape=jax.ShapeDtypeStruct((8, n), F32),
        name="ada",
    )(cc, w_ada, b_ada)


def _inproj_kernel(x_ref, mod_ref, gpre_ref, win_ref, proj_ref, gates_ref, w_ref, *, row):
    @pl.when(pl.program_id(0) == 0)
    def _():
        nb = 512
        for j in range(PROJ_W // nb):
            w_ref[:, j * nb:(j + 1) * nb] = win_ref[:, j * nb:(j + 1) * nb].astype(BF16)
        w_ref[:, PROJ_W:PROJ_W + LANES] = jnp.zeros((D_MODEL, LANES), BF16)
        w_ref[:, PROJ_W:PROJ_W + GATE_W] = win_ref[:, PROJ_W:PROJ_W + GATE_W].astype(BF16)

    x = x_ref[...]
    xn = x * lax.rsqrt(jnp.mean(x * x, axis=-1, keepdims=True) + EPS)
    shift = mod_ref[row:row + 1, 0:D_MODEL]
    scale = mod_ref[row:row + 1, D_MODEL:2 * D_MODEL]
    h = (xn * gpre_ref[...]) * (1.0 + scale) + shift
    hb = h.astype(BF16)
    nb = 512
    for j in range(PROJ_W // nb):
        proj_ref[:, j * nb:(j + 1) * nb] = _dot(hb, w_ref[:, j * nb:(j + 1) * nb]).astype(BF16)
    gates_ref[...] = _dot(hb, w_ref[:, PROJ_W:PROJ_W + LANES])


def _inproj(x2, mod, gpre, w_in, row, tm):
    n = x2.shape[0]
    return pl.pallas_call(
        functools.partial(_inproj_kernel, row=row),
        grid=(n // tm,),
        in_specs=[pl.BlockSpec((tm, D_MODEL), lambda i: (i, 0)),
                  pl.BlockSpec((8, 3 * D_MODEL), lambda i: (0, 0)),
                  pl.BlockSpec((1, D_MODEL), lambda i: (0, 0)),
                  pl.BlockSpec((D_MODEL, PROJ_W + GATE_W), lambda i: (0, 0),
                               pipeline_mode=pl.Buffered(1))],
        out_specs=[pl.BlockSpec((tm, PROJ_W), lambda i: (i, 0)),
                   pl.BlockSpec((tm, LANES), lambda i: (i, 0))],
        out_shape=[jax.ShapeDtypeStruct((n, PROJ_W), BF16),
                   jax.ShapeDtypeStruct((n, LANES), F32)],
        scratch_shapes=[pltpu.VMEM((D_MODEL, PROJ_W + LANES), BF16)],
        compiler_params=pltpu.CompilerParams(
            dimension_semantics=("arbitrary",), vmem_limit_bytes=VMEM_LIMIT),
        name="inproj",
    )(x2, mod, gpre, w_in)


NA_RO = 2 * NA_KH - 1
NA_CO = 2 * NA_KW - 1
NA_ROWS_PER_ITER = 4


def _na_build_tables(rpb_ref, tab_ref, hp):
    qc = lax.broadcasted_iota(jnp.int32, (GRID_W, LANES), 0)
    kcol = lax.broadcasted_iota(jnp.int32, (GRID_W, LANES), 1) % GRID_W
    lo = lax.broadcasted_iota(jnp.int32, (1, LANES), 1) < GRID_W
    diff = kcol - qc + (NA_KW - 1)
    col_start = jnp.clip(qc - NA_KW // 2, 0, GRID_W - NA_KW)
    inwin = (kcol >= col_start) & (kcol < col_start + NA_KW)
    for hh in range(2):
        def ro_body(ro, carry):
            base = ((hp * 2 + hh) * NA_RO + ro) * NA_CO
            acc = jnp.zeros((GRID_W, LANES), F32)
            for j in range(NA_CO):
                val = jnp.where(lo, rpb_ref[base + j], rpb_ref[base + NA_CO + j])
                acc = jnp.where(diff == j, val, acc)
            tab_ref[hh, ro] = jnp.where(inwin, acc, NEG)
            return carry
        lax.fori_loop(0, NA_RO - 1, ro_body, 0)


def _na_kernel(rpb_ref, q_ref, k_ref, v_ref, z_ref, kc_ref, vc_ref, o_ref, tab_ref,
               *, rb_rows, rows):
    hp = pl.program_id(0)
    rb = pl.program_id(1)

    @pl.when(rb == 0)
    def _():
        _na_build_tables(rpb_ref, tab_ref, hp)

    lane = lax.broadcasted_iota(jnp.int32, (1, LANES), 1)
    first = lane < NA_DH
    kc = kc_ref[...]
    vc = vc_ref[...]
    win = NA_KH * GRID_W

    def body(it, carry):
        items = []
        for rr in range(NA_ROWS_PER_ITER):
            i = it * NA_ROWS_PER_ITER + rr
            r = rb * rb_rows + i
            rs = jnp.clip(r - NA_KH // 2, 0, rows - NA_KH)
            ro0 = NA_KH - 1 - (r - rs)
            t0 = pl.multiple_of(i * GRID_W, GRID_W)
            q = q_ref[pl.ds(t0, GRID_W), :] * jnp.asarray(NA_DH ** -0.5, BF16)
            k0 = pl.multiple_of(rs * GRID_W, GRID_W)
            kw = k_ref[pl.ds(k0, win), :]
            vw = v_ref[pl.ds(k0, win), :]
            zq = jnp.zeros_like(q)
            qs = jnp.concatenate([jnp.where(first, q, zq), jnp.where(first, zq, q)], axis=0)
            items.append((t0, ro0, qs, kw, vw))
        s_loc = [_dot_nt(qs, kw)
                 + jnp.concatenate(
                     [jnp.concatenate([tab_ref[hh, ro0 + 2 * m] for m in range(NA_KH // 2)], axis=1)
                      for hh in range(2)], axis=0)
                 for (_, ro0, qs, kw, _) in items]
        s_ctx = [_dot_nt(qs, kc) for (_, _, qs, _, _) in items]
        mx = [jnp.maximum(jnp.max(a, axis=-1, keepdims=True), jnp.max(b, axis=-1, keepdims=True))
              for a, b in zip(s_loc, s_ctx)]
        p_loc = [jnp.exp(a - m) for a, m in zip(s_loc, mx)]
        p_ctx = [jnp.exp(b - m) for b, m in zip(s_ctx, mx)]
        inv = [1.0 / (jnp.sum(a, axis=-1, keepdims=True) + jnp.sum(b, axis=-1, keepdims=True))
               for a, b in zip(p_loc, p_ctx)]
        outs = [(_dot(a.astype(BF16), it_[4]) + _dot(b.astype(BF16), vc)) * il
                for a, b, il, it_ in zip(p_loc, p_ctx, inv, items)]
        for (t0, _, _, _, _), o2 in zip(items, outs):
            o = jnp.where(first, o2[0:GRID_W], o2[GRID_W:2 * GRID_W])
            z = z_ref[pl.ds(t0, GRID_W), :].astype(F32)
            o_ref[pl.ds(t0, GRID_W), :] = (o * _silu(z)).astype(BF16)
        return carry

    lax.fori_loop(0, rb_rows // NA_ROWS_PER_ITER, body, 0)


def _natten(proj, projc, rpb_flat, rb_rows=16):
    n = proj.shape[0]
    nc = projc.shape[0]
    rows = n // GRID_W
    tq = rb_rows * GRID_W
    kcol = NA_W // LANES
    return pl.pallas_call(
        functools.partial(_na_kernel, rb_rows=rb_rows, rows=rows),
        grid=(NA_W // LANES, rows // rb_rows),
        in_specs=[pl.BlockSpec(memory_space=pltpu.SMEM),
                  pl.BlockSpec((tq, LANES), lambda hp, rb: (rb, hp)),
                  pl.BlockSpec((n, LANES), lambda hp, rb: (0, kcol + hp)),
                  pl.BlockSpec((n, LANES), lambda hp, rb: (0, 2 * kcol + hp)),
                  pl.BlockSpec((tq, LANES), lambda hp, rb: (rb, 3 * kcol + hp)),
                  pl.BlockSpec((nc, LANES), lambda hp, rb: (0, kcol + hp)),
                  pl.BlockSpec((nc, LANES), lambda hp, rb: (0, 2 * kcol + hp))],
        out_specs=pl.BlockSpec((tq, LANES), lambda hp, rb: (rb, hp)),
        out_shape=jax.ShapeDtypeStruct((n, NA_W), BF16),
        scratch_shapes=[pltpu.VMEM((2, NA_RO - 1, GRID_W, LANES), F32)],
        compiler_params=pltpu.CompilerParams(
            dimension_semantics=("arbitrary", "arbitrary"), vmem_limit_bytes=VMEM_LIMIT),
        name="natten",
    )(rpb_flat, proj, proj, proj, proj, projc, projc)


GP_TOK = 256
GP_CH = GP_TOK // CHUNK
GP_LOCK = 4
INV_BASE = 8
HALO = 16
CONV_ROWS = 128
XE_ROWS = GP_TOK + CONV_ROWS


def _gdnprep_kernel(main_ref, left_ref, right_ref, gates_ref, cw_ref, alog_ref, dtb_ref,
                    rowtab_ref, coltab_ref,
                    u_ref, w_ref, qd_ref, kd_ref, qk_ref, gl_ref,
                    xe_ref, shift_ref, qn_ref, kn_ref, vv_ref, beta_ref, gc_ref, gct_ref, eg_ref,
                    ek_ref):
    i = pl.program_id(0)
    last = pl.num_programs(0) - 1
    qkv_w = 3 * GDN_W

    @pl.when(i == 0)
    def _():
        t = lax.broadcasted_iota(jnp.int32, (CONV_K * CONV_ROWS, 2 * CONV_ROWS), 0)
        r = lax.broadcasted_iota(jnp.int32, (CONV_K * CONV_ROWS, 2 * CONV_ROWS), 1)
        hit = r == (t % CONV_ROWS) + (t // CONV_ROWS) + (HALO - CONV_K // 2)
        shift_ref[...] = jnp.where(hit, 1.0, 0.0).astype(BF16)
        xe_ref[HALO + GP_TOK + HALO:, :] = jnp.zeros(
            (XE_ROWS - GP_TOK - 2 * HALO, qkv_w), BF16)

    lh = left_ref[:, 0:qkv_w]
    rh = right_ref[:, 0:qkv_w]
    xe_ref[0:HALO, :] = jnp.where(i > 0, lh, jnp.zeros_like(lh))
    xe_ref[HALO:HALO + GP_TOK, :] = main_ref[:, 0:qkv_w]
    xe_ref[HALO + GP_TOK:HALO + GP_TOK + HALO, :] = jnp.where(i < last, rh, jnp.zeros_like(rh))

    row_lanes = lax.broadcasted_iota(jnp.int32, (1, LANES), 1) < ROPE_AXIS_DIM

    def rope_table(kind):
        return jnp.concatenate(
            [jnp.where(row_lanes, rowtab_ref[kind * GP_CH + c:kind * GP_CH + c + 1, :],
                       coltab_ref[kind]) for c in range(GP_CH)], axis=0)

    cos, sina, sinb = rope_table(0), rope_table(1), rope_table(2)
    pending = {}
    for cb, b in [(cb, b) for cb in range(qkv_w // LANES) for b in range(GP_TOK // CONV_ROWS)]:
        cols = slice(cb * LANES, (cb + 1) * LANES)
        rows = slice(b * CONV_ROWS, (b + 1) * CONV_ROWS)
        if cb % 2 == 0:
            cols2 = slice(cb * LANES, (cb + 2) * LANES)
            taps2 = _dot(shift_ref[...], xe_ref[b * CONV_ROWS:(b + 2) * CONV_ROWS, cols2])
            acc2 = None
            for j in range(CONV_K):
                term = cw_ref[j:j + 1, cols2] * taps2[j * CONV_ROWS:(j + 1) * CONV_ROWS]
                acc2 = term if acc2 is None else acc2 + term
            pending[b] = _silu(acc2)
        y = pending[b][:, (cb % 2) * LANES:(cb % 2 + 1) * LANES]
        if cb < 2 * GDN_HEADS:
            y = y * lax.rsqrt(jnp.sum(y * y, axis=-1, keepdims=True) + EPS)
            y = (y * cos[rows] + pltpu.roll(y, LANES - ROPE_AXIS_DIM // 2, 1) * sina[rows]
                 + pltpu.roll(y, ROPE_AXIS_DIM // 2, 1) * sinb[rows])
            if cb < GDN_HEADS:
                qn_ref[rows, cols] = y * (GDN_DK ** -0.5)
            else:
                kn_ref[rows, (cb - GDN_HEADS) * LANES:(cb - GDN_HEADS + 1) * LANES] = y
        else:
            vv_ref[rows, (cb - 2 * GDN_HEADS) * LANES:(cb - 2 * GDN_HEADS + 1) * LANES] = y

    gates = gates_ref[...]
    beta_ref[...] = jax.nn.sigmoid(gates)
    xa = gates + dtb_ref[...]
    softplus = jnp.maximum(xa, 0.0) + jnp.log1p(jnp.exp(-jnp.abs(xa)))
    g = -jnp.exp(alog_ref[...]) * softplus

    ti = lax.broadcasted_iota(jnp.int32, (GP_TOK, GP_TOK), 0)
    tj = lax.broadcasted_iota(jnp.int32, (GP_TOK, GP_TOK), 1)
    same = (ti // CHUNK) == (tj // CHUNK)
    lower = jnp.where(same & (ti >= tj), 1.0, 0.0).astype(BF16)
    upper = jnp.where(same & (ti <= tj), 1.0, 0.0).astype(BF16)
    g3 = _split3(g)
    gc_f = _dot(lower, g3[0]) + (_dot(lower, g3[1]) + _dot(lower, g3[2]))
    gc_b = _dot(upper, g3[0]) + (_dot(upper, g3[1]) + _dot(upper, g3[2]))
    gc_ref[0] = gc_f
    gc_ref[1] = gc_b
    gl_ref[...] = jnp.zeros_like(gl_ref)
    for c in range(GP_CH):
        rows = slice(c * CHUNK, (c + 1) * CHUNK)
        for d, gc in enumerate((gc_f, gc_b)):
            blk = gc[rows]
            gct_ref[c, d] = jnp.concatenate([blk, pltpu.roll(blk, LANES - 1, 1)], axis=0).T
        gend_f = gc_f[(c + 1) * CHUNK - 1:(c + 1) * CHUNK, :]
        gend_b = gc_b[c * CHUNK:c * CHUNK + 1, :]
        gl_ref[c, 0:1, :] = jnp.exp(gend_f)
        gl_ref[c, 1:2, :] = jnp.exp(gend_b)
        ek_ref[0, rows] = jnp.exp(gend_f - gc_f[rows])
        ek_ref[1, rows] = jnp.exp(gend_b - gc_b[rows])
    eg_ref[0] = jnp.exp(gc_f)
    eg_ref[1] = jnp.exp(gc_b)

    ii = lax.broadcasted_iota(jnp.int32, (CHUNK, LANES), 0)
    jj = lax.broadcasted_iota(jnp.int32, (CHUNK, LANES), 1) % CHUNK
    lo = lax.broadcasted_iota(jnp.int32, (1, LANES), 1) < CHUNK
    lo_wide = lax.broadcasted_iota(jnp.int32, (1, 2 * LANES), 1) < LANES
    eye = jnp.where(ii == jj, 1.0, 0.0).astype(F32)
    bodies = [(d, p) for d in range(2) for p in range(GDN_HEADS // 2)]
    same_block = lambda size: (ii // size) == (jj // size)
    base_blocks = same_block(INV_BASE)
    merge_masks = []
    size = INV_BASE
    while size < CHUNK:
        merge_masks.append(same_block(2 * size) & jnp.logical_not(same_block(size)))
        size *= 2

    def blockdiag(y, first):
        z = jnp.zeros_like(y)
        return jnp.concatenate([jnp.where(first, y, z), jnp.where(first, z, y)], axis=0)

    def pair_mm(x16, ybd16):
        return _dot(x16, ybd16)

    def chunk_body(cg, carry):
        a_l, kbeg_l, vb_l, where_l = [], [], [], []
        for cc, (d, p) in [(cc, b) for cc in range(GP_LOCK) for b in bodies]:
            c = cg * GP_LOCK + cc
            c0 = pl.multiple_of(c * CHUNK, CHUNK)
            tok = pl.ds(c0, CHUNK)
            incl = (ii >= jj) if d == 0 else (ii <= jj)
            strict = (ii > jj) if d == 0 else (ii < jj)
            pl2 = slice(2 * p * LANES, (2 * p + 2) * LANES)
            lb = d * GDN_HEADS + 2 * p
            lg = 2 * GDN_HEADS + lb
            q = qn_ref[tok, pl2]
            k = kn_ref[tok, pl2]
            v = vv_ref[tok, pl2]
            beta = jnp.where(lo_wide, beta_ref[tok, lb:lb + 1], beta_ref[tok, lb + 1:lb + 2])
            g0 = gc_ref[d, tok, lg:lg + 1]
            g1 = gc_ref[d, tok, lg + 1:lg + 2]
            gcol = jnp.where(lo, g0, g1)
            grow = gct_ref[c, d, lg:lg + 1, :]
            eg = jnp.where(lo_wide, eg_ref[d, tok, lg:lg + 1], eg_ref[d, tok, lg + 1:lg + 2])
            ek = jnp.where(lo_wide, ek_ref[d, tok, lg:lg + 1], ek_ref[d, tok, lg + 1:lg + 2])
            dec = jnp.exp(jnp.where(incl, gcol - grow, NEG))
            kb = k * beta
            k_nt = blockdiag(k.astype(BF16), lo_wide)
            a = jnp.where(strict, _dot_nt(kb.astype(BF16), k_nt) * dec, 0.0)
            qk = _dot_nt(q.astype(BF16), k_nt) * dec
            qk_ref[d, tok, p * LANES:(p + 1) * LANES] = qk.astype(BF16)
            qd_ref[d, tok, pl2] = (q * eg).astype(BF16)
            kd_ref[d, tok, pl2] = (k * ek).astype(BF16)
            a_l.append(a)
            kbeg_l.append((kb * eg).astype(BF16))
            vb_l.append((v * beta).astype(BF16))
            where_l.append((d, tok, pl2))
        x_l = [jnp.where(base_blocks, -a, 0.0) for a in a_l]
        t_l = [eye + x for x in x_l]
        x_l = [x.astype(BF16) for x in x_l]
        x_l = [pair_mm(x, blockdiag(x, lo)).astype(BF16) for x in x_l]
        r_l = [pair_mm(jnp.concatenate([x, t.astype(BF16)], axis=0), blockdiag(x, lo))
               for x, t in zip(x_l, t_l)]
        x_l = [r[0:CHUNK].astype(BF16) for r in r_l]
        t_l = [t + r[CHUNK:2 * CHUNK] for t, r in zip(t_l, r_l)]
        t_l = [t + pair_mm(t.astype(BF16), blockdiag(x, lo)) for t, x in zip(t_l, x_l)]
        for merged in merge_masks:
            t16_l = [t.astype(BF16) for t in t_l]
            te_l = [pair_mm(t16, blockdiag(jnp.where(merged, a, 0.0).astype(BF16), lo))
                    for t16, a in zip(t16_l, a_l)]
            t_l = [t - pair_mm(te.astype(BF16), blockdiag(t16, lo))
                   for t, te, t16 in zip(t_l, te_l, t16_l)]
        for (d, tok, pl2), t, vb, kbeg in zip(where_l, t_l, vb_l, kbeg_l):
            t16 = t.astype(BF16)
            u_ref[d, tok, pl2] = _dot(t16, blockdiag(vb, lo_wide))
            w_ref[d, tok, pl2] = _dot(t16, blockdiag(kbeg, lo_wide)).astype(BF16)
        return carry

    if GP_LOCK == GP_CH:
        chunk_body(0, 0)
    else:
        lax.fori_loop(0, GP_CH // GP_LOCK, chunk_body, 0)


def _gdnprep(proj, gates, conv_w, alog, dtb, rowtab, coltab):
    n = proj.shape[0]
    nt = n // GP_TOK
    hb = GP_TOK // HALO
    nhalo = n // HALO
    qkv_w = 3 * GDN_W
    tok_spec = lambda w: pl.BlockSpec((GP_TOK, w), lambda i: (i, 0))
    dir_spec = lambda w: pl.BlockSpec((2, GP_TOK, w), lambda i: (0, i, 0))
    return pl.pallas_call(
        _gdnprep_kernel,
        grid=(nt,),
        in_specs=[pl.BlockSpec((GP_TOK, PROJ_W // 2), lambda i: (i, 1)),
                  pl.BlockSpec((HALO, PROJ_W // 2), lambda i: (jnp.maximum(i * hb - 1, 0), 1)),
                  pl.BlockSpec((HALO, PROJ_W // 2),
                               lambda i: (jnp.minimum((i + 1) * hb, nhalo - 1), 1)),
                  tok_spec(LANES),
                  pl.BlockSpec((8, qkv_w), lambda i: (0, 0)),
                  pl.BlockSpec((1, LANES), lambda i: (0, 0)),
                  pl.BlockSpec((1, LANES), lambda i: (0, 0)),
                  pl.BlockSpec((None, 16, LANES), lambda i: (i, 0, 0)),
                  pl.BlockSpec((3, GRID_W, LANES), lambda i: (0, 0, 0))],
        out_specs=[dir_spec(GDN_W), dir_spec(GDN_W), dir_spec(GDN_W), dir_spec(GDN_W),
                   dir_spec(GDN_HEADS * CHUNK),
                   pl.BlockSpec((GP_CH, 8, LANES), lambda i: (i, 0, 0))],
        out_shape=[jax.ShapeDtypeStruct((2, n, GDN_W), F32),
                   jax.ShapeDtypeStruct((2, n, GDN_W), BF16),
                   jax.ShapeDtypeStruct((2, n, GDN_W), BF16),
                   jax.ShapeDtypeStruct((2, n, GDN_W), BF16),
                   jax.ShapeDtypeStruct((2, n, GDN_HEADS * CHUNK), BF16),
                   jax.ShapeDtypeStruct((n // CHUNK, 8, LANES), F32)],
        scratch_shapes=[pltpu.VMEM((XE_ROWS, qkv_w), BF16),
                        pltpu.VMEM((CONV_K * CONV_ROWS, 2 * CONV_ROWS), BF16),
                        pltpu.VMEM((GP_TOK, GDN_W), F32),
                        pltpu.VMEM((GP_TOK, GDN_W), F32),
                        pltpu.VMEM((GP_TOK, GDN_W), F32),
                        pltpu.VMEM((GP_TOK, LANES), F32),
                        pltpu.VMEM((2, GP_TOK, LANES), F32),
                        pltpu.VMEM((GP_CH, 2, LANES, LANES), F32),
                        pltpu.VMEM((2, GP_TOK, LANES), F32),
                        pltpu.VMEM((2, GP_TOK, LANES), F32)],
        compiler_params=pltpu.CompilerParams(
            dimension_semantics=("arbitrary",), vmem_limit_bytes=VMEM_LIMIT),
        name="gdnprep",
    )(proj, proj, proj, gates, conv_w, alog, dtb, rowtab, coltab)


SC_CH = 4


def _scan_kernel(gl_ref, s0_ref, uf_ref, wf_ref, qdf_ref, kdf_ref, qkf_ref,
                 ub_ref, wb_ref, qdb_ref, kdb_ref, qkb_ref,
                 of_ref, ob_ref, sfin_ref, s_ref):
    n = pl.program_id(0)
    nsteps = pl.num_programs(0)
    nch = nsteps * SC_CH

    @pl.when(n == 0)
    def _():
        s_ref[...] = s0_ref[...]

    streams = ((uf_ref, wf_ref, qdf_ref, kdf_ref, qkf_ref, of_ref),
               (ub_ref, wb_ref, qdb_ref, kdb_ref, qkb_ref, ob_ref))
    lo_wide = lax.broadcasted_iota(jnp.int32, (1, 2 * LANES), 1) < LANES
    chains = [(d, h) for d in range(2) for h in range(GDN_HEADS)]
    hl = lambda h: slice(h * LANES, (h + 1) * LANES)
    for j in range(SC_CH):
        local = (j, SC_CH - 1 - j)
        chunk = (n * SC_CH + j, (nsteps - 1 - n) * SC_CH + SC_CH - 1 - j)
        tok = [slice(c * CHUNK, (c + 1) * CHUNK) for c in local]
        s32 = [s_ref[d, h] for d, h in chains]
        s16 = [s.astype(BF16) for s in s32]
        wq = [_dot(jnp.concatenate([streams[d][1][tok[d], hl(h)], streams[d][2][tok[d], hl(h)]],
                                   axis=0), s)
              for (d, h), s in zip(chains, s16)]
        v16 = [(streams[d][0][tok[d], hl(h)] - x[0:CHUNK]).astype(BF16)
               for (d, h), x in zip(chains, wq)]
        upd = [_dot_tn(streams[d][3][tok[d], hl(h)], v) for (d, h), v in zip(chains, v16)]
        for (d, h), s, x in zip(chains, s32, upd):
            decay = gl_ref[(d * nch + chunk[d]) * GDN_HEADS + h]
            s_ref[d, h] = s * decay + x
        inter = [x[CHUNK:2 * CHUNK] for x in wq]
        for d in range(2):
            qk_ref, o_ref = streams[d][4], streams[d][5]
            for p in range(GDN_HEADS // 2):
                v_pair = jnp.concatenate(
                    v16[d * GDN_HEADS + 2 * p:d * GDN_HEADS + 2 * p + 2], axis=1)
                zero = jnp.zeros_like(v_pair)
                v_bd = jnp.concatenate([jnp.where(lo_wide, v_pair, zero),
                                        jnp.where(lo_wide, zero, v_pair)], axis=0)
                o_intra = _dot(qk_ref[tok[d], p * LANES:(p + 1) * LANES], v_bd)
                for jh in range(2):
                    h = 2 * p + jh
                    o_ref[tok[d], hl(h)] = (inter[d * GDN_HEADS + h]
                                            + o_intra[:, jh * LANES:(jh + 1) * LANES])

    @pl.when(n == nsteps - 1)
    def _():
        sfin_ref[...] = s_ref[...]


def _scan(gl, s0, u, w, qd, kd, qk):
    n = u.shape[1]
    tok = SC_CH * CHUNK
    nsteps = n // tok
    fwd = lambda wd: pl.BlockSpec((None, tok, wd), lambda i: (0, i, 0))
    bwd = lambda wd: pl.BlockSpec((None, tok, wd), lambda i: (1, nsteps - 1 - i, 0))
    qkw = GDN_HEADS * CHUNK
    state_spec = pl.BlockSpec((2, GDN_HEADS, GDN_DK, GDN_DV), lambda i: (0, 0, 0, 0))
    return pl.pallas_call(
        _scan_kernel,
        grid=(nsteps,),
        in_specs=[pl.BlockSpec(memory_space=pltpu.SMEM), state_spec,
                  fwd(GDN_W), fwd(GDN_W), fwd(GDN_W), fwd(GDN_W), fwd(qkw),
                  bwd(GDN_W), bwd(GDN_W), bwd(GDN_W), bwd(GDN_W), bwd(qkw)],
        out_specs=[pl.BlockSpec((tok, GDN_W), lambda i: (i, 0)),
                   pl.BlockSpec((tok, GDN_W), lambda i: (nsteps - 1 - i, 0)),
                   state_spec],
        out_shape=[jax.ShapeDtypeStruct((n, GDN_W), F32),
                   jax.ShapeDtypeStruct((n, GDN_W), F32),
                   jax.ShapeDtypeStruct((2, GDN_HEADS, GDN_DK, GDN_DV), F32)],
        scratch_shapes=[pltpu.VMEM((2, GDN_HEADS, GDN_DK, GDN_DV), F32)],
        compiler_params=pltpu.CompilerParams(dimension_semantics=("arbitrary",)),
        name="scan",
    )(gl, s0, u, w, qd, kd, qk, u, w, qd, kd, qk)


def _outproj_kernel(x_ref, na_ref, of_ref, ob_ref, gz_ref, gnw_ref, wout_ref, gpost_ref, mod_ref,
                    o_ref):
    og = of_ref[...] + ob_ref[...]
    gz = gz_ref[...].astype(F32)
    gnw = gnw_ref[...]
    parts = []
    for h in range(GDN_HEADS):
        seg = og[:, h * LANES:(h + 1) * LANES]
        seg = seg * lax.rsqrt(jnp.mean(seg * seg, axis=-1, keepdims=True) + EPS)
        parts.append(seg * gnw)
    gd = (jnp.concatenate(parts, axis=-1) * _silu(gz)).astype(BF16)
    y = _dot(na_ref[...], wout_ref[0:NA_W, :]) + _dot(gd, wout_ref[NA_W:NA_W + GDN_W, :])
    yn = y * lax.rsqrt(jnp.mean(y * y, axis=-1, keepdims=True) + EPS)
    gate = mod_ref[0:1, 2 * D_MODEL:3 * D_MODEL]
    o_ref[...] = x_ref[...] + gate * (yn * gpost_ref[...])


def _outproj(x2, na, o_f, o_b, proj, gnw, w_out, gpost, mod, tm=512):
    n = x2.shape[0]
    gz_col = (PROJ_W - GDN_W) // GDN_W
    return pl.pallas_call(
        _outproj_kernel,
        grid=(n // tm,),
        in_specs=[pl.BlockSpec((tm, D_MODEL), lambda i: (i, 0)),
                  pl.BlockSpec((tm, NA_W), lambda i: (i, 0)),
                  pl.BlockSpec((tm, GDN_W), lambda i: (i, 0)),
                  pl.BlockSpec((tm, GDN_W), lambda i: (i, 0)),
                  pl.BlockSpec((tm, GDN_W), lambda i: (i, gz_col)),
                  pl.BlockSpec((1, LANES), lambda i: (0, 0)),
                  pl.BlockSpec((NA_W + GDN_W, D_MODEL), lambda i: (0, 0)),
                  pl.BlockSpec((1, D_MODEL), lambda i: (0, 0)),
                  pl.BlockSpec((8, 3 * D_MODEL), lambda i: (0, 0))],
        out_specs=pl.BlockSpec((tm, D_MODEL), lambda i: (i, 0)),
        out_shape=jax.ShapeDtypeStruct((n, D_MODEL), F32),
        compiler_params=pltpu.CompilerParams(
            dimension_semantics=("arbitrary",), vmem_limit_bytes=VMEM_LIMIT),
        name="outproj",
    )(x2, na, o_f, o_b, proj, gnw, w_out, gpost, mod)


def _rope_tables(rows, identity):
    inv_freq = ROPE_BASE ** (-jnp.arange(0, ROPE_AXIS_DIM, 2, dtype=F32) / ROPE_AXIS_DIM)

    def tables(count):
        ang = jnp.arange(count, dtype=F32)[:, None] * inv_freq[None, :]
        if identity:
            ang = jnp.zeros_like(ang)
        c, s = jnp.cos(ang), jnp.sin(ang)
        z = jnp.zeros_like(s)
        return (jnp.concatenate([c, c], -1), jnp.concatenate([-s, z], -1),
                jnp.concatenate([z, s], -1))

    half = jnp.zeros((rows, LANES // 2), F32)
    rowtab = jnp.stack([jnp.concatenate([t, half], -1) for t in tables(rows)], axis=0)
    rowtab = rowtab.reshape(3, rows // GP_CH, GP_CH, LANES).transpose(1, 0, 2, 3)
    rowtab = rowtab.reshape(rows // GP_CH, 3 * GP_CH, LANES)
    rowtab = jnp.pad(rowtab, ((0, 0), (0, 16 - 3 * GP_CH), (0, 0)))
    halfc = jnp.zeros((GRID_W, LANES // 2), F32)
    coltab = jnp.stack([jnp.concatenate([halfc, t], -1) for t in tables(GRID_W)], axis=0)
    return rowtab, coltab


def _lane_row(vals, offset):
    return jnp.zeros((1, LANES), F32).at[0, offset:offset + vals.shape[0]].set(vals)


def kernel(x, c, ctx, c_ctx, w_ada, b_ada, g_pre, g_post, w_in, conv_w, rpb, A_log, dt_bias,
           gdn_norm_w, w_out):
    n = x.shape[1]
    nc = ctx.shape[1]
    x2 = x[0]
    xc2 = ctx[0]

    cc = jnp.zeros((8, D_MODEL), F32).at[0].set(c[0]).at[1].set(c_ctx)
    mod = _ada(cc, w_ada[0], b_ada[0][None, :])

    gpre = g_pre[0][None, :]
    proj, gates = _inproj(x2, mod, gpre, w_in[0], row=0, tm=512)
    projc, gatesc = _inproj(xc2, mod, gpre, w_in[0], row=1, tm=nc)

    na = _natten(proj, projc, rpb[0].reshape(-1))

    cw = jnp.pad(conv_w[0], ((0, 8 - CONV_K), (0, 0)))
    alog = _lane_row(A_log[0].reshape(-1), 2 * GDN_HEADS)
    dtb = _lane_row(dt_bias[0].reshape(-1), 2 * GDN_HEADS)
    rowtab, coltab = _rope_tables(n // GRID_W, identity=False)
    rowtab_c, coltab_c = _rope_tables(nc // GRID_W, identity=True)

    def gl_rows(gl):
        return jnp.stack([gl[:, 0, 2 * GDN_HEADS:3 * GDN_HEADS],
                          gl[:, 1, 3 * GDN_HEADS:4 * GDN_HEADS]], axis=0).reshape(-1)

    uc, wc, qdc, kdc, qkc, glc = _gdnprep(projc, gatesc, cw, alog, dtb, rowtab_c, coltab_c)
    s0 = jnp.zeros((2, GDN_HEADS, GDN_DK, GDN_DV), F32)
    _, _, s_ctx = _scan(gl_rows(glc), s0, uc, wc, qdc, kdc, qkc)

    u, w, qd, kd, qk, gl = _gdnprep(proj, gates, cw, alog, dtb, rowtab, coltab)
    o_f, o_b, _ = _scan(gl_rows(gl), s_ctx, u, w, qd, kd, qk)

    gnw = gdn_norm_w[0][None, :]
    out = _outproj(x2, na, o_f, o_b, proj, gnw, w_out[0].astype(BF16), g_post[0][None, :], mod)
    return out[None]
```

```python
import functools
import math

import jax
import jax.numpy as jnp
from jax import lax
from jax.experimental import pallas as pl
from jax.experimental.pallas import tpu as pltpu

F32 = jnp.float32
BF16 = jnp.bfloat16

D_MODEL = 1024
GRID_W = 64
NA_HEADS = 8
NA_DH = 64
NA_W = NA_HEADS * NA_DH
NA_KH = 8
NA_KW = 16
GDN_HEADS = 4
GDN_DK = 128
GDN_DV = 128
GDN_W = GDN_HEADS * GDN_DV
CHUNK = 64
CONV_K = 5
ROPE_AXIS_DIM = GDN_DK // 2
ROPE_BASE = 10000.0
EPS = 1e-6
PROJ_W = 4 * NA_W + 3 * GDN_W + GDN_W
GATE_W = 2 * 2 * GDN_HEADS
LANES = 128
NEG = -1e30

VMEM_LIMIT = 56 * 1024 * 1024


def _silu(x):
    h = 0.5 * x
    return h + h * jnp.tanh(h)


def _dot(a, b):
    return jnp.dot(a, b, preferred_element_type=F32)


def _dot_nt(a, b):
    return lax.dot_general(a, b, (((1,), (1,)), ((), ())), preferred_element_type=F32)


def _dot_tn(a, b):
    return lax.dot_general(a, b, (((0,), (0,)), ((), ())), preferred_element_type=F32)


def _split2(x):
    hi = x.astype(BF16)
    lo = (x - hi.astype(F32)).astype(BF16)
    return hi, lo


def _split3(x):
    hi = x.astype(BF16)
    r = x - hi.astype(F32)
    mid = r.astype(BF16)
    lo = (r - mid.astype(F32)).astype(BF16)
    return hi, mid, lo


def _mm3(a, b):
    ah, al = _split2(a)
    bh, bl = _split2(b)
    return _dot(ah, bh) + (_dot(al, bh) + _dot(ah, bl))


def _ada_kernel(c_ref, w_ref, b_ref, o_ref):
    s = _silu(c_ref[...])
    o_ref[...] = _mm3(s, w_ref[...]) + b_ref[...]


def _ada(cc, w_ada, b_ada):
    tn = 512
    n = w_ada.shape[1]
    return pl.pallas_call(
        _ada_kernel,
        grid=(n // tn,),
        in_specs=[pl.BlockSpec((8, D_MODEL), lambda j: (0, 0)),
                  pl.BlockSpec((D_MODEL, tn), lambda j: (0, j)),
                  pl.BlockSpec((1, tn), lambda j: (0, j))],
        out_specs=pl.BlockSpec((8, tn), lambda j: (0, j)),
        out_shape=jax.ShapeDtypeStruct((8, n), F32),
        name="ada",
    )(cc, w_ada, b_ada)


def _inproj_kernel(x_ref, mod_ref, gpre_ref, win_ref, proj_ref, gates_ref, w_ref, *, row):
    @pl.when(pl.program_id(0) == 0)
    def _():
        nb = 512
        for j in range(PROJ_W // nb):
            w_ref[j * nb:(j + 1) * nb, :] = win_ref[j * nb:(j + 1) * nb, :].astype(BF16)
        w_ref[PROJ_W:PROJ_W + LANES, :] = jnp.zeros((LANES, D_MODEL), BF16)
        w_ref[PROJ_W:PROJ_W + GATE_W, :] = win_ref[PROJ_W:PROJ_W + GATE_W, :].astype(BF16)

    x = x_ref[...]
    xn = x * lax.rsqrt(jnp.mean(x * x, axis=-1, keepdims=True) + EPS)
    shift = mod_ref[row:row + 1, 0:D_MODEL]
    scale = mod_ref[row:row + 1, D_MODEL:2 * D_MODEL]
    h = (xn * gpre_ref[...]) * (1.0 + scale) + shift
    hb = h.astype(BF16)
    nb = 512
    for j in range(PROJ_W // nb):
        proj_ref[:, j * nb:(j + 1) * nb] = _dot_nt(hb, w_ref[j * nb:(j + 1) * nb, :]).astype(BF16)
    gates_ref[...] = _dot_nt(hb, w_ref[PROJ_W:PROJ_W + LANES, :])


def _inproj(x2, mod, gpre, w_in, row, tm):
    n = x2.shape[0]
    return pl.pallas_call(
        functools.partial(_inproj_kernel, row=row),
        grid=(n // tm,),
        in_specs=[pl.BlockSpec((tm, D_MODEL), lambda i: (i, 0)),
                  pl.BlockSpec((8, 3 * D_MODEL), lambda i: (0, 0)),
                  pl.BlockSpec((1, D_MODEL), lambda i: (0, 0)),
                  pl.BlockSpec((None, PROJ_W + GATE_W, D_MODEL), lambda i: (0, 0, 0),
                               pipeline_mode=pl.Buffered(1))],
        out_specs=[pl.BlockSpec((tm, PROJ_W), lambda i: (i, 0)),
                   pl.BlockSpec((tm, LANES), lambda i: (i, 0))],
        out_shape=[jax.ShapeDtypeStruct((n, PROJ_W), BF16),
                   jax.ShapeDtypeStruct((n, LANES), F32)],
        scratch_shapes=[pltpu.VMEM((PROJ_W + LANES, D_MODEL), BF16)],
        compiler_params=pltpu.CompilerParams(
            dimension_semantics=("arbitrary",), vmem_limit_bytes=VMEM_LIMIT),
        name="inproj",
    )(x2, mod, gpre, w_in)


NA_RO = 2 * NA_KH - 1
NA_CO = 2 * NA_KW - 1
NA_ROWS_PER_ITER = 4


def _na_build_tables(rpb_ref, tab_ref, hp):
    qc = lax.broadcasted_iota(jnp.int32, (GRID_W, LANES), 0)
    kcol = lax.broadcasted_iota(jnp.int32, (GRID_W, LANES), 1) % GRID_W
    lo = lax.broadcasted_iota(jnp.int32, (1, LANES), 1) < GRID_W
    diff = kcol - qc + (NA_KW - 1)
    col_start = jnp.clip(qc - NA_KW // 2, 0, GRID_W - NA_KW)
    inwin = (kcol >= col_start) & (kcol < col_start + NA_KW)
    for hh in range(2):
        def ro_body(ro, carry):
            base = ((hp * 2 + hh) * NA_RO + ro) * NA_CO
            acc = jnp.zeros((GRID_W, LANES), F32)
            for j in range(NA_CO):
                val = jnp.where(lo, rpb_ref[base + j], rpb_ref[base + NA_CO + j])
                acc = jnp.where(diff == j, val, acc)
            tab_ref[hh, ro] = jnp.where(inwin, acc, NEG)
            return carry
        lax.fori_loop(0, NA_RO - 1, ro_body, 0)


def _na_kernel(rpb_ref, q_ref, k_ref, v_ref, z_ref, kc_ref, vc_ref, o_ref, tab_ref,
               *, rb_rows, rows):
    hp = pl.program_id(0)
    rb = pl.program_id(1)

    @pl.when(rb == 0)
    def _():
        _na_build_tables(rpb_ref, tab_ref, hp)

    lane = lax.broadcasted_iota(jnp.int32, (1, LANES), 1)
    first = lane < NA_DH
    kc = kc_ref[...]
    vc = vc_ref[...]
    win = NA_KH * GRID_W

    def body(it, carry):
        items = []
        for rr in range(NA_ROWS_PER_ITER):
            i = it * NA_ROWS_PER_ITER + rr
            r = rb * rb_rows + i
            rs = jnp.clip(r - NA_KH // 2, 0, rows - NA_KH)
            ro0 = NA_KH - 1 - (r - rs)
            t0 = pl.multiple_of(i * GRID_W, GRID_W)
            q = q_ref[pl.ds(t0, GRID_W), :] * jnp.asarray(NA_DH ** -0.5, BF16)
            k0 = pl.multiple_of(rs * GRID_W, GRID_W)
            kw = k_ref[pl.ds(k0, win), :]
            vw = v_ref[pl.ds(k0, win), :]
            zq = jnp.zeros_like(q)
            qs = jnp.concatenate([jnp.where(first, q, zq), jnp.where(first, zq, q)], axis=0)
            items.append((t0, ro0, qs, kw, vw))
        s_loc = [_dot_nt(qs, kw)
                 + jnp.concatenate(
                     [jnp.concatenate([tab_ref[hh, ro0 + 2 * m] for m in range(NA_KH // 2)], axis=1)
                      for hh in range(2)], axis=0)
                 for (_, ro0, qs, kw, _) in items]
        s_ctx = [_dot_nt(qs, kc) for (_, _, qs, _, _) in items]
        mx = [jnp.maximum(jnp.max(a, axis=-1, keepdims=True), jnp.max(b, axis=-1, keepdims=True))
              for a, b in zip(s_loc, s_ctx)]
        p_loc = [jnp.exp(a - m) for a, m in zip(s_loc, mx)]
        p_ctx = [jnp.exp(b - m) for b, m in zip(s_ctx, mx)]
        inv = [1.0 / (jnp.sum(a, axis=-1, keepdims=True) + jnp.sum(b, axis=-1, keepdims=True))
               for a, b in zip(p_loc, p_ctx)]
        outs = [(_dot(a.astype(BF16), it_[4]) + _dot(b.astype(BF16), vc)) * il
                for a, b, il, it_ in zip(p_loc, p_ctx, inv, items)]
        for (t0, _, _, _, _), o2 in zip(items, outs):
            o = jnp.where(first, o2[0:GRID_W], o2[GRID_W:2 * GRID_W])
            z = z_ref[pl.ds(t0, GRID_W), :].astype(F32)
            o_ref[pl.ds(t0, GRID_W), :] = (o * _silu(z)).astype(BF16)
        return carry

    lax.fori_loop(0, rb_rows // NA_ROWS_PER_ITER, body, 0)


def _natten(proj, projc, rpb_flat, rb_rows=16):
    n = proj.shape[0]
    nc = projc.shape[0]
    rows = n // GRID_W
    tq = rb_rows * GRID_W
    kcol = NA_W // LANES
    return pl.pallas_call(
        functools.partial(_na_kernel, rb_rows=rb_rows, rows=rows),
        grid=(NA_W // LANES, rows // rb_rows),
        in_specs=[pl.BlockSpec(memory_space=pltpu.SMEM),
                  pl.BlockSpec((tq, LANES), lambda hp, rb: (rb, hp)),
                  pl.BlockSpec((n, LANES), lambda hp, rb: (0, kcol + hp)),
                  pl.BlockSpec((n, LANES), lambda hp, rb: (0, 2 * kcol + hp)),
                  pl.BlockSpec((tq, LANES), lambda hp, rb: (rb, 3 * kcol + hp)),
                  pl.BlockSpec((nc, LANES), lambda hp, rb: (0, kcol + hp)),
                  pl.BlockSpec((nc, LANES), lambda hp, rb: (0, 2 * kcol + hp))],
        out_specs=pl.BlockSpec((tq, LANES), lambda hp, rb: (rb, hp)),
        out_shape=jax.ShapeDtypeStruct((n, NA_W), BF16),
        scratch_shapes=[pltpu.VMEM((2, NA_RO - 1, GRID_W, LANES), F32)],
        compiler_params=pltpu.CompilerParams(
            dimension_semantics=("arbitrary", "arbitrary"), vmem_limit_bytes=VMEM_LIMIT),
        name="natten",
    )(rpb_flat, proj, proj, proj, proj, projc, projc)


GP_TOK = 256
GP_CH = GP_TOK // CHUNK
GP_LOCK = 4
INV_BASE = 8
HALO = 16
CONV_ROWS = 128
XE_ROWS = GP_TOK + CONV_ROWS


def _gdnprep_kernel(main_ref, left_ref, right_ref, gates_ref, cw_ref, alog_ref, dtb_ref,
                    rowtab_ref, coltab_ref,
                    u_ref, w_ref, qd_ref, kd_ref, qk_ref, gl_ref,
                    xe_ref, shift_ref, qn_ref, kn_ref, vv_ref, beta_ref, gc_ref, gct_ref, eg_ref,
                    ek_ref):
    i = pl.program_id(0)
    last = pl.num_programs(0) - 1
    qkv_w = 3 * GDN_W

    @pl.when(i == 0)
    def _():
        t = lax.broadcasted_iota(jnp.int32, (CONV_K * CONV_ROWS, 2 * CONV_ROWS), 0)
        r = lax.broadcasted_iota(jnp.int32, (CONV_K * CONV_ROWS, 2 * CONV_ROWS), 1)
        hit = r == (t % CONV_ROWS) + (t // CONV_ROWS) + (HALO - CONV_K // 2)
        shift_ref[...] = jnp.where(hit, 1.0, 0.0).astype(BF16)
        xe_ref[HALO + GP_TOK + HALO:, :] = jnp.zeros(
            (XE_ROWS - GP_TOK - 2 * HALO, qkv_w), BF16)

    lh = left_ref[:, 0:qkv_w]
    rh = right_ref[:, 0:qkv_w]
    xe_ref[0:HALO, :] = jnp.where(i > 0, lh, jnp.zeros_like(lh))
    xe_ref[HALO:HALO + GP_TOK, :] = main_ref[:, 0:qkv_w]
    xe_ref[HALO + GP_TOK:HALO + GP_TOK + HALO, :] = jnp.where(i < last, rh, jnp.zeros_like(rh))

    row_lanes = lax.broadcasted_iota(jnp.int32, (1, LANES), 1) < ROPE_AXIS_DIM

    def rope_table(kind):
        return jnp.concatenate(
            [jnp.where(row_lanes, rowtab_ref[kind * GP_CH + c:kind * GP_CH + c + 1, :],
                       coltab_ref[kind]) for c in range(GP_CH)], axis=0)

    cos, sina, sinb = rope_table(0), rope_table(1), rope_table(2)
    pending = {}
    for cb, b in [(cb, b) for cb in range(qkv_w // LANES) for b in range(GP_TOK // CONV_ROWS)]:
        cols = slice(cb * LANES, (cb + 1) * LANES)
        rows = slice(b * CONV_ROWS, (b + 1) * CONV_ROWS)
        if cb % 2 == 0:
            cols2 = slice(cb * LANES, (cb + 2) * LANES)
            taps2 = _dot(shift_ref[...], xe_ref[b * CONV_ROWS:(b + 2) * CONV_ROWS, cols2])
            acc2 = None
            for j in range(CONV_K):
                term = cw_ref[j:j + 1, cols2] * taps2[j * CONV_ROWS:(j + 1) * CONV_ROWS]
                acc2 = term if acc2 is None else acc2 + term
            pending[b] = _silu(acc2)
        y = pending[b][:, (cb % 2) * LANES:(cb % 2 + 1) * LANES]
        if cb < 2 * GDN_HEADS:
            y = y * lax.rsqrt(jnp.sum(y * y, axis=-1, keepdims=True) + EPS)
            y = (y * cos[rows] + pltpu.roll(y, LANES - ROPE_AXIS_DIM // 2, 1) * sina[rows]
                 + pltpu.roll(y, ROPE_AXIS_DIM // 2, 1) * sinb[rows])
            if cb < GDN_HEADS:
                qn_ref[rows, cols] = y * (GDN_DK ** -0.5)
            else:
                kn_ref[rows, (cb - GDN_HEADS) * LANES:(cb - GDN_HEADS + 1) * LANES] = y
        else:
            vv_ref[rows, (cb - 2 * GDN_HEADS) * LANES:(cb - 2 * GDN_HEADS + 1) * LANES] = y

    gates = gates_ref[...]
    beta_ref[...] = jax.nn.sigmoid(gates)
    xa = gates + dtb_ref[...]
    softplus = jnp.maximum(xa, 0.0) + jnp.log1p(jnp.exp(-jnp.abs(xa)))
    g = -jnp.exp(alog_ref[...]) * softplus

    ti = lax.broadcasted_iota(jnp.int32, (GP_TOK, GP_TOK), 0)
    tj = lax.broadcasted_iota(jnp.int32, (GP_TOK, GP_TOK), 1)
    same = (ti // CHUNK) == (tj // CHUNK)
    lower = jnp.where(same & (ti >= tj), 1.0, 0.0).astype(BF16)
    upper = jnp.where(same & (ti <= tj), 1.0, 0.0).astype(BF16)
    g3 = _split3(g)
    gc_f = _dot(lower, g3[0]) + (_dot(lower, g3[1]) + _dot(lower, g3[2]))
    gc_b = _dot(upper, g3[0]) + (_dot(upper, g3[1]) + _dot(upper, g3[2]))
    gc_ref[0] = gc_f
    gc_ref[1] = gc_b
    gl_ref[...] = jnp.zeros_like(gl_ref)
    for c in range(GP_CH):
        rows = slice(c * CHUNK, (c + 1) * CHUNK)
        for d, gc in enumerate((gc_f, gc_b)):
            blk = gc[rows]
            gct_ref[c, d] = jnp.concatenate([blk, pltpu.roll(blk, LANES - 1, 1)], axis=0).T
        gend_f = gc_f[(c + 1) * CHUNK - 1:(c + 1) * CHUNK, :]
        gend_b = gc_b[c * CHUNK:c * CHUNK + 1, :]
        gl_ref[c, 0:1, :] = jnp.exp(gend_f)
        gl_ref[c, 1:2, :] = jnp.exp(gend_b)
        ek_ref[0, rows] = jnp.exp(gend_f - gc_f[rows])
        ek_ref[1, rows] = jnp.exp(gend_b - gc_b[rows])
    eg_ref[0] = jnp.exp(gc_f)
    eg_ref[1] = jnp.exp(gc_b)

    ii = lax.broadcasted_iota(jnp.int32, (CHUNK, LANES), 0)
    jj = lax.broadcasted_iota(jnp.int32, (CHUNK, LANES), 1) % CHUNK
    lo = lax.broadcasted_iota(jnp.int32, (1, LANES), 1) < CHUNK
    lo_wide = lax.broadcasted_iota(jnp.int32, (1, 2 * LANES), 1) < LANES
    eye = jnp.where(ii == jj, 1.0, 0.0).astype(F32)
    bodies = [(d, p) for d in range(2) for p in range(GDN_HEADS // 2)]
    same_block = lambda size: (ii // size) == (jj // size)
    base_blocks = same_block(INV_BASE)
    merge_masks = []
    size = INV_BASE
    while size < CHUNK:
        merge_masks.append(same_block(2 * size) & jnp.logical_not(same_block(size)))
        size *= 2

    def blockdiag(y, first):
        z = jnp.zeros_like(y)
        return jnp.concatenate([jnp.where(first, y, z), jnp.where(first, z, y)], axis=0)

    def pair_mm(x16, ybd16):
        return _dot(x16, ybd16)

    def chunk_body(cg, carry):
        a_l, kbeg_l, vb_l, where_l = [], [], [], []
        for cc, (d, p) in [(cc, b) for cc in range(GP_LOCK) for b in bodies]:
            c = cg * GP_LOCK + cc
            c0 = pl.multiple_of(c * CHUNK, CHUNK)
            tok = pl.ds(c0, CHUNK)
            incl = (ii >= jj) if d == 0 else (ii <= jj)
            strict = (ii > jj) if d == 0 else (ii < jj)
            pl2 = slice(2 * p * LANES, (2 * p + 2) * LANES)
            lb = d * GDN_HEADS + 2 * p
            lg = 2 * GDN_HEADS + lb
            q = qn_ref[tok, pl2]
            k = kn_ref[tok, pl2]
            v = vv_ref[tok, pl2]
            beta = jnp.where(lo_wide, beta_ref[tok, lb:lb + 1], beta_ref[tok, lb + 1:lb + 2])
            g0 = gc_ref[d, tok, lg:lg + 1]
            g1 = gc_ref[d, tok, lg + 1:lg + 2]
            gcol = jnp.where(lo, g0, g1)
            grow = gct_ref[c, d, lg:lg + 1, :]
            eg = jnp.where(lo_wide, eg_ref[d, tok, lg:lg + 1], eg_ref[d, tok, lg + 1:lg + 2])
            ek = jnp.where(lo_wide, ek_ref[d, tok, lg:lg + 1], ek_ref[d, tok, lg + 1:lg + 2])
            dec = jnp.exp(jnp.where(incl, gcol - grow, NEG))
            kb = k * beta
            k_nt = blockdiag(k.astype(BF16), lo_wide)
            a = jnp.where(strict, _dot_nt(kb.astype(BF16), k_nt) * dec, 0.0)
            qk = _dot_nt(q.astype(BF16), k_nt) * dec
            qk_ref[d, tok, p * LANES:(p + 1) * LANES] = qk.astype(BF16)
            qd_ref[d, tok, pl2] = (q * eg).astype(BF16)
            kd_ref[d, tok, pl2] = (k * ek).astype(BF16)
            a_l.append(a)
            kbeg_l.append((kb * eg).astype(BF16))
            vb_l.append((v * beta).astype(BF16))
            where_l.append((d, tok, pl2))
        x_l = [jnp.where(base_blocks, -a, 0.0) for a in a_l]
        t_l = [eye + x for x in x_l]
        x_l = [x.astype(BF16) for x in x_l]
        x_l = [pair_mm(x, blockdiag(x, lo)).astype(BF16) for x in x_l]
        r_l = [pair_mm(jnp.concatenate([x, t.astype(BF16)], axis=0), blockdiag(x, lo))
               for x, t in zip(x_l, t_l)]
        x_l = [r[0:CHUNK].astype(BF16) for r in r_l]
        t_l = [t + r[CHUNK:2 * CHUNK] for t, r in zip(t_l, r_l)]
        t_l = [t + pair_mm(t.astype(BF16), blockdiag(x, lo)) for t, x in zip(t_l, x_l)]
        for merged in merge_masks:
            t16_l = [t.astype(BF16) for t in t_l]
            te_l = [pair_mm(t16, blockdiag(jnp.where(merged, a, 0.0).astype(BF16), lo))
                    for t16, a in zip(t16_l, a_l)]
            t_l = [t - pair_mm(te.astype(BF16), blockdiag(t16, lo))
                   for t, te, t16 in zip(t_l, te_l, t16_l)]
        for (d, tok, pl2), t, vb, kbeg in zip(where_l, t_l, vb_l, kbeg_l):
            t16 = t.astype(BF16)
            u_ref[d, tok, pl2] = _dot(t16, blockdiag(vb, lo_wide))
            w_ref[d, tok, pl2] = _dot(t16, blockdiag(kbeg, lo_wide)).astype(BF16)
        return carry

    if GP_LOCK == GP_CH:
        chunk_body(0, 0)
    else:
        lax.fori_loop(0, GP_CH // GP_LOCK, chunk_body, 0)


def _gdnprep(proj, gates, conv_w, alog, dtb, rowtab, coltab):
    n = proj.shape[0]
    nt = n // GP_TOK
    hb = GP_TOK // HALO
    nhalo = n // HALO
    qkv_w = 3 * GDN_W
    tok_spec = lambda w: pl.BlockSpec((GP_TOK, w), lambda i: (i, 0))
    dir_spec = lambda w: pl.BlockSpec((2, GP_TOK, w), lambda i: (0, i, 0))
    return pl.pallas_call(
        _gdnprep_kernel,
        grid=(nt,),
        in_specs=[pl.BlockSpec((GP_TOK, PROJ_W // 2), lambda i: (i, 1)),
                  pl.BlockSpec((HALO, PROJ_W // 2), lambda i: (jnp.maximum(i * hb - 1, 0), 1)),
                  pl.BlockSpec((HALO, PROJ_W // 2),
                               lambda i: (jnp.minimum((i + 1) * hb, nhalo - 1), 1)),
                  tok_spec(LANES),
                  pl.BlockSpec((8, qkv_w), lambda i: (0, 0)),
                  pl.BlockSpec((1, LANES), lambda i: (0, 0)),
                  pl.BlockSpec((1, LANES), lambda i: (0, 0)),
                  pl.BlockSpec((None, 16, LANES), lambda i: (i, 0, 0)),
                  pl.BlockSpec((3, GRID_W, LANES), lambda i: (0, 0, 0))],
        out_specs=[dir_spec(GDN_W), dir_spec(GDN_W), dir_spec(GDN_W), dir_spec(GDN_W),
                   dir_spec(GDN_HEADS * CHUNK),
                   pl.BlockSpec((GP_CH, 8, LANES), lambda i: (i, 0, 0))],
        out_shape=[jax.ShapeDtypeStruct((2, n, GDN_W), F32),
                   jax.ShapeDtypeStruct((2, n, GDN_W), BF16),
                   jax.ShapeDtypeStruct((2, n, GDN_W), BF16),
                   jax.ShapeDtypeStruct((2, n, GDN_W), BF16),
                   jax.ShapeDtypeStruct((2, n, GDN_HEADS * CHUNK), BF16),
                   jax.ShapeDtypeStruct((n // CHUNK, 8, LANES), F32)],
        scratch_shapes=[pltpu.VMEM((XE_ROWS, qkv_w), BF16),
                        pltpu.VMEM((CONV_K * CONV_ROWS, 2 * CONV_ROWS), BF16),
                        pltpu.VMEM((GP_TOK, GDN_W), F32),
                        pltpu.VMEM((GP_TOK, GDN_W), F32),
                        pltpu.VMEM((GP_TOK, GDN_W), F32),
                        pltpu.VMEM((GP_TOK, LANES), F32),
                        pltpu.VMEM((2, GP_TOK, LANES), F32),
                        pltpu.VMEM((GP_CH, 2, LANES, LANES), F32),
                        pltpu.VMEM((2, GP_TOK, LANES), F32),
                        pltpu.VMEM((2, GP_TOK, LANES), F32)],
        compiler_params=pltpu.CompilerParams(
            dimension_semantics=("arbitrary",), vmem_limit_bytes=VMEM_LIMIT),
        name="gdnprep",
    )(proj, proj, proj, gates, conv_w, alog, dtb, rowtab, coltab)


SC_CH = 4


def _scan_kernel(gl_ref, s0_ref, uf_ref, wf_ref, qdf_ref, kdf_ref, qkf_ref,
                 ub_ref, wb_ref, qdb_ref, kdb_ref, qkb_ref,
                 of_ref, ob_ref, sfin_ref, s_ref):
    n = pl.program_id(0)
    nsteps = pl.num_programs(0)
    nch = nsteps * SC_CH

    @pl.when(n == 0)
    def _():
        s_ref[...] = s0_ref[...]

    streams = ((uf_ref, wf_ref, qdf_ref, kdf_ref, qkf_ref, of_ref),
               (ub_ref, wb_ref, qdb_ref, kdb_ref, qkb_ref, ob_ref))
    lo_wide = lax.broadcasted_iota(jnp.int32, (1, 2 * LANES), 1) < LANES
    chains = [(d, h) for d in range(2) for h in range(GDN_HEADS)]
    hl = lambda h: slice(h * LANES, (h + 1) * LANES)
    for j in range(SC_CH):
        local = (j, SC_CH - 1 - j)
        chunk = (n * SC_CH + j, (nsteps - 1 - n) * SC_CH + SC_CH - 1 - j)
        tok = [slice(c * CHUNK, (c + 1) * CHUNK) for c in local]
        s32 = [s_ref[d, h] for d, h in chains]
        s16 = [s.astype(BF16) for s in s32]
        wq = [_dot(jnp.concatenate([streams[d][1][tok[d], hl(h)], streams[d][2][tok[d], hl(h)]],
                                   axis=0), s)
              for (d, h), s in zip(chains, s16)]
        v16 = [(streams[d][0][tok[d], hl(h)] - x[0:CHUNK]).astype(BF16)
               for (d, h), x in zip(chains, wq)]
        upd = [_dot_tn(streams[d][3][tok[d], hl(h)], v) for (d, h), v in zip(chains, v16)]
        for (d, h), s, x in zip(chains, s32, upd):
            decay = gl_ref[(d * nch + chunk[d]) * GDN_HEADS + h]
            s_ref[d, h] = s * decay + x
        inter = [x[CHUNK:2 * CHUNK] for x in wq]
        for d in range(2):
            qk_ref, o_ref = streams[d][4], streams[d][5]
            for p in range(GDN_HEADS // 2):
                v_pair = jnp.concatenate(
                    v16[d * GDN_HEADS + 2 * p:d * GDN_HEADS + 2 * p + 2], axis=1)
                zero = jnp.zeros_like(v_pair)
                v_bd = jnp.concatenate([jnp.where(lo_wide, v_pair, zero),
                                        jnp.where(lo_wide, zero, v_pair)], axis=0)
                o_intra = _dot(qk_ref[tok[d], p * LANES:(p + 1) * LANES], v_bd)
                for jh in range(2):
                    h = 2 * p + jh
                    o_ref[tok[d], hl(h)] = (inter[d * GDN_HEADS + h]
                                            + o_intra[:, jh * LANES:(jh + 1) * LANES])

    @pl.when(n == nsteps - 1)
    def _():
        sfin_ref[...] = s_ref[...]


def _scan(gl, s0, u, w, qd, kd, qk):
    n = u.shape[1]
    tok = SC_CH * CHUNK
    nsteps = n // tok
    fwd = lambda wd: pl.BlockSpec((None, tok, wd), lambda i: (0, i, 0))
    bwd = lambda wd: pl.BlockSpec((None, tok, wd), lambda i: (1, nsteps - 1 - i, 0))
    qkw = GDN_HEADS * CHUNK
    state_spec = pl.BlockSpec((2, GDN_HEADS, GDN_DK, GDN_DV), lambda i: (0, 0, 0, 0))
    return pl.pallas_call(
        _scan_kernel,
        grid=(nsteps,),
        in_specs=[pl.BlockSpec(memory_space=pltpu.SMEM), state_spec,
                  fwd(GDN_W), fwd(GDN_W), fwd(GDN_W), fwd(GDN_W), fwd(qkw),
                  bwd(GDN_W), bwd(GDN_W), bwd(GDN_W), bwd(GDN_W), bwd(qkw)],
        out_specs=[pl.BlockSpec((tok, GDN_W), lambda i: (i, 0)),
                   pl.BlockSpec((tok, GDN_W), lambda i: (nsteps - 1 - i, 0)),
                   state_spec],
        out_shape=[jax.ShapeDtypeStruct((n, GDN_W), F32),
                   jax.ShapeDtypeStruct((n, GDN_W), F32),
                   jax.ShapeDtypeStruct((2, GDN_HEADS, GDN_DK, GDN_DV), F32)],
        scratch_shapes=[pltpu.VMEM((2, GDN_HEADS, GDN_DK, GDN_DV), F32)],
        compiler_params=pltpu.CompilerParams(dimension_semantics=("arbitrary",)),
        name="scan",
    )(gl, s0, u, w, qd, kd, qk, u, w, qd, kd, qk)


def _outproj_kernel(x_ref, na_ref, of_ref, ob_ref, gz_ref, gnw_ref, wout_ref, gpost_ref, mod_ref,
                    o_ref):
    og = of_ref[...] + ob_ref[...]
    gz = gz_ref[...].astype(F32)
    gnw = gnw_ref[...]
    parts = []
    for h in range(GDN_HEADS):
        seg = og[:, h * LANES:(h + 1) * LANES]
        seg = seg * lax.rsqrt(jnp.mean(seg * seg, axis=-1, keepdims=True) + EPS)
        parts.append(seg * gnw)
    gd = (jnp.concatenate(parts, axis=-1) * _silu(gz)).astype(BF16)
    y = _dot(na_ref[...], wout_ref[0:NA_W, :]) + _dot(gd, wout_ref[NA_W:NA_W + GDN_W, :])
    yn = y * lax.rsqrt(jnp.mean(y * y, axis=-1, keepdims=True) + EPS)
    gate = mod_ref[0:1, 2 * D_MODEL:3 * D_MODEL]
    o_ref[...] = x_ref[...] + gate * (yn * gpost_ref[...])


def _outproj(x2, na, o_f, o_b, proj, gnw, w_out, gpost, mod, tm=512):
    n = x2.shape[0]
    gz_col = (PROJ_W - GDN_W) // GDN_W
    return pl.pallas_call(
        _outproj_kernel,
        grid=(n // tm,),
        in_specs=[pl.BlockSpec((tm, D_MODEL), lambda i: (i, 0)),
                  pl.BlockSpec((tm, NA_W), lambda i: (i, 0)),
                  pl.BlockSpec((tm, GDN_W), lambda i: (i, 0)),
                  pl.BlockSpec((tm, GDN_W), lambda i: (i, 0)),
                  pl.BlockSpec((tm, GDN_W), lambda i: (i, gz_col)),
                  pl.BlockSpec((1, LANES), lambda i: (0, 0)),
                  pl.BlockSpec((NA_W + GDN_W, D_MODEL), lambda i: (0, 0)),
                  pl.BlockSpec((1, D_MODEL), lambda i: (0, 0)),
                  pl.BlockSpec((8, 3 * D_MODEL), lambda i: (0, 0))],
        out_specs=pl.BlockSpec((tm, D_MODEL), lambda i: (i, 0)),
        out_shape=jax.ShapeDtypeStruct((n, D_MODEL), F32),
        compiler_params=pltpu.CompilerParams(
            dimension_semantics=("arbitrary",), vmem_limit_bytes=VMEM_LIMIT),
        name="outproj",
    )(x2, na, o_f, o_b, proj, gnw, w_out, gpost, mod)


def _rope_tables(rows, identity):
    inv_freq = ROPE_BASE ** (-jnp.arange(0, ROPE_AXIS_DIM, 2, dtype=F32) / ROPE_AXIS_DIM)

    def tables(count):
        ang = jnp.arange(count, dtype=F32)[:, None] * inv_freq[None, :]
        if identity:
            ang = jnp.zeros_like(ang)
        c, s = jnp.cos(ang), jnp.sin(ang)
        z = jnp.zeros_like(s)
        return (jnp.concatenate([c, c], -1), jnp.concatenate([-s, z], -1),
                jnp.concatenate([z, s], -1))

    half = jnp.zeros((rows, LANES // 2), F32)
    rowtab = jnp.stack([jnp.concatenate([t, half], -1) for t in tables(rows)], axis=0)
    rowtab = rowtab.reshape(3, rows // GP_CH, GP_CH, LANES).transpose(1, 0, 2, 3)
    rowtab = rowtab.reshape(rows // GP_CH, 3 * GP_CH, LANES)
    rowtab = jnp.pad(rowtab, ((0, 0), (0, 16 - 3 * GP_CH), (0, 0)))
    halfc = jnp.zeros((GRID_W, LANES // 2), F32)
    coltab = jnp.stack([jnp.concatenate([halfc, t], -1) for t in tables(GRID_W)], axis=0)
    return rowtab, coltab


def _lane_row(vals, offset):
    return jnp.zeros((1, LANES), F32).at[0, offset:offset + vals.shape[0]].set(vals)


def kernel(x, c, ctx, c_ctx, w_ada, b_ada, g_pre, g_post, w_in, conv_w, rpb, A_log, dt_bias,
           gdn_norm_w, w_out):
    n = x.shape[1]
    nc = ctx.shape[1]
    x2 = x[0]
    xc2 = ctx[0]

    cc = jnp.zeros((8, D_MODEL), F32).at[0].set(c[0]).at[1].set(c_ctx)
    mod = _ada(cc, w_ada[0], b_ada[0][None, :])

    gpre = g_pre[0][None, :]
    w_in_t = jnp.swapaxes(w_in, 1, 2)
    proj, gates = _inproj(x2, mod, gpre, w_in_t, row=0, tm=512)
    projc, gatesc = _inproj(xc2, mod, gpre, w_in_t, row=1, tm=nc)

    na = _natten(proj, projc, rpb[0].reshape(-1))

    cw = jnp.pad(conv_w[0], ((0, 8 - CONV_K), (0, 0)))
    alog = _lane_row(A_log[0].reshape(-1), 2 * GDN_HEADS)
    dtb = _lane_row(dt_bias[0].reshape(-1), 2 * GDN_HEADS)
    rowtab, coltab = _rope_tables(n // GRID_W, identity=False)
    rowtab_c, coltab_c = _rope_tables(nc // GRID_W, identity=True)

    def gl_rows(gl):
        return jnp.stack([gl[:, 0, 2 * GDN_HEADS:3 * GDN_HEADS],
                          gl[:, 1, 3 * GDN_HEADS:4 * GDN_HEADS]], axis=0).reshape(-1)

    uc, wc, qdc, kdc, qkc, glc = _gdnprep(projc, gatesc, cw, alog, dtb, rowtab_c, coltab_c)
    s0 = jnp.zeros((2, GDN_HEADS, GDN_DK, GDN_DV), F32)
    _, _, s_ctx = _scan(gl_rows(glc), s0, uc, wc, qdc, kdc, qkc)

    u, w, qd, kd, qk, gl = _gdnprep(proj, gates, cw, alog, dtb, rowtab, coltab)
    o_f, o_b, _ = _scan(gl_rows(gl), s_ctx, u, w, qd, kd, qk)

    gnw = gdn_norm_w[0][None, :]
    out = _outproj(x2, na, o_f, o_b, proj, gnw, w_out[0].astype(BF16), g_post[0][None, :], mod)
    return out[None]
```

```python
import functools
import math

import jax
import jax.numpy as jnp
from jax import lax
from jax.experimental import pallas as pl
from jax.experimental.pallas import tpu as pltpu

F32 = jnp.float32
BF16 = jnp.bfloat16

D_MODEL = 1024
GRID_W = 64
NA_HEADS = 8
NA_DH = 64
NA_W = NA_HEADS * NA_DH
NA_KH = 8
NA_KW = 16
GDN_HEADS = 4
GDN_DK = 128
GDN_DV = 128
GDN_W = GDN_HEADS * GDN_DV
CHUNK = 64
CONV_K = 5
ROPE_AXIS_DIM = GDN_DK // 2
ROPE_BASE = 10000.0
EPS = 1e-6
PROJ_W = 4 * NA_W + 3 * GDN_W + GDN_W
GATE_W = 2 * 2 * GDN_HEADS
LANES = 128
NEG = -1e30

VMEM_LIMIT = 56 * 1024 * 1024


def _silu(x):
    h = 0.5 * x
    return h + h * jnp.tanh(h)


def _dot(a, b):
    return jnp.dot(a, b, preferred_element_type=F32)


def _dot_nt(a, b):
    return lax.dot_general(a, b, (((1,), (1,)), ((), ())), preferred_element_type=F32)


def _dot_tn(a, b):
    return lax.dot_general(a, b, (((0,), (0,)), ((), ())), preferred_element_type=F32)


def _split2(x):
    hi = x.astype(BF16)
    lo = (x - hi.astype(F32)).astype(BF16)
    return hi, lo


def _split3(x):
    hi = x.astype(BF16)
    r = x - hi.astype(F32)
    mid = r.astype(BF16)
    lo = (r - mid.astype(F32)).astype(BF16)
    return hi, mid, lo


def _mm3(a, b):
    ah, al = _split2(a)
    bh, bl = _split2(b)
    return _dot(ah, bh) + (_dot(al, bh) + _dot(ah, bl))


def _ada_kernel(c_ref, w_ref, b_ref, o_ref):
    s = _silu(c_ref[...])
    o_ref[...] = _mm3(s, w_ref[...]) + b_ref[...]


def _ada(cc, w_ada, b_ada):
    tn = 512
    n = w_ada.shape[1]
    return pl.pallas_call(
        _ada_kernel,
        grid=(n // tn,),
        in_specs=[pl.BlockSpec((8, D_MODEL), lambda j: (0, 0)),
                  pl.BlockSpec((D_MODEL, tn), lambda j: (0, j)),
                  pl.BlockSpec((1, tn), lambda j: (0, j))],
        out_specs=pl.BlockSpec((8, tn), lambda j: (0, j)),
        out_shape=jax.ShapeDtypeStruct((8, n), F32),
        name="ada",
    )(cc, w_ada, b_ada)


def _inproj_kernel(x_ref, mod_ref, gpre_ref, win_ref, proj_ref, gates_ref, w_ref, *, row):
    @pl.when(pl.program_id(0) == 0)
    def _():
        nb = 512
        for j in range(PROJ_W // nb):
            w_ref[j * nb:(j + 1) * nb, :] = win_ref[j * nb:(j + 1) * nb, :].astype(BF16)
        w_ref[PROJ_W:PROJ_W + LANES, :] = jnp.zeros((LANES, D_MODEL), BF16)
        w_ref[PROJ_W:PROJ_W + GATE_W, :] = win_ref[PROJ_W:PROJ_W + GATE_W, :].astype(BF16)

    x = x_ref[...]
    xn = x * lax.rsqrt(jnp.mean(x * x, axis=-1, keepdims=True) + EPS)
    shift = mod_ref[row:row + 1, 0:D_MODEL]
    scale = mod_ref[row:row + 1, D_MODEL:2 * D_MODEL]
    h = (xn * gpre_ref[...]) * (1.0 + scale) + shift
    hb = h.astype(BF16)
    nb = 512
    for j in range(PROJ_W // nb):
        proj_ref[:, j * nb:(j + 1) * nb] = _dot_nt(hb, w_ref[j * nb:(j + 1) * nb, :]).astype(BF16)
    gates_ref[...] = _dot_nt(hb, w_ref[PROJ_W:PROJ_W + LANES, :])


def _inproj(x2, mod, gpre, w_in, row, tm):
    n = x2.shape[0]
    return pl.pallas_call(
        functools.partial(_inproj_kernel, row=row),
        grid=(n // tm,),
        in_specs=[pl.BlockSpec((tm, D_MODEL), lambda i: (i, 0)),
                  pl.BlockSpec((8, 3 * D_MODEL), lambda i: (0, 0)),
                  pl.BlockSpec((1, D_MODEL), lambda i: (0, 0)),
                  pl.BlockSpec((None, PROJ_W + GATE_W, D_MODEL), lambda i: (0, 0, 0),
                               pipeline_mode=pl.Buffered(1))],
        out_specs=[pl.BlockSpec((tm, PROJ_W), lambda i: (i, 0)),
                   pl.BlockSpec((tm, LANES), lambda i: (i, 0))],
        out_shape=[jax.ShapeDtypeStruct((n, PROJ_W), BF16),
                   jax.ShapeDtypeStruct((n, LANES), F32)],
        scratch_shapes=[pltpu.VMEM((PROJ_W + LANES, D_MODEL), BF16)],
        compiler_params=pltpu.CompilerParams(
            dimension_semantics=("arbitrary",), vmem_limit_bytes=VMEM_LIMIT),
        name="inproj",
    )(x2, mod, gpre, w_in)


NA_RO = 2 * NA_KH - 1
NA_CO = 2 * NA_KW - 1
NA_ROWS_PER_ITER = 4


def _na_build_tables(rpb_ref, tab_ref, hp):
    qc = lax.broadcasted_iota(jnp.int32, (GRID_W, LANES), 0)
    kcol = lax.broadcasted_iota(jnp.int32, (GRID_W, LANES), 1) % GRID_W
    lo = lax.broadcasted_iota(jnp.int32, (1, LANES), 1) < GRID_W
    diff = kcol - qc + (NA_KW - 1)
    col_start = jnp.clip(qc - NA_KW // 2, 0, GRID_W - NA_KW)
    inwin = (kcol >= col_start) & (kcol < col_start + NA_KW)
    for hh in range(2):
        def ro_body(ro, carry):
            base = ((hp * 2 + hh) * NA_RO + ro) * NA_CO
            acc = jnp.zeros((GRID_W, LANES), F32)
            for j in range(NA_CO):
                val = jnp.where(lo, rpb_ref[base + j], rpb_ref[base + NA_CO + j])
                acc = jnp.where(diff == j, val, acc)
            tab_ref[hh, ro] = jnp.where(inwin, acc, NEG)
            return carry
        lax.fori_loop(0, NA_RO - 1, ro_body, 0)


def _na_kernel(rpb_ref, q_ref, k_ref, v_ref, z_ref, kc_ref, vc_ref, o_ref, tab_ref,
               *, rb_rows, rows):
    hp = pl.program_id(0)
    rb = pl.program_id(1)

    @pl.when(rb == 0)
    def _():
        _na_build_tables(rpb_ref, tab_ref, hp)

    lane = lax.broadcasted_iota(jnp.int32, (1, LANES), 1)
    first = lane < NA_DH
    kc = kc_ref[...]
    vc = vc_ref[...]
    win = NA_KH * GRID_W

    def body(it, carry):
        items = []
        for rr in range(NA_ROWS_PER_ITER):
            i = it * NA_ROWS_PER_ITER + rr
            r = rb * rb_rows + i
            rs = jnp.clip(r - NA_KH // 2, 0, rows - NA_KH)
            ro0 = NA_KH - 1 - (r - rs)
            t0 = pl.multiple_of(i * GRID_W, GRID_W)
            q = q_ref[pl.ds(t0, GRID_W), :] * jnp.asarray(NA_DH ** -0.5, BF16)
            k0 = pl.multiple_of(rs * GRID_W, GRID_W)
            kw = k_ref[pl.ds(k0, win), :]
            vw = v_ref[pl.ds(k0, win), :]
            zq = jnp.zeros_like(q)
            qs = jnp.concatenate([jnp.where(first, q, zq), jnp.where(first, zq, q)], axis=0)
            items.append((t0, ro0, qs, kw, vw))
        s_loc = [_dot_nt(qs, kw)
                 + jnp.concatenate(
                     [jnp.concatenate([tab_ref[hh, ro0 + 2 * m] for m in range(NA_KH // 2)], axis=1)
                      for hh in range(2)], axis=0)
                 for (_, ro0, qs, kw, _) in items]
        s_ctx = [_dot_nt(qs, kc) for (_, _, qs, _, _) in items]
        mx = [jnp.maximum(jnp.max(a, axis=-1, keepdims=True), jnp.max(b, axis=-1, keepdims=True))
              for a, b in zip(s_loc, s_ctx)]
        p_loc = [jnp.exp(a - m) for a, m in zip(s_loc, mx)]
        p_ctx = [jnp.exp(b - m) for b, m in zip(s_ctx, mx)]
        inv = [1.0 / (jnp.sum(a, axis=-1, keepdims=True) + jnp.sum(b, axis=-1, keepdims=True))
               for a, b in zip(p_loc, p_ctx)]
        outs = [(_dot(a.astype(BF16), it_[4]) + _dot(b.astype(BF16), vc)) * il
                for a, b, il, it_ in zip(p_loc, p_ctx, inv, items)]
        for (t0, _, _, _, _), o2 in zip(items, outs):
            o = jnp.where(first, o2[0:GRID_W], o2[GRID_W:2 * GRID_W])
            z = z_ref[pl.ds(t0, GRID_W), :].astype(F32)
            o_ref[pl.ds(t0, GRID_W), :] = (o * _silu(z)).astype(BF16)
        return carry

    lax.fori_loop(0, rb_rows // NA_ROWS_PER_ITER, body, 0)


def _natten(proj, projc, rpb_flat, rb_rows=16):
    n = proj.shape[0]
    nc = projc.shape[0]
    rows = n // GRID_W
    tq = rb_rows * GRID_W
    kcol = NA_W // LANES
    return pl.pallas_call(
        functools.partial(_na_kernel, rb_rows=rb_rows, rows=rows),
        grid=(NA_W // LANES, rows // rb_rows),
        in_specs=[pl.BlockSpec(memory_space=pltpu.SMEM),
                  pl.BlockSpec((tq, LANES), lambda hp, rb: (rb, hp)),
                  pl.BlockSpec((n, LANES), lambda hp, rb: (0, kcol + hp)),
                  pl.BlockSpec((n, LANES), lambda hp, rb: (0, 2 * kcol + hp)),
                  pl.BlockSpec((tq, LANES), lambda hp, rb: (rb, 3 * kcol + hp)),
                  pl.BlockSpec((nc, LANES), lambda hp, rb: (0, kcol + hp)),
                  pl.BlockSpec((nc, LANES), lambda hp, rb: (0, 2 * kcol + hp))],
        out_specs=pl.BlockSpec((tq, LANES), lambda hp, rb: (rb, hp)),
        out_shape=jax.ShapeDtypeStruct((n, NA_W), BF16),
        scratch_shapes=[pltpu.VMEM((2, NA_RO - 1, GRID_W, LANES), F32)],
        compiler_params=pltpu.CompilerParams(
            dimension_semantics=("arbitrary", "arbitrary"), vmem_limit_bytes=VMEM_LIMIT),
        name="natten",
    )(rpb_flat, proj, proj, proj, proj, projc, projc)


GP_TOK = 256
GP_CH = GP_TOK // CHUNK
GP_LOCK = 2
INV_BASE = 8
HALO = 16
CONV_ROWS = 128
XE_ROWS = GP_TOK + CONV_ROWS


def _gdnprep_kernel(main_ref, left_ref, right_ref, gates_ref, cw_ref, alog_ref, dtb_ref,
                    rowtab_ref, coltab_ref,
                    u_ref, w_ref, qd_ref, kd_ref, qk_ref, gl_ref,
                    xe_ref, shift_ref, qn_ref, kn_ref, vv_ref, beta_ref, gc_ref, gct_ref, eg_ref,
                    ek_ref, *, n_tiles):
    i = pl.program_id(0)
    tile = jnp.minimum(i, n_tiles - 1)
    qkv_w = 3 * GDN_W
    staged = (qn_ref, kn_ref, vv_ref, beta_ref, gc_ref, gct_ref, eg_ref, ek_ref)

    @pl.when(i == 0)
    def _():
        t = lax.broadcasted_iota(jnp.int32, (CONV_K * CONV_ROWS, 2 * CONV_ROWS), 0)
        r = lax.broadcasted_iota(jnp.int32, (CONV_K * CONV_ROWS, 2 * CONV_ROWS), 1)
        hit = r == (t % CONV_ROWS) + (t // CONV_ROWS) + (HALO - CONV_K // 2)
        shift_ref[...] = jnp.where(hit, 1.0, 0.0).astype(BF16)
        xe_ref[HALO + GP_TOK + HALO:, :] = jnp.zeros(
            (XE_ROWS - GP_TOK - 2 * HALO, qkv_w), BF16)
        for ref in staged:
            ref[1] = jnp.zeros(ref.shape[1:], ref.dtype)

    ii = lax.broadcasted_iota(jnp.int32, (CHUNK, LANES), 0)
    jj = lax.broadcasted_iota(jnp.int32, (CHUNK, LANES), 1) % CHUNK
    lo = lax.broadcasted_iota(jnp.int32, (1, LANES), 1) < CHUNK
    lo_wide = lax.broadcasted_iota(jnp.int32, (1, 2 * LANES), 1) < LANES
    eye = jnp.where(ii == jj, 1.0, 0.0).astype(F32)
    bodies = [(d, p) for d in range(2) for p in range(GDN_HEADS // 2)]
    same_block = lambda size: (ii // size) == (jj // size)
    base_blocks = same_block(INV_BASE)
    merge_masks = []
    size = INV_BASE
    while size < CHUNK:
        merge_masks.append(same_block(2 * size) & jnp.logical_not(same_block(size)))
        size *= 2

    def blockdiag(y, first):
        z = jnp.zeros_like(y)
        return jnp.concatenate([jnp.where(first, y, z), jnp.where(first, z, y)], axis=0)

    def prepare(slot):
        lh = left_ref[:, 0:qkv_w]
        rh = right_ref[:, 0:qkv_w]
        xe_ref[0:HALO, :] = jnp.where(tile > 0, lh, jnp.zeros_like(lh))
        xe_ref[HALO:HALO + GP_TOK, :] = main_ref[:, 0:qkv_w]
        xe_ref[HALO + GP_TOK:HALO + GP_TOK + HALO, :] = jnp.where(
            tile < n_tiles - 1, rh, jnp.zeros_like(rh))

        row_lanes = lax.broadcasted_iota(jnp.int32, (1, LANES), 1) < ROPE_AXIS_DIM

        def rope_table(kind):
            return jnp.concatenate(
                [jnp.where(row_lanes, rowtab_ref[kind * GP_CH + c:kind * GP_CH + c + 1, :],
                           coltab_ref[kind]) for c in range(GP_CH)], axis=0)

        cos, sina, sinb = rope_table(0), rope_table(1), rope_table(2)
        pending = {}
        for cb, b in [(cb, b) for cb in range(qkv_w // LANES) for b in range(GP_TOK // CONV_ROWS)]:
            cols = slice(cb * LANES, (cb + 1) * LANES)
            rows = slice(b * CONV_ROWS, (b + 1) * CONV_ROWS)
            if cb % 2 == 0:
                cols2 = slice(cb * LANES, (cb + 2) * LANES)
                taps2 = _dot(shift_ref[...], xe_ref[b * CONV_ROWS:(b + 2) * CONV_ROWS, cols2])
                acc2 = None
                for j in range(CONV_K):
                    term = cw_ref[j:j + 1, cols2] * taps2[j * CONV_ROWS:(j + 1) * CONV_ROWS]
                    acc2 = term if acc2 is None else acc2 + term
                pending[b] = _silu(acc2)
            y = pending[b][:, (cb % 2) * LANES:(cb % 2 + 1) * LANES]
            if cb < 2 * GDN_HEADS:
                y = y * lax.rsqrt(jnp.sum(y * y, axis=-1, keepdims=True) + EPS)
                y = (y * cos[rows] + pltpu.roll(y, LANES - ROPE_AXIS_DIM // 2, 1) * sina[rows]
                     + pltpu.roll(y, ROPE_AXIS_DIM // 2, 1) * sinb[rows])
                if cb < GDN_HEADS:
                    qn_ref[slot, rows, cols] = y * (GDN_DK ** -0.5)
                else:
                    kn_ref[slot, rows, (cb - GDN_HEADS) * LANES:(cb - GDN_HEADS + 1) * LANES] = y
            else:
                vv_ref[slot, rows,
                       (cb - 2 * GDN_HEADS) * LANES:(cb - 2 * GDN_HEADS + 1) * LANES] = y
            yield

        gates = gates_ref[...]
        beta_ref[slot] = jax.nn.sigmoid(gates)
        xa = gates + dtb_ref[...]
        softplus = jnp.maximum(xa, 0.0) + jnp.log1p(jnp.exp(-jnp.abs(xa)))
        g = -jnp.exp(alog_ref[...]) * softplus

        ti = lax.broadcasted_iota(jnp.int32, (GP_TOK, GP_TOK), 0)
        tj = lax.broadcasted_iota(jnp.int32, (GP_TOK, GP_TOK), 1)
        same = (ti // CHUNK) == (tj // CHUNK)
        lower = jnp.where(same & (ti >= tj), 1.0, 0.0).astype(BF16)
        upper = jnp.where(same & (ti <= tj), 1.0, 0.0).astype(BF16)
        g3 = _split3(g)
        gc_f = _dot(lower, g3[0]) + (_dot(lower, g3[1]) + _dot(lower, g3[2]))
        gc_b = _dot(upper, g3[0]) + (_dot(upper, g3[1]) + _dot(upper, g3[2]))
        gc_ref[slot, 0] = gc_f
        gc_ref[slot, 1] = gc_b
        yield
        for c in range(GP_CH):
            rows = slice(c * CHUNK, (c + 1) * CHUNK)
            for d, gc in enumerate((gc_f, gc_b)):
                blk = gc[rows]
                gct_ref[slot, c, d] = jnp.concatenate(
                    [blk, pltpu.roll(blk, LANES - 1, 1)], axis=0).T
            ek_ref[slot, 0, rows] = jnp.exp(gc_f[(c + 1) * CHUNK - 1:(c + 1) * CHUNK, :] - gc_f[rows])
            ek_ref[slot, 1, rows] = jnp.exp(gc_b[c * CHUNK:c * CHUNK + 1, :] - gc_b[rows])
            yield
        eg_ref[slot, 0] = jnp.exp(gc_f)
        eg_ref[slot, 1] = jnp.exp(gc_b)

    def solve(slot):
        gl_ref[...] = jnp.zeros_like(gl_ref)
        for c in range(GP_CH):
            gl_ref[c, 0:1, :] = eg_ref[slot, 0, (c + 1) * CHUNK - 1:(c + 1) * CHUNK, :]
            gl_ref[c, 1:2, :] = eg_ref[slot, 1, c * CHUNK:c * CHUNK + 1, :]
        for first_chunk in range(0, GP_CH, GP_LOCK):
            yield from solve_group(slot, range(first_chunk, first_chunk + GP_LOCK))

    def solve_group(slot, chunks):
        a_l, kbeg_l, vb_l, where_l = [], [], [], []
        for c, (d, p) in [(c, b) for c in chunks for b in bodies]:
            tok = slice(c * CHUNK, (c + 1) * CHUNK)
            incl = (ii >= jj) if d == 0 else (ii <= jj)
            strict = (ii > jj) if d == 0 else (ii < jj)
            pl2 = slice(2 * p * LANES, (2 * p + 2) * LANES)
            lb = d * GDN_HEADS + 2 * p
            lg = 2 * GDN_HEADS + lb
            q = qn_ref[slot, tok, pl2]
            k = kn_ref[slot, tok, pl2]
            v = vv_ref[slot, tok, pl2]
            beta = jnp.where(lo_wide, beta_ref[slot, tok, lb:lb + 1],
                             beta_ref[slot, tok, lb + 1:lb + 2])
            gcol = jnp.where(lo, gc_ref[slot, d, tok, lg:lg + 1], gc_ref[slot, d, tok, lg + 1:lg + 2])
            grow = gct_ref[slot, c, d, lg:lg + 1, :]
            eg = jnp.where(lo_wide, eg_ref[slot, d, tok, lg:lg + 1],
                           eg_ref[slot, d, tok, lg + 1:lg + 2])
            ek = jnp.where(lo_wide, ek_ref[slot, d, tok, lg:lg + 1],
                           ek_ref[slot, d, tok, lg + 1:lg + 2])
            dec = jnp.exp(jnp.where(incl, gcol - grow, NEG))
            kb = k * beta
            k_nt = blockdiag(k.astype(BF16), lo_wide)
            a = jnp.where(strict, _dot_nt(kb.astype(BF16), k_nt) * dec, 0.0)
            qk = _dot_nt(q.astype(BF16), k_nt) * dec
            qk_ref[d, tok, p * LANES:(p + 1) * LANES] = qk.astype(BF16)
            qd_ref[d, tok, pl2] = (q * eg).astype(BF16)
            kd_ref[d, tok, pl2] = (k * ek).astype(BF16)
            a_l.append(a)
            kbeg_l.append((kb * eg).astype(BF16))
            vb_l.append((v * beta).astype(BF16))
            where_l.append((d, tok, pl2))
            if p == GDN_HEADS // 2 - 1:
                yield
        x_l = [jnp.where(base_blocks, -a, 0.0) for a in a_l]
        t_l = [eye + x for x in x_l]
        x_l = [x.astype(BF16) for x in x_l]
        x_l = [_dot(x, blockdiag(x, lo)).astype(BF16) for x in x_l]
        yield
        r_l = [_dot(jnp.concatenate([x, t.astype(BF16)], axis=0), blockdiag(x, lo))
               for x, t in zip(x_l, t_l)]
        x_l = [r[0:CHUNK].astype(BF16) for r in r_l]
        t_l = [t + r[CHUNK:2 * CHUNK] for t, r in zip(t_l, r_l)]
        yield
        t_l = [t + _dot(t.astype(BF16), blockdiag(x, lo)) for t, x in zip(t_l, x_l)]
        yield
        for merged in merge_masks:
            t16_l = [t.astype(BF16) for t in t_l]
            te_l = [_dot(t16, blockdiag(jnp.where(merged, a, 0.0).astype(BF16), lo))
                    for t16, a in zip(t16_l, a_l)]
            yield
            t_l = [t - _dot(te.astype(BF16), blockdiag(t16, lo))
                   for t, te, t16 in zip(t_l, te_l, t16_l)]
            yield
        for n_done, ((d, tok, pl2), t, vb, kbeg) in enumerate(zip(where_l, t_l, vb_l, kbeg_l)):
            t16 = t.astype(BF16)
            u_ref[d, tok, pl2] = _dot(t16, blockdiag(vb, lo_wide))
            w_ref[d, tok, pl2] = _dot(t16, blockdiag(kbeg, lo_wide)).astype(BF16)
            if n_done % len(bodies) == len(bodies) - 1:
                yield

    def interleave(*stages):
        live = list(stages)
        while live:
            for g in list(live):
                if next(g, live) is live:
                    live.remove(g)

    for parity in range(2):
        @pl.when(i % 2 == parity)
        def _():
            interleave(solve(1 - parity), prepare(parity))


def _gdnprep(proj, gates, conv_w, alog, dtb, rowtab, coltab):
    n = proj.shape[0]
    nt = n // GP_TOK
    hb = GP_TOK // HALO
    nhalo = n // HALO
    qkv_w = 3 * GDN_W
    cur = lambda i: jnp.minimum(i, nt - 1)
    done = lambda i: jnp.maximum(i - 1, 0)
    tok_spec = lambda w: pl.BlockSpec((GP_TOK, w), lambda i: (cur(i), 0))
    dir_spec = lambda w: pl.BlockSpec((2, GP_TOK, w), lambda i: (0, done(i), 0))
    stage = lambda *shape: pltpu.VMEM((2,) + shape, F32)
    return pl.pallas_call(
        functools.partial(_gdnprep_kernel, n_tiles=nt),
        grid=(nt + 1,),
        in_specs=[pl.BlockSpec((GP_TOK, PROJ_W // 2), lambda i: (cur(i), 1)),
                  pl.BlockSpec((HALO, PROJ_W // 2),
                               lambda i: (jnp.maximum(cur(i) * hb - 1, 0), 1)),
                  pl.BlockSpec((HALO, PROJ_W // 2),
                               lambda i: (jnp.minimum((cur(i) + 1) * hb, nhalo - 1), 1)),
                  tok_spec(LANES),
                  pl.BlockSpec((8, qkv_w), lambda i: (0, 0)),
                  pl.BlockSpec((1, LANES), lambda i: (0, 0)),
                  pl.BlockSpec((1, LANES), lambda i: (0, 0)),
                  pl.BlockSpec((None, 16, LANES), lambda i: (cur(i), 0, 0)),
                  pl.BlockSpec((3, GRID_W, LANES), lambda i: (0, 0, 0))],
        out_specs=[dir_spec(GDN_W), dir_spec(GDN_W), dir_spec(GDN_W), dir_spec(GDN_W),
                   dir_spec(GDN_HEADS * CHUNK),
                   pl.BlockSpec((GP_CH, 8, LANES), lambda i: (done(i), 0, 0))],
        out_shape=[jax.ShapeDtypeStruct((2, n, GDN_W), F32),
                   jax.ShapeDtypeStruct((2, n, GDN_W), BF16),
                   jax.ShapeDtypeStruct((2, n, GDN_W), BF16),
                   jax.ShapeDtypeStruct((2, n, GDN_W), BF16),
                   jax.ShapeDtypeStruct((2, n, GDN_HEADS * CHUNK), BF16),
                   jax.ShapeDtypeStruct((n // CHUNK, 8, LANES), F32)],
        scratch_shapes=[pltpu.VMEM((XE_ROWS, qkv_w), BF16),
                        pltpu.VMEM((CONV_K * CONV_ROWS, 2 * CONV_ROWS), BF16),
                        stage(GP_TOK, GDN_W), stage(GP_TOK, GDN_W), stage(GP_TOK, GDN_W),
                        stage(GP_TOK, LANES),
                        stage(2, GP_TOK, LANES),
                        stage(GP_CH, 2, LANES, LANES),
                        stage(2, GP_TOK, LANES),
                        stage(2, GP_TOK, LANES)],
        compiler_params=pltpu.CompilerParams(
            dimension_semantics=("arbitrary",), vmem_limit_bytes=VMEM_LIMIT),
        name="gdnprep",
    )(proj, proj, proj, gates, conv_w, alog, dtb, rowtab, coltab)


SC_CH = 4


def _scan_kernel(gl_ref, s0_ref, uf_ref, wf_ref, qdf_ref, kdf_ref, qkf_ref,
                 ub_ref, wb_ref, qdb_ref, kdb_ref, qkb_ref,
                 of_ref, ob_ref, sfin_ref, s_ref):
    n = pl.program_id(0)
    nsteps = pl.num_programs(0)
    nch = nsteps * SC_CH

    @pl.when(n == 0)
    def _():
        s_ref[...] = s0_ref[...]

    streams = ((uf_ref, wf_ref, qdf_ref, kdf_ref, qkf_ref, of_ref),
               (ub_ref, wb_ref, qdb_ref, kdb_ref, qkb_ref, ob_ref))
    lo_wide = lax.broadcasted_iota(jnp.int32, (1, 2 * LANES), 1) < LANES
    chains = [(d, h) for d in range(2) for h in range(GDN_HEADS)]
    hl = lambda h: slice(h * LANES, (h + 1) * LANES)
    for j in range(SC_CH):
        local = (j, SC_CH - 1 - j)
        chunk = (n * SC_CH + j, (nsteps - 1 - n) * SC_CH + SC_CH - 1 - j)
        tok = [slice(c * CHUNK, (c + 1) * CHUNK) for c in local]
        s32 = [s_ref[d, h] for d, h in chains]
        s16 = [s.astype(BF16) for s in s32]
        wq = [_dot(jnp.concatenate([streams[d][1][tok[d], hl(h)], streams[d][2][tok[d], hl(h)]],
                                   axis=0), s)
              for (d, h), s in zip(chains, s16)]
        v16 = [(streams[d][0][tok[d], hl(h)] - x[0:CHUNK]).astype(BF16)
               for (d, h), x in zip(chains, wq)]
        upd = [_dot_tn(streams[d][3][tok[d], hl(h)], v) for (d, h), v in zip(chains, v16)]
        for (d, h), s, x in zip(chains, s32, upd):
            decay = gl_ref[(d * nch + chunk[d]) * GDN_HEADS + h]
            s_ref[d, h] = s * decay + x
        inter = [x[CHUNK:2 * CHUNK] for x in wq]
        for d in range(2):
            qk_ref, o_ref = streams[d][4], streams[d][5]
            for p in range(GDN_HEADS // 2):
                v_pair = jnp.concatenate(
                    v16[d * GDN_HEADS + 2 * p:d * GDN_HEADS + 2 * p + 2], axis=1)
                zero = jnp.zeros_like(v_pair)
                v_bd = jnp.concatenate([jnp.where(lo_wide, v_pair, zero),
                                        jnp.where(lo_wide, zero, v_pair)], axis=0)
                o_intra = _dot(qk_ref[tok[d], p * LANES:(p + 1) * LANES], v_bd)
                for jh in range(2):
                    h = 2 * p + jh
                    o_ref[tok[d], hl(h)] = (inter[d * GDN_HEADS + h]
                                            + o_intra[:, jh * LANES:(jh + 1) * LANES])

    @pl.when(n == nsteps - 1)
    def _():
        sfin_ref[...] = s_ref[...]


def _scan(gl, s0, u, w, qd, kd, qk):
    n = u.shape[1]
    tok = SC_CH * CHUNK
    nsteps = n // tok
    fwd = lambda wd: pl.BlockSpec((None, tok, wd), lambda i: (0, i, 0))
    bwd = lambda wd: pl.BlockSpec((None, tok, wd), lambda i: (1, nsteps - 1 - i, 0))
    qkw = GDN_HEADS * CHUNK
    state_spec = pl.BlockSpec((2, GDN_HEADS, GDN_DK, GDN_DV), lambda i: (0, 0, 0, 0))
    return pl.pallas_call(
        _scan_kernel,
        grid=(nsteps,),
        in_specs=[pl.BlockSpec(memory_space=pltpu.SMEM), state_spec,
                  fwd(GDN_W), fwd(GDN_W), fwd(GDN_W), fwd(GDN_W), fwd(qkw),
                  bwd(GDN_W), bwd(GDN_W), bwd(GDN_W), bwd(GDN_W), bwd(qkw)],
        out_specs=[pl.BlockSpec((tok, GDN_W), lambda i: (i, 0)),
                   pl.BlockSpec((tok, GDN_W), lambda i: (nsteps - 1 - i, 0)),
                   state_spec],
        out_shape=[jax.ShapeDtypeStruct((n, GDN_W), F32),
                   jax.ShapeDtypeStruct((n, GDN_W), F32),
                   jax.ShapeDtypeStruct((2, GDN_HEADS, GDN_DK, GDN_DV), F32)],
        scratch_shapes=[pltpu.VMEM((2, GDN_HEADS, GDN_DK, GDN_DV), F32)],
        compiler_params=pltpu.CompilerParams(dimension_semantics=("arbitrary",)),
        name="scan",
    )(gl, s0, u, w, qd, kd, qk, u, w, qd, kd, qk)


def _outproj_kernel(x_ref, na_ref, of_ref, ob_ref, gz_ref, gnw_ref, wout_ref, gpost_ref, mod_ref,
                    o_ref):
    og = of_ref[...] + ob_ref[...]
    gz = gz_ref[...].astype(F32)
    gnw = gnw_ref[...]
    parts = []
    for h in range(GDN_HEADS):
        seg = og[:, h * LANES:(h + 1) * LANES]
        seg = seg * lax.rsqrt(jnp.mean(seg * seg, axis=-1, keepdims=True) + EPS)
        parts.append(seg * gnw)
    gd = (jnp.concatenate(parts, axis=-1) * _silu(gz)).astype(BF16)
    y = _dot(na_ref[...], wout_ref[0:NA_W, :]) + _dot(gd, wout_ref[NA_W:NA_W + GDN_W, :])
    yn = y * lax.rsqrt(jnp.mean(y * y, axis=-1, keepdims=True) + EPS)
    gate = mod_ref[0:1, 2 * D_MODEL:3 * D_MODEL]
    o_ref[...] = x_ref[...] + gate * (yn * gpost_ref[...])


def _outproj(x2, na, o_f, o_b, proj, gnw, w_out, gpost, mod, tm=512):
    n = x2.shape[0]
    gz_col = (PROJ_W - GDN_W) // GDN_W
    return pl.pallas_call(
        _outproj_kernel,
        grid=(n // tm,),
        in_specs=[pl.BlockSpec((tm, D_MODEL), lambda i: (i, 0)),
                  pl.BlockSpec((tm, NA_W), lambda i: (i, 0)),
                  pl.BlockSpec((tm, GDN_W), lambda i: (i, 0)),
                  pl.BlockSpec((tm, GDN_W), lambda i: (i, 0)),
                  pl.BlockSpec((tm, GDN_W), lambda i: (i, gz_col)),
                  pl.BlockSpec((1, LANES), lambda i: (0, 0)),
                  pl.BlockSpec((NA_W + GDN_W, D_MODEL), lambda i: (0, 0)),
                  pl.BlockSpec((1, D_MODEL), lambda i: (0, 0)),
                  pl.BlockSpec((8, 3 * D_MODEL), lambda i: (0, 0))],
        out_specs=pl.BlockSpec((tm, D_MODEL), lambda i: (i, 0)),
        out_shape=jax.ShapeDtypeStruct((n, D_MODEL), F32),
        compiler_params=pltpu.CompilerParams(
            dimension_semantics=("arbitrary",), vmem_limit_bytes=VMEM_LIMIT),
        name="outproj",
    )(x2, na, o_f, o_b, proj, gnw, w_out, gpost, mod)


def _rope_tables(rows, identity):
    inv_freq = ROPE_BASE ** (-jnp.arange(0, ROPE_AXIS_DIM, 2, dtype=F32) / ROPE_AXIS_DIM)

    def tables(count):
        ang = jnp.arange(count, dtype=F32)[:, None] * inv_freq[None, :]
        if identity:
            ang = jnp.zeros_like(ang)
        c, s = jnp.cos(ang), jnp.sin(ang)
        z = jnp.zeros_like(s)
        return (jnp.concatenate([c, c], -1), jnp.concatenate([-s, z], -1),
                jnp.concatenate([z, s], -1))

    half = jnp.zeros((rows, LANES // 2), F32)
    rowtab = jnp.stack([jnp.concatenate([t, half], -1) for t in tables(rows)], axis=0)
    rowtab = rowtab.reshape(3, rows // GP_CH, GP_CH, LANES).transpose(1, 0, 2, 3)
    rowtab = rowtab.reshape(rows // GP_CH, 3 * GP_CH, LANES)
    rowtab = jnp.pad(rowtab, ((0, 0), (0, 16 - 3 * GP_CH), (0, 0)))
    halfc = jnp.zeros((GRID_W, LANES // 2), F32)
    coltab = jnp.stack([jnp.concatenate([halfc, t], -1) for t in tables(GRID_W)], axis=0)
    return rowtab, coltab


def _lane_row(vals, offset):
    return jnp.zeros((1, LANES), F32).at[0, offset:offset + vals.shape[0]].set(vals)


def kernel(x, c, ctx, c_ctx, w_ada, b_ada, g_pre, g_post, w_in, conv_w, rpb, A_log, dt_bias,
           gdn_norm_w, w_out):
    n = x.shape[1]
    nc = ctx.shape[1]
    x2 = x[0]
    xc2 = ctx[0]

    cc = jnp.zeros((8, D_MODEL), F32).at[0].set(c[0]).at[1].set(c_ctx)
    mod = _ada(cc, w_ada[0], b_ada[0][None, :])

    gpre = g_pre[0][None, :]
    w_in_t = jnp.swapaxes(w_in, 1, 2)
    proj, gates = _inproj(x2, mod, gpre, w_in_t, row=0, tm=512)
    projc, gatesc = _inproj(xc2, mod, gpre, w_in_t, row=1, tm=nc)

    na = _natten(proj, projc, rpb[0].reshape(-1))

    cw = jnp.pad(conv_w[0], ((0, 8 - CONV_K), (0, 0)))
    alog = _lane_row(A_log[0].reshape(-1), 2 * GDN_HEADS)
    dtb = _lane_row(dt_bias[0].reshape(-1), 2 * GDN_HEADS)
    rowtab, coltab = _rope_tables(n // GRID_W, identity=False)
    rowtab_c, coltab_c = _rope_tables(nc // GRID_W, identity=True)

    def gl_rows(gl):
        return jnp.stack([gl[:, 0, 2 * GDN_HEADS:3 * GDN_HEADS],
                          gl[:, 1, 3 * GDN_HEADS:4 * GDN_HEADS]], axis=0).reshape(-1)

    uc, wc, qdc, kdc, qkc, glc = _gdnprep(projc, gatesc, cw, alog, dtb, rowtab_c, coltab_c)
    s0 = jnp.zeros((2, GDN_HEADS, GDN_DK, GDN_DV), F32)
    _, _, s_ctx = _scan(gl_rows(glc), s0, uc, wc, qdc, kdc, qkc)

    u, w, qd, kd, qk, gl = _gdnprep(proj, gates, cw, alog, dtb, rowtab, coltab)
    o_f, o_b, _ = _scan(gl_rows(gl), s_ctx, u, w, qd, kd, qk)

    gnw = gdn_norm_w[0][None, :]
    out = _outproj(x2, na, o_f, o_b, proj, gnw, w_out[0].astype(BF16), g_post[0][None, :], mod)
    return out[None]
```

```python
import functools
import math

import jax
import jax.numpy as jnp
from jax import lax
from jax.experimental import pallas as pl
from jax.experimental.pallas import tpu as pltpu

F32 = jnp.float32
BF16 = jnp.bfloat16

D_MODEL = 1024
GRID_W = 64
NA_HEADS = 8
NA_DH = 64
NA_W = NA_HEADS * NA_DH
NA_KH = 8
NA_KW = 16
GDN_HEADS = 4
GDN_DK = 128
GDN_DV = 128
GDN_W = GDN_HEADS * GDN_DV
CHUNK = 64
CONV_K = 5
ROPE_AXIS_DIM = GDN_DK // 2
ROPE_BASE = 10000.0
EPS = 1e-6
PROJ_W = 4 * NA_W + 3 * GDN_W + GDN_W
GATE_W = 2 * 2 * GDN_HEADS
LANES = 128
NEG = -1e30

VMEM_LIMIT = 56 * 1024 * 1024


def _silu(x):
    h = 0.5 * x
    return h + h * jnp.tanh(h)


def _dot(a, b):
    return jnp.dot(a, b, preferred_element_type=F32)


def _dot_nt(a, b):
    return lax.dot_general(a, b, (((1,), (1,)), ((), ())), preferred_element_type=F32)


def _dot_tn(a, b):
    return lax.dot_general(a, b, (((0,), (0,)), ((), ())), preferred_element_type=F32)


def _split2(x):
    hi = x.astype(BF16)
    lo = (x - hi.astype(F32)).astype(BF16)
    return hi, lo


def _split3(x):
    hi = x.astype(BF16)
    r = x - hi.astype(F32)
    mid = r.astype(BF16)
    lo = (r - mid.astype(F32)).astype(BF16)
    return hi, mid, lo


def _mm3(a, b):
    ah, al = _split2(a)
    bh, bl = _split2(b)
    return _dot(ah, bh) + (_dot(al, bh) + _dot(ah, bl))


def _ada_kernel(c_ref, w_ref, b_ref, o_ref):
    s = _silu(c_ref[...])
    o_ref[...] = _mm3(s, w_ref[...]) + b_ref[...]


def _ada(cc, w_ada, b_ada):
    tn = 512
    n = w_ada.shape[1]
    return pl.pallas_call(
        _ada_kernel,
        grid=(n // tn,),
        in_specs=[pl.BlockSpec((8, D_MODEL), lambda j: (0, 0)),
                  pl.BlockSpec((D_MODEL, tn), lambda j: (0, j)),
                  pl.BlockSpec((1, tn), lambda j: (0, j))],
        out_specs=pl.BlockSpec((8, tn), lambda j: (0, j)),
        out_shape=jax.ShapeDtypeStruct((8, n), F32),
        name="ada",
    )(cc, w_ada, b_ada)


LOG2E = math.log2(math.e)
NA_QSCALE = NA_DH ** -0.5 * LOG2E


def _inproj_kernel(x_ref, xc_ref, mod_ref, gpre_ref, win_ref,
                   proj_ref, gates_ref, projc_ref, gatesc_ref, w_ref):
    def project(x, row, proj_out, gates_out):
        xn = x * lax.rsqrt(jnp.mean(x * x, axis=-1, keepdims=True) + EPS)
        shift = mod_ref[row:row + 1, 0:D_MODEL]
        scale = mod_ref[row:row + 1, D_MODEL:2 * D_MODEL]
        h = (xn * gpre_ref[...]) * (1.0 + scale) + shift
        hb = h.astype(BF16)
        nb = 512
        for j in range(PROJ_W // nb):
            y = _dot_nt(hb, w_ref[j * nb:(j + 1) * nb, :])
            if j * nb < NA_W:
                y = y * NA_QSCALE
            proj_out[:, j * nb:(j + 1) * nb] = y.astype(BF16)
        gates_out[...] = _dot_nt(hb, w_ref[PROJ_W:PROJ_W + LANES, :])

    @pl.when(pl.program_id(0) == 0)
    def _():
        nb = 512
        for j in range(PROJ_W // nb):
            w_ref[j * nb:(j + 1) * nb, :] = win_ref[j * nb:(j + 1) * nb, :].astype(BF16)
        w_ref[PROJ_W:PROJ_W + LANES, :] = jnp.zeros((LANES, D_MODEL), BF16)
        w_ref[PROJ_W:PROJ_W + GATE_W, :] = win_ref[PROJ_W:PROJ_W + GATE_W, :].astype(BF16)
        project(xc_ref[...], 1, projc_ref, gatesc_ref)

    project(x_ref[...], 0, proj_ref, gates_ref)


def _inproj(x2, xc2, mod, gpre, w_in, tm):
    n = x2.shape[0]
    nc = xc2.shape[0]
    return pl.pallas_call(
        _inproj_kernel,
        grid=(n // tm,),
        in_specs=[pl.BlockSpec((tm, D_MODEL), lambda i: (i, 0)),
                  pl.BlockSpec((nc, D_MODEL), lambda i: (0, 0)),
                  pl.BlockSpec((8, 3 * D_MODEL), lambda i: (0, 0)),
                  pl.BlockSpec((1, D_MODEL), lambda i: (0, 0)),
                  pl.BlockSpec((None, PROJ_W + GATE_W, D_MODEL), lambda i: (0, 0, 0),
                               pipeline_mode=pl.Buffered(1))],
        out_specs=[pl.BlockSpec((tm, PROJ_W), lambda i: (i, 0)),
                   pl.BlockSpec((tm, LANES), lambda i: (i, 0)),
                   pl.BlockSpec((nc, PROJ_W), lambda i: (0, 0)),
                   pl.BlockSpec((nc, LANES), lambda i: (0, 0))],
        out_shape=[jax.ShapeDtypeStruct((n, PROJ_W), BF16),
                   jax.ShapeDtypeStruct((n, LANES), F32),
                   jax.ShapeDtypeStruct((nc, PROJ_W), BF16),
                   jax.ShapeDtypeStruct((nc, LANES), F32)],
        scratch_shapes=[pltpu.VMEM((PROJ_W + LANES, D_MODEL), BF16)],
        compiler_params=pltpu.CompilerParams(
            dimension_semantics=("arbitrary",), vmem_limit_bytes=VMEM_LIMIT),
        name="inproj",
    )(x2, xc2, mod, gpre, w_in)


NA_RO = 2 * NA_KH - 1
NA_CO = 2 * NA_KW - 1
NA_ROWS_PER_ITER = 4


def _na_build_tables(rpb_ref, tab_ref, hp):
    qc = lax.broadcasted_iota(jnp.int32, (GRID_W, LANES), 0)
    kcol = lax.broadcasted_iota(jnp.int32, (GRID_W, LANES), 1) % GRID_W
    lo = lax.broadcasted_iota(jnp.int32, (1, LANES), 1) < GRID_W
    diff = kcol - qc + (NA_KW - 1)
    col_start = jnp.clip(qc - NA_KW // 2, 0, GRID_W - NA_KW)
    inwin = (kcol >= col_start) & (kcol < col_start + NA_KW)
    for hh in range(2):
        def ro_body(ro, carry):
            base = ((hp * 2 + hh) * NA_RO + ro) * NA_CO
            acc = jnp.zeros((GRID_W, LANES), F32)
            for j in range(NA_CO):
                val = jnp.where(lo, rpb_ref[base + j], rpb_ref[base + NA_CO + j])
                acc = jnp.where(diff == j, val, acc)
            tab_ref[hh, ro] = jnp.where(inwin, acc * LOG2E, NEG)
            return carry
        lax.fori_loop(0, NA_RO - 1, ro_body, 0)


def _na_kernel(rpb_ref, q_ref, k_ref, v_ref, z_ref, kc_ref, vc_ref, o_ref, tab_ref,
               *, rb_rows, rows):
    hp = pl.program_id(0)
    rb = pl.program_id(1)

    @pl.when(rb == 0)
    def _():
        _na_build_tables(rpb_ref, tab_ref, hp)

    lane = lax.broadcasted_iota(jnp.int32, (1, LANES), 1)
    first = lane < NA_DH
    kc = kc_ref[...]
    vc = vc_ref[...]
    win = NA_KH * GRID_W

    def body(it, carry):
        items = []
        for rr in range(NA_ROWS_PER_ITER):
            i = it * NA_ROWS_PER_ITER + rr
            r = rb * rb_rows + i
            rs = jnp.clip(r - NA_KH // 2, 0, rows - NA_KH)
            ro0 = NA_KH - 1 - (r - rs)
            t0 = pl.multiple_of(i * GRID_W, GRID_W)
            q = q_ref[pl.ds(t0, GRID_W), :]
            k0 = pl.multiple_of(rs * GRID_W, GRID_W)
            kw = k_ref[pl.ds(k0, win), :]
            vw = v_ref[pl.ds(k0, win), :]
            zq = jnp.zeros_like(q)
            qs = jnp.concatenate([jnp.where(first, q, zq), jnp.where(first, zq, q)], axis=0)
            items.append((t0, ro0, qs, kw, vw))
        s_loc = [_dot_nt(qs, kw)
                 + jnp.concatenate(
                     [jnp.concatenate([tab_ref[hh, ro0 + 2 * m] for m in range(NA_KH // 2)], axis=1)
                      for hh in range(2)], axis=0)
                 for (_, ro0, qs, kw, _) in items]
        s_ctx = [_dot_nt(qs, kc) for (_, _, qs, _, _) in items]
        mx = [jnp.maximum(jnp.max(a, axis=-1, keepdims=True), jnp.max(b, axis=-1, keepdims=True))
              for a, b in zip(s_loc, s_ctx)]
        p_loc = [jnp.exp2(a - m) for a, m in zip(s_loc, mx)]
        p_ctx = [jnp.exp2(b - m) for b, m in zip(s_ctx, mx)]
        inv = [1.0 / (jnp.sum(a, axis=-1, keepdims=True) + jnp.sum(b, axis=-1, keepdims=True))
               for a, b in zip(p_loc, p_ctx)]
        outs = [(_dot(a.astype(BF16), it_[4]) + _dot(b.astype(BF16), vc)) * il
                for a, b, il, it_ in zip(p_loc, p_ctx, inv, items)]
        for (t0, _, _, _, _), o2 in zip(items, outs):
            o = jnp.where(first, o2[0:GRID_W], o2[GRID_W:2 * GRID_W])
            z = z_ref[pl.ds(t0, GRID_W), :].astype(F32)
            o_ref[pl.ds(t0, GRID_W), :] = (o * _silu(z)).astype(BF16)
        return carry

    lax.fori_loop(0, rb_rows // NA_ROWS_PER_ITER, body, 0)


def _natten(proj, projc, rpb_flat, rb_rows=16):
    n = proj.shape[0]
    nc = projc.shape[0]
    rows = n // GRID_W
    tq = rb_rows * GRID_W
    kcol = NA_W // LANES
    return pl.pallas_call(
        functools.partial(_na_kernel, rb_rows=rb_rows, rows=rows),
        grid=(NA_W // LANES, rows // rb_rows),
        in_specs=[pl.BlockSpec(memory_space=pltpu.SMEM),
                  pl.BlockSpec((tq, LANES), lambda hp, rb: (rb, hp)),
                  pl.BlockSpec((n, LANES), lambda hp, rb: (0, kcol + hp)),
                  pl.BlockSpec((n, LANES), lambda hp, rb: (0, 2 * kcol + hp)),
                  pl.BlockSpec((tq, LANES), lambda hp, rb: (rb, 3 * kcol + hp)),
                  pl.BlockSpec((nc, LANES), lambda hp, rb: (0, kcol + hp)),
                  pl.BlockSpec((nc, LANES), lambda hp, rb: (0, 2 * kcol + hp))],
        out_specs=pl.BlockSpec((tq, LANES), lambda hp, rb: (rb, hp)),
        out_shape=jax.ShapeDtypeStruct((n, NA_W), BF16),
        scratch_shapes=[pltpu.VMEM((2, NA_RO - 1, GRID_W, LANES), F32)],
        compiler_params=pltpu.CompilerParams(
            dimension_semantics=("arbitrary", "arbitrary"), vmem_limit_bytes=VMEM_LIMIT),
        name="natten",
    )(rpb_flat, proj, proj, proj, proj, projc, projc)


GP_TOK = 256
GP_CH = GP_TOK // CHUNK
GP_LOCK = 2
INV_BASE = 8
HALO = 16
CONV_ROWS = 128
XE_ROWS = GP_TOK + CONV_ROWS


def _gdnprep_kernel(main_ref, left_ref, right_ref, gates_ref, cw_ref, alog_ref, dtb_ref,
                    rowtab_ref, coltab_ref,
                    u_ref, w_ref, qd_ref, kd_ref, qk_ref, gl_ref,
                    xe_ref, shift_ref, qn_ref, kn_ref, vv_ref, beta_ref, gc_ref, gct_ref, eg_ref,
                    ek_ref, *, n_tiles):
    i = pl.program_id(0)
    tile = jnp.minimum(i, n_tiles - 1)
    qkv_w = 3 * GDN_W
    staged = (qn_ref, kn_ref, vv_ref, beta_ref, gc_ref, gct_ref, eg_ref, ek_ref)

    @pl.when(i == 0)
    def _():
        t = lax.broadcasted_iota(jnp.int32, (CONV_K * CONV_ROWS, 2 * CONV_ROWS), 0)
        r = lax.broadcasted_iota(jnp.int32, (CONV_K * CONV_ROWS, 2 * CONV_ROWS), 1)
        hit = r == (t % CONV_ROWS) + (t // CONV_ROWS) + (HALO - CONV_K // 2)
        shift_ref[...] = jnp.where(hit, 1.0, 0.0).astype(BF16)
        xe_ref[HALO + GP_TOK + HALO:, :] = jnp.zeros(
            (XE_ROWS - GP_TOK - 2 * HALO, qkv_w), BF16)
        for ref in staged:
            ref[1] = jnp.zeros(ref.shape[1:], ref.dtype)

    ii = lax.broadcasted_iota(jnp.int32, (CHUNK, LANES), 0)
    jj = lax.broadcasted_iota(jnp.int32, (CHUNK, LANES), 1) % CHUNK
    lo = lax.broadcasted_iota(jnp.int32, (1, LANES), 1) < CHUNK
    lo_wide = lax.broadcasted_iota(jnp.int32, (1, 2 * LANES), 1) < LANES
    eye = jnp.where(ii == jj, 1.0, 0.0).astype(F32)
    bodies = [(d, p) for d in range(2) for p in range(GDN_HEADS // 2)]
    same_block = lambda size: (ii // size) == (jj // size)
    base_blocks = same_block(INV_BASE)
    merge_masks = []
    size = INV_BASE
    while size < CHUNK:
        merge_masks.append(same_block(2 * size) & jnp.logical_not(same_block(size)))
        size *= 2

    def blockdiag(y, first):
        z = jnp.zeros_like(y)
        return jnp.concatenate([jnp.where(first, y, z), jnp.where(first, z, y)], axis=0)

    def prepare(slot):
        lh = left_ref[:, 0:qkv_w]
        rh = right_ref[:, 0:qkv_w]
        xe_ref[0:HALO, :] = jnp.where(tile > 0, lh, jnp.zeros_like(lh))
        xe_ref[HALO:HALO + GP_TOK, :] = main_ref[:, 0:qkv_w]
        xe_ref[HALO + GP_TOK:HALO + GP_TOK + HALO, :] = jnp.where(
            tile < n_tiles - 1, rh, jnp.zeros_like(rh))

        row_lanes = lax.broadcasted_iota(jnp.int32, (1, LANES), 1) < ROPE_AXIS_DIM

        def rope_table(kind):
            return jnp.concatenate(
                [jnp.where(row_lanes, rowtab_ref[kind * GP_CH + c:kind * GP_CH + c + 1, :],
                           coltab_ref[kind]) for c in range(GP_CH)], axis=0)

        cos, sina, sinb = rope_table(0), rope_table(1), rope_table(2)
        pending = {}
        for cb, b in [(cb, b) for cb in range(qkv_w // LANES) for b in range(GP_TOK // CONV_ROWS)]:
            cols = slice(cb * LANES, (cb + 1) * LANES)
            rows = slice(b * CONV_ROWS, (b + 1) * CONV_ROWS)
            if cb % 2 == 0:
                cols2 = slice(cb * LANES, (cb + 2) * LANES)
                taps2 = _dot(shift_ref[...], xe_ref[b * CONV_ROWS:(b + 2) * CONV_ROWS, cols2])
                acc2 = None
                for j in range(CONV_K):
                    term = cw_ref[j:j + 1, cols2] * taps2[j * CONV_ROWS:(j + 1) * CONV_ROWS]
                    acc2 = term if acc2 is None else acc2 + term
                pending[b] = _silu(acc2)
            y = pending[b][:, (cb % 2) * LANES:(cb % 2 + 1) * LANES]
            if cb < 2 * GDN_HEADS:
                y = y * lax.rsqrt(jnp.sum(y * y, axis=-1, keepdims=True) + EPS)
                y = (y * cos[rows] + pltpu.roll(y, LANES - ROPE_AXIS_DIM // 2, 1) * sina[rows]
                     + pltpu.roll(y, ROPE_AXIS_DIM // 2, 1) * sinb[rows])
                if cb < GDN_HEADS:
                    qn_ref[slot, rows, cols] = y * (GDN_DK ** -0.5)
                else:
                    kn_ref[slot, rows, (cb - GDN_HEADS) * LANES:(cb - GDN_HEADS + 1) * LANES] = y
            else:
                vv_ref[slot, rows,
                       (cb - 2 * GDN_HEADS) * LANES:(cb - 2 * GDN_HEADS + 1) * LANES] = y
            yield

        gates = gates_ref[...]
        beta_ref[slot] = jax.nn.sigmoid(gates)
        xa = gates + dtb_ref[...]
        softplus = jnp.maximum(xa, 0.0) + jnp.log1p(jnp.exp(-jnp.abs(xa)))
        g = -jnp.exp(alog_ref[...]) * softplus

        ti = lax.broadcasted_iota(jnp.int32, (GP_TOK, GP_TOK), 0)
        tj = lax.broadcasted_iota(jnp.int32, (GP_TOK, GP_TOK), 1)
        same = (ti // CHUNK) == (tj // CHUNK)
        lower = jnp.where(same & (ti >= tj), 1.0, 0.0).astype(BF16)
        upper = jnp.where(same & (ti <= tj), 1.0, 0.0).astype(BF16)
        g3 = _split3(g)
        gc_f = _dot(lower, g3[0]) + (_dot(lower, g3[1]) + _dot(lower, g3[2]))
        gc_b = _dot(upper, g3[0]) + (_dot(upper, g3[1]) + _dot(upper, g3[2]))
        gc_ref[slot, 0] = gc_f
        gc_ref[slot, 1] = gc_b
        yield
        for c in range(GP_CH):
            rows = slice(c * CHUNK, (c + 1) * CHUNK)
            for d, gc in enumerate((gc_f, gc_b)):
                blk = gc[rows]
                gct_ref[slot, c, d] = jnp.concatenate(
                    [blk, pltpu.roll(blk, LANES - 1, 1)], axis=0).T
            ek_ref[slot, 0, rows] = jnp.exp(gc_f[(c + 1) * CHUNK - 1:(c + 1) * CHUNK, :] - gc_f[rows])
            ek_ref[slot, 1, rows] = jnp.exp(gc_b[c * CHUNK:c * CHUNK + 1, :] - gc_b[rows])
            yield
        eg_ref[slot, 0] = jnp.exp(gc_f)
        eg_ref[slot, 1] = jnp.exp(gc_b)

    def solve(slot):
        gl_ref[...] = jnp.zeros_like(gl_ref)
        for c in range(GP_CH):
            gl_ref[c, 0:1, :] = eg_ref[slot, 0, (c + 1) * CHUNK - 1:(c + 1) * CHUNK, :]
            gl_ref[c, 1:2, :] = eg_ref[slot, 1, c * CHUNK:c * CHUNK + 1, :]
        for first_chunk in range(0, GP_CH, GP_LOCK):
            yield from solve_group(slot, range(first_chunk, first_chunk + GP_LOCK))

    def solve_group(slot, chunks):
        a_l, kbeg_l, vb_l, where_l = [], [], [], []
        for c, (d, p) in [(c, b) for c in chunks for b in bodies]:
            tok = slice(c * CHUNK, (c + 1) * CHUNK)
            incl = (ii >= jj) if d == 0 else (ii <= jj)
            strict = (ii > jj) if d == 0 else (ii < jj)
            pl2 = slice(2 * p * LANES, (2 * p + 2) * LANES)
            lb = d * GDN_HEADS + 2 * p
            lg = 2 * GDN_HEADS + lb
            q = qn_ref[slot, tok, pl2]
            k = kn_ref[slot, tok, pl2]
            v = vv_ref[slot, tok, pl2]
            beta = jnp.where(lo_wide, beta_ref[slot, tok, lb:lb + 1],
                             beta_ref[slot, tok, lb + 1:lb + 2])
            gcol = jnp.where(lo, gc_ref[slot, d, tok, lg:lg + 1], gc_ref[slot, d, tok, lg + 1:lg + 2])
            grow = gct_ref[slot, c, d, lg:lg + 1, :]
            eg = jnp.where(lo_wide, eg_ref[slot, d, tok, lg:lg + 1],
                           eg_ref[slot, d, tok, lg + 1:lg + 2])
            ek = jnp.where(lo_wide, ek_ref[slot, d, tok, lg:lg + 1],
                           ek_ref[slot, d, tok, lg + 1:lg + 2])
            dec = jnp.exp(jnp.where(incl, gcol - grow, NEG))
            kb = k * beta
            k_nt = blockdiag(k.astype(BF16), lo_wide)
            a = jnp.where(strict, _dot_nt(kb.astype(BF16), k_nt) * dec, 0.0)
            qk = _dot_nt(q.astype(BF16), k_nt) * dec
            qk_ref[d, tok, p * LANES:(p + 1) * LANES] = qk.astype(BF16)
            qd_ref[d, tok, pl2] = (q * eg).astype(BF16)
            kd_ref[d, tok, pl2] = (k * ek).astype(BF16)
            a_l.append(a)
            kbeg_l.append((kb * eg).astype(BF16))
            vb_l.append((v * beta).astype(BF16))
            where_l.append((d, tok, pl2))
            if p == GDN_HEADS // 2 - 1:
                yield
        x_l = [jnp.where(base_blocks, -a, 0.0) for a in a_l]
        t_l = [eye + x for x in x_l]
        x_l = [x.astype(BF16) for x in x_l]
        x_l = [_dot(x, blockdiag(x, lo)).astype(BF16) for x in x_l]
        yield
        r_l = [_dot(jnp.concatenate([x, t.astype(BF16)], axis=0), blockdiag(x, lo))
               for x, t in zip(x_l, t_l)]
        x_l = [r[0:CHUNK].astype(BF16) for r in r_l]
        t_l = [t + r[CHUNK:2 * CHUNK] for t, r in zip(t_l, r_l)]
        yield
        t_l = [t + _dot(t.astype(BF16), blockdiag(x, lo)) for t, x in zip(t_l, x_l)]
        yield
        for merged in merge_masks:
            t16_l = [t.astype(BF16) for t in t_l]
            te_l = [_dot(t16, blockdiag(jnp.where(merged, a, 0.0).astype(BF16), lo))
                    for t16, a in zip(t16_l, a_l)]
            yield
            t_l = [t - _dot(te.astype(BF16), blockdiag(t16, lo))
                   for t, te, t16 in zip(t_l, te_l, t16_l)]
            yield
        for n_done, ((d, tok, pl2), t, vb, kbeg) in enumerate(zip(where_l, t_l, vb_l, kbeg_l)):
            t16 = t.astype(BF16)
            u_ref[d, tok, pl2] = _dot(t16, blockdiag(vb, lo_wide))
            w_ref[d, tok, pl2] = _dot(t16, blockdiag(kbeg, lo_wide)).astype(BF16)
            if n_done % len(bodies) == len(bodies) - 1:
                yield

    def interleave(*stages):
        live = list(stages)
        while live:
            for g in list(live):
                if next(g, live) is live:
                    live.remove(g)

    for parity in range(2):
        @pl.when(i % 2 == parity)
        def _():
            interleave(solve(1 - parity), prepare(parity))


def _gdnprep(proj, gates, conv_w, alog, dtb, rowtab, coltab):
    n = proj.shape[0]
    nt = n // GP_TOK
    hb = GP_TOK // HALO
    nhalo = n // HALO
    qkv_w = 3 * GDN_W
    cur = lambda i: jnp.minimum(i, nt - 1)
    done = lambda i: jnp.maximum(i - 1, 0)
    tok_spec = lambda w: pl.BlockSpec((GP_TOK, w), lambda i: (cur(i), 0))
    dir_spec = lambda w: pl.BlockSpec((2, GP_TOK, w), lambda i: (0, done(i), 0))
    stage = lambda *shape: pltpu.VMEM((2,) + shape, F32)
    return pl.pallas_call(
        functools.partial(_gdnprep_kernel, n_tiles=nt),
        grid=(nt + 1,),
        in_specs=[pl.BlockSpec((GP_TOK, PROJ_W // 2), lambda i: (cur(i), 1)),
                  pl.BlockSpec((HALO, PROJ_W // 2),
                               lambda i: (jnp.maximum(cur(i) * hb - 1, 0), 1)),
                  pl.BlockSpec((HALO, PROJ_W // 2),
                               lambda i: (jnp.minimum((cur(i) + 1) * hb, nhalo - 1), 1)),
                  tok_spec(LANES),
                  pl.BlockSpec((8, qkv_w), lambda i: (0, 0)),
                  pl.BlockSpec((1, LANES), lambda i: (0, 0)),
                  pl.BlockSpec((1, LANES), lambda i: (0, 0)),
                  pl.BlockSpec((None, 16, LANES), lambda i: (cur(i), 0, 0)),
                  pl.BlockSpec((3, GRID_W, LANES), lambda i: (0, 0, 0))],
        out_specs=[dir_spec(GDN_W), dir_spec(GDN_W), dir_spec(GDN_W), dir_spec(GDN_W),
                   dir_spec(GDN_HEADS * CHUNK),
                   pl.BlockSpec((GP_CH, 8, LANES), lambda i: (done(i), 0, 0))],
        out_shape=[jax.ShapeDtypeStruct((2, n, GDN_W), F32),
                   jax.ShapeDtypeStruct((2, n, GDN_W), BF16),
                   jax.ShapeDtypeStruct((2, n, GDN_W), BF16),
                   jax.ShapeDtypeStruct((2, n, GDN_W), BF16),
                   jax.ShapeDtypeStruct((2, n, GDN_HEADS * CHUNK), BF16),
                   jax.ShapeDtypeStruct((n // CHUNK, 8, LANES), F32)],
        scratch_shapes=[pltpu.VMEM((XE_ROWS, qkv_w), BF16),
                        pltpu.VMEM((CONV_K * CONV_ROWS, 2 * CONV_ROWS), BF16),
                        stage(GP_TOK, GDN_W), stage(GP_TOK, GDN_W), stage(GP_TOK, GDN_W),
                        stage(GP_TOK, LANES),
                        stage(2, GP_TOK, LANES),
                        stage(GP_CH, 2, LANES, LANES),
                        stage(2, GP_TOK, LANES),
                        stage(2, GP_TOK, LANES)],
        compiler_params=pltpu.CompilerParams(
            dimension_semantics=("arbitrary",), vmem_limit_bytes=VMEM_LIMIT),
        name="gdnprep",
    )(proj, proj, proj, gates, conv_w, alog, dtb, rowtab, coltab)


SC_CH_MAX = 8


def _scan_kernel(gl_ref, s0_ref, uf_ref, wf_ref, qdf_ref, kdf_ref, qkf_ref,
                 ub_ref, wb_ref, qdb_ref, kdb_ref, qkb_ref,
                 of_ref, ob_ref, sfin_ref, s_ref, *, SC_CH):
    n = pl.program_id(0)
    nsteps = pl.num_programs(0)
    nch = nsteps * SC_CH

    @pl.when(n == 0)
    def _():
        s_ref[...] = s0_ref[...]

    streams = ((uf_ref, wf_ref, qdf_ref, kdf_ref, qkf_ref, of_ref),
               (ub_ref, wb_ref, qdb_ref, kdb_ref, qkb_ref, ob_ref))
    lo_wide = lax.broadcasted_iota(jnp.int32, (1, 2 * LANES), 1) < LANES
    chains = [(d, h) for d in range(2) for h in range(GDN_HEADS)]
    hl = lambda h: slice(h * LANES, (h + 1) * LANES)
    for j in range(SC_CH):
        local = (j, SC_CH - 1 - j)
        chunk = (n * SC_CH + j, (nsteps - 1 - n) * SC_CH + SC_CH - 1 - j)
        tok = [slice(c * CHUNK, (c + 1) * CHUNK) for c in local]
        s32 = [s_ref[d, h] for d, h in chains]
        s16 = [s.astype(BF16) for s in s32]
        wq = [_dot(jnp.concatenate([streams[d][1][tok[d], hl(h)], streams[d][2][tok[d], hl(h)]],
                                   axis=0), s)
              for (d, h), s in zip(chains, s16)]
        v16 = [(streams[d][0][tok[d], hl(h)] - x[0:CHUNK]).astype(BF16)
               for (d, h), x in zip(chains, wq)]
        upd = [_dot_tn(streams[d][3][tok[d], hl(h)], v) for (d, h), v in zip(chains, v16)]
        for (d, h), s, x in zip(chains, s32, upd):
            decay = gl_ref[(d * nch + chunk[d]) * GDN_HEADS + h]
            s_ref[d, h] = s * decay + x
        inter = [x[CHUNK:2 * CHUNK] for x in wq]
        for d in range(2):
            qk_ref, o_ref = streams[d][4], streams[d][5]
            for p in range(GDN_HEADS // 2):
                v_pair = jnp.concatenate(
                    v16[d * GDN_HEADS + 2 * p:d * GDN_HEADS + 2 * p + 2], axis=1)
                zero = jnp.zeros_like(v_pair)
                v_bd = jnp.concatenate([jnp.where(lo_wide, v_pair, zero),
                                        jnp.where(lo_wide, zero, v_pair)], axis=0)
                o_intra = _dot(qk_ref[tok[d], p * LANES:(p + 1) * LANES], v_bd)
                for jh in range(2):
                    h = 2 * p + jh
                    o_ref[tok[d], hl(h)] = (inter[d * GDN_HEADS + h]
                                            + o_intra[:, jh * LANES:(jh + 1) * LANES])

    @pl.when(n == nsteps - 1)
    def _():
        sfin_ref[...] = s_ref[...]


def _scan(gl, s0, u, w, qd, kd, qk):
    n = u.shape[1]
    sc_ch = min(SC_CH_MAX, n // CHUNK)
    tok = sc_ch * CHUNK
    nsteps = n // tok
    fwd = lambda wd: pl.BlockSpec((None, tok, wd), lambda i: (0, i, 0))
    bwd = lambda wd: pl.BlockSpec((None, tok, wd), lambda i: (1, nsteps - 1 - i, 0))
    qkw = GDN_HEADS * CHUNK
    state_spec = pl.BlockSpec((2, GDN_HEADS, GDN_DK, GDN_DV), lambda i: (0, 0, 0, 0))
    return pl.pallas_call(
        functools.partial(_scan_kernel, SC_CH=sc_ch),
        grid=(nsteps,),
        in_specs=[pl.BlockSpec(memory_space=pltpu.SMEM), state_spec,
                  fwd(GDN_W), fwd(GDN_W), fwd(GDN_W), fwd(GDN_W), fwd(qkw),
                  bwd(GDN_W), bwd(GDN_W), bwd(GDN_W), bwd(GDN_W), bwd(qkw)],
        out_specs=[pl.BlockSpec((tok, GDN_W), lambda i: (i, 0)),
                   pl.BlockSpec((tok, GDN_W), lambda i: (nsteps - 1 - i, 0)),
                   state_spec],
        out_shape=[jax.ShapeDtypeStruct((n, GDN_W), F32),
                   jax.ShapeDtypeStruct((n, GDN_W), F32),
                   jax.ShapeDtypeStruct((2, GDN_HEADS, GDN_DK, GDN_DV), F32)],
        scratch_shapes=[pltpu.VMEM((2, GDN_HEADS, GDN_DK, GDN_DV), F32)],
        compiler_params=pltpu.CompilerParams(dimension_semantics=("arbitrary",)),
        name="scan",
    )(gl, s0, u, w, qd, kd, qk, u, w, qd, kd, qk)


def _outproj_kernel(x_ref, na_ref, of_ref, ob_ref, gz_ref, gnw_ref, wout_ref, gpost_ref, mod_ref,
                    o_ref):
    og = of_ref[...] + ob_ref[...]
    gz = gz_ref[...].astype(F32)
    gnw = gnw_ref[...]
    parts = []
    for h in range(GDN_HEADS):
        seg = og[:, h * LANES:(h + 1) * LANES]
        seg = seg * lax.rsqrt(jnp.mean(seg * seg, axis=-1, keepdims=True) + EPS)
        parts.append(seg * gnw)
    gd = (jnp.concatenate(parts, axis=-1) * _silu(gz)).astype(BF16)
    y = _dot(na_ref[...], wout_ref[0:NA_W, :]) + _dot(gd, wout_ref[NA_W:NA_W + GDN_W, :])
    yn = y * lax.rsqrt(jnp.mean(y * y, axis=-1, keepdims=True) + EPS)
    gate = mod_ref[0:1, 2 * D_MODEL:3 * D_MODEL]
    o_ref[...] = x_ref[...] + gate * (yn * gpost_ref[...])


def _outproj(x2, na, o_f, o_b, proj, gnw, w_out, gpost, mod, tm=512):
    n = x2.shape[0]
    gz_col = (PROJ_W - GDN_W) // GDN_W
    return pl.pallas_call(
        _outproj_kernel,
        grid=(n // tm,),
        in_specs=[pl.BlockSpec((tm, D_MODEL), lambda i: (i, 0)),
                  pl.BlockSpec((tm, NA_W), lambda i: (i, 0)),
                  pl.BlockSpec((tm, GDN_W), lambda i: (i, 0)),
                  pl.BlockSpec((tm, GDN_W), lambda i: (i, 0)),
                  pl.BlockSpec((tm, GDN_W), lambda i: (i, gz_col)),
                  pl.BlockSpec((1, LANES), lambda i: (0, 0)),
                  pl.BlockSpec((NA_W + GDN_W, D_MODEL), lambda i: (0, 0)),
                  pl.BlockSpec((1, D_MODEL), lambda i: (0, 0)),
                  pl.BlockSpec((8, 3 * D_MODEL), lambda i: (0, 0))],
        out_specs=pl.BlockSpec((tm, D_MODEL), lambda i: (i, 0)),
        out_shape=jax.ShapeDtypeStruct((n, D_MODEL), F32),
        compiler_params=pltpu.CompilerParams(
            dimension_semantics=("arbitrary",), vmem_limit_bytes=VMEM_LIMIT),
        name="outproj",
    )(x2, na, o_f, o_b, proj, gnw, w_out, gpost, mod)


def _rope_tables(rows, identity):
    inv_freq = ROPE_BASE ** (-jnp.arange(0, ROPE_AXIS_DIM, 2, dtype=F32) / ROPE_AXIS_DIM)

    def tables(count):
        ang = jnp.arange(count, dtype=F32)[:, None] * inv_freq[None, :]
        if identity:
            ang = jnp.zeros_like(ang)
        c, s = jnp.cos(ang), jnp.sin(ang)
        z = jnp.zeros_like(s)
        return (jnp.concatenate([c, c], -1), jnp.concatenate([-s, z], -1),
                jnp.concatenate([z, s], -1))

    half = jnp.zeros((rows, LANES // 2), F32)
    rowtab = jnp.stack([jnp.concatenate([t, half], -1) for t in tables(rows)], axis=0)
    rowtab = rowtab.reshape(3, rows // GP_CH, GP_CH, LANES).transpose(1, 0, 2, 3)
    rowtab = rowtab.reshape(rows // GP_CH, 3 * GP_CH, LANES)
    rowtab = jnp.pad(rowtab, ((0, 0), (0, 16 - 3 * GP_CH), (0, 0)))
    halfc = jnp.zeros((GRID_W, LANES // 2), F32)
    coltab = jnp.stack([jnp.concatenate([halfc, t], -1) for t in tables(GRID_W)], axis=0)
    return rowtab, coltab


def _lane_row(vals, offset):
    return jnp.zeros((1, LANES), F32).at[0, offset:offset + vals.shape[0]].set(vals)


def kernel(x, c, ctx, c_ctx, w_ada, b_ada, g_pre, g_post, w_in, conv_w, rpb, A_log, dt_bias,
           gdn_norm_w, w_out):
    n = x.shape[1]
    nc = ctx.shape[1]
    x2 = x[0]
    xc2 = ctx[0]

    cc = jnp.zeros((8, D_MODEL), F32).at[0].set(c[0]).at[1].set(c_ctx)
    mod = _ada(cc, w_ada[0], b_ada[0][None, :])

    gpre = g_pre[0][None, :]
    w_in_t = jnp.swapaxes(w_in, 1, 2)
    proj, gates, projc, gatesc = _inproj(x2, xc2, mod, gpre, w_in_t, tm=512)

    na = _natten(proj, projc, rpb[0].reshape(-1))

    cw = jnp.pad(conv_w[0], ((0, 8 - CONV_K), (0, 0)))
    alog = _lane_row(A_log[0].reshape(-1), 2 * GDN_HEADS)
    dtb = _lane_row(dt_bias[0].reshape(-1), 2 * GDN_HEADS)
    rowtab, coltab = _rope_tables(n // GRID_W, identity=False)
    rowtab_c, coltab_c = _rope_tables(nc // GRID_W, identity=True)

    def gl_rows(gl):
        return jnp.stack([gl[:, 0, 2 * GDN_HEADS:3 * GDN_HEADS],
                          gl[:, 1, 3 * GDN_HEADS:4 * GDN_HEADS]], axis=0).reshape(-1)

    uc, wc, qdc, kdc, qkc, glc = _gdnprep(projc, gatesc, cw, alog, dtb, rowtab_c, coltab_c)
    s0 = jnp.zeros((2, GDN_HEADS, GDN_DK, GDN_DV), F32)
    _, _, s_ctx = _scan(gl_rows(glc), s0, uc, wc, qdc, kdc, qkc)

    u, w, qd, kd, qk, gl = _gdnprep(proj, gates, cw, alog, dtb, rowtab, coltab)
    o_f, o_b, _ = _scan(gl_rows(gl), s_ctx, u, w, qd, kd, qk)

    gnw = gdn_norm_w[0][None, :]
    out = _outproj(x2, na, o_f, o_b, proj, gnw, w_out[0].astype(BF16), g_post[0][None, :], mod)
    return out[None]
```

```python
import functools
import math

import jax
import jax.numpy as jnp
from jax import lax
from jax.experimental import pallas as pl
from jax.experimental.pallas import tpu as pltpu

F32 = jnp.float32
BF16 = jnp.bfloat16

D_MODEL = 1024
GRID_W = 64
NA_HEADS = 8
NA_DH = 64
NA_W = NA_HEADS * NA_DH
NA_KH = 8
NA_KW = 16
GDN_HEADS = 4
GDN_DK = 128
GDN_DV = 128
GDN_W = GDN_HEADS * GDN_DV
CHUNK = 64
CONV_K = 5
ROPE_AXIS_DIM = GDN_DK // 2
ROPE_BASE = 10000.0
EPS = 1e-6
PROJ_W = 4 * NA_W + 3 * GDN_W + GDN_W
GATE_W = 2 * 2 * GDN_HEADS
LANES = 128
NEG = -1e30

VMEM_LIMIT = 56 * 1024 * 1024


def _silu(x):
    h = 0.5 * x
    return h + h * jnp.tanh(h)


def _dot(a, b):
    return jnp.dot(a, b, preferred_element_type=F32)


def _dot_nt(a, b):
    return lax.dot_general(a, b, (((1,), (1,)), ((), ())), preferred_element_type=F32)


def _dot_tn(a, b):
    return lax.dot_general(a, b, (((0,), (0,)), ((), ())), preferred_element_type=F32)


def _split2(x):
    hi = x.astype(BF16)
    lo = (x - hi.astype(F32)).astype(BF16)
    return hi, lo


def _split3(x):
    hi = x.astype(BF16)
    r = x - hi.astype(F32)
    mid = r.astype(BF16)
    lo = (r - mid.astype(F32)).astype(BF16)
    return hi, mid, lo


def _mm3(a, b):
    ah, al = _split2(a)
    bh, bl = _split2(b)
    return _dot(ah, bh) + (_dot(al, bh) + _dot(ah, bl))


def _ada_kernel(c_ref, w_ref, b_ref, o_ref):
    s = _silu(c_ref[...])
    o_ref[...] = _mm3(s, w_ref[...]) + b_ref[...]


def _ada(cc, w_ada, b_ada):
    tn = 512
    n = w_ada.shape[1]
    return pl.pallas_call(
        _ada_kernel,
        grid=(n // tn,),
        in_specs=[pl.BlockSpec((8, D_MODEL), lambda j: (0, 0)),
                  pl.BlockSpec((D_MODEL, tn), lambda j: (0, j)),
                  pl.BlockSpec((1, tn), lambda j: (0, j))],
        out_specs=pl.BlockSpec((8, tn), lambda j: (0, j)),
        out_shape=jax.ShapeDtypeStruct((8, n), F32),
        name="ada",
    )(cc, w_ada, b_ada)


LOG2E = math.log2(math.e)
NA_QSCALE = NA_DH ** -0.5 * LOG2E


def _inproj_kernel(x_ref, xc_ref, mod_ref, gpre_ref, win_ref,
                   proj_ref, gates_ref, projc_ref, gatesc_ref, w_ref):
    def project(x, row, proj_out, gates_out):
        xn = x * lax.rsqrt(jnp.mean(x * x, axis=-1, keepdims=True) + EPS)
        shift = mod_ref[row:row + 1, 0:D_MODEL]
        scale = mod_ref[row:row + 1, D_MODEL:2 * D_MODEL]
        h = (xn * gpre_ref[...]) * (1.0 + scale) + shift
        hb = h.astype(BF16)
        nb = 512
        for j in range(PROJ_W // nb):
            y = _dot_nt(hb, w_ref[j * nb:(j + 1) * nb, :])
            if j * nb < NA_W:
                y = y * NA_QSCALE
            proj_out[:, j * nb:(j + 1) * nb] = y.astype(BF16)
        gates_out[...] = _dot_nt(hb, w_ref[PROJ_W:PROJ_W + LANES, :])

    @pl.when(pl.program_id(0) == 0)
    def _():
        nb = 512
        for j in range(PROJ_W // nb):
            w_ref[j * nb:(j + 1) * nb, :] = win_ref[j * nb:(j + 1) * nb, :].astype(BF16)
        w_ref[PROJ_W:PROJ_W + LANES, :] = jnp.zeros((LANES, D_MODEL), BF16)
        w_ref[PROJ_W:PROJ_W + GATE_W, :] = win_ref[PROJ_W:PROJ_W + GATE_W, :].astype(BF16)
        project(xc_ref[...], 1, projc_ref, gatesc_ref)

    project(x_ref[...], 0, proj_ref, gates_ref)


def _inproj(x2, xc2, mod, gpre, w_in, tm):
    n = x2.shape[0]
    nc = xc2.shape[0]
    return pl.pallas_call(
        _inproj_kernel,
        grid=(n // tm,),
        in_specs=[pl.BlockSpec((tm, D_MODEL), lambda i: (i, 0)),
                  pl.BlockSpec((nc, D_MODEL), lambda i: (0, 0)),
                  pl.BlockSpec((8, 3 * D_MODEL), lambda i: (0, 0)),
                  pl.BlockSpec((1, D_MODEL), lambda i: (0, 0)),
                  pl.BlockSpec((None, PROJ_W + GATE_W, D_MODEL), lambda i: (0, 0, 0),
                               pipeline_mode=pl.Buffered(1))],
        out_specs=[pl.BlockSpec((tm, PROJ_W), lambda i: (i, 0)),
                   pl.BlockSpec((tm, LANES), lambda i: (i, 0)),
                   pl.BlockSpec((nc, PROJ_W), lambda i: (0, 0)),
                   pl.BlockSpec((nc, LANES), lambda i: (0, 0))],
        out_shape=[jax.ShapeDtypeStruct((n, PROJ_W), BF16),
                   jax.ShapeDtypeStruct((n, LANES), F32),
                   jax.ShapeDtypeStruct((nc, PROJ_W), BF16),
                   jax.ShapeDtypeStruct((nc, LANES), F32)],
        scratch_shapes=[pltpu.VMEM((PROJ_W + LANES, D_MODEL), BF16)],
        compiler_params=pltpu.CompilerParams(
            dimension_semantics=("arbitrary",), vmem_limit_bytes=VMEM_LIMIT),
        name="inproj",
    )(x2, xc2, mod, gpre, w_in)


NA_RO = 2 * NA_KH - 1
NA_CO = 2 * NA_KW - 1
NA_ROWS_PER_ITER = 4


def _na_build_tables(rpb_ref, tab_ref, hp):
    qc = lax.broadcasted_iota(jnp.int32, (GRID_W, LANES), 0)
    kcol = lax.broadcasted_iota(jnp.int32, (GRID_W, LANES), 1) % GRID_W
    lo = lax.broadcasted_iota(jnp.int32, (1, LANES), 1) < GRID_W
    diff = kcol - qc + (NA_KW - 1)
    col_start = jnp.clip(qc - NA_KW // 2, 0, GRID_W - NA_KW)
    inwin = (kcol >= col_start) & (kcol < col_start + NA_KW)
    for hh in range(2):
        def ro_body(ro, carry):
            base = ((hp * 2 + hh) * NA_RO + ro) * NA_CO
            acc = jnp.zeros((GRID_W, LANES), F32)
            for j in range(NA_CO):
                val = jnp.where(lo, rpb_ref[base + j], rpb_ref[base + NA_CO + j])
                acc = jnp.where(diff == j, val, acc)
            tab_ref[hh, ro] = jnp.where(inwin, acc * LOG2E, NEG)
            return carry
        lax.fori_loop(0, NA_RO - 1, ro_body, 0)


def _na_kernel(rpb_ref, q_ref, k_ref, v_ref, z_ref, kc_ref, vc_ref, o_ref, tab_ref,
               *, rb_rows, rows):
    hp = pl.program_id(0)
    rb = pl.program_id(1)

    @pl.when(rb == 0)
    def _():
        _na_build_tables(rpb_ref, tab_ref, hp)

    lane = lax.broadcasted_iota(jnp.int32, (1, LANES), 1)
    first = lane < NA_DH
    kc = kc_ref[...]
    vc = vc_ref[...]
    win = NA_KH * GRID_W

    def body(it, carry):
        items = []
        for rr in range(NA_ROWS_PER_ITER):
            i = it * NA_ROWS_PER_ITER + rr
            r = rb * rb_rows + i
            rs = jnp.clip(r - NA_KH // 2, 0, rows - NA_KH)
            ro0 = NA_KH - 1 - (r - rs)
            t0 = pl.multiple_of(i * GRID_W, GRID_W)
            q = q_ref[pl.ds(t0, GRID_W), :]
            k0 = pl.multiple_of(rs * GRID_W, GRID_W)
            kw = k_ref[pl.ds(k0, win), :]
            vw = v_ref[pl.ds(k0, win), :]
            zq = jnp.zeros_like(q)
            qs = jnp.concatenate([jnp.where(first, q, zq), jnp.where(first, zq, q)], axis=0)
            items.append((t0, ro0, qs, kw, vw))
        s_loc = [_dot_nt(qs, kw)
                 + jnp.concatenate(
                     [jnp.concatenate([tab_ref[hh, ro0 + 2 * m] for m in range(NA_KH // 2)], axis=1)
                      for hh in range(2)], axis=0)
                 for (_, ro0, qs, kw, _) in items]
        s_ctx = [_dot_nt(qs, kc) for (_, _, qs, _, _) in items]
        mx = [jnp.maximum(jnp.max(a, axis=-1, keepdims=True), jnp.max(b, axis=-1, keepdims=True))
              for a, b in zip(s_loc, s_ctx)]
        p_loc = [jnp.exp2(a - m) for a, m in zip(s_loc, mx)]
        p_ctx = [jnp.exp2(b - m) for b, m in zip(s_ctx, mx)]
        inv = [1.0 / (jnp.sum(a, axis=-1, keepdims=True) + jnp.sum(b, axis=-1, keepdims=True))
               for a, b in zip(p_loc, p_ctx)]
        outs = [(_dot(a.astype(BF16), it_[4]) + _dot(b.astype(BF16), vc)) * il
                for a, b, il, it_ in zip(p_loc, p_ctx, inv, items)]
        for (t0, _, _, _, _), o2 in zip(items, outs):
            o = jnp.where(first, o2[0:GRID_W], o2[GRID_W:2 * GRID_W])
            z = z_ref[pl.ds(t0, GRID_W), :].astype(F32)
            o_ref[pl.ds(t0, GRID_W), :] = (o * _silu(z)).astype(BF16)
        return carry

    lax.fori_loop(0, rb_rows // NA_ROWS_PER_ITER, body, 0)


def _natten(proj, projc, rpb_flat, rb_rows=32):
    n = proj.shape[0]
    nc = projc.shape[0]
    rows = n // GRID_W
    tq = rb_rows * GRID_W
    kcol = NA_W // LANES
    return pl.pallas_call(
        functools.partial(_na_kernel, rb_rows=rb_rows, rows=rows),
        grid=(NA_W // LANES, rows // rb_rows),
        in_specs=[pl.BlockSpec(memory_space=pltpu.SMEM),
                  pl.BlockSpec((tq, LANES), lambda hp, rb: (rb, hp)),
                  pl.BlockSpec((n, LANES), lambda hp, rb: (0, kcol + hp)),
                  pl.BlockSpec((n, LANES), lambda hp, rb: (0, 2 * kcol + hp)),
                  pl.BlockSpec((tq, LANES), lambda hp, rb: (rb, 3 * kcol + hp)),
                  pl.BlockSpec((nc, LANES), lambda hp, rb: (0, kcol + hp)),
                  pl.BlockSpec((nc, LANES), lambda hp, rb: (0, 2 * kcol + hp))],
        out_specs=pl.BlockSpec((tq, LANES), lambda hp, rb: (rb, hp)),
        out_shape=jax.ShapeDtypeStruct((n, NA_W), BF16),
        scratch_shapes=[pltpu.VMEM((2, NA_RO - 1, GRID_W, LANES), F32)],
        compiler_params=pltpu.CompilerParams(
            dimension_semantics=("arbitrary", "arbitrary"), vmem_limit_bytes=VMEM_LIMIT),
        name="natten",
    )(rpb_flat, proj, proj, proj, proj, projc, projc)


GP_TOK = 256
GP_CH = GP_TOK // CHUNK
GP_LOCK = 2
INV_BASE = 8
HALO = 16
CONV_ROWS = 128
XE_ROWS = GP_TOK + CONV_ROWS


def _gdnprep_kernel(main_ref, left_ref, right_ref, gates_ref, cw_ref, alog_ref, dtb_ref,
                    rowtab_ref, coltab_ref,
                    u_ref, w_ref, qd_ref, kd_ref, qk_ref, gl_ref,
                    xe_ref, shift_ref, qn_ref, kn_ref, vv_ref, beta_ref, gc_ref, gct_ref, eg_ref,
                    ek_ref, *, n_tiles):
    i = pl.program_id(0)
    tile = jnp.minimum(i, n_tiles - 1)
    qkv_w = 3 * GDN_W
    staged = (qn_ref, kn_ref, vv_ref, beta_ref, gc_ref, gct_ref, eg_ref, ek_ref)

    @pl.when(i == 0)
    def _():
        t = lax.broadcasted_iota(jnp.int32, (CONV_K * CONV_ROWS, 2 * CONV_ROWS), 0)
        r = lax.broadcasted_iota(jnp.int32, (CONV_K * CONV_ROWS, 2 * CONV_ROWS), 1)
        hit = r == (t % CONV_ROWS) + (t // CONV_ROWS) + (HALO - CONV_K // 2)
        shift_ref[...] = jnp.where(hit, 1.0, 0.0).astype(BF16)
        xe_ref[HALO + GP_TOK + HALO:, :] = jnp.zeros(
            (XE_ROWS - GP_TOK - 2 * HALO, qkv_w), BF16)
        for ref in staged:
            ref[1] = jnp.zeros(ref.shape[1:], ref.dtype)

    ii = lax.broadcasted_iota(jnp.int32, (CHUNK, LANES), 0)
    jj = lax.broadcasted_iota(jnp.int32, (CHUNK, LANES), 1) % CHUNK
    lo = lax.broadcasted_iota(jnp.int32, (1, LANES), 1) < CHUNK
    lo_wide = lax.broadcasted_iota(jnp.int32, (1, 2 * LANES), 1) < LANES
    eye = jnp.where(ii == jj, 1.0, 0.0).astype(F32)
    bodies = [(d, p) for d in range(2) for p in range(GDN_HEADS // 2)]
    same_block = lambda size: (ii // size) == (jj // size)
    base_blocks = same_block(INV_BASE)
    merge_masks = []
    size = INV_BASE
    while size < CHUNK:
        merge_masks.append(same_block(2 * size) & jnp.logical_not(same_block(size)))
        size *= 2

    def blockdiag(y, first):
        z = jnp.zeros_like(y)
        return jnp.concatenate([jnp.where(first, y, z), jnp.where(first, z, y)], axis=0)

    def prepare(slot):
        lh = left_ref[:, 0:qkv_w]
        rh = right_ref[:, 0:qkv_w]
        xe_ref[0:HALO, :] = jnp.where(tile > 0, lh, jnp.zeros_like(lh))
        xe_ref[HALO:HALO + GP_TOK, :] = main_ref[:, 0:qkv_w]
        xe_ref[HALO + GP_TOK:HALO + GP_TOK + HALO, :] = jnp.where(
            tile < n_tiles - 1, rh, jnp.zeros_like(rh))

        row_lanes = lax.broadcasted_iota(jnp.int32, (1, LANES), 1) < ROPE_AXIS_DIM

        def rope_table(kind):
            return jnp.concatenate(
                [jnp.where(row_lanes, rowtab_ref[kind * GP_CH + c:kind * GP_CH + c + 1, :],
                           coltab_ref[kind]) for c in range(GP_CH)], axis=0)

        cos, sina, sinb = rope_table(0), rope_table(1), rope_table(2)
        pending = {}
        for cb, b in [(cb, b) for cb in range(qkv_w // LANES) for b in range(GP_TOK // CONV_ROWS)]:
            cols = slice(cb * LANES, (cb + 1) * LANES)
            rows = slice(b * CONV_ROWS, (b + 1) * CONV_ROWS)
            if cb % 2 == 0:
                cols2 = slice(cb * LANES, (cb + 2) * LANES)
                taps2 = _dot(shift_ref[...], xe_ref[b * CONV_ROWS:(b + 2) * CONV_ROWS, cols2])
                acc2 = None
                for j in range(CONV_K):
                    term = cw_ref[j:j + 1, cols2] * taps2[j * CONV_ROWS:(j + 1) * CONV_ROWS]
                    acc2 = term if acc2 is None else acc2 + term
                pending[b] = _silu(acc2)
            y = pending[b][:, (cb % 2) * LANES:(cb % 2 + 1) * LANES]
            if cb < 2 * GDN_HEADS:
                y = y * lax.rsqrt(jnp.sum(y * y, axis=-1, keepdims=True) + EPS)
                y = (y * cos[rows] + pltpu.roll(y, LANES - ROPE_AXIS_DIM // 2, 1) * sina[rows]
                     + pltpu.roll(y, ROPE_AXIS_DIM // 2, 1) * sinb[rows])
                if cb < GDN_HEADS:
                    qn_ref[slot, rows, cols] = y * (GDN_DK ** -0.5)
                else:
                    kn_ref[slot, rows, (cb - GDN_HEADS) * LANES:(cb - GDN_HEADS + 1) * LANES] = y
            else:
                vv_ref[slot, rows,
                       (cb - 2 * GDN_HEADS) * LANES:(cb - 2 * GDN_HEADS + 1) * LANES] = y
            yield

        gates = gates_ref[...]
        beta_ref[slot] = jax.nn.sigmoid(gates)
        xa = gates + dtb_ref[...]
        softplus = jnp.maximum(xa, 0.0) + jnp.log1p(jnp.exp(-jnp.abs(xa)))
        g = -jnp.exp(alog_ref[...]) * softplus

        ti = lax.broadcasted_iota(jnp.int32, (GP_TOK, GP_TOK), 0)
        tj = lax.broadcasted_iota(jnp.int32, (GP_TOK, GP_TOK), 1)
        same = (ti // CHUNK) == (tj // CHUNK)
        lower = jnp.where(same & (ti >= tj), 1.0, 0.0).astype(BF16)
        upper = jnp.where(same & (ti <= tj), 1.0, 0.0).astype(BF16)
        g3 = _split3(g)
        gc_f = _dot(lower, g3[0]) + (_dot(lower, g3[1]) + _dot(lower, g3[2]))
        gc_b = _dot(upper, g3[0]) + (_dot(upper, g3[1]) + _dot(upper, g3[2]))
        gc_ref[slot, 0] = gc_f
        gc_ref[slot, 1] = gc_b
        yield
        for c in range(GP_CH):
            rows = slice(c * CHUNK, (c + 1) * CHUNK)
            for d, gc in enumerate((gc_f, gc_b)):
                blk = gc[rows]
                gct_ref[slot, c, d] = jnp.concatenate(
                    [blk, pltpu.roll(blk, LANES - 1, 1)], axis=0).T
            ek_ref[slot, 0, rows] = jnp.exp(gc_f[(c + 1) * CHUNK - 1:(c + 1) * CHUNK, :] - gc_f[rows])
            ek_ref[slot, 1, rows] = jnp.exp(gc_b[c * CHUNK:c * CHUNK + 1, :] - gc_b[rows])
            yield
        eg_ref[slot, 0] = jnp.exp(gc_f)
        eg_ref[slot, 1] = jnp.exp(gc_b)

    def solve(slot):
        gl_ref[...] = jnp.zeros_like(gl_ref)
        for c in range(GP_CH):
            gl_ref[c, 0:1, :] = eg_ref[slot, 0, (c + 1) * CHUNK - 1:(c + 1) * CHUNK, :]
            gl_ref[c, 1:2, :] = eg_ref[slot, 1, c * CHUNK:c * CHUNK + 1, :]
        for first_chunk in range(0, GP_CH, GP_LOCK):
            yield from solve_group(slot, range(first_chunk, first_chunk + GP_LOCK))

    def solve_group(slot, chunks):
        a_l, kbeg_l, vb_l, where_l = [], [], [], []
        for c, (d, p) in [(c, b) for c in chunks for b in bodies]:
            tok = slice(c * CHUNK, (c + 1) * CHUNK)
            incl = (ii >= jj) if d == 0 else (ii <= jj)
            strict = (ii > jj) if d == 0 else (ii < jj)
            pl2 = slice(2 * p * LANES, (2 * p + 2) * LANES)
            lb = d * GDN_HEADS + 2 * p
            lg = 2 * GDN_HEADS + lb
            q = qn_ref[slot, tok, pl2]
            k = kn_ref[slot, tok, pl2]
            v = vv_ref[slot, tok, pl2]
            beta = jnp.where(lo_wide, beta_ref[slot, tok, lb:lb + 1],
                             beta_ref[slot, tok, lb + 1:lb + 2])
            gcol = jnp.where(lo, gc_ref[slot, d, tok, lg:lg + 1], gc_ref[slot, d, tok, lg + 1:lg + 2])
            grow = gct_ref[slot, c, d, lg:lg + 1, :]
            eg = jnp.where(lo_wide, eg_ref[slot, d, tok, lg:lg + 1],
                           eg_ref[slot, d, tok, lg + 1:lg + 2])
            ek = jnp.where(lo_wide, ek_ref[slot, d, tok, lg:lg + 1],
                           ek_ref[slot, d, tok, lg + 1:lg + 2])
            dec = jnp.exp(jnp.where(incl, gcol - grow, NEG))
            kb = k * beta
            k_nt = blockdiag(k.astype(BF16), lo_wide)
            a = jnp.where(strict, _dot_nt(kb.astype(BF16), k_nt) * dec, 0.0)
            qk = _dot_nt(q.astype(BF16), k_nt) * dec
            qk_ref[d, tok, p * LANES:(p + 1) * LANES] = qk.astype(BF16)
            qd_ref[d, tok, pl2] = (q * eg).astype(BF16)
            kd_ref[d, tok, pl2] = (k * ek).astype(BF16)
            a_l.append(a)
            kbeg_l.append((kb * eg).astype(BF16))
            vb_l.append((v * beta).astype(BF16))
            where_l.append((d, tok, pl2))
            if p == GDN_HEADS // 2 - 1:
                yield
        x_l = [jnp.where(base_blocks, -a, 0.0) for a in a_l]
        t_l = [eye + x for x in x_l]
        x_l = [x.astype(BF16) for x in x_l]
        x_l = [_dot(x, blockdiag(x, lo)).astype(BF16) for x in x_l]
        yield
        r_l = [_dot(jnp.concatenate([x, t.astype(BF16)], axis=0), blockdiag(x, lo))
               for x, t in zip(x_l, t_l)]
        x_l = [r[0:CHUNK].astype(BF16) for r in r_l]
        t_l = [t + r[CHUNK:2 * CHUNK] for t, r in zip(t_l, r_l)]
        yield
        t_l = [t + _dot(t.astype(BF16), blockdiag(x, lo)) for t, x in zip(t_l, x_l)]
        yield
        for merged in merge_masks:
            t16_l = [t.astype(BF16) for t in t_l]
            te_l = [_dot(t16, blockdiag(jnp.where(merged, a, 0.0).astype(BF16), lo))
                    for t16, a in zip(t16_l, a_l)]
            yield
            t_l = [t - _dot(te.astype(BF16), blockdiag(t16, lo))
                   for t, te, t16 in zip(t_l, te_l, t16_l)]
            yield
        for n_done, ((d, tok, pl2), t, vb, kbeg) in enumerate(zip(where_l, t_l, vb_l, kbeg_l)):
            t16 = t.astype(BF16)
            u_ref[d, tok, pl2] = _dot(t16, blockdiag(vb, lo_wide))
            w_ref[d, tok, pl2] = _dot(t16, blockdiag(kbeg, lo_wide)).astype(BF16)
            if n_done % len(bodies) == len(bodies) - 1:
                yield

    def interleave(*stages):
        live = list(stages)
        while live:
            for g in list(live):
                if next(g, live) is live:
                    live.remove(g)

    for parity in range(2):
        @pl.when(i % 2 == parity)
        def _():
            interleave(solve(1 - parity), prepare(parity))


def _gdnprep(proj, gates, conv_w, alog, dtb, rowtab, coltab):
    n = proj.shape[0]
    nt = n // GP_TOK
    hb = GP_TOK // HALO
    nhalo = n // HALO
    qkv_w = 3 * GDN_W
    cur = lambda i: jnp.minimum(i, nt - 1)
    done = lambda i: jnp.maximum(i - 1, 0)
    tok_spec = lambda w: pl.BlockSpec((GP_TOK, w), lambda i: (cur(i), 0))
    dir_spec = lambda w: pl.BlockSpec((2, GP_TOK, w), lambda i: (0, done(i), 0))
    stage = lambda *shape: pltpu.VMEM((2,) + shape, F32)
    return pl.pallas_call(
        functools.partial(_gdnprep_kernel, n_tiles=nt),
        grid=(nt + 1,),
        in_specs=[pl.BlockSpec((GP_TOK, PROJ_W // 2), lambda i: (cur(i), 1)),
                  pl.BlockSpec((HALO, PROJ_W // 2),
                               lambda i: (jnp.maximum(cur(i) * hb - 1, 0), 1)),
                  pl.BlockSpec((HALO, PROJ_W // 2),
                               lambda i: (jnp.minimum((cur(i) + 1) * hb, nhalo - 1), 1)),
                  tok_spec(LANES),
                  pl.BlockSpec((8, qkv_w), lambda i: (0, 0)),
                  pl.BlockSpec((1, LANES), lambda i: (0, 0)),
                  pl.BlockSpec((1, LANES), lambda i: (0, 0)),
                  pl.BlockSpec((None, 16, LANES), lambda i: (cur(i), 0, 0)),
                  pl.BlockSpec((3, GRID_W, LANES), lambda i: (0, 0, 0))],
        out_specs=[dir_spec(GDN_W), dir_spec(GDN_W), dir_spec(GDN_W), dir_spec(GDN_W),
                   dir_spec(GDN_HEADS * CHUNK),
                   pl.BlockSpec((GP_CH, 8, LANES), lambda i: (done(i), 0, 0))],
        out_shape=[jax.ShapeDtypeStruct((2, n, GDN_W), F32),
                   jax.ShapeDtypeStruct((2, n, GDN_W), BF16),
                   jax.ShapeDtypeStruct((2, n, GDN_W), BF16),
                   jax.ShapeDtypeStruct((2, n, GDN_W), BF16),
                   jax.ShapeDtypeStruct((2, n, GDN_HEADS * CHUNK), BF16),
                   jax.ShapeDtypeStruct((n // CHUNK, 8, LANES), F32)],
        scratch_shapes=[pltpu.VMEM((XE_ROWS, qkv_w), BF16),
                        pltpu.VMEM((CONV_K * CONV_ROWS, 2 * CONV_ROWS), BF16),
                        stage(GP_TOK, GDN_W), stage(GP_TOK, GDN_W), stage(GP_TOK, GDN_W),
                        stage(GP_TOK, LANES),
                        stage(2, GP_TOK, LANES),
                        stage(GP_CH, 2, LANES, LANES),
                        stage(2, GP_TOK, LANES),
                        stage(2, GP_TOK, LANES)],
        compiler_params=pltpu.CompilerParams(
            dimension_semantics=("arbitrary",), vmem_limit_bytes=VMEM_LIMIT),
        name="gdnprep",
    )(proj, proj, proj, gates, conv_w, alog, dtb, rowtab, coltab)


SC_CH_MAX = 16


def _scan_kernel(gl_ref, s0_ref, uf_ref, wf_ref, qdf_ref, kdf_ref, qkf_ref,
                 ub_ref, wb_ref, qdb_ref, kdb_ref, qkb_ref,
                 of_ref, ob_ref, sfin_ref, s_ref, *, SC_CH):
    n = pl.program_id(0)
    nsteps = pl.num_programs(0)
    nch = nsteps * SC_CH

    @pl.when(n == 0)
    def _():
        s_ref[...] = s0_ref[...]

    streams = ((uf_ref, wf_ref, qdf_ref, kdf_ref, qkf_ref, of_ref),
               (ub_ref, wb_ref, qdb_ref, kdb_ref, qkb_ref, ob_ref))
    lo_wide = lax.broadcasted_iota(jnp.int32, (1, 2 * LANES), 1) < LANES
    chains = [(d, h) for d in range(2) for h in range(GDN_HEADS)]
    hl = lambda h: slice(h * LANES, (h + 1) * LANES)
    for j in range(SC_CH):
        local = (j, SC_CH - 1 - j)
        chunk = (n * SC_CH + j, (nsteps - 1 - n) * SC_CH + SC_CH - 1 - j)
        tok = [slice(c * CHUNK, (c + 1) * CHUNK) for c in local]
        s32 = [s_ref[d, h] for d, h in chains]
        s16 = [s.astype(BF16) for s in s32]
        wq = [_dot(jnp.concatenate([streams[d][1][tok[d], hl(h)], streams[d][2][tok[d], hl(h)]],
                                   axis=0), s)
              for (d, h), s in zip(chains, s16)]
        v16 = [(streams[d][0][tok[d], hl(h)] - x[0:CHUNK]).astype(BF16)
               for (d, h), x in zip(chains, wq)]
        upd = [_dot_tn(streams[d][3][tok[d], hl(h)], v) for (d, h), v in zip(chains, v16)]
        for (d, h), s, x in zip(chains, s32, upd):
            decay = gl_ref[(d * nch + chunk[d]) * GDN_HEADS + h]
            s_ref[d, h] = s * decay + x
        inter = [x[CHUNK:2 * CHUNK] for x in wq]
        for d in range(2):
            qk_ref, o_ref = streams[d][4], streams[d][5]
            for p in range(GDN_HEADS // 2):
                v_pair = jnp.concatenate(
                    v16[d * GDN_HEADS + 2 * p:d * GDN_HEADS + 2 * p + 2], axis=1)
                zero = jnp.zeros_like(v_pair)
                v_bd = jnp.concatenate([jnp.where(lo_wide, v_pair, zero),
                                        jnp.where(lo_wide, zero, v_pair)], axis=0)
                o_intra = _dot(qk_ref[tok[d], p * LANES:(p + 1) * LANES], v_bd)
                for jh in range(2):
                    h = 2 * p + jh
                    o_ref[tok[d], hl(h)] = (inter[d * GDN_HEADS + h]
                                            + o_intra[:, jh * LANES:(jh + 1) * LANES]
                                            ).astype(o_ref.dtype)

    @pl.when(n == nsteps - 1)
    def _():
        sfin_ref[...] = s_ref[...]


def _scan(gl, s0, u, w, qd, kd, qk):
    n = u.shape[1]
    sc_ch = min(SC_CH_MAX, n // CHUNK)
    tok = sc_ch * CHUNK
    nsteps = n // tok
    fwd = lambda wd: pl.BlockSpec((None, tok, wd), lambda i: (0, i, 0))
    bwd = lambda wd: pl.BlockSpec((None, tok, wd), lambda i: (1, nsteps - 1 - i, 0))
    qkw = GDN_HEADS * CHUNK
    state_spec = pl.BlockSpec((2, GDN_HEADS, GDN_DK, GDN_DV), lambda i: (0, 0, 0, 0))
    return pl.pallas_call(
        functools.partial(_scan_kernel, SC_CH=sc_ch),
        grid=(nsteps,),
        in_specs=[pl.BlockSpec(memory_space=pltpu.SMEM), state_spec,
                  fwd(GDN_W), fwd(GDN_W), fwd(GDN_W), fwd(GDN_W), fwd(qkw),
                  bwd(GDN_W), bwd(GDN_W), bwd(GDN_W), bwd(GDN_W), bwd(qkw)],
        out_specs=[pl.BlockSpec((tok, GDN_W), lambda i: (i, 0)),
                   pl.BlockSpec((tok, GDN_W), lambda i: (nsteps - 1 - i, 0)),
                   state_spec],
        out_shape=[jax.ShapeDtypeStruct((n, GDN_W), BF16),
                   jax.ShapeDtypeStruct((n, GDN_W), BF16),
                   jax.ShapeDtypeStruct((2, GDN_HEADS, GDN_DK, GDN_DV), F32)],
        scratch_shapes=[pltpu.VMEM((2, GDN_HEADS, GDN_DK, GDN_DV), F32)],
        compiler_params=pltpu.CompilerParams(dimension_semantics=("arbitrary",)),
        name="scan",
    )(gl, s0, u, w, qd, kd, qk, u, w, qd, kd, qk)


def _outproj_kernel(x_ref, na_ref, of_ref, ob_ref, gz_ref, gnw_ref, wout_ref, gpost_ref, mod_ref,
                    o_ref):
    og = of_ref[...].astype(F32) + ob_ref[...].astype(F32)
    gz = gz_ref[...].astype(F32)
    gnw = gnw_ref[...]
    parts = []
    for h in range(GDN_HEADS):
        seg = og[:, h * LANES:(h + 1) * LANES]
        seg = seg * lax.rsqrt(jnp.mean(seg * seg, axis=-1, keepdims=True) + EPS)
        parts.append(seg * gnw)
    gd = (jnp.concatenate(parts, axis=-1) * _silu(gz)).astype(BF16)
    y = _dot(na_ref[...], wout_ref[0:NA_W, :]) + _dot(gd, wout_ref[NA_W:NA_W + GDN_W, :])
    yn = y * lax.rsqrt(jnp.mean(y * y, axis=-1, keepdims=True) + EPS)
    gate = mod_ref[0:1, 2 * D_MODEL:3 * D_MODEL]
    o_ref[...] = x_ref[...] + gate * (yn * gpost_ref[...])


def _outproj(x2, na, o_f, o_b, proj, gnw, w_out, gpost, mod, tm=512):
    n = x2.shape[0]
    gz_col = (PROJ_W - GDN_W) // GDN_W
    return pl.pallas_call(
        _outproj_kernel,
        grid=(n // tm,),
        in_specs=[pl.BlockSpec((tm, D_MODEL), lambda i: (i, 0)),
                  pl.BlockSpec((tm, NA_W), lambda i: (i, 0)),
                  pl.BlockSpec((tm, GDN_W), lambda i: (i, 0)),
                  pl.BlockSpec((tm, GDN_W), lambda i: (i, 0)),
                  pl.BlockSpec((tm, GDN_W), lambda i: (i, gz_col)),
                  pl.BlockSpec((1, LANES), lambda i: (0, 0)),
                  pl.BlockSpec((NA_W + GDN_W, D_MODEL), lambda i: (0, 0)),
                  pl.BlockSpec((1, D_MODEL), lambda i: (0, 0)),
                  pl.BlockSpec((8, 3 * D_MODEL), lambda i: (0, 0))],
        out_specs=pl.BlockSpec((tm, D_MODEL), lambda i: (i, 0)),
        out_shape=jax.ShapeDtypeStruct((n, D_MODEL), F32),
        compiler_params=pltpu.CompilerParams(
            dimension_semantics=("arbitrary",), vmem_limit_bytes=VMEM_LIMIT),
        name="outproj",
    )(x2, na, o_f, o_b, proj, gnw, w_out, gpost, mod)


def _rope_tables(rows, identity):
    inv_freq = ROPE_BASE ** (-jnp.arange(0, ROPE_AXIS_DIM, 2, dtype=F32) / ROPE_AXIS_DIM)

    def tables(count):
        ang = jnp.arange(count, dtype=F32)[:, None] * inv_freq[None, :]
        if identity:
            ang = jnp.zeros_like(ang)
        c, s = jnp.cos(ang), jnp.sin(ang)
        z = jnp.zeros_like(s)
        return (jnp.concatenate([c, c], -1), jnp.concatenate([-s, z], -1),
                jnp.concatenate([z, s], -1))

    half = jnp.zeros((rows, LANES // 2), F32)
    rowtab = jnp.stack([jnp.concatenate([t, half], -1) for t in tables(rows)], axis=0)
    rowtab = rowtab.reshape(3, rows // GP_CH, GP_CH, LANES).transpose(1, 0, 2, 3)
    rowtab = rowtab.reshape(rows // GP_CH, 3 * GP_CH, LANES)
    rowtab = jnp.pad(rowtab, ((0, 0), (0, 16 - 3 * GP_CH), (0, 0)))
    halfc = jnp.zeros((GRID_W, LANES // 2), F32)
    coltab = jnp.stack([jnp.concatenate([halfc, t], -1) for t in tables(GRID_W)], axis=0)
    return rowtab, coltab


def _lane_row(vals, offset):
    return jnp.zeros((1, LANES), F32).at[0, offset:offset + vals.shape[0]].set(vals)


def kernel(x, c, ctx, c_ctx, w_ada, b_ada, g_pre, g_post, w_in, conv_w, rpb, A_log, dt_bias,
           gdn_norm_w, w_out):
    n = x.shape[1]
    nc = ctx.shape[1]
    x2 = x[0]
    xc2 = ctx[0]

    cc = jnp.zeros((8, D_MODEL), F32).at[0].set(c[0]).at[1].set(c_ctx)
    mod = _ada(cc, w_ada[0], b_ada[0][None, :])

    gpre = g_pre[0][None, :]
    w_in_t = jnp.swapaxes(w_in, 1, 2)
    proj, gates, projc, gatesc = _inproj(x2, xc2, mod, gpre, w_in_t, tm=512)

    na = _natten(proj, projc, rpb[0].reshape(-1))

    cw = jnp.pad(conv_w[0], ((0, 8 - CONV_K), (0, 0)))
    alog = _lane_row(A_log[0].reshape(-1), 2 * GDN_HEADS)
    dtb = _lane_row(dt_bias[0].reshape(-1), 2 * GDN_HEADS)
    rowtab, coltab = _rope_tables(n // GRID_W, identity=False)
    rowtab_c, coltab_c = _rope_tables(nc // GRID_W, identity=True)

    def gl_rows(gl):
        return jnp.stack([gl[:, 0, 2 * GDN_HEADS:3 * GDN_HEADS],
                          gl[:, 1, 3 * GDN_HEADS:4 * GDN_HEADS]], axis=0).reshape(-1)

    uc, wc, qdc, kdc, qkc, glc = _gdnprep(projc, gatesc, cw, alog, dtb, rowtab_c, coltab_c)
    s0 = jnp.zeros((2, GDN_HEADS, GDN_DK, GDN_DV), F32)
    _, _, s_ctx = _scan(gl_rows(glc), s0, uc, wc, qdc, kdc, qkc)

    u, w, qd, kd, qk, gl = _gdnprep(proj, gates, cw, alog, dtb, rowtab, coltab)
    o_f, o_b, _ = _scan(gl_rows(gl), s_ctx, u, w, qd, kd, qk)

    gnw = gdn_norm_w[0][None, :]
    out = _outproj(x2, na, o_f, o_b, proj, gnw, w_out[0].astype(BF16), g_post[0][None, :], mod)
    return out[None]
```

```python
import functools
import math

import jax
import jax.numpy as jnp
from jax import lax
from jax.experimental import pallas as pl
from jax.experimental.pallas import tpu as pltpu

F32 = jnp.float32
BF16 = jnp.bfloat16

D_MODEL = 1024
GRID_W = 64
NA_HEADS = 8
NA_DH = 64
NA_W = NA_HEADS * NA_DH
NA_KH = 8
NA_KW = 16
GDN_HEADS = 4
GDN_DK = 128
GDN_DV = 128
GDN_W = GDN_HEADS * GDN_DV
CHUNK = 64
CONV_K = 5
ROPE_AXIS_DIM = GDN_DK // 2
ROPE_BASE = 10000.0
EPS = 1e-6
PROJ_W = 4 * NA_W + 3 * GDN_W + GDN_W
GATE_W = 2 * 2 * GDN_HEADS
LANES = 128
NEG = -1e30

VMEM_LIMIT = 56 * 1024 * 1024


def _silu(x):
    h = 0.5 * x
    return h + h * jnp.tanh(h)


def _dot(a, b):
    return jnp.dot(a, b, preferred_element_type=F32)


def _dot_nt(a, b):
    return lax.dot_general(a, b, (((1,), (1,)), ((), ())), preferred_element_type=F32)


def _dot_tn(a, b):
    return lax.dot_general(a, b, (((0,), (0,)), ((), ())), preferred_element_type=F32)


def _split2(x):
    hi = x.astype(BF16)
    lo = (x - hi.astype(F32)).astype(BF16)
    return hi, lo


def _split3(x):
    hi = x.astype(BF16)
    r = x - hi.astype(F32)
    mid = r.astype(BF16)
    lo = (r - mid.astype(F32)).astype(BF16)
    return hi, mid, lo


def _mm3(a, b):
    ah, al = _split2(a)
    bh, bl = _split2(b)
    return _dot(ah, bh) + (_dot(al, bh) + _dot(ah, bl))


def _ada_kernel(c_ref, w_ref, b_ref, o_ref):
    s = _silu(c_ref[...])
    o_ref[...] = _mm3(s, w_ref[...]) + b_ref[...]


def _ada(cc, w_ada, b_ada):
    tn = 512
    n = w_ada.shape[1]
    return pl.pallas_call(
        _ada_kernel,
        grid=(n // tn,),
        in_specs=[pl.BlockSpec((8, D_MODEL), lambda j: (0, 0)),
                  pl.BlockSpec((D_MODEL, tn), lambda j: (0, j)),
                  pl.BlockSpec((1, tn), lambda j: (0, j))],
        out_specs=pl.BlockSpec((8, tn), lambda j: (0, j)),
        out_shape=jax.ShapeDtypeStruct((8, n), F32),
        name="ada",
    )(cc, w_ada, b_ada)


LOG2E = math.log2(math.e)
NA_QSCALE = NA_DH ** -0.5 * LOG2E


def _inproj_kernel(x_ref, xc_ref, mod_ref, gpre_ref, win_ref,
                   proj_ref, gates_ref, projc_ref, gatesc_ref, w_ref):
    def project(x, row, proj_out, gates_out):
        xn = x * lax.rsqrt(jnp.mean(x * x, axis=-1, keepdims=True) + EPS)
        shift = mod_ref[row:row + 1, 0:D_MODEL]
        scale = mod_ref[row:row + 1, D_MODEL:2 * D_MODEL]
        h = (xn * gpre_ref[...]) * (1.0 + scale) + shift
        hb = h.astype(BF16)
        nb = 512
        for j in range(PROJ_W // nb):
            y = _dot_nt(hb, w_ref[j * nb:(j + 1) * nb, :])
            if j * nb < NA_W:
                y = y * NA_QSCALE
            proj_out[:, j * nb:(j + 1) * nb] = y.astype(BF16)
        gates_out[...] = _dot_nt(hb, w_ref[PROJ_W:PROJ_W + LANES, :])

    @pl.when(pl.program_id(0) == 0)
    def _():
        nb = 512
        for j in range(PROJ_W // nb):
            w_ref[j * nb:(j + 1) * nb, :] = win_ref[j * nb:(j + 1) * nb, :].astype(BF16)
        w_ref[PROJ_W:PROJ_W + LANES, :] = jnp.zeros((LANES, D_MODEL), BF16)
        w_ref[PROJ_W:PROJ_W + GATE_W, :] = win_ref[PROJ_W:PROJ_W + GATE_W, :].astype(BF16)
        project(xc_ref[...], 1, projc_ref, gatesc_ref)

    project(x_ref[...], 0, proj_ref, gates_ref)


def _inproj(x2, xc2, mod, gpre, w_in, tm):
    n = x2.shape[0]
    nc = xc2.shape[0]
    return pl.pallas_call(
        _inproj_kernel,
        grid=(n // tm,),
        in_specs=[pl.BlockSpec((tm, D_MODEL), lambda i: (i, 0)),
                  pl.BlockSpec((nc, D_MODEL), lambda i: (0, 0)),
                  pl.BlockSpec((8, 3 * D_MODEL), lambda i: (0, 0)),
                  pl.BlockSpec((1, D_MODEL), lambda i: (0, 0)),
                  pl.BlockSpec((None, PROJ_W + GATE_W, D_MODEL), lambda i: (0, 0, 0),
                               pipeline_mode=pl.Buffered(1))],
        out_specs=[pl.BlockSpec((tm, PROJ_W), lambda i: (i, 0)),
                   pl.BlockSpec((tm, LANES), lambda i: (i, 0)),
                   pl.BlockSpec((nc, PROJ_W), lambda i: (0, 0)),
                   pl.BlockSpec((nc, LANES), lambda i: (0, 0))],
        out_shape=[jax.ShapeDtypeStruct((n, PROJ_W), BF16),
                   jax.ShapeDtypeStruct((n, LANES), F32),
                   jax.ShapeDtypeStruct((nc, PROJ_W), BF16),
                   jax.ShapeDtypeStruct((nc, LANES), F32)],
        scratch_shapes=[pltpu.VMEM((PROJ_W + LANES, D_MODEL), BF16)],
        compiler_params=pltpu.CompilerParams(
            dimension_semantics=("arbitrary",), vmem_limit_bytes=VMEM_LIMIT),
        name="inproj",
    )(x2, xc2, mod, gpre, w_in)


NA_RO = 2 * NA_KH - 1
NA_CO = 2 * NA_KW - 1
NA_ROWS_PER_ITER = 4


def _na_build_tables(rpb_ref, tab_ref, hp):
    qc = lax.broadcasted_iota(jnp.int32, (GRID_W, LANES), 0)
    kcol = lax.broadcasted_iota(jnp.int32, (GRID_W, LANES), 1) % GRID_W
    lo = lax.broadcasted_iota(jnp.int32, (1, LANES), 1) < GRID_W
    diff = kcol - qc + (NA_KW - 1)
    col_start = jnp.clip(qc - NA_KW // 2, 0, GRID_W - NA_KW)
    inwin = (kcol >= col_start) & (kcol < col_start + NA_KW)
    for hh in range(2):
        def ro_body(ro, carry):
            base = ((hp * 2 + hh) * NA_RO + ro) * NA_CO
            acc = jnp.zeros((GRID_W, LANES), F32)
            for j in range(NA_CO):
                val = jnp.where(lo, rpb_ref[base + j], rpb_ref[base + NA_CO + j])
                acc = jnp.where(diff == j, val, acc)
            tab_ref[hh, ro] = jnp.where(inwin, acc * LOG2E, NEG)
            return carry
        lax.fori_loop(0, NA_RO - 1, ro_body, 0)


def _na_kernel(rpb_ref, q_ref, k_ref, v_ref, z_ref, kc_ref, vc_ref, o_ref, tab_ref,
               *, rb_rows, rows):
    hp = pl.program_id(0)
    rb = pl.program_id(1)

    @pl.when(rb == 0)
    def _():
        _na_build_tables(rpb_ref, tab_ref, hp)

    lane = lax.broadcasted_iota(jnp.int32, (1, LANES), 1)
    first = lane < NA_DH
    kc = kc_ref[...]
    vc = vc_ref[...]
    win = NA_KH * GRID_W

    def body(it, carry):
        items = []
        for rr in range(NA_ROWS_PER_ITER):
            i = it * NA_ROWS_PER_ITER + rr
            r = rb * rb_rows + i
            rs = jnp.clip(r - NA_KH // 2, 0, rows - NA_KH)
            ro0 = NA_KH - 1 - (r - rs)
            t0 = pl.multiple_of(i * GRID_W, GRID_W)
            q = q_ref[pl.ds(t0, GRID_W), :]
            k0 = pl.multiple_of(rs * GRID_W, GRID_W)
            kw = k_ref[pl.ds(k0, win), :]
            vw = v_ref[pl.ds(k0, win), :]
            zq = jnp.zeros_like(q)
            qs = jnp.concatenate([jnp.where(first, q, zq), jnp.where(first, zq, q)], axis=0)
            items.append((t0, ro0, qs, kw, vw))
        s_all = [_dot_nt(qs, jnp.concatenate([kw, kc], axis=0)) for (_, _, qs, kw, _) in items]
        s_loc = [s[:, 0:win]
                 + jnp.concatenate(
                     [jnp.concatenate([tab_ref[hh, ro0 + 2 * m] for m in range(NA_KH // 2)], axis=1)
                      for hh in range(2)], axis=0)
                 for s, (_, ro0, _, _, _) in zip(s_all, items)]
        s_ctx = [s[:, win:] for s in s_all]
        mx = [jnp.maximum(jnp.max(a, axis=-1, keepdims=True), jnp.max(b, axis=-1, keepdims=True))
              for a, b in zip(s_loc, s_ctx)]
        p_loc = [jnp.exp2(a - m) for a, m in zip(s_loc, mx)]
        p_ctx = [jnp.exp2(b - m) for b, m in zip(s_ctx, mx)]
        inv = [1.0 / (jnp.sum(a, axis=-1, keepdims=True) + jnp.sum(b, axis=-1, keepdims=True))
               for a, b in zip(p_loc, p_ctx)]
        outs = [_dot(jnp.concatenate([a.astype(BF16), b.astype(BF16)], axis=1),
                     jnp.concatenate([it_[4], vc], axis=0)) * il
                for a, b, il, it_ in zip(p_loc, p_ctx, inv, items)]
        for (t0, _, _, _, _), o2 in zip(items, outs):
            o = jnp.where(first, o2[0:GRID_W], o2[GRID_W:2 * GRID_W])
            z = z_ref[pl.ds(t0, GRID_W), :].astype(F32)
            o_ref[pl.ds(t0, GRID_W), :] = (o * _silu(z)).astype(BF16)
        return carry

    lax.fori_loop(0, rb_rows // NA_ROWS_PER_ITER, body, 0)


def _natten(proj, projc, rpb_flat, rb_rows=32):
    n = proj.shape[0]
    nc = projc.shape[0]
    rows = n // GRID_W
    tq = rb_rows * GRID_W
    kcol = NA_W // LANES
    return pl.pallas_call(
        functools.partial(_na_kernel, rb_rows=rb_rows, rows=rows),
        grid=(NA_W // LANES, rows // rb_rows),
        in_specs=[pl.BlockSpec(memory_space=pltpu.SMEM),
                  pl.BlockSpec((tq, LANES), lambda hp, rb: (rb, hp)),
                  pl.BlockSpec((n, LANES), lambda hp, rb: (0, kcol + hp)),
                  pl.BlockSpec((n, LANES), lambda hp, rb: (0, 2 * kcol + hp)),
                  pl.BlockSpec((tq, LANES), lambda hp, rb: (rb, 3 * kcol + hp)),
                  pl.BlockSpec((nc, LANES), lambda hp, rb: (0, kcol + hp)),
                  pl.BlockSpec((nc, LANES), lambda hp, rb: (0, 2 * kcol + hp))],
        out_specs=pl.BlockSpec((tq, LANES), lambda hp, rb: (rb, hp)),
        out_shape=jax.ShapeDtypeStruct((n, NA_W), BF16),
        scratch_shapes=[pltpu.VMEM((2, NA_RO - 1, GRID_W, LANES), F32)],
        compiler_params=pltpu.CompilerParams(
            dimension_semantics=("arbitrary", "arbitrary"), vmem_limit_bytes=VMEM_LIMIT),
        name="natten",
    )(rpb_flat, proj, proj, proj, proj, projc, projc)


GP_TOK = 256
GP_CH = GP_TOK // CHUNK
GP_LOCK = 2
INV_BASE = 8
HALO = 16
CONV_ROWS = 128
XE_ROWS = GP_TOK + CONV_ROWS


def _gdnprep_kernel(main_ref, left_ref, right_ref, gates_ref, cw_ref, alog_ref, dtb_ref,
                    rowtab_ref, coltab_ref,
                    u_ref, w_ref, qd_ref, kd_ref, qk_ref, gl_ref,
                    xe_ref, shift_ref, qn_ref, kn_ref, vv_ref, beta_ref, gc_ref, gct_ref, eg_ref,
                    ek_ref, *, n_tiles):
    i = pl.program_id(0)
    tile = jnp.minimum(i, n_tiles - 1)
    qkv_w = 3 * GDN_W
    staged = (qn_ref, kn_ref, vv_ref, beta_ref, gc_ref, gct_ref, eg_ref, ek_ref)

    @pl.when(i == 0)
    def _():
        m = lax.broadcasted_iota(jnp.int32, (CONV_K * CONV_ROWS, 2 * CONV_ROWS), 0)
        r = lax.broadcasted_iota(jnp.int32, (CONV_K * CONV_ROWS, 2 * CONV_ROWS), 1)
        g, j, r8 = m // (8 * CONV_K), (m // 8) % CONV_K, m % 8
        hit = r == 8 * g + r8 + j + (HALO - CONV_K // 2)
        shift_ref[...] = jnp.where(hit, 1.0, 0.0).astype(BF16)
        xe_ref[HALO + GP_TOK + HALO:, :] = jnp.zeros(
            (XE_ROWS - GP_TOK - 2 * HALO, qkv_w), BF16)
        for ref in staged:
            ref[1] = jnp.zeros(ref.shape[1:], ref.dtype)

    ii = lax.broadcasted_iota(jnp.int32, (CHUNK, LANES), 0)
    jj = lax.broadcasted_iota(jnp.int32, (CHUNK, LANES), 1) % CHUNK
    lo = lax.broadcasted_iota(jnp.int32, (1, LANES), 1) < CHUNK
    lo_wide = lax.broadcasted_iota(jnp.int32, (1, 2 * LANES), 1) < LANES
    eye = jnp.where(ii == jj, 1.0, 0.0).astype(F32)
    bodies = [(d, p) for d in range(2) for p in range(GDN_HEADS // 2)]
    same_block = lambda size: (ii // size) == (jj // size)
    base_blocks = same_block(INV_BASE)
    merge_masks = []
    size = INV_BASE
    while size < CHUNK:
        merge_masks.append(same_block(2 * size) & jnp.logical_not(same_block(size)))
        size *= 2

    def blockdiag(y, first):
        z = jnp.zeros_like(y)
        return jnp.concatenate([jnp.where(first, y, z), jnp.where(first, z, y)], axis=0)

    def prepare(slot):
        lh = left_ref[:, 0:qkv_w]
        rh = right_ref[:, 0:qkv_w]
        xe_ref[0:HALO, :] = jnp.where(tile > 0, lh, jnp.zeros_like(lh))
        xe_ref[HALO:HALO + GP_TOK, :] = main_ref[:, 0:qkv_w]
        xe_ref[HALO + GP_TOK:HALO + GP_TOK + HALO, :] = jnp.where(
            tile < n_tiles - 1, rh, jnp.zeros_like(rh))

        row_lanes = lax.broadcasted_iota(jnp.int32, (1, LANES), 1) < ROPE_AXIS_DIM

        def rope_table(kind):
            return jnp.concatenate(
                [jnp.where(row_lanes, rowtab_ref[kind * GP_CH + c:kind * GP_CH + c + 1, :],
                           coltab_ref[kind]) for c in range(GP_CH)], axis=0)

        cos, sina, sinb = rope_table(0), rope_table(1), rope_table(2)
        pending = {}
        for cb, b in [(cb, b) for cb in range(qkv_w // LANES) for b in range(GP_TOK // CONV_ROWS)]:
            cols = slice(cb * LANES, (cb + 1) * LANES)
            rows = slice(b * CONV_ROWS, (b + 1) * CONV_ROWS)
            if cb % 2 == 0:
                cols2 = slice(cb * LANES, (cb + 2) * LANES)
                taps2 = _dot(shift_ref[...], xe_ref[b * CONV_ROWS:(b + 2) * CONV_ROWS, cols2])
                wts = [jnp.broadcast_to(cw_ref[j:j + 1, cols2], (8, 2 * LANES))
                       for j in range(CONV_K)]
                groups = []
                for g in range(CONV_ROWS // 8):
                    acc2 = None
                    for j in range(CONV_K):
                        r0 = (g * CONV_K + j) * 8
                        term = wts[j] * taps2[r0:r0 + 8]
                        acc2 = term if acc2 is None else acc2 + term
                    groups.append(acc2)
                pending[b] = _silu(jnp.concatenate(groups, axis=0))
            y = pending[b][:, (cb % 2) * LANES:(cb % 2 + 1) * LANES]
            if cb < 2 * GDN_HEADS:
                y = y * lax.rsqrt(jnp.sum(y * y, axis=-1, keepdims=True) + EPS)
                y = (y * cos[rows] + pltpu.roll(y, LANES - ROPE_AXIS_DIM // 2, 1) * sina[rows]
                     + pltpu.roll(y, ROPE_AXIS_DIM // 2, 1) * sinb[rows])
                if cb < GDN_HEADS:
                    qn_ref[slot, rows, cols] = y * (GDN_DK ** -0.5)
                else:
                    kn_ref[slot, rows, (cb - GDN_HEADS) * LANES:(cb - GDN_HEADS + 1) * LANES] = y
            else:
                vv_ref[slot, rows,
                       (cb - 2 * GDN_HEADS) * LANES:(cb - 2 * GDN_HEADS + 1) * LANES] = y
            yield

        gates = gates_ref[...]
        beta_ref[slot] = jax.nn.sigmoid(gates)
        xa = gates + dtb_ref[...]
        softplus = jnp.maximum(xa, 0.0) + jnp.log1p(jnp.exp(-jnp.abs(xa)))
        g = -jnp.exp(alog_ref[...]) * softplus

        ti = lax.broadcasted_iota(jnp.int32, (GP_TOK, GP_TOK), 0)
        tj = lax.broadcasted_iota(jnp.int32, (GP_TOK, GP_TOK), 1)
        same = (ti // CHUNK) == (tj // CHUNK)
        lower = jnp.where(same & (ti >= tj), 1.0, 0.0).astype(BF16)
        upper = jnp.where(same & (ti <= tj), 1.0, 0.0).astype(BF16)
        g3 = _split3(g)
        gc_f = _dot(lower, g3[0]) + (_dot(lower, g3[1]) + _dot(lower, g3[2]))
        gc_b = _dot(upper, g3[0]) + (_dot(upper, g3[1]) + _dot(upper, g3[2]))
        gc_ref[slot, 0] = gc_f
        gc_ref[slot, 1] = gc_b
        yield
        for c in range(GP_CH):
            rows = slice(c * CHUNK, (c + 1) * CHUNK)
            for d, gc in enumerate((gc_f, gc_b)):
                blk = gc[rows]
                gct_ref[slot, c, d] = jnp.concatenate(
                    [blk, pltpu.roll(blk, LANES - 1, 1)], axis=0).T
            ek_ref[slot, 0, rows] = jnp.exp(gc_f[(c + 1) * CHUNK - 1:(c + 1) * CHUNK, :] - gc_f[rows])
            ek_ref[slot, 1, rows] = jnp.exp(gc_b[c * CHUNK:c * CHUNK + 1, :] - gc_b[rows])
            yield
        eg_ref[slot, 0] = jnp.exp(gc_f)
        eg_ref[slot, 1] = jnp.exp(gc_b)

    def solve(slot):
        gl_ref[...] = jnp.zeros_like(gl_ref)
        for c in range(GP_CH):
            gl_ref[c, 0:1, :] = eg_ref[slot, 0, (c + 1) * CHUNK - 1:(c + 1) * CHUNK, :]
            gl_ref[c, 1:2, :] = eg_ref[slot, 1, c * CHUNK:c * CHUNK + 1, :]
        for first_chunk in range(0, GP_CH, GP_LOCK):
            yield from solve_group(slot, range(first_chunk, first_chunk + GP_LOCK))

    def solve_group(slot, chunks):
        a_l, kbeg_l, vb_l, where_l = [], [], [], []
        for c, (d, p) in [(c, b) for c in chunks for b in bodies]:
            tok = slice(c * CHUNK, (c + 1) * CHUNK)
            incl = (ii >= jj) if d == 0 else (ii <= jj)
            strict = (ii > jj) if d == 0 else (ii < jj)
            pl2 = slice(2 * p * LANES, (2 * p + 2) * LANES)
            lb = d * GDN_HEADS + 2 * p
            lg = 2 * GDN_HEADS + lb
            q = qn_ref[slot, tok, pl2]
            k = kn_ref[slot, tok, pl2]
            v = vv_ref[slot, tok, pl2]
            beta = jnp.where(lo_wide, beta_ref[slot, tok, lb:lb + 1],
                             beta_ref[slot, tok, lb + 1:lb + 2])
            gcol = jnp.where(lo, gc_ref[slot, d, tok, lg:lg + 1], gc_ref[slot, d, tok, lg + 1:lg + 2])
            grow = gct_ref[slot, c, d, lg:lg + 1, :]
            eg = jnp.where(lo_wide, eg_ref[slot, d, tok, lg:lg + 1],
                           eg_ref[slot, d, tok, lg + 1:lg + 2])
            ek = jnp.where(lo_wide, ek_ref[slot, d, tok, lg:lg + 1],
                           ek_ref[slot, d, tok, lg + 1:lg + 2])
            dec = jnp.exp(jnp.where(incl, gcol - grow, NEG))
            kb = k * beta
            k_nt = blockdiag(k.astype(BF16), lo_wide)
            kq = _dot_nt(jnp.concatenate([kb.astype(BF16), q.astype(BF16)], axis=0), k_nt)
            a = jnp.where(strict, kq[0:CHUNK] * dec, 0.0)
            qk = kq[CHUNK:2 * CHUNK] * dec
            qk_ref[d, tok, p * LANES:(p + 1) * LANES] = qk.astype(BF16)
            qd_ref[d, tok, pl2] = (q * eg).astype(BF16)
            kd_ref[d, tok, pl2] = (k * ek).astype(BF16)
            a_l.append(a)
            kbeg_l.append((kb * eg).astype(BF16))
            vb_l.append((v * beta).astype(BF16))
            where_l.append((d, tok, pl2))
            if p == GDN_HEADS // 2 - 1:
                yield
        x_l = [jnp.where(base_blocks, -a, 0.0) for a in a_l]
        t_l = [eye + x for x in x_l]
        x_l = [x.astype(BF16) for x in x_l]
        x_l = [_dot(x, blockdiag(x, lo)).astype(BF16) for x in x_l]
        yield
        r_l = [_dot(jnp.concatenate([x, t.astype(BF16)], axis=0), blockdiag(x, lo))
               for x, t in zip(x_l, t_l)]
        x_l = [r[0:CHUNK].astype(BF16) for r in r_l]
        t_l = [t + r[CHUNK:2 * CHUNK] for t, r in zip(t_l, r_l)]
        yield
        t_l = [t + _dot(t.astype(BF16), blockdiag(x, lo)) for t, x in zip(t_l, x_l)]
        yield
        for merged in merge_masks:
            t16_l = [t.astype(BF16) for t in t_l]
            te_l = [_dot(t16, blockdiag(jnp.where(merged, a, 0.0).astype(BF16), lo))
                    for t16, a in zip(t16_l, a_l)]
            yield
            t_l = [t - _dot(te.astype(BF16), blockdiag(t16, lo))
                   for t, te, t16 in zip(t_l, te_l, t16_l)]
            yield
        for n_done, ((d, tok, pl2), t, vb, kbeg) in enumerate(zip(where_l, t_l, vb_l, kbeg_l)):
            t16 = t.astype(BF16)
            uw = _dot(t16, jnp.concatenate([blockdiag(vb, lo_wide), blockdiag(kbeg, lo_wide)],
                                           axis=1))
            u_ref[d, tok, pl2] = uw[:, 0:2 * LANES]
            w_ref[d, tok, pl2] = uw[:, 2 * LANES:4 * LANES].astype(BF16)
            if n_done % len(bodies) == len(bodies) - 1:
                yield

    def interleave(*stages):
        live = list(stages)
        while live:
            for g in list(live):
                if next(g, live) is live:
                    live.remove(g)

    for parity in range(2):
        @pl.when(i % 2 == parity)
        def _():
            interleave(solve(1 - parity), prepare(parity))


def _gdnprep(proj, gates, conv_w, alog, dtb, rowtab, coltab):
    n = proj.shape[0]
    nt = n // GP_TOK
    hb = GP_TOK // HALO
    nhalo = n // HALO
    qkv_w = 3 * GDN_W
    cur = lambda i: jnp.minimum(i, nt - 1)
    done = lambda i: jnp.maximum(i - 1, 0)
    tok_spec = lambda w: pl.BlockSpec((GP_TOK, w), lambda i: (cur(i), 0))
    dir_spec = lambda w: pl.BlockSpec((2, GP_TOK, w), lambda i: (0, done(i), 0))
    stage = lambda *shape: pltpu.VMEM((2,) + shape, F32)
    return pl.pallas_call(
        functools.partial(_gdnprep_kernel, n_tiles=nt),
        grid=(nt + 1,),
        in_specs=[pl.BlockSpec((GP_TOK, PROJ_W // 2), lambda i: (cur(i), 1)),
                  pl.BlockSpec((HALO, PROJ_W // 2),
                               lambda i: (jnp.maximum(cur(i) * hb - 1, 0), 1)),
                  pl.BlockSpec((HALO, PROJ_W // 2),
                               lambda i: (jnp.minimum((cur(i) + 1) * hb, nhalo - 1), 1)),
                  tok_spec(LANES),
                  pl.BlockSpec((8, qkv_w), lambda i: (0, 0)),
                  pl.BlockSpec((1, LANES), lambda i: (0, 0)),
                  pl.BlockSpec((1, LANES), lambda i: (0, 0)),
                  pl.BlockSpec((None, 16, LANES), lambda i: (cur(i), 0, 0)),
                  pl.BlockSpec((3, GRID_W, LANES), lambda i: (0, 0, 0))],
        out_specs=[dir_spec(GDN_W), dir_spec(GDN_W), dir_spec(GDN_W), dir_spec(GDN_W),
                   dir_spec(GDN_HEADS * CHUNK),
                   pl.BlockSpec((GP_CH, 8, LANES), lambda i: (done(i), 0, 0))],
        out_shape=[jax.ShapeDtypeStruct((2, n, GDN_W), F32),
                   jax.ShapeDtypeStruct((2, n, GDN_W), BF16),
                   jax.ShapeDtypeStruct((2, n, GDN_W), BF16),
                   jax.ShapeDtypeStruct((2, n, GDN_W), BF16),
                   jax.ShapeDtypeStruct((2, n, GDN_HEADS * CHUNK), BF16),
                   jax.ShapeDtypeStruct((n // CHUNK, 8, LANES), F32)],
        scratch_shapes=[pltpu.VMEM((XE_ROWS, qkv_w), BF16),
                        pltpu.VMEM((CONV_K * CONV_ROWS, 2 * CONV_ROWS), BF16),
                        stage(GP_TOK, GDN_W), stage(GP_TOK, GDN_W), stage(GP_TOK, GDN_W),
                        stage(GP_TOK, LANES),
                        stage(2, GP_TOK, LANES),
                        stage(GP_CH, 2, LANES, LANES),
                        stage(2, GP_TOK, LANES),
                        stage(2, GP_TOK, LANES)],
        compiler_params=pltpu.CompilerParams(
            dimension_semantics=("arbitrary",), vmem_limit_bytes=VMEM_LIMIT),
        name="gdnprep",
    )(proj, proj, proj, gates, conv_w, alog, dtb, rowtab, coltab)


SC_CH_MAX = 16


def _scan_kernel(gl_ref, s0_ref, uf_ref, wf_ref, qdf_ref, kdf_ref, qkf_ref,
                 ub_ref, wb_ref, qdb_ref, kdb_ref, qkb_ref,
                 of_ref, ob_ref, sfin_ref, s_ref, *, SC_CH):
    n = pl.program_id(0)
    nsteps = pl.num_programs(0)
    nch = nsteps * SC_CH

    @pl.when(n == 0)
    def _():
        s_ref[...] = s0_ref[...]

    streams = ((uf_ref, wf_ref, qdf_ref, kdf_ref, qkf_ref, of_ref),
               (ub_ref, wb_ref, qdb_ref, kdb_ref, qkb_ref, ob_ref))
    lo_wide = lax.broadcasted_iota(jnp.int32, (1, 2 * LANES), 1) < LANES
    chains = [(d, h) for d in range(2) for h in range(GDN_HEADS)]
    hl = lambda h: slice(h * LANES, (h + 1) * LANES)
    for j in range(SC_CH):
        local = (j, SC_CH - 1 - j)
        chunk = (n * SC_CH + j, (nsteps - 1 - n) * SC_CH + SC_CH - 1 - j)
        tok = [slice(c * CHUNK, (c + 1) * CHUNK) for c in local]
        s32 = [s_ref[d, h] for d, h in chains]
        s16 = [s.astype(BF16) for s in s32]
        wq = [_dot(jnp.concatenate([streams[d][1][tok[d], hl(h)], streams[d][2][tok[d], hl(h)]],
                                   axis=0), s)
              for (d, h), s in zip(chains, s16)]
        v16 = [(streams[d][0][tok[d], hl(h)] - x[0:CHUNK]).astype(BF16)
               for (d, h), x in zip(chains, wq)]
        upd = [_dot_tn(streams[d][3][tok[d], hl(h)], v) for (d, h), v in zip(chains, v16)]
        for (d, h), s, x in zip(chains, s32, upd):
            decay = gl_ref[(d * nch + chunk[d]) * GDN_HEADS + h]
            s_ref[d, h] = s * decay + x
        inter = [x[CHUNK:2 * CHUNK] for x in wq]
        for d in range(2):
            qk_ref, o_ref = streams[d][4], streams[d][5]
            for p in range(GDN_HEADS // 2):
                v_pair = jnp.concatenate(
                    v16[d * GDN_HEADS + 2 * p:d * GDN_HEADS + 2 * p + 2], axis=1)
                zero = jnp.zeros_like(v_pair)
                v_bd = jnp.concatenate([jnp.where(lo_wide, v_pair, zero),
                                        jnp.where(lo_wide, zero, v_pair)], axis=0)
                o_intra = _dot(qk_ref[tok[d], p * LANES:(p + 1) * LANES], v_bd)
                for jh in range(2):
                    h = 2 * p + jh
                    o_ref[tok[d], hl(h)] = (inter[d * GDN_HEADS + h]
                                            + o_intra[:, jh * LANES:(jh + 1) * LANES]
                                            ).astype(o_ref.dtype)

    @pl.when(n == nsteps - 1)
    def _():
        sfin_ref[...] = s_ref[...]


def _scan(gl, s0, u, w, qd, kd, qk):
    n = u.shape[1]
    sc_ch = min(SC_CH_MAX, n // CHUNK)
    tok = sc_ch * CHUNK
    nsteps = n // tok
    fwd = lambda wd: pl.BlockSpec((None, tok, wd), lambda i: (0, i, 0))
    bwd = lambda wd: pl.BlockSpec((None, tok, wd), lambda i: (1, nsteps - 1 - i, 0))
    qkw = GDN_HEADS * CHUNK
    state_spec = pl.BlockSpec((2, GDN_HEADS, GDN_DK, GDN_DV), lambda i: (0, 0, 0, 0))
    return pl.pallas_call(
        functools.partial(_scan_kernel, SC_CH=sc_ch),
        grid=(nsteps,),
        in_specs=[pl.BlockSpec(memory_space=pltpu.SMEM), state_spec,
                  fwd(GDN_W), fwd(GDN_W), fwd(GDN_W), fwd(GDN_W), fwd(qkw),
                  bwd(GDN_W), bwd(GDN_W), bwd(GDN_W), bwd(GDN_W), bwd(qkw)],
        out_specs=[pl.BlockSpec((tok, GDN_W), lambda i: (i, 0)),
                   pl.BlockSpec((tok, GDN_W), lambda i: (nsteps - 1 - i, 0)),
                   state_spec],
        out_shape=[jax.ShapeDtypeStruct((n, GDN_W), BF16),
                   jax.ShapeDtypeStruct((n, GDN_W), BF16),
                   jax.ShapeDtypeStruct((2, GDN_HEADS, GDN_DK, GDN_DV), F32)],
        scratch_shapes=[pltpu.VMEM((2, GDN_HEADS, GDN_DK, GDN_DV), F32)],
        compiler_params=pltpu.CompilerParams(dimension_semantics=("arbitrary",)),
        name="scan",
    )(gl, s0, u, w, qd, kd, qk, u, w, qd, kd, qk)


def _outproj_kernel(x_ref, na_ref, of_ref, ob_ref, gz_ref, gnw_ref, wout_ref, gpost_ref, mod_ref,
                    o_ref):
    og = of_ref[...].astype(F32) + ob_ref[...].astype(F32)
    gz = gz_ref[...].astype(F32)
    gnw = gnw_ref[...]
    parts = []
    for h in range(GDN_HEADS):
        seg = og[:, h * LANES:(h + 1) * LANES]
        seg = seg * lax.rsqrt(jnp.mean(seg * seg, axis=-1, keepdims=True) + EPS)
        parts.append(seg * gnw)
    gd = (jnp.concatenate(parts, axis=-1) * _silu(gz)).astype(BF16)
    y = _dot(na_ref[...], wout_ref[0:NA_W, :]) + _dot(gd, wout_ref[NA_W:NA_W + GDN_W, :])
    yn = y * lax.rsqrt(jnp.mean(y * y, axis=-1, keepdims=True) + EPS)
    gate = mod_ref[0:1, 2 * D_MODEL:3 * D_MODEL]
    o_ref[...] = x_ref[...] + gate * (yn * gpost_ref[...])


def _outproj(x2, na, o_f, o_b, proj, gnw, w_out, gpost, mod, tm=512):
    n = x2.shape[0]
    gz_col = (PROJ_W - GDN_W) // GDN_W
    return pl.pallas_call(
        _outproj_kernel,
        grid=(n // tm,),
        in_specs=[pl.BlockSpec((tm, D_MODEL), lambda i: (i, 0)),
                  pl.BlockSpec((tm, NA_W), lambda i: (i, 0)),
                  pl.BlockSpec((tm, GDN_W), lambda i: (i, 0)),
                  pl.BlockSpec((tm, GDN_W), lambda i: (i, 0)),
                  pl.BlockSpec((tm, GDN_W), lambda i: (i, gz_col)),
                  pl.BlockSpec((1, LANES), lambda i: (0, 0)),
                  pl.BlockSpec((NA_W + GDN_W, D_MODEL), lambda i: (0, 0)),
                  pl.BlockSpec((1, D_MODEL), lambda i: (0, 0)),
                  pl.BlockSpec((8, 3 * D_MODEL), lambda i: (0, 0))],
        out_specs=pl.BlockSpec((tm, D_MODEL), lambda i: (i, 0)),
        out_shape=jax.ShapeDtypeStruct((n, D_MODEL), F32),
        compiler_params=pltpu.CompilerParams(
            dimension_semantics=("arbitrary",), vmem_limit_bytes=VMEM_LIMIT),
        name="outproj",
    )(x2, na, o_f, o_b, proj, gnw, w_out, gpost, mod)


def _rope_tables(rows, identity):
    inv_freq = ROPE_BASE ** (-jnp.arange(0, ROPE_AXIS_DIM, 2, dtype=F32) / ROPE_AXIS_DIM)

    def tables(count):
        ang = jnp.arange(count, dtype=F32)[:, None] * inv_freq[None, :]
        if identity:
            ang = jnp.zeros_like(ang)
        c, s = jnp.cos(ang), jnp.sin(ang)
        z = jnp.zeros_like(s)
        return (jnp.concatenate([c, c], -1), jnp.concatenate([-s, z], -1),
                jnp.concatenate([z, s], -1))

    half = jnp.zeros((rows, LANES // 2), F32)
    rowtab = jnp.stack([jnp.concatenate([t, half], -1) for t in tables(rows)], axis=0)
    rowtab = rowtab.reshape(3, rows // GP_CH, GP_CH, LANES).transpose(1, 0, 2, 3)
    rowtab = rowtab.reshape(rows // GP_CH, 3 * GP_CH, LANES)
    rowtab = jnp.pad(rowtab, ((0, 0), (0, 16 - 3 * GP_CH), (0, 0)))
    halfc = jnp.zeros((GRID_W, LANES // 2), F32)
    coltab = jnp.stack([jnp.concatenate([halfc, t], -1) for t in tables(GRID_W)], axis=0)
    return rowtab, coltab


def _lane_row(vals, offset):
    return jnp.zeros((1, LANES), F32).at[0, offset:offset + vals.shape[0]].set(vals)


def kernel(x, c, ctx, c_ctx, w_ada, b_ada, g_pre, g_post, w_in, conv_w, rpb, A_log, dt_bias,
           gdn_norm_w, w_out):
    n = x.shape[1]
    nc = ctx.shape[1]
    x2 = x[0]
    xc2 = ctx[0]

    cc = jnp.zeros((8, D_MODEL), F32).at[0].set(c[0]).at[1].set(c_ctx)
    mod = _ada(cc, w_ada[0], b_ada[0][None, :])

    gpre = g_pre[0][None, :]
    w_in_t = jnp.swapaxes(w_in, 1, 2)
    proj, gates, projc, gatesc = _inproj(x2, xc2, mod, gpre, w_in_t, tm=512)

    na = _natten(proj, projc, rpb[0].reshape(-1))

    cw = jnp.pad(conv_w[0], ((0, 8 - CONV_K), (0, 0)))
    alog = _lane_row(A_log[0].reshape(-1), 2 * GDN_HEADS)
    dtb = _lane_row(dt_bias[0].reshape(-1), 2 * GDN_HEADS)
    rowtab, coltab = _rope_tables(n // GRID_W, identity=False)
    rowtab_c, coltab_c = _rope_tables(nc // GRID_W, identity=True)

    def gl_rows(gl):
        return jnp.stack([gl[:, 0, 2 * GDN_HEADS:3 * GDN_HEADS],
                          gl[:, 1, 3 * GDN_HEADS:4 * GDN_HEADS]], axis=0).reshape(-1)

    uc, wc, qdc, kdc, qkc, glc = _gdnprep(projc, gatesc, cw, alog, dtb, rowtab_c, coltab_c)
    s0 = jnp.zeros((2, GDN_HEADS, GDN_DK, GDN_DV), F32)
    _, _, s_ctx = _scan(gl_rows(glc), s0, uc, wc, qdc, kdc, qkc)

    u, w, qd, kd, qk, gl = _gdnprep(proj, gates, cw, alog, dtb, rowtab, coltab)
    o_f, o_b, _ = _scan(gl_rows(gl), s_ctx, u, w, qd, kd, qk)

    gnw = gdn_norm_w[0][None, :]
    out = _outproj(x2, na, o_f, o_b, proj, gnw, w_out[0].astype(BF16), g_post[0][None, :], mod)
    return out[None]
```

```python
import functools
import math

import jax
import jax.numpy as jnp
from jax import lax
from jax.experimental import pallas as pl
from jax.experimental.pallas import tpu as pltpu

F32 = jnp.float32
BF16 = jnp.bfloat16

D_MODEL = 1024
GRID_W = 64
NA_HEADS = 8
NA_DH = 64
NA_W = NA_HEADS * NA_DH
NA_KH = 8
NA_KW = 16
GDN_HEADS = 4
GDN_DK = 128
GDN_DV = 128
GDN_W = GDN_HEADS * GDN_DV
CHUNK = 64
CONV_K = 5
ROPE_AXIS_DIM = GDN_DK // 2
ROPE_BASE = 10000.0
EPS = 1e-6
PROJ_W = 4 * NA_W + 3 * GDN_W + GDN_W
GATE_W = 2 * 2 * GDN_HEADS
LANES = 128
NEG = -1e30

VMEM_LIMIT = 56 * 1024 * 1024


def _silu(x):
    h = 0.5 * x
    return h + h * jnp.tanh(h)


def _dot(a, b):
    return jnp.dot(a, b, preferred_element_type=F32)


def _dot_nt(a, b):
    return lax.dot_general(a, b, (((1,), (1,)), ((), ())), preferred_element_type=F32)


def _dot_tn(a, b):
    return lax.dot_general(a, b, (((0,), (0,)), ((), ())), preferred_element_type=F32)


def _split2(x):
    hi = x.astype(BF16)
    lo = (x - hi.astype(F32)).astype(BF16)
    return hi, lo


def _split3(x):
    hi = x.astype(BF16)
    r = x - hi.astype(F32)
    mid = r.astype(BF16)
    lo = (r - mid.astype(F32)).astype(BF16)
    return hi, mid, lo


def _mm3(a, b):
    ah, al = _split2(a)
    bh, bl = _split2(b)
    return _dot(ah, bh) + (_dot(al, bh) + _dot(ah, bl))


def _ada_kernel(c_ref, w_ref, b_ref, o_ref):
    s = _silu(c_ref[...])
    o_ref[...] = _mm3(s, w_ref[...]) + b_ref[...]


def _ada(cc, w_ada, b_ada):
    tn = 512
    n = w_ada.shape[1]
    return pl.pallas_call(
        _ada_kernel,
        grid=(n // tn,),
        in_specs=[pl.BlockSpec((8, D_MODEL), lambda j: (0, 0)),
                  pl.BlockSpec((D_MODEL, tn), lambda j: (0, j)),
                  pl.BlockSpec((1, tn), lambda j: (0, j))],
        out_specs=pl.BlockSpec((8, tn), lambda j: (0, j)),
        out_shape=jax.ShapeDtypeStruct((8, n), F32),
        name="ada",
    )(cc, w_ada, b_ada)


LOG2E = math.log2(math.e)
NA_QSCALE = NA_DH ** -0.5 * LOG2E


def _inproj_kernel(x_ref, xc_ref, mod_ref, gpre_ref, win_ref,
                   proj_ref, gates_ref, projc_ref, gatesc_ref, w_ref):
    def project(x, row, proj_out, gates_out):
        xn = x * lax.rsqrt(jnp.mean(x * x, axis=-1, keepdims=True) + EPS)
        shift = mod_ref[row:row + 1, 0:D_MODEL]
        scale = mod_ref[row:row + 1, D_MODEL:2 * D_MODEL]
        h = (xn * gpre_ref[...]) * (1.0 + scale) + shift
        hb = h.astype(BF16)
        nb = 512
        for j in range(PROJ_W // nb):
            y = _dot_nt(hb, w_ref[j * nb:(j + 1) * nb, :])
            if j * nb < NA_W:
                y = y * NA_QSCALE
            proj_out[:, j * nb:(j + 1) * nb] = y.astype(BF16)
        gates_out[...] = _dot_nt(hb, w_ref[PROJ_W:PROJ_W + LANES, :])

    @pl.when(pl.program_id(0) == 0)
    def _():
        nb = 512
        for j in range(PROJ_W // nb):
            w_ref[j * nb:(j + 1) * nb, :] = win_ref[j * nb:(j + 1) * nb, :].astype(BF16)
        w_ref[PROJ_W:PROJ_W + LANES, :] = jnp.zeros((LANES, D_MODEL), BF16)
        w_ref[PROJ_W:PROJ_W + GATE_W, :] = win_ref[PROJ_W:PROJ_W + GATE_W, :].astype(BF16)
        project(xc_ref[...], 1, projc_ref, gatesc_ref)

    project(x_ref[...], 0, proj_ref, gates_ref)


def _inproj(x2, xc2, mod, gpre, w_in, tm):
    n = x2.shape[0]
    nc = xc2.shape[0]
    return pl.pallas_call(
        _inproj_kernel,
        grid=(n // tm,),
        in_specs=[pl.BlockSpec((tm, D_MODEL), lambda i: (i, 0)),
                  pl.BlockSpec((nc, D_MODEL), lambda i: (0, 0)),
                  pl.BlockSpec((8, 3 * D_MODEL), lambda i: (0, 0)),
                  pl.BlockSpec((1, D_MODEL), lambda i: (0, 0)),
                  pl.BlockSpec((None, PROJ_W + GATE_W, D_MODEL), lambda i: (0, 0, 0),
                               pipeline_mode=pl.Buffered(1))],
        out_specs=[pl.BlockSpec((tm, PROJ_W), lambda i: (i, 0)),
                   pl.BlockSpec((tm, LANES), lambda i: (i, 0)),
                   pl.BlockSpec((nc, PROJ_W), lambda i: (0, 0)),
                   pl.BlockSpec((nc, LANES), lambda i: (0, 0))],
        out_shape=[jax.ShapeDtypeStruct((n, PROJ_W), BF16),
                   jax.ShapeDtypeStruct((n, LANES), F32),
                   jax.ShapeDtypeStruct((nc, PROJ_W), BF16),
                   jax.ShapeDtypeStruct((nc, LANES), F32)],
        scratch_shapes=[pltpu.VMEM((PROJ_W + LANES, D_MODEL), BF16)],
        compiler_params=pltpu.CompilerParams(
            dimension_semantics=("arbitrary",), vmem_limit_bytes=VMEM_LIMIT),
        name="inproj",
    )(x2, xc2, mod, gpre, w_in)


NA_RO = 2 * NA_KH - 1
NA_CO = 2 * NA_KW - 1
NA_ROWS_PER_ITER = 8


def _na_build_tables(rpb_ref, tab_ref, hp):
    qc = lax.broadcasted_iota(jnp.int32, (GRID_W, LANES), 0)
    kcol = lax.broadcasted_iota(jnp.int32, (GRID_W, LANES), 1) % GRID_W
    lo = lax.broadcasted_iota(jnp.int32, (1, LANES), 1) < GRID_W
    diff = kcol - qc + (NA_KW - 1)
    col_start = jnp.clip(qc - NA_KW // 2, 0, GRID_W - NA_KW)
    inwin = (kcol >= col_start) & (kcol < col_start + NA_KW)
    for hh in range(2):
        def ro_body(ro, carry):
            base = ((hp * 2 + hh) * NA_RO + ro) * NA_CO
            acc = jnp.zeros((GRID_W, LANES), F32)
            for j in range(NA_CO):
                val = jnp.where(lo, rpb_ref[base + j], rpb_ref[base + NA_CO + j])
                acc = jnp.where(diff == j, val, acc)
            tab_ref[hh, ro] = jnp.where(inwin, acc * LOG2E, NEG)
            return carry
        lax.fori_loop(0, NA_RO - 1, ro_body, 0)


def _na_kernel(rpb_ref, q_ref, k_ref, v_ref, z_ref, kc_ref, vc_ref, o_ref, tab_ref,
               *, rb_rows, rows):
    hp = pl.program_id(0)
    rb = pl.program_id(1)

    @pl.when(rb == 0)
    def _():
        _na_build_tables(rpb_ref, tab_ref, hp)

    lane = lax.broadcasted_iota(jnp.int32, (1, LANES), 1)
    first = lane < NA_DH
    kc = kc_ref[...]
    vc = vc_ref[...]
    win = NA_KH * GRID_W

    def body(it, carry):
        items = []
        for rr in range(NA_ROWS_PER_ITER):
            i = it * NA_ROWS_PER_ITER + rr
            r = rb * rb_rows + i
            rs = jnp.clip(r - NA_KH // 2, 0, rows - NA_KH)
            ro0 = NA_KH - 1 - (r - rs)
            t0 = pl.multiple_of(i * GRID_W, GRID_W)
            q = q_ref[pl.ds(t0, GRID_W), :]
            k0 = pl.multiple_of(rs * GRID_W, GRID_W)
            kw = k_ref[pl.ds(k0, win), :]
            vw = v_ref[pl.ds(k0, win), :]
            zq = jnp.zeros_like(q)
            qs = jnp.concatenate([jnp.where(first, q, zq), jnp.where(first, zq, q)], axis=0)
            items.append((t0, ro0, qs, kw, vw))
        s_all = [_dot_nt(qs, jnp.concatenate([kw, kc], axis=0)) for (_, _, qs, kw, _) in items]
        s_loc = [s[:, 0:win]
                 + jnp.concatenate(
                     [jnp.concatenate([tab_ref[hh, ro0 + 2 * m] for m in range(NA_KH // 2)], axis=1)
                      for hh in range(2)], axis=0)
                 for s, (_, ro0, _, _, _) in zip(s_all, items)]
        s_ctx = [s[:, win:] for s in s_all]
        mx = [jnp.maximum(jnp.max(a, axis=-1, keepdims=True), jnp.max(b, axis=-1, keepdims=True))
              for a, b in zip(s_loc, s_ctx)]
        p_loc = [jnp.exp2(a - m) for a, m in zip(s_loc, mx)]
        p_ctx = [jnp.exp2(b - m) for b, m in zip(s_ctx, mx)]
        inv = [1.0 / (jnp.sum(a, axis=-1, keepdims=True) + jnp.sum(b, axis=-1, keepdims=True))
               for a, b in zip(p_loc, p_ctx)]
        outs = [_dot(jnp.concatenate([a.astype(BF16), b.astype(BF16)], axis=1),
                     jnp.concatenate([it_[4], vc], axis=0)) * il
                for a, b, il, it_ in zip(p_loc, p_ctx, inv, items)]
        for (t0, _, _, _, _), o2 in zip(items, outs):
            o = jnp.where(first, o2[0:GRID_W], o2[GRID_W:2 * GRID_W])
            z = z_ref[pl.ds(t0, GRID_W), :].astype(F32)
            o_ref[pl.ds(t0, GRID_W), :] = (o * _silu(z)).astype(BF16)
        return carry

    lax.fori_loop(0, rb_rows // NA_ROWS_PER_ITER, body, 0)


def _natten(proj, projc, rpb_flat, rb_rows=32):
    n = proj.shape[0]
    nc = projc.shape[0]
    rows = n // GRID_W
    tq = rb_rows * GRID_W
    kcol = NA_W // LANES
    return pl.pallas_call(
        functools.partial(_na_kernel, rb_rows=rb_rows, rows=rows),
        grid=(NA_W // LANES, rows // rb_rows),
        in_specs=[pl.BlockSpec(memory_space=pltpu.SMEM),
                  pl.BlockSpec((tq, LANES), lambda hp, rb: (rb, hp)),
                  pl.BlockSpec((n, LANES), lambda hp, rb: (0, kcol + hp)),
                  pl.BlockSpec((n, LANES), lambda hp, rb: (0, 2 * kcol + hp)),
                  pl.BlockSpec((tq, LANES), lambda hp, rb: (rb, 3 * kcol + hp)),
                  pl.BlockSpec((nc, LANES), lambda hp, rb: (0, kcol + hp)),
                  pl.BlockSpec((nc, LANES), lambda hp, rb: (0, 2 * kcol + hp))],
        out_specs=pl.BlockSpec((tq, LANES), lambda hp, rb: (rb, hp)),
        out_shape=jax.ShapeDtypeStruct((n, NA_W), BF16),
        scratch_shapes=[pltpu.VMEM((2, NA_RO - 1, GRID_W, LANES), F32)],
        compiler_params=pltpu.CompilerParams(
            dimension_semantics=("arbitrary", "arbitrary"), vmem_limit_bytes=VMEM_LIMIT),
        name="natten",
    )(rpb_flat, proj, proj, proj, proj, projc, projc)


GP_TOK = 256
GP_CH = GP_TOK // CHUNK
GP_LOCK = 2
INV_BASE = 8
HALO = 16
CONV_ROWS = 128
XE_ROWS = GP_TOK + CONV_ROWS


def _gdnprep_kernel(main_ref, left_ref, right_ref, gates_ref, cw_ref, alog_ref, dtb_ref,
                    rowtab_ref, coltab_ref,
                    u_ref, w_ref, qd_ref, kd_ref, qk_ref, gl_ref,
                    xe_ref, shift_ref, tri_ref, qn_ref, kn_ref, vv_ref, beta_ref, gc_ref, gct_ref,
                    eg_ref, ek_ref, *, n_tiles):
    i = pl.program_id(0)
    tile = jnp.minimum(i, n_tiles - 1)
    qkv_w = 3 * GDN_W
    staged = (qn_ref, kn_ref, vv_ref, beta_ref, gc_ref, gct_ref, eg_ref, ek_ref)

    @pl.when(i == 0)
    def _():
        m = lax.broadcasted_iota(jnp.int32, (CONV_K * CONV_ROWS, 2 * CONV_ROWS), 0)
        r = lax.broadcasted_iota(jnp.int32, (CONV_K * CONV_ROWS, 2 * CONV_ROWS), 1)
        g, j, r8 = m // (8 * CONV_K), (m // 8) % CONV_K, m % 8
        hit = r == 8 * g + r8 + j + (HALO - CONV_K // 2)
        shift_ref[...] = jnp.where(hit, 1.0, 0.0).astype(BF16)
        ti = lax.broadcasted_iota(jnp.int32, (2 * GP_TOK, GP_TOK), 0) % GP_TOK
        tj = lax.broadcasted_iota(jnp.int32, (2 * GP_TOK, GP_TOK), 1)
        is_lower = lax.broadcasted_iota(jnp.int32, (2 * GP_TOK, GP_TOK), 0) < GP_TOK
        ordered = (is_lower & (ti >= tj)) | (jnp.logical_not(is_lower) & (ti <= tj))
        keep = ((ti // CHUNK) == (tj // CHUNK)) & ordered
        tri_ref[...] = jnp.where(keep, 1.0, 0.0).astype(BF16)
        xe_ref[HALO + GP_TOK + HALO:, :] = jnp.zeros(
            (XE_ROWS - GP_TOK - 2 * HALO, qkv_w), BF16)
        for ref in staged:
            ref[1] = jnp.zeros(ref.shape[1:], ref.dtype)

    ii = lax.broadcasted_iota(jnp.int32, (CHUNK, LANES), 0)
    jj = lax.broadcasted_iota(jnp.int32, (CHUNK, LANES), 1) % CHUNK
    lo = lax.broadcasted_iota(jnp.int32, (1, LANES), 1) < CHUNK
    lo_wide = lax.broadcasted_iota(jnp.int32, (1, 2 * LANES), 1) < LANES
    eye = jnp.where(ii == jj, 1.0, 0.0).astype(F32)
    bodies = [(d, p) for d in range(2) for p in range(GDN_HEADS // 2)]
    same_block = lambda size: (ii // size) == (jj // size)
    base_blocks = same_block(INV_BASE)
    merge_masks = []
    size = INV_BASE
    while size < CHUNK:
        merge_masks.append(same_block(2 * size) & jnp.logical_not(same_block(size)))
        size *= 2

    def blockdiag(y, first):
        z = jnp.zeros_like(y)
        return jnp.concatenate([jnp.where(first, y, z), jnp.where(first, z, y)], axis=0)

    def prepare(slot):
        lh = left_ref[:, 0:qkv_w]
        rh = right_ref[:, 0:qkv_w]
        xe_ref[0:HALO, :] = jnp.where(tile > 0, lh, jnp.zeros_like(lh))
        xe_ref[HALO:HALO + GP_TOK, :] = main_ref[:, 0:qkv_w]
        xe_ref[HALO + GP_TOK:HALO + GP_TOK + HALO, :] = jnp.where(
            tile < n_tiles - 1, rh, jnp.zeros_like(rh))

        row_lanes = lax.broadcasted_iota(jnp.int32, (1, LANES), 1) < ROPE_AXIS_DIM

        def rope_table(kind):
            return jnp.concatenate(
                [jnp.where(row_lanes, rowtab_ref[kind * GP_CH + c:kind * GP_CH + c + 1, :],
                           coltab_ref[kind]) for c in range(GP_CH)], axis=0)

        cos, sina, sinb = rope_table(0), rope_table(1), rope_table(2)
        pending = {}
        for cb, b in [(cb, b) for cb in range(qkv_w // LANES) for b in range(GP_TOK // CONV_ROWS)]:
            cols = slice(cb * LANES, (cb + 1) * LANES)
            rows = slice(b * CONV_ROWS, (b + 1) * CONV_ROWS)
            if cb % 2 == 0:
                cols2 = slice(cb * LANES, (cb + 2) * LANES)
                taps2 = _dot(shift_ref[...], xe_ref[b * CONV_ROWS:(b + 2) * CONV_ROWS, cols2])
                wts = [jnp.broadcast_to(cw_ref[j:j + 1, cols2], (8, 2 * LANES))
                       for j in range(CONV_K)]
                groups = []
                for g in range(CONV_ROWS // 8):
                    acc2 = None
                    for j in range(CONV_K):
                        r0 = (g * CONV_K + j) * 8
                        term = wts[j] * taps2[r0:r0 + 8]
                        acc2 = term if acc2 is None else acc2 + term
                    groups.append(acc2)
                pending[b] = _silu(jnp.concatenate(groups, axis=0))
            y = pending[b][:, (cb % 2) * LANES:(cb % 2 + 1) * LANES]
            if cb < 2 * GDN_HEADS:
                y = y * lax.rsqrt(jnp.sum(y * y, axis=-1, keepdims=True) + EPS)
                y = (y * cos[rows] + pltpu.roll(y, LANES - ROPE_AXIS_DIM // 2, 1) * sina[rows]
                     + pltpu.roll(y, ROPE_AXIS_DIM // 2, 1) * sinb[rows])
                if cb < GDN_HEADS:
                    qn_ref[slot, rows, cols] = y * (GDN_DK ** -0.5)
                else:
                    kn_ref[slot, rows, (cb - GDN_HEADS) * LANES:(cb - GDN_HEADS + 1) * LANES] = y
            else:
                vv_ref[slot, rows,
                       (cb - 2 * GDN_HEADS) * LANES:(cb - 2 * GDN_HEADS + 1) * LANES] = y
            yield

        gates = gates_ref[...]
        beta_ref[slot] = 0.5 + 0.5 * jnp.tanh(0.5 * gates)
        xa = gates + dtb_ref[...]
        softplus = jnp.maximum(xa, 0.0) + jnp.log1p(jnp.exp(-jnp.abs(xa)))
        g = -jnp.exp(alog_ref[...]) * softplus

        g3 = _split3(g)
        lower = tri_ref[0:GP_TOK, :]
        upper = tri_ref[GP_TOK:2 * GP_TOK, :]
        gc_f = _dot(lower, g3[0]) + (_dot(lower, g3[1]) + _dot(lower, g3[2]))
        gc_b = _dot(upper, g3[0]) + (_dot(upper, g3[1]) + _dot(upper, g3[2]))
        gc_ref[slot, 0] = gc_f
        gc_ref[slot, 1] = gc_b
        yield
        for c in range(GP_CH):
            rows = slice(c * CHUNK, (c + 1) * CHUNK)
            for d, gc in enumerate((gc_f, gc_b)):
                blk = gc[rows]
                gct_ref[slot, c, d] = jnp.concatenate(
                    [blk, pltpu.roll(blk, LANES - 1, 1)], axis=0).T
            ek_ref[slot, 0, rows] = jnp.exp(gc_f[(c + 1) * CHUNK - 1:(c + 1) * CHUNK, :] - gc_f[rows])
            ek_ref[slot, 1, rows] = jnp.exp(gc_b[c * CHUNK:c * CHUNK + 1, :] - gc_b[rows])
            yield
        eg_ref[slot, 0] = jnp.exp(gc_f)
        eg_ref[slot, 1] = jnp.exp(gc_b)

    def solve(slot):
        gl_ref[...] = jnp.zeros_like(gl_ref)
        for c in range(GP_CH):
            gl_ref[c, 0:1, :] = eg_ref[slot, 0, (c + 1) * CHUNK - 1:(c + 1) * CHUNK, :]
            gl_ref[c, 1:2, :] = eg_ref[slot, 1, c * CHUNK:c * CHUNK + 1, :]
        for first_chunk in range(0, GP_CH, GP_LOCK):
            yield from solve_group(slot, range(first_chunk, first_chunk + GP_LOCK))

    def solve_group(slot, chunks):
        a_l, kbeg_l, vb_l, where_l = [], [], [], []
        for c, (d, p) in [(c, b) for c in chunks for b in bodies]:
            tok = slice(c * CHUNK, (c + 1) * CHUNK)
            incl = (ii >= jj) if d == 0 else (ii <= jj)
            strict = (ii > jj) if d == 0 else (ii < jj)
            pl2 = slice(2 * p * LANES, (2 * p + 2) * LANES)
            lb = d * GDN_HEADS + 2 * p
            lg = 2 * GDN_HEADS + lb
            q = qn_ref[slot, tok, pl2]
            k = kn_ref[slot, tok, pl2]
            v = vv_ref[slot, tok, pl2]
            beta = jnp.where(lo_wide, beta_ref[slot, tok, lb:lb + 1],
                             beta_ref[slot, tok, lb + 1:lb + 2])
            gcol = jnp.where(lo, gc_ref[slot, d, tok, lg:lg + 1], gc_ref[slot, d, tok, lg + 1:lg + 2])
            grow = gct_ref[slot, c, d, lg:lg + 1, :]
            eg = jnp.where(lo_wide, eg_ref[slot, d, tok, lg:lg + 1],
                           eg_ref[slot, d, tok, lg + 1:lg + 2])
            ek = jnp.where(lo_wide, ek_ref[slot, d, tok, lg:lg + 1],
                           ek_ref[slot, d, tok, lg + 1:lg + 2])
            dec = jnp.exp(jnp.where(incl, gcol - grow, NEG))
            kb = k * beta
            k_nt = blockdiag(k.astype(BF16), lo_wide)
            kq = _dot_nt(jnp.concatenate([kb.astype(BF16), q.astype(BF16)], axis=0), k_nt)
            a = jnp.where(strict, kq[0:CHUNK] * dec, 0.0)
            qk = kq[CHUNK:2 * CHUNK] * dec
            qk_ref[d, tok, p * LANES:(p + 1) * LANES] = qk.astype(BF16)
            qd_ref[d, tok, pl2] = (q * eg).astype(BF16)
            kd_ref[d, tok, pl2] = (k * ek).astype(BF16)
            a_l.append(a)
            kbeg_l.append((kb * eg).astype(BF16))
            vb_l.append((v * beta).astype(BF16))
            where_l.append((d, tok, pl2))
            if p == GDN_HEADS // 2 - 1:
                yield
        x_l = [jnp.where(base_blocks, -a, 0.0) for a in a_l]
        t_l = [eye + x for x in x_l]
        x_l = [x.astype(BF16) for x in x_l]
        x_l = [_dot(x, blockdiag(x, lo)).astype(BF16) for x in x_l]
        yield
        r_l = [_dot(jnp.concatenate([x, t.astype(BF16)], axis=0), blockdiag(x, lo))
               for x, t in zip(x_l, t_l)]
        x_l = [r[0:CHUNK].astype(BF16) for r in r_l]
        t_l = [t + r[CHUNK:2 * CHUNK] for t, r in zip(t_l, r_l)]
        yield
        t_l = [t + _dot(t.astype(BF16), blockdiag(x, lo)) for t, x in zip(t_l, x_l)]
        yield
        for merged in merge_masks:
            t16_l = [t.astype(BF16) for t in t_l]
            te_l = [_dot(t16, blockdiag(jnp.where(merged, a, 0.0).astype(BF16), lo))
                    for t16, a in zip(t16_l, a_l)]
            yield
            t_l = [t - _dot(te.astype(BF16), blockdiag(t16, lo))
                   for t, te, t16 in zip(t_l, te_l, t16_l)]
            yield
        for n_done, ((d, tok, pl2), t, vb, kbeg) in enumerate(zip(where_l, t_l, vb_l, kbeg_l)):
            t16 = t.astype(BF16)
            uw = _dot(t16, jnp.concatenate([blockdiag(vb, lo_wide), blockdiag(kbeg, lo_wide)],
                                           axis=1))
            u_ref[d, tok, pl2] = uw[:, 0:2 * LANES]
            w_ref[d, tok, pl2] = uw[:, 2 * LANES:4 * LANES].astype(BF16)
            if n_done % len(bodies) == len(bodies) - 1:
                yield

    def interleave(*stages):
        live = list(stages)
        while live:
            for g in list(live):
                if next(g, live) is live:
                    live.remove(g)

    for parity in range(2):
        @pl.when(i % 2 == parity)
        def _():
            interleave(solve(1 - parity), prepare(parity))


def _gdnprep(proj, gates, conv_w, alog, dtb, rowtab, coltab):
    n = proj.shape[0]
    nt = n // GP_TOK
    hb = GP_TOK // HALO
    nhalo = n // HALO
    qkv_w = 3 * GDN_W
    cur = lambda i: jnp.minimum(i, nt - 1)
    done = lambda i: jnp.maximum(i - 1, 0)
    tok_spec = lambda w: pl.BlockSpec((GP_TOK, w), lambda i: (cur(i), 0))
    dir_spec = lambda w: pl.BlockSpec((2, GP_TOK, w), lambda i: (0, done(i), 0))
    stage = lambda *shape: pltpu.VMEM((2,) + shape, F32)
    return pl.pallas_call(
        functools.partial(_gdnprep_kernel, n_tiles=nt),
        grid=(nt + 1,),
        in_specs=[pl.BlockSpec((GP_TOK, PROJ_W // 2), lambda i: (cur(i), 1)),
                  pl.BlockSpec((HALO, PROJ_W // 2),
                               lambda i: (jnp.maximum(cur(i) * hb - 1, 0), 1)),
                  pl.BlockSpec((HALO, PROJ_W // 2),
                               lambda i: (jnp.minimum((cur(i) + 1) * hb, nhalo - 1), 1)),
                  tok_spec(LANES),
                  pl.BlockSpec((8, qkv_w), lambda i: (0, 0)),
                  pl.BlockSpec((1, LANES), lambda i: (0, 0)),
                  pl.BlockSpec((1, LANES), lambda i: (0, 0)),
                  pl.BlockSpec((None, 16, LANES), lambda i: (cur(i), 0, 0)),
                  pl.BlockSpec((3, GRID_W, LANES), lambda i: (0, 0, 0))],
        out_specs=[dir_spec(GDN_W), dir_spec(GDN_W), dir_spec(GDN_W), dir_spec(GDN_W),
                   dir_spec(GDN_HEADS * CHUNK),
                   pl.BlockSpec((GP_CH, 8, LANES), lambda i: (done(i), 0, 0))],
        out_shape=[jax.ShapeDtypeStruct((2, n, GDN_W), F32),
                   jax.ShapeDtypeStruct((2, n, GDN_W), BF16),
                   jax.ShapeDtypeStruct((2, n, GDN_W), BF16),
                   jax.ShapeDtypeStruct((2, n, GDN_W), BF16),
                   jax.ShapeDtypeStruct((2, n, GDN_HEADS * CHUNK), BF16),
                   jax.ShapeDtypeStruct((n // CHUNK, 8, LANES), F32)],
        scratch_shapes=[pltpu.VMEM((XE_ROWS, qkv_w), BF16),
                        pltpu.VMEM((CONV_K * CONV_ROWS, 2 * CONV_ROWS), BF16),
                        pltpu.VMEM((2 * GP_TOK, GP_TOK), BF16),
                        stage(GP_TOK, GDN_W), stage(GP_TOK, GDN_W), stage(GP_TOK, GDN_W),
                        stage(GP_TOK, LANES),
                        stage(2, GP_TOK, LANES),
                        stage(GP_CH, 2, LANES, LANES),
                        stage(2, GP_TOK, LANES),
                        stage(2, GP_TOK, LANES)],
        compiler_params=pltpu.CompilerParams(
            dimension_semantics=("arbitrary",), vmem_limit_bytes=VMEM_LIMIT),
        name="gdnprep",
    )(proj, proj, proj, gates, conv_w, alog, dtb, rowtab, coltab)


SC_CH_MAX = 16


def _scan_kernel(gl_ref, s0_ref, uf_ref, wf_ref, qdf_ref, kdf_ref, qkf_ref,
                 ub_ref, wb_ref, qdb_ref, kdb_ref, qkb_ref,
                 of_ref, ob_ref, sfin_ref, s_ref, *, SC_CH):
    n = pl.program_id(0)
    nsteps = pl.num_programs(0)
    nch = nsteps * SC_CH

    @pl.when(n == 0)
    def _():
        s_ref[...] = s0_ref[...]

    streams = ((uf_ref, wf_ref, qdf_ref, kdf_ref, qkf_ref, of_ref),
               (ub_ref, wb_ref, qdb_ref, kdb_ref, qkb_ref, ob_ref))
    lo_wide = lax.broadcasted_iota(jnp.int32, (1, 2 * LANES), 1) < LANES
    chains = [(d, h) for d in range(2) for h in range(GDN_HEADS)]
    hl = lambda h: slice(h * LANES, (h + 1) * LANES)
    for j in range(SC_CH):
        local = (j, SC_CH - 1 - j)
        chunk = (n * SC_CH + j, (nsteps - 1 - n) * SC_CH + SC_CH - 1 - j)
        tok = [slice(c * CHUNK, (c + 1) * CHUNK) for c in local]
        s32 = [s_ref[d, h] for d, h in chains]
        s16 = [s.astype(BF16) for s in s32]
        wq = [_dot(jnp.concatenate([streams[d][1][tok[d], hl(h)], streams[d][2][tok[d], hl(h)]],
                                   axis=0), s)
              for (d, h), s in zip(chains, s16)]
        v16 = [(streams[d][0][tok[d], hl(h)] - x[0:CHUNK]).astype(BF16)
               for (d, h), x in zip(chains, wq)]
        upd = [_dot_tn(streams[d][3][tok[d], hl(h)], v) for (d, h), v in zip(chains, v16)]
        for (d, h), s, x in zip(chains, s32, upd):
            decay = gl_ref[(d * nch + chunk[d]) * GDN_HEADS + h]
            s_ref[d, h] = s * decay + x
        inter = [x[CHUNK:2 * CHUNK] for x in wq]
        for d in range(2):
            qk_ref, o_ref = streams[d][4], streams[d][5]
            for p in range(GDN_HEADS // 2):
                v_pair = jnp.concatenate(
                    v16[d * GDN_HEADS + 2 * p:d * GDN_HEADS + 2 * p + 2], axis=1)
                zero = jnp.zeros_like(v_pair)
                v_bd = jnp.concatenate([jnp.where(lo_wide, v_pair, zero),
                                        jnp.where(lo_wide, zero, v_pair)], axis=0)
                o_intra = _dot(qk_ref[tok[d], p * LANES:(p + 1) * LANES], v_bd)
                for jh in range(2):
                    h = 2 * p + jh
                    o_ref[tok[d], hl(h)] = (inter[d * GDN_HEADS + h]
                                            + o_intra[:, jh * LANES:(jh + 1) * LANES]
                                            ).astype(o_ref.dtype)

    @pl.when(n == nsteps - 1)
    def _():
        sfin_ref[...] = s_ref[...]


def _scan(gl, s0, u, w, qd, kd, qk):
    n = u.shape[1]
    sc_ch = min(SC_CH_MAX, n // CHUNK)
    tok = sc_ch * CHUNK
    nsteps = n // tok
    fwd = lambda wd: pl.BlockSpec((None, tok, wd), lambda i: (0, i, 0))
    bwd = lambda wd: pl.BlockSpec((None, tok, wd), lambda i: (1, nsteps - 1 - i, 0))
    qkw = GDN_HEADS * CHUNK
    state_spec = pl.BlockSpec((2, GDN_HEADS, GDN_DK, GDN_DV), lambda i: (0, 0, 0, 0))
    return pl.pallas_call(
        functools.partial(_scan_kernel, SC_CH=sc_ch),
        grid=(nsteps,),
        in_specs=[pl.BlockSpec(memory_space=pltpu.SMEM), state_spec,
                  fwd(GDN_W), fwd(GDN_W), fwd(GDN_W), fwd(GDN_W), fwd(qkw),
                  bwd(GDN_W), bwd(GDN_W), bwd(GDN_W), bwd(GDN_W), bwd(qkw)],
        out_specs=[pl.BlockSpec((tok, GDN_W), lambda i: (i, 0)),
                   pl.BlockSpec((tok, GDN_W), lambda i: (nsteps - 1 - i, 0)),
                   state_spec],
        out_shape=[jax.ShapeDtypeStruct((n, GDN_W), BF16),
                   jax.ShapeDtypeStruct((n, GDN_W), BF16),
                   jax.ShapeDtypeStruct((2, GDN_HEADS, GDN_DK, GDN_DV), F32)],
        scratch_shapes=[pltpu.VMEM((2, GDN_HEADS, GDN_DK, GDN_DV), F32)],
        compiler_params=pltpu.CompilerParams(dimension_semantics=("arbitrary",)),
        name="scan",
    )(gl, s0, u, w, qd, kd, qk, u, w, qd, kd, qk)


def _outproj_kernel(x_ref, na_ref, of_ref, ob_ref, gz_ref, gnw_ref, wout_ref, gpost_ref, mod_ref,
                    o_ref):
    og = of_ref[...].astype(F32) + ob_ref[...].astype(F32)
    gz = gz_ref[...].astype(F32)
    gnw = gnw_ref[...]
    parts = []
    for h in range(GDN_HEADS):
        seg = og[:, h * LANES:(h + 1) * LANES]
        seg = seg * lax.rsqrt(jnp.mean(seg * seg, axis=-1, keepdims=True) + EPS)
        parts.append(seg * gnw)
    gd = (jnp.concatenate(parts, axis=-1) * _silu(gz)).astype(BF16)
    y = _dot(jnp.concatenate([na_ref[...], gd], axis=-1), wout_ref[...])
    yn = y * lax.rsqrt(jnp.mean(y * y, axis=-1, keepdims=True) + EPS)
    gate = mod_ref[0:1, 2 * D_MODEL:3 * D_MODEL]
    o_ref[...] = x_ref[...] + gate * (yn * gpost_ref[...])


def _outproj(x2, na, o_f, o_b, proj, gnw, w_out, gpost, mod, tm=512):
    n = x2.shape[0]
    gz_col = (PROJ_W - GDN_W) // GDN_W
    return pl.pallas_call(
        _outproj_kernel,
        grid=(n // tm,),
        in_specs=[pl.BlockSpec((tm, D_MODEL), lambda i: (i, 0)),
                  pl.BlockSpec((tm, NA_W), lambda i: (i, 0)),
                  pl.BlockSpec((tm, GDN_W), lambda i: (i, 0)),
                  pl.BlockSpec((tm, GDN_W), lambda i: (i, 0)),
                  pl.BlockSpec((tm, GDN_W), lambda i: (i, gz_col)),
                  pl.BlockSpec((1, LANES), lambda i: (0, 0)),
                  pl.BlockSpec((NA_W + GDN_W, D_MODEL), lambda i: (0, 0)),
                  pl.BlockSpec((1, D_MODEL), lambda i: (0, 0)),
                  pl.BlockSpec((8, 3 * D_MODEL), lambda i: (0, 0))],
        out_specs=pl.BlockSpec((tm, D_MODEL), lambda i: (i, 0)),
        out_shape=jax.ShapeDtypeStruct((n, D_MODEL), F32),
        compiler_params=pltpu.CompilerParams(
            dimension_semantics=("arbitrary",), vmem_limit_bytes=VMEM_LIMIT),
        name="outproj",
    )(x2, na, o_f, o_b, proj, gnw, w_out, gpost, mod)


def _rope_tables(rows, identity):
    inv_freq = ROPE_BASE ** (-jnp.arange(0, ROPE_AXIS_DIM, 2, dtype=F32) / ROPE_AXIS_DIM)

    def tables(count):
        ang = jnp.arange(count, dtype=F32)[:, None] * inv_freq[None, :]
        if identity:
            ang = jnp.zeros_like(ang)
        c, s = jnp.cos(ang), jnp.sin(ang)
        z = jnp.zeros_like(s)
        return (jnp.concatenate([c, c], -1), jnp.concatenate([-s, z], -1),
                jnp.concatenate([z, s], -1))

    half = jnp.zeros((rows, LANES // 2), F32)
    rowtab = jnp.stack([jnp.concatenate([t, half], -1) for t in tables(rows)], axis=0)
    rowtab = rowtab.reshape(3, rows // GP_CH, GP_CH, LANES).transpose(1, 0, 2, 3)
    rowtab = rowtab.reshape(rows // GP_CH, 3 * GP_CH, LANES)
    rowtab = jnp.pad(rowtab, ((0, 0), (0, 16 - 3 * GP_CH), (0, 0)))
    halfc = jnp.zeros((GRID_W, LANES // 2), F32)
    coltab = jnp.stack([jnp.concatenate([halfc, t], -1) for t in tables(GRID_W)], axis=0)
    return rowtab, coltab


def _lane_row(vals, offset):
    return jnp.zeros((1, LANES), F32).at[0, offset:offset + vals.shape[0]].set(vals)


def kernel(x, c, ctx, c_ctx, w_ada, b_ada, g_pre, g_post, w_in, conv_w, rpb, A_log, dt_bias,
           gdn_norm_w, w_out):
    n = x.shape[1]
    nc = ctx.shape[1]
    x2 = x[0]
    xc2 = ctx[0]

    cc = jnp.zeros((8, D_MODEL), F32).at[0].set(c[0]).at[1].set(c_ctx)
    mod = _ada(cc, w_ada[0], b_ada[0][None, :])

    gpre = g_pre[0][None, :]
    w_in_t = jnp.swapaxes(w_in, 1, 2)
    proj, gates, projc, gatesc = _inproj(x2, xc2, mod, gpre, w_in_t, tm=512)

    na = _natten(proj, projc, rpb[0].reshape(-1))

    cw = jnp.pad(conv_w[0], ((0, 8 - CONV_K), (0, 0)))
    alog = _lane_row(A_log[0].reshape(-1), 2 * GDN_HEADS)
    dtb = _lane_row(dt_bias[0].reshape(-1), 2 * GDN_HEADS)
    rowtab, coltab = _rope_tables(n // GRID_W, identity=False)
    rowtab_c, coltab_c = _rope_tables(nc // GRID_W, identity=True)

    def gl_rows(gl):
        return jnp.stack([gl[:, 0, 2 * GDN_HEADS:3 * GDN_HEADS],
                          gl[:, 1, 3 * GDN_HEADS:4 * GDN_HEADS]], axis=0).reshape(-1)

    uc, wc, qdc, kdc, qkc, glc = _gdnprep(projc, gatesc, cw, alog, dtb, rowtab_c, coltab_c)
    s0 = jnp.zeros((2, GDN_HEADS, GDN_DK, GDN_DV), F32)
    _, _, s_ctx = _scan(gl_rows(glc), s0, uc, wc, qdc, kdc, qkc)

    u, w, qd, kd, qk, gl = _gdnprep(proj, gates, cw, alog, dtb, rowtab, coltab)
    o_f, o_b, _ = _scan(gl_rows(gl), s_ctx, u, w, qd, kd, qk)

    gnw = gdn_norm_w[0][None, :]
    out = _outproj(x2, na, o_f, o_b, proj, gnw, w_out[0].astype(BF16), g_post[0][None, :], mod)
    return out[None]
```

```python
import functools
import math

import jax
import jax.numpy as jnp
from jax import lax
from jax.experimental import pallas as pl
from jax.experimental.pallas import tpu as pltpu

F32 = jnp.float32
BF16 = jnp.bfloat16

D_MODEL = 1024
GRID_W = 64
NA_HEADS = 8
NA_DH = 64
NA_W = NA_HEADS * NA_DH
NA_KH = 8
NA_KW = 16
GDN_HEADS = 4
GDN_DK = 128
GDN_DV = 128
GDN_W = GDN_HEADS * GDN_DV
CHUNK = 64
CONV_K = 5
ROPE_AXIS_DIM = GDN_DK // 2
ROPE_BASE = 10000.0
EPS = 1e-6
PROJ_W = 4 * NA_W + 3 * GDN_W + GDN_W
GATE_W = 2 * 2 * GDN_HEADS
LANES = 128
NEG = -1e30

VMEM_LIMIT = 56 * 1024 * 1024


def _silu(x):
    h = 0.5 * x
    return h + h * jnp.tanh(h)


def _dot(a, b):
    return jnp.dot(a, b, preferred_element_type=F32)


def _dot_nt(a, b):
    return lax.dot_general(a, b, (((1,), (1,)), ((), ())), preferred_element_type=F32)


def _dot_tn(a, b):
    return lax.dot_general(a, b, (((0,), (0,)), ((), ())), preferred_element_type=F32)


def _split2(x):
    hi = x.astype(BF16)
    lo = (x - hi.astype(F32)).astype(BF16)
    return hi, lo


def _split3(x):
    hi = x.astype(BF16)
    r = x - hi.astype(F32)
    mid = r.astype(BF16)
    lo = (r - mid.astype(F32)).astype(BF16)
    return hi, mid, lo


def _mm3(a, b):
    ah, al = _split2(a)
    bh, bl = _split2(b)
    return _dot(ah, bh) + (_dot(al, bh) + _dot(ah, bl))


def _ada_kernel(c_ref, w_ref, b_ref, o_ref):
    s = _silu(c_ref[...])
    o_ref[...] = _mm3(s, w_ref[...]) + b_ref[...]


def _ada(cc, w_ada, b_ada):
    tn = 512
    n = w_ada.shape[1]
    return pl.pallas_call(
        _ada_kernel,
        grid=(n // tn,),
        in_specs=[pl.BlockSpec((8, D_MODEL), lambda j: (0, 0)),
                  pl.BlockSpec((D_MODEL, tn), lambda j: (0, j)),
                  pl.BlockSpec((1, tn), lambda j: (0, j))],
        out_specs=pl.BlockSpec((8, tn), lambda j: (0, j)),
        out_shape=jax.ShapeDtypeStruct((8, n), F32),
        name="ada",
    )(cc, w_ada, b_ada)


LOG2E = math.log2(math.e)
NA_QSCALE = NA_DH ** -0.5 * LOG2E


def _inproj_kernel(x_ref, xc_ref, mod_ref, gpre_ref, win_ref,
                   proj_ref, gates_ref, projc_ref, gatesc_ref, w_ref):
    def project(x, row, proj_out, gates_out):
        xn = x * lax.rsqrt(jnp.mean(x * x, axis=-1, keepdims=True) + EPS)
        shift = mod_ref[row:row + 1, 0:D_MODEL]
        scale = mod_ref[row:row + 1, D_MODEL:2 * D_MODEL]
        h = (xn * gpre_ref[...]) * (1.0 + scale) + shift
        hb = h.astype(BF16)
        nb = 512
        for j in range(PROJ_W // nb):
            y = _dot_nt(hb, w_ref[j * nb:(j + 1) * nb, :])
            if j * nb < NA_W:
                y = y * NA_QSCALE
            proj_out[:, j * nb:(j + 1) * nb] = y.astype(BF16)
        gates_out[...] = _dot_nt(hb, w_ref[PROJ_W:PROJ_W + LANES, :])

    @pl.when(pl.program_id(0) == 0)
    def _():
        nb = 512
        for j in range(PROJ_W // nb):
            w_ref[j * nb:(j + 1) * nb, :] = win_ref[j * nb:(j + 1) * nb, :].astype(BF16)
        w_ref[PROJ_W:PROJ_W + LANES, :] = jnp.zeros((LANES, D_MODEL), BF16)
        w_ref[PROJ_W:PROJ_W + GATE_W, :] = win_ref[PROJ_W:PROJ_W + GATE_W, :].astype(BF16)
        project(xc_ref[...], 1, projc_ref, gatesc_ref)

    project(x_ref[...], 0, proj_ref, gates_ref)


def _inproj(x2, xc2, mod, gpre, w_in, tm):
    n = x2.shape[0]
    nc = xc2.shape[0]
    return pl.pallas_call(
        _inproj_kernel,
        grid=(n // tm,),
        in_specs=[pl.BlockSpec((tm, D_MODEL), lambda i: (i, 0)),
                  pl.BlockSpec((nc, D_MODEL), lambda i: (0, 0)),
                  pl.BlockSpec((8, 3 * D_MODEL), lambda i: (0, 0)),
                  pl.BlockSpec((1, D_MODEL), lambda i: (0, 0)),
                  pl.BlockSpec((None, PROJ_W + GATE_W, D_MODEL), lambda i: (0, 0, 0),
                               pipeline_mode=pl.Buffered(1))],
        out_specs=[pl.BlockSpec((tm, PROJ_W), lambda i: (i, 0)),
                   pl.BlockSpec((tm, LANES), lambda i: (i, 0)),
                   pl.BlockSpec((nc, PROJ_W), lambda i: (0, 0)),
                   pl.BlockSpec((nc, LANES), lambda i: (0, 0))],
        out_shape=[jax.ShapeDtypeStruct((n, PROJ_W), BF16),
                   jax.ShapeDtypeStruct((n, LANES), F32),
                   jax.ShapeDtypeStruct((nc, PROJ_W), BF16),
                   jax.ShapeDtypeStruct((nc, LANES), F32)],
        scratch_shapes=[pltpu.VMEM((PROJ_W + LANES, D_MODEL), BF16)],
        compiler_params=pltpu.CompilerParams(
            dimension_semantics=("arbitrary",), vmem_limit_bytes=VMEM_LIMIT),
        name="inproj",
    )(x2, xc2, mod, gpre, w_in)


NA_RO = 2 * NA_KH - 1
NA_CO = 2 * NA_KW - 1
NA_ROWS_PER_ITER = 16


def _na_build_tables(rpb_ref, tab_ref, hp):
    qc = lax.broadcasted_iota(jnp.int32, (GRID_W, LANES), 0)
    kcol = lax.broadcasted_iota(jnp.int32, (GRID_W, LANES), 1) % GRID_W
    lo = lax.broadcasted_iota(jnp.int32, (1, LANES), 1) < GRID_W
    diff = kcol - qc + (NA_KW - 1)
    col_start = jnp.clip(qc - NA_KW // 2, 0, GRID_W - NA_KW)
    inwin = (kcol >= col_start) & (kcol < col_start + NA_KW)
    for hh in range(2):
        def ro_body(ro, carry):
            base = ((hp * 2 + hh) * NA_RO + ro) * NA_CO
            acc = jnp.zeros((GRID_W, LANES), F32)
            for j in range(NA_CO):
                val = jnp.where(lo, rpb_ref[base + j], rpb_ref[base + NA_CO + j])
                acc = jnp.where(diff == j, val, acc)
            tab_ref[hh, ro] = jnp.where(inwin, acc * LOG2E, NEG)
            return carry
        lax.fori_loop(0, NA_RO - 1, ro_body, 0)


def _na_kernel(rpb_ref, q_ref, k_ref, v_ref, z_ref, kc_ref, vc_ref, o_ref, tab_ref,
               *, rb_rows, rows):
    hp = pl.program_id(0)
    rb = pl.program_id(1)

    @pl.when(rb == 0)
    def _():
        _na_build_tables(rpb_ref, tab_ref, hp)

    lane = lax.broadcasted_iota(jnp.int32, (1, LANES), 1)
    first = lane < NA_DH
    kc = kc_ref[...]
    vc = vc_ref[...]
    win = NA_KH * GRID_W

    def body(it, carry):
        items = []
        for rr in range(NA_ROWS_PER_ITER):
            i = it * NA_ROWS_PER_ITER + rr
            r = rb * rb_rows + i
            rs = jnp.clip(r - NA_KH // 2, 0, rows - NA_KH)
            ro0 = NA_KH - 1 - (r - rs)
            t0 = pl.multiple_of(i * GRID_W, GRID_W)
            q = q_ref[pl.ds(t0, GRID_W), :]
            k0 = pl.multiple_of(rs * GRID_W, GRID_W)
            kw = k_ref[pl.ds(k0, win), :]
            vw = v_ref[pl.ds(k0, win), :]
            zq = jnp.zeros_like(q)
            qs = jnp.concatenate([jnp.where(first, q, zq), jnp.where(first, zq, q)], axis=0)
            items.append((t0, ro0, qs, kw, vw))
        s_all = [_dot_nt(qs, jnp.concatenate([kw, kc], axis=0)) for (_, _, qs, kw, _) in items]
        s_loc = [s[:, 0:win]
                 + jnp.concatenate(
                     [jnp.concatenate([tab_ref[hh, ro0 + 2 * m] for m in range(NA_KH // 2)], axis=1)
                      for hh in range(2)], axis=0)
                 for s, (_, ro0, _, _, _) in zip(s_all, items)]
        s_ctx = [s[:, win:] for s in s_all]
        mx = [jnp.maximum(jnp.max(a, axis=-1, keepdims=True), jnp.max(b, axis=-1, keepdims=True))
              for a, b in zip(s_loc, s_ctx)]
        p_loc = [jnp.exp2(a - m) for a, m in zip(s_loc, mx)]
        p_ctx = [jnp.exp2(b - m) for b, m in zip(s_ctx, mx)]
        inv = [1.0 / (jnp.sum(a, axis=-1, keepdims=True) + jnp.sum(b, axis=-1, keepdims=True))
               for a, b in zip(p_loc, p_ctx)]
        outs = [_dot(jnp.concatenate([a.astype(BF16), b.astype(BF16)], axis=1),
                     jnp.concatenate([it_[4], vc], axis=0)) * il
                for a, b, il, it_ in zip(p_loc, p_ctx, inv, items)]
        for (t0, _, _, _, _), o2 in zip(items, outs):
            o = jnp.where(first, o2[0:GRID_W], o2[GRID_W:2 * GRID_W])
            z = z_ref[pl.ds(t0, GRID_W), :].astype(F32)
            o_ref[pl.ds(t0, GRID_W), :] = (o * _silu(z)).astype(BF16)
        return carry

    lax.fori_loop(0, rb_rows // NA_ROWS_PER_ITER, body, 0)


def _natten(proj, projc, rpb_flat, rb_rows=32):
    n = proj.shape[0]
    nc = projc.shape[0]
    rows = n // GRID_W
    tq = rb_rows * GRID_W
    kcol = NA_W // LANES
    return pl.pallas_call(
        functools.partial(_na_kernel, rb_rows=rb_rows, rows=rows),
        grid=(NA_W // LANES, rows // rb_rows),
        in_specs=[pl.BlockSpec(memory_space=pltpu.SMEM),
                  pl.BlockSpec((tq, LANES), lambda hp, rb: (rb, hp)),
                  pl.BlockSpec((n, LANES), lambda hp, rb: (0, kcol + hp)),
                  pl.BlockSpec((n, LANES), lambda hp, rb: (0, 2 * kcol + hp)),
                  pl.BlockSpec((tq, LANES), lambda hp, rb: (rb, 3 * kcol + hp)),
                  pl.BlockSpec((nc, LANES), lambda hp, rb: (0, kcol + hp)),
                  pl.BlockSpec((nc, LANES), lambda hp, rb: (0, 2 * kcol + hp))],
        out_specs=pl.BlockSpec((tq, LANES), lambda hp, rb: (rb, hp)),
        out_shape=jax.ShapeDtypeStruct((n, NA_W), BF16),
        scratch_shapes=[pltpu.VMEM((2, NA_RO - 1, GRID_W, LANES), F32)],
        compiler_params=pltpu.CompilerParams(
            dimension_semantics=("arbitrary", "arbitrary"), vmem_limit_bytes=VMEM_LIMIT),
        name="natten",
    )(rpb_flat, proj, proj, proj, proj, projc, projc)


GP_TOK = 256
GP_CH = GP_TOK // CHUNK
GP_LOCK = 2
INV_BASE = 8
HALO = 16
CONV_ROWS = 128
XE_ROWS = GP_TOK + CONV_ROWS


def _gdnprep_kernel(main_ref, left_ref, right_ref, gates_ref, cw_ref, alog_ref, dtb_ref,
                    rowtab_ref, coltab_ref,
                    u_ref, w_ref, qd_ref, kd_ref, qk_ref, gl_ref,
                    xe_ref, shift_ref, tri_ref, qn_ref, kn_ref, vv_ref, beta_ref, gc_ref, gct_ref,
                    eg_ref, ek_ref, *, n_tiles):
    i = pl.program_id(0)
    tile = jnp.minimum(i, n_tiles - 1)
    qkv_w = 3 * GDN_W
    staged = (qn_ref, kn_ref, vv_ref, beta_ref, gc_ref, gct_ref, eg_ref, ek_ref)

    @pl.when(i == 0)
    def _():
        m = lax.broadcasted_iota(jnp.int32, (CONV_K * CONV_ROWS, 2 * CONV_ROWS), 0)
        r = lax.broadcasted_iota(jnp.int32, (CONV_K * CONV_ROWS, 2 * CONV_ROWS), 1)
        g, j, r8 = m // (8 * CONV_K), (m // 8) % CONV_K, m % 8
        hit = r == 8 * g + r8 + j + (HALO - CONV_K // 2)
        shift_ref[...] = jnp.where(hit, 1.0, 0.0).astype(BF16)
        ti = lax.broadcasted_iota(jnp.int32, (2 * GP_TOK, GP_TOK), 0) % GP_TOK
        tj = lax.broadcasted_iota(jnp.int32, (2 * GP_TOK, GP_TOK), 1)
        is_lower = lax.broadcasted_iota(jnp.int32, (2 * GP_TOK, GP_TOK), 0) < GP_TOK
        ordered = (is_lower & (ti >= tj)) | (jnp.logical_not(is_lower) & (ti <= tj))
        keep = ((ti // CHUNK) == (tj // CHUNK)) & ordered
        tri_ref[...] = jnp.where(keep, 1.0, 0.0).astype(BF16)
        xe_ref[HALO + GP_TOK + HALO:, :] = jnp.zeros(
            (XE_ROWS - GP_TOK - 2 * HALO, qkv_w), BF16)
        for ref in staged:
            ref[1] = jnp.zeros(ref.shape[1:], ref.dtype)

    ii = lax.broadcasted_iota(jnp.int32, (CHUNK, LANES), 0)
    jj = lax.broadcasted_iota(jnp.int32, (CHUNK, LANES), 1) % CHUNK
    lo = lax.broadcasted_iota(jnp.int32, (1, LANES), 1) < CHUNK
    lo_wide = lax.broadcasted_iota(jnp.int32, (1, 2 * LANES), 1) < LANES
    eye = jnp.where(ii == jj, 1.0, 0.0).astype(F32)
    bodies = [(d, p) for d in range(2) for p in range(GDN_HEADS // 2)]
    same_block = lambda size: (ii // size) == (jj // size)
    base_blocks = same_block(INV_BASE)
    merge_masks = []
    size = INV_BASE
    while size < CHUNK:
        merge_masks.append(same_block(2 * size) & jnp.logical_not(same_block(size)))
        size *= 2

    def blockdiag(y, first):
        z = jnp.zeros_like(y)
        return jnp.concatenate([jnp.where(first, y, z), jnp.where(first, z, y)], axis=0)

    def prepare(slot):
        lh = left_ref[:, 0:qkv_w]
        rh = right_ref[:, 0:qkv_w]
        xe_ref[0:HALO, :] = jnp.where(tile > 0, lh, jnp.zeros_like(lh))
        xe_ref[HALO:HALO + GP_TOK, :] = main_ref[:, 0:qkv_w]
        xe_ref[HALO + GP_TOK:HALO + GP_TOK + HALO, :] = jnp.where(
            tile < n_tiles - 1, rh, jnp.zeros_like(rh))

        row_lanes = lax.broadcasted_iota(jnp.int32, (1, LANES), 1) < ROPE_AXIS_DIM

        def rope_table(kind):
            return jnp.concatenate(
                [jnp.where(row_lanes, rowtab_ref[kind * GP_CH + c:kind * GP_CH + c + 1, :],
                           coltab_ref[kind]) for c in range(GP_CH)], axis=0)

        cos, sina, sinb = rope_table(0), rope_table(1), rope_table(2)
        pending = {}
        for cb, b in [(cb, b) for cb in range(qkv_w // LANES) for b in range(GP_TOK // CONV_ROWS)]:
            cols = slice(cb * LANES, (cb + 1) * LANES)
            rows = slice(b * CONV_ROWS, (b + 1) * CONV_ROWS)
            if cb % 2 == 0:
                cols2 = slice(cb * LANES, (cb + 2) * LANES)
                taps2 = _dot(shift_ref[...], xe_ref[b * CONV_ROWS:(b + 2) * CONV_ROWS, cols2])
                wts = [jnp.broadcast_to(cw_ref[j:j + 1, cols2], (8, 2 * LANES))
                       for j in range(CONV_K)]
                groups = []
                for g in range(CONV_ROWS // 8):
                    acc2 = None
                    for j in range(CONV_K):
                        r0 = (g * CONV_K + j) * 8
                        term = wts[j] * taps2[r0:r0 + 8]
                        acc2 = term if acc2 is None else acc2 + term
                    groups.append(acc2)
                pending[b] = _silu(jnp.concatenate(groups, axis=0))
            y = pending[b][:, (cb % 2) * LANES:(cb % 2 + 1) * LANES]
            if cb < 2 * GDN_HEADS:
                y = y * lax.rsqrt(jnp.sum(y * y, axis=-1, keepdims=True) + EPS)
                y = (y * cos[rows] + pltpu.roll(y, LANES - ROPE_AXIS_DIM // 2, 1) * sina[rows]
                     + pltpu.roll(y, ROPE_AXIS_DIM // 2, 1) * sinb[rows])
                if cb < GDN_HEADS:
                    qn_ref[slot, rows, cols] = y * (GDN_DK ** -0.5)
                else:
                    kn_ref[slot, rows, (cb - GDN_HEADS) * LANES:(cb - GDN_HEADS + 1) * LANES] = y
            else:
                vv_ref[slot, rows,
                       (cb - 2 * GDN_HEADS) * LANES:(cb - 2 * GDN_HEADS + 1) * LANES] = y
            yield

        gates = gates_ref[...]
        beta_ref[slot] = 0.5 + 0.5 * jnp.tanh(0.5 * gates)
        xa = gates + dtb_ref[...]
        softplus = jnp.maximum(xa, 0.0) + jnp.log1p(jnp.exp(-jnp.abs(xa)))
        g = -jnp.exp(alog_ref[...]) * softplus

        g3 = _split3(g)
        lower = tri_ref[0:GP_TOK, :]
        upper = tri_ref[GP_TOK:2 * GP_TOK, :]
        gc_f = _dot(lower, g3[0]) + (_dot(lower, g3[1]) + _dot(lower, g3[2]))
        gc_b = _dot(upper, g3[0]) + (_dot(upper, g3[1]) + _dot(upper, g3[2]))
        gc_ref[slot, 0] = gc_f
        gc_ref[slot, 1] = gc_b
        yield
        for c in range(GP_CH):
            rows = slice(c * CHUNK, (c + 1) * CHUNK)
            for d, gc in enumerate((gc_f, gc_b)):
                blk = gc[rows]
                gct_ref[slot, c, d] = jnp.concatenate(
                    [blk, pltpu.roll(blk, LANES - 1, 1)], axis=0).T
            ek_ref[slot, 0, rows] = jnp.exp(gc_f[(c + 1) * CHUNK - 1:(c + 1) * CHUNK, :] - gc_f[rows])
            ek_ref[slot, 1, rows] = jnp.exp(gc_b[c * CHUNK:c * CHUNK + 1, :] - gc_b[rows])
            yield
        eg_ref[slot, 0] = jnp.exp(gc_f)
        eg_ref[slot, 1] = jnp.exp(gc_b)

    def solve(slot):
        gl_ref[...] = jnp.zeros_like(gl_ref)
        for c in range(GP_CH):
            gl_ref[c, 0:1, :] = eg_ref[slot, 0, (c + 1) * CHUNK - 1:(c + 1) * CHUNK, :]
            gl_ref[c, 1:2, :] = eg_ref[slot, 1, c * CHUNK:c * CHUNK + 1, :]
        for first_chunk in range(0, GP_CH, GP_LOCK):
            yield from solve_group(slot, range(first_chunk, first_chunk + GP_LOCK))

    def solve_group(slot, chunks):
        a_l, kbeg_l, vb_l, where_l = [], [], [], []
        for c, (d, p) in [(c, b) for c in chunks for b in bodies]:
            tok = slice(c * CHUNK, (c + 1) * CHUNK)
            incl = (ii >= jj) if d == 0 else (ii <= jj)
            strict = (ii > jj) if d == 0 else (ii < jj)
            pl2 = slice(2 * p * LANES, (2 * p + 2) * LANES)
            lb = d * GDN_HEADS + 2 * p
            lg = 2 * GDN_HEADS + lb
            q = qn_ref[slot, tok, pl2]
            k = kn_ref[slot, tok, pl2]
            v = vv_ref[slot, tok, pl2]
            beta = jnp.where(lo_wide, beta_ref[slot, tok, lb:lb + 1],
                             beta_ref[slot, tok, lb + 1:lb + 2])
            gcol = jnp.where(lo, gc_ref[slot, d, tok, lg:lg + 1], gc_ref[slot, d, tok, lg + 1:lg + 2])
            grow = gct_ref[slot, c, d, lg:lg + 1, :]
            eg = jnp.where(lo_wide, eg_ref[slot, d, tok, lg:lg + 1],
                           eg_ref[slot, d, tok, lg + 1:lg + 2])
            ek = jnp.where(lo_wide, ek_ref[slot, d, tok, lg:lg + 1],
                           ek_ref[slot, d, tok, lg + 1:lg + 2])
            dec = jnp.exp(jnp.where(incl, gcol - grow, NEG))
            kb = k * beta
            k_nt = blockdiag(k.astype(BF16), lo_wide)
            kq = _dot_nt(jnp.concatenate([kb.astype(BF16), q.astype(BF16)], axis=0), k_nt)
            a = jnp.where(strict, kq[0:CHUNK] * dec, 0.0)
            qk = kq[CHUNK:2 * CHUNK] * dec
            qk_ref[d, tok, p * LANES:(p + 1) * LANES] = qk.astype(BF16)
            qd_ref[d, tok, pl2] = (q * eg).astype(BF16)
            kd_ref[d, tok, pl2] = (k * ek).astype(BF16)
            a_l.append(a)
            kbeg_l.append((kb * eg).astype(BF16))
            vb_l.append((v * beta).astype(BF16))
            where_l.append((d, tok, pl2))
            if p == GDN_HEADS // 2 - 1:
                yield
        x_l = [jnp.where(base_blocks, -a, 0.0) for a in a_l]
        t_l = [eye + x for x in x_l]
        x_l = [x.astype(BF16) for x in x_l]
        x_l = [_dot(x, blockdiag(x, lo)).astype(BF16) for x in x_l]
        yield
        r_l = [_dot(jnp.concatenate([x, t.astype(BF16)], axis=0), blockdiag(x, lo))
               for x, t in zip(x_l, t_l)]
        x_l = [r[0:CHUNK].astype(BF16) for r in r_l]
        t_l = [t + r[CHUNK:2 * CHUNK] for t, r in zip(t_l, r_l)]
        yield
        t_l = [t + _dot(t.astype(BF16), blockdiag(x, lo)) for t, x in zip(t_l, x_l)]
        yield
        for merged in merge_masks:
            t16_l = [t.astype(BF16) for t in t_l]
            te_l = [_dot(t16, blockdiag(jnp.where(merged, a, 0.0).astype(BF16), lo))
                    for t16, a in zip(t16_l, a_l)]
            yield
            t_l = [t - _dot(te.astype(BF16), blockdiag(t16, lo))
                   for t, te, t16 in zip(t_l, te_l, t16_l)]
            yield
        for n_done, ((d, tok, pl2), t, vb, kbeg) in enumerate(zip(where_l, t_l, vb_l, kbeg_l)):
            t16 = t.astype(BF16)
            uw = _dot(t16, jnp.concatenate([blockdiag(vb, lo_wide), blockdiag(kbeg, lo_wide)],
                                           axis=1))
            u_ref[d, tok, pl2] = uw[:, 0:2 * LANES]
            w_ref[d, tok, pl2] = uw[:, 2 * LANES:4 * LANES].astype(BF16)
            if n_done % len(bodies) == len(bodies) - 1:
                yield

    def interleave(*stages):
        live = list(stages)
        while live:
            for g in list(live):
                if next(g, live) is live:
                    live.remove(g)

    for parity in range(2):
        @pl.when(i % 2 == parity)
        def _():
            interleave(solve(1 - parity), prepare(parity))


def _gdnprep(proj, gates, conv_w, alog, dtb, rowtab, coltab):
    n = proj.shape[0]
    nt = n // GP_TOK
    hb = GP_TOK // HALO
    nhalo = n // HALO
    qkv_w = 3 * GDN_W
    cur = lambda i: jnp.minimum(i, nt - 1)
    done = lambda i: jnp.maximum(i - 1, 0)
    tok_spec = lambda w: pl.BlockSpec((GP_TOK, w), lambda i: (cur(i), 0))
    dir_spec = lambda w: pl.BlockSpec((2, GP_TOK, w), lambda i: (0, done(i), 0))
    stage = lambda *shape: pltpu.VMEM((2,) + shape, F32)
    return pl.pallas_call(
        functools.partial(_gdnprep_kernel, n_tiles=nt),
        grid=(nt + 1,),
        in_specs=[pl.BlockSpec((GP_TOK, PROJ_W // 2), lambda i: (cur(i), 1)),
                  pl.BlockSpec((HALO, PROJ_W // 2),
                               lambda i: (jnp.maximum(cur(i) * hb - 1, 0), 1)),
                  pl.BlockSpec((HALO, PROJ_W // 2),
                               lambda i: (jnp.minimum((cur(i) + 1) * hb, nhalo - 1), 1)),
                  tok_spec(LANES),
                  pl.BlockSpec((8, qkv_w), lambda i: (0, 0)),
                  pl.BlockSpec((1, LANES), lambda i: (0, 0)),
                  pl.BlockSpec((1, LANES), lambda i: (0, 0)),
                  pl.BlockSpec((None, 16, LANES), lambda i: (cur(i), 0, 0)),
                  pl.BlockSpec((3, GRID_W, LANES), lambda i: (0, 0, 0))],
        out_specs=[dir_spec(GDN_W), dir_spec(GDN_W), dir_spec(GDN_W), dir_spec(GDN_W),
                   dir_spec(GDN_HEADS * CHUNK),
                   pl.BlockSpec((GP_CH, 8, LANES), lambda i: (done(i), 0, 0))],
        out_shape=[jax.ShapeDtypeStruct((2, n, GDN_W), F32),
                   jax.ShapeDtypeStruct((2, n, GDN_W), BF16),
                   jax.ShapeDtypeStruct((2, n, GDN_W), BF16),
                   jax.ShapeDtypeStruct((2, n, GDN_W), BF16),
                   jax.ShapeDtypeStruct((2, n, GDN_HEADS * CHUNK), BF16),
                   jax.ShapeDtypeStruct((n // CHUNK, 8, LANES), F32)],
        scratch_shapes=[pltpu.VMEM((XE_ROWS, qkv_w), BF16),
                        pltpu.VMEM((CONV_K * CONV_ROWS, 2 * CONV_ROWS), BF16),
                        pltpu.VMEM((2 * GP_TOK, GP_TOK), BF16),
                        stage(GP_TOK, GDN_W), stage(GP_TOK, GDN_W), stage(GP_TOK, GDN_W),
                        stage(GP_TOK, LANES),
                        stage(2, GP_TOK, LANES),
                        stage(GP_CH, 2, LANES, LANES),
                        stage(2, GP_TOK, LANES),
                        stage(2, GP_TOK, LANES)],
        compiler_params=pltpu.CompilerParams(
            dimension_semantics=("arbitrary",), vmem_limit_bytes=VMEM_LIMIT),
        name="gdnprep",
    )(proj, proj, proj, gates, conv_w, alog, dtb, rowtab, coltab)


SC_CH_MAX = 16


def _scan_kernel(gl_ref, s0_ref, uf_ref, wf_ref, qdf_ref, kdf_ref, qkf_ref,
                 ub_ref, wb_ref, qdb_ref, kdb_ref, qkb_ref,
                 of_ref, ob_ref, sfin_ref, s_ref, *, SC_CH):
    n = pl.program_id(0)
    nsteps = pl.num_programs(0)
    nch = nsteps * SC_CH

    @pl.when(n == 0)
    def _():
        s_ref[...] = s0_ref[...]

    streams = ((uf_ref, wf_ref, qdf_ref, kdf_ref, qkf_ref, of_ref),
               (ub_ref, wb_ref, qdb_ref, kdb_ref, qkb_ref, ob_ref))
    lo_wide = lax.broadcasted_iota(jnp.int32, (1, 2 * LANES), 1) < LANES
    chains = [(d, h) for d in range(2) for h in range(GDN_HEADS)]
    hl = lambda h: slice(h * LANES, (h + 1) * LANES)
    for j in range(SC_CH):
        local = (j, SC_CH - 1 - j)
        chunk = (n * SC_CH + j, (nsteps - 1 - n) * SC_CH + SC_CH - 1 - j)
        tok = [slice(c * CHUNK, (c + 1) * CHUNK) for c in local]
        s32 = [s_ref[d, h] for d, h in chains]
        s16 = [s.astype(BF16) for s in s32]
        wq = [_dot(jnp.concatenate([streams[d][1][tok[d], hl(h)], streams[d][2][tok[d], hl(h)]],
                                   axis=0), s)
              for (d, h), s in zip(chains, s16)]
        v16 = [(streams[d][0][tok[d], hl(h)] - x[0:CHUNK]).astype(BF16)
               for (d, h), x in zip(chains, wq)]
        upd = [_dot_tn(streams[d][3][tok[d], hl(h)], v) for (d, h), v in zip(chains, v16)]
        for (d, h), s, x in zip(chains, s32, upd):
            decay = gl_ref[(d * nch + chunk[d]) * GDN_HEADS + h]
            s_ref[d, h] = s * decay + x
        inter = [x[CHUNK:2 * CHUNK] for x in wq]
        for d in range(2):
            qk_ref, o_ref = streams[d][4], streams[d][5]
            for p in range(GDN_HEADS // 2):
                v_pair = jnp.concatenate(
                    v16[d * GDN_HEADS + 2 * p:d * GDN_HEADS + 2 * p + 2], axis=1)
                zero = jnp.zeros_like(v_pair)
                v_bd = jnp.concatenate([jnp.where(lo_wide, v_pair, zero),
                                        jnp.where(lo_wide, zero, v_pair)], axis=0)
                o_intra = _dot(qk_ref[tok[d], p * LANES:(p + 1) * LANES], v_bd)
                for jh in range(2):
                    h = 2 * p + jh
                    o_ref[tok[d], hl(h)] = (inter[d * GDN_HEADS + h]
                                            + o_intra[:, jh * LANES:(jh + 1) * LANES]
                                            ).astype(o_ref.dtype)

    @pl.when(n == nsteps - 1)
    def _():
        sfin_ref[...] = s_ref[...]


def _scan(gl, s0, u, w, qd, kd, qk):
    n = u.shape[1]
    sc_ch = min(SC_CH_MAX, n // CHUNK)
    tok = sc_ch * CHUNK
    nsteps = n // tok
    fwd = lambda wd: pl.BlockSpec((None, tok, wd), lambda i: (0, i, 0))
    bwd = lambda wd: pl.BlockSpec((None, tok, wd), lambda i: (1, nsteps - 1 - i, 0))
    qkw = GDN_HEADS * CHUNK
    state_spec = pl.BlockSpec((2, GDN_HEADS, GDN_DK, GDN_DV), lambda i: (0, 0, 0, 0))
    return pl.pallas_call(
        functools.partial(_scan_kernel, SC_CH=sc_ch),
        grid=(nsteps,),
        in_specs=[pl.BlockSpec(memory_space=pltpu.SMEM), state_spec,
                  fwd(GDN_W), fwd(GDN_W), fwd(GDN_W), fwd(GDN_W), fwd(qkw),
                  bwd(GDN_W), bwd(GDN_W), bwd(GDN_W), bwd(GDN_W), bwd(qkw)],
        out_specs=[pl.BlockSpec((tok, GDN_W), lambda i: (i, 0)),
                   pl.BlockSpec((tok, GDN_W), lambda i: (nsteps - 1 - i, 0)),
                   state_spec],
        out_shape=[jax.ShapeDtypeStruct((n, GDN_W), BF16),
                   jax.ShapeDtypeStruct((n, GDN_W), BF16),
                   jax.ShapeDtypeStruct((2, GDN_HEADS, GDN_DK, GDN_DV), F32)],
        scratch_shapes=[pltpu.VMEM((2, GDN_HEADS, GDN_DK, GDN_DV), F32)],
        compiler_params=pltpu.CompilerParams(dimension_semantics=("arbitrary",)),
        name="scan",
    )(gl, s0, u, w, qd, kd, qk, u, w, qd, kd, qk)


def _outproj_kernel(x_ref, na_ref, of_ref, ob_ref, gz_ref, gnw_ref, wout_ref, gpost_ref, mod_ref,
                    o_ref):
    og = of_ref[...].astype(F32) + ob_ref[...].astype(F32)
    gz = gz_ref[...].astype(F32)
    gnw = gnw_ref[...]
    parts = []
    for h in range(GDN_HEADS):
        seg = og[:, h * LANES:(h + 1) * LANES]
        seg = seg * lax.rsqrt(jnp.mean(seg * seg, axis=-1, keepdims=True) + EPS)
        parts.append(seg * gnw)
    gd = (jnp.concatenate(parts, axis=-1) * _silu(gz)).astype(BF16)
    y = _dot(jnp.concatenate([na_ref[...], gd], axis=-1), wout_ref[...])
    yn = y * lax.rsqrt(jnp.mean(y * y, axis=-1, keepdims=True) + EPS)
    gate = mod_ref[0:1, 2 * D_MODEL:3 * D_MODEL]
    o_ref[...] = x_ref[...] + gate * (yn * gpost_ref[...])


def _outproj(x2, na, o_f, o_b, proj, gnw, w_out, gpost, mod, tm=1024):
    n = x2.shape[0]
    gz_col = (PROJ_W - GDN_W) // GDN_W
    return pl.pallas_call(
        _outproj_kernel,
        grid=(n // tm,),
        in_specs=[pl.BlockSpec((tm, D_MODEL), lambda i: (i, 0)),
                  pl.BlockSpec((tm, NA_W), lambda i: (i, 0)),
                  pl.BlockSpec((tm, GDN_W), lambda i: (i, 0)),
                  pl.BlockSpec((tm, GDN_W), lambda i: (i, 0)),
                  pl.BlockSpec((tm, GDN_W), lambda i: (i, gz_col)),
                  pl.BlockSpec((1, LANES), lambda i: (0, 0)),
                  pl.BlockSpec((NA_W + GDN_W, D_MODEL), lambda i: (0, 0)),
                  pl.BlockSpec((1, D_MODEL), lambda i: (0, 0)),
                  pl.BlockSpec((8, 3 * D_MODEL), lambda i: (0, 0))],
        out_specs=pl.BlockSpec((tm, D_MODEL), lambda i: (i, 0)),
        out_shape=jax.ShapeDtypeStruct((n, D_MODEL), F32),
        compiler_params=pltpu.CompilerParams(
            dimension_semantics=("arbitrary",), vmem_limit_bytes=VMEM_LIMIT),
        name="outproj",
    )(x2, na, o_f, o_b, proj, gnw, w_out, gpost, mod)


def _rope_tables(rows, identity):
    inv_freq = ROPE_BASE ** (-jnp.arange(0, ROPE_AXIS_DIM, 2, dtype=F32) / ROPE_AXIS_DIM)

    def tables(count):
        ang = jnp.arange(count, dtype=F32)[:, None] * inv_freq[None, :]
        if identity:
            ang = jnp.zeros_like(ang)
        c, s = jnp.cos(ang), jnp.sin(ang)
        z = jnp.zeros_like(s)
        return (jnp.concatenate([c, c], -1), jnp.concatenate([-s, z], -1),
                jnp.concatenate([z, s], -1))

    half = jnp.zeros((rows, LANES // 2), F32)
    rowtab = jnp.stack([jnp.concatenate([t, half], -1) for t in tables(rows)], axis=0)
    rowtab = rowtab.reshape(3, rows // GP_CH, GP_CH, LANES).transpose(1, 0, 2, 3)
    rowtab = rowtab.reshape(rows // GP_CH, 3 * GP_CH, LANES)
    rowtab = jnp.pad(rowtab, ((0, 0), (0, 16 - 3 * GP_CH), (0, 0)))
    halfc = jnp.zeros((GRID_W, LANES // 2), F32)
    coltab = jnp.stack([jnp.concatenate([halfc, t], -1) for t in tables(GRID_W)], axis=0)
    return rowtab, coltab


def _lane_row(vals, offset):
    return jnp.zeros((1, LANES), F32).at[0, offset:offset + vals.shape[0]].set(vals)


def kernel(x, c, ctx, c_ctx, w_ada, b_ada, g_pre, g_post, w_in, conv_w, rpb, A_log, dt_bias,
           gdn_norm_w, w_out):
    n = x.shape[1]
    nc = ctx.shape[1]
    x2 = x[0]
    xc2 = ctx[0]

    cc = jnp.zeros((8, D_MODEL), F32).at[0].set(c[0]).at[1].set(c_ctx)
    mod = _ada(cc, w_ada[0], b_ada[0][None, :])

    gpre = g_pre[0][None, :]
    w_in_t = jnp.swapaxes(w_in, 1, 2)
    proj, gates, projc, gatesc = _inproj(x2, xc2, mod, gpre, w_in_t, tm=512)

    na = _natten(proj, projc, rpb[0].reshape(-1))

    cw = jnp.pad(conv_w[0], ((0, 8 - CONV_K), (0, 0)))
    alog = _lane_row(A_log[0].reshape(-1), 2 * GDN_HEADS)
    dtb = _lane_row(dt_bias[0].reshape(-1), 2 * GDN_HEADS)
    rowtab, coltab = _rope_tables(n // GRID_W, identity=False)
    rowtab_c, coltab_c = _rope_tables(nc // GRID_W, identity=True)

    def gl_rows(gl):
        return jnp.stack([gl[:, 0, 2 * GDN_HEADS:3 * GDN_HEADS],
                          gl[:, 1, 3 * GDN_HEADS:4 * GDN_HEADS]], axis=0).reshape(-1)

    uc, wc, qdc, kdc, qkc, glc = _gdnprep(projc, gatesc, cw, alog, dtb, rowtab_c, coltab_c)
    s0 = jnp.zeros((2, GDN_HEADS, GDN_DK, GDN_DV), F32)
    _, _, s_ctx = _scan(gl_rows(glc), s0, uc, wc, qdc, kdc, qkc)

    u, w, qd, kd, qk, gl = _gdnprep(proj, gates, cw, alog, dtb, rowtab, coltab)
    o_f, o_b, _ = _scan(gl_rows(gl), s_ctx, u, w, qd, kd, qk)

    gnw = gdn_norm_w[0][None, :]
    out = _outproj(x2, na, o_f, o_b, proj, gnw, w_out[0].astype(BF16), g_post[0][None, :], mod)
    return out[None]
```

```python
import functools
import math

import jax
import jax.numpy as jnp
from jax import lax
from jax.experimental import pallas as pl
from jax.experimental.pallas import tpu as pltpu

F32 = jnp.float32
BF16 = jnp.bfloat16

D_MODEL = 1024
GRID_W = 64
NA_HEADS = 8
NA_DH = 64
NA_W = NA_HEADS * NA_DH
NA_KH = 8
NA_KW = 16
GDN_HEADS = 4
GDN_DK = 128
GDN_DV = 128
GDN_W = GDN_HEADS * GDN_DV
CHUNK = 64
CONV_K = 5
ROPE_AXIS_DIM = GDN_DK // 2
ROPE_BASE = 10000.0
EPS = 1e-6
PROJ_W = 4 * NA_W + 3 * GDN_W + GDN_W
GATE_W = 2 * 2 * GDN_HEADS
LANES = 128
NEG = -1e30

VMEM_LIMIT = 56 * 1024 * 1024


def _silu(x):
    h = 0.5 * x
    return h + h * jnp.tanh(h)


def _dot(a, b):
    return jnp.dot(a, b, preferred_element_type=F32)


def _dot_nt(a, b):
    return lax.dot_general(a, b, (((1,), (1,)), ((), ())), preferred_element_type=F32)


def _dot_tn(a, b):
    return lax.dot_general(a, b, (((0,), (0,)), ((), ())), preferred_element_type=F32)


def _split2(x):
    hi = x.astype(BF16)
    lo = (x - hi.astype(F32)).astype(BF16)
    return hi, lo


def _split3(x):
    hi = x.astype(BF16)
    r = x - hi.astype(F32)
    mid = r.astype(BF16)
    lo = (r - mid.astype(F32)).astype(BF16)
    return hi, mid, lo


def _mm3(a, b):
    ah, al = _split2(a)
    bh, bl = _split2(b)
    return _dot(ah, bh) + (_dot(al, bh) + _dot(ah, bl))


def _ada_kernel(c_ref, w_ref, b_ref, o_ref):
    s = _silu(c_ref[...])
    o_ref[...] = _mm3(s, w_ref[...]) + b_ref[...]


def _ada(cc, w_ada, b_ada):
    tn = 512
    n = w_ada.shape[1]
    return pl.pallas_call(
        _ada_kernel,
        grid=(n // tn,),
        in_specs=[pl.BlockSpec((8, D_MODEL), lambda j: (0, 0)),
                  pl.BlockSpec((D_MODEL, tn), lambda j: (0, j)),
                  pl.BlockSpec((1, tn), lambda j: (0, j))],
        out_specs=pl.BlockSpec((8, tn), lambda j: (0, j)),
        out_shape=jax.ShapeDtypeStruct((8, n), F32),
        name="ada",
    )(cc, w_ada, b_ada)


LOG2E = math.log2(math.e)
NA_QSCALE = NA_DH ** -0.5 * LOG2E


def _inproj_kernel(x_ref, xc_ref, mod_ref, gpre_ref, win_ref,
                   proj_ref, gates_ref, projc_ref, gatesc_ref, w_ref):
    def project(x, row, proj_out, gates_out):
        xn = x * lax.rsqrt(jnp.mean(x * x, axis=-1, keepdims=True) + EPS)
        shift = mod_ref[row:row + 1, 0:D_MODEL]
        scale = mod_ref[row:row + 1, D_MODEL:2 * D_MODEL]
        h = (xn * gpre_ref[...]) * (1.0 + scale) + shift
        hb = h.astype(BF16)
        nb = 512
        for j in range(PROJ_W // nb):
            y = _dot_nt(hb, w_ref[j * nb:(j + 1) * nb, :])
            if j * nb < NA_W:
                y = y * NA_QSCALE
            proj_out[:, j * nb:(j + 1) * nb] = y.astype(BF16)
        gates_out[...] = _dot_nt(hb, w_ref[PROJ_W:PROJ_W + LANES, :])

    @pl.when(pl.program_id(0) == 0)
    def _():
        nb = 512
        for j in range(PROJ_W // nb):
            w_ref[j * nb:(j + 1) * nb, :] = win_ref[j * nb:(j + 1) * nb, :].astype(BF16)
        w_ref[PROJ_W:PROJ_W + LANES, :] = jnp.zeros((LANES, D_MODEL), BF16)
        w_ref[PROJ_W:PROJ_W + GATE_W, :] = win_ref[PROJ_W:PROJ_W + GATE_W, :].astype(BF16)
        project(xc_ref[...], 1, projc_ref, gatesc_ref)

    project(x_ref[...], 0, proj_ref, gates_ref)


def _inproj(x2, xc2, mod, gpre, w_in, tm):
    n = x2.shape[0]
    nc = xc2.shape[0]
    return pl.pallas_call(
        _inproj_kernel,
        grid=(n // tm,),
        in_specs=[pl.BlockSpec((tm, D_MODEL), lambda i: (i, 0)),
                  pl.BlockSpec((nc, D_MODEL), lambda i: (0, 0)),
                  pl.BlockSpec((8, 3 * D_MODEL), lambda i: (0, 0)),
                  pl.BlockSpec((1, D_MODEL), lambda i: (0, 0)),
                  pl.BlockSpec((None, PROJ_W + GATE_W, D_MODEL), lambda i: (0, 0, 0),
                               pipeline_mode=pl.Buffered(1))],
        out_specs=[pl.BlockSpec((tm, PROJ_W), lambda i: (i, 0)),
                   pl.BlockSpec((tm, LANES), lambda i: (i, 0)),
                   pl.BlockSpec((nc, PROJ_W), lambda i: (0, 0)),
                   pl.BlockSpec((nc, LANES), lambda i: (0, 0))],
        out_shape=[jax.ShapeDtypeStruct((n, PROJ_W), BF16),
                   jax.ShapeDtypeStruct((n, LANES), F32),
                   jax.ShapeDtypeStruct((nc, PROJ_W), BF16),
                   jax.ShapeDtypeStruct((nc, LANES), F32)],
        scratch_shapes=[pltpu.VMEM((PROJ_W + LANES, D_MODEL), BF16)],
        compiler_params=pltpu.CompilerParams(
            dimension_semantics=("arbitrary",), vmem_limit_bytes=VMEM_LIMIT),
        name="inproj",
    )(x2, xc2, mod, gpre, w_in)


NA_RO = 2 * NA_KH - 1
NA_CO = 2 * NA_KW - 1
NA_ROWS_PER_ITER = 32


def _na_build_tables(rpb_ref, tab_ref, hp):
    qc = lax.broadcasted_iota(jnp.int32, (GRID_W, LANES), 0)
    kcol = lax.broadcasted_iota(jnp.int32, (GRID_W, LANES), 1) % GRID_W
    lo = lax.broadcasted_iota(jnp.int32, (1, LANES), 1) < GRID_W
    diff = kcol - qc + (NA_KW - 1)
    col_start = jnp.clip(qc - NA_KW // 2, 0, GRID_W - NA_KW)
    inwin = (kcol >= col_start) & (kcol < col_start + NA_KW)
    for hh in range(2):
        def ro_body(ro, carry):
            base = ((hp * 2 + hh) * NA_RO + ro) * NA_CO
            acc = jnp.zeros((GRID_W, LANES), F32)
            for j in range(NA_CO):
                val = jnp.where(lo, rpb_ref[base + j], rpb_ref[base + NA_CO + j])
                acc = jnp.where(diff == j, val, acc)
            tab_ref[hh, ro] = jnp.where(inwin, acc * LOG2E, NEG)
            return carry
        lax.fori_loop(0, NA_RO - 1, ro_body, 0)


def _na_kernel(rpb_ref, q_ref, k_ref, v_ref, z_ref, kc_ref, vc_ref, o_ref, tab_ref,
               *, rb_rows, rows):
    hp = pl.program_id(0)
    rb = pl.program_id(1)

    @pl.when(rb == 0)
    def _():
        _na_build_tables(rpb_ref, tab_ref, hp)

    lane = lax.broadcasted_iota(jnp.int32, (1, LANES), 1)
    first = lane < NA_DH
    kc = kc_ref[...]
    vc = vc_ref[...]
    win = NA_KH * GRID_W

    def body(it, carry):
        items = []
        for rr in range(NA_ROWS_PER_ITER):
            i = it * NA_ROWS_PER_ITER + rr
            r = rb * rb_rows + i
            rs = jnp.clip(r - NA_KH // 2, 0, rows - NA_KH)
            ro0 = NA_KH - 1 - (r - rs)
            t0 = pl.multiple_of(i * GRID_W, GRID_W)
            q = q_ref[pl.ds(t0, GRID_W), :]
            k0 = pl.multiple_of(rs * GRID_W, GRID_W)
            kw = k_ref[pl.ds(k0, win), :]
            vw = v_ref[pl.ds(k0, win), :]
            zq = jnp.zeros_like(q)
            qs = jnp.concatenate([jnp.where(first, q, zq), jnp.where(first, zq, q)], axis=0)
            items.append((t0, ro0, qs, kw, vw))
        s_all = [_dot_nt(qs, jnp.concatenate([kw, kc], axis=0)) for (_, _, qs, kw, _) in items]
        s_loc = [s[:, 0:win]
                 + jnp.concatenate(
                     [jnp.concatenate([tab_ref[hh, ro0 + 2 * m] for m in range(NA_KH // 2)], axis=1)
                      for hh in range(2)], axis=0)
                 for s, (_, ro0, _, _, _) in zip(s_all, items)]
        s_ctx = [s[:, win:] for s in s_all]
        mx = [jnp.maximum(jnp.max(a, axis=-1, keepdims=True), jnp.max(b, axis=-1, keepdims=True))
              for a, b in zip(s_loc, s_ctx)]
        p_loc = [jnp.exp2(a - m) for a, m in zip(s_loc, mx)]
        p_ctx = [jnp.exp2(b - m) for b, m in zip(s_ctx, mx)]
        inv = [1.0 / (jnp.sum(a, axis=-1, keepdims=True) + jnp.sum(b, axis=-1, keepdims=True))
               for a, b in zip(p_loc, p_ctx)]
        outs = [_dot(jnp.concatenate([a.astype(BF16), b.astype(BF16)], axis=1),
                     jnp.concatenate([it_[4], vc], axis=0)) * il
                for a, b, il, it_ in zip(p_loc, p_ctx, inv, items)]
        for (t0, _, _, _, _), o2 in zip(items, outs):
            o = jnp.where(first, o2[0:GRID_W], o2[GRID_W:2 * GRID_W])
            z = z_ref[pl.ds(t0, GRID_W), :].astype(F32)
            o_ref[pl.ds(t0, GRID_W), :] = (o * _silu(z)).astype(BF16)
        return carry

    lax.fori_loop(0, rb_rows // NA_ROWS_PER_ITER, body, 0)


def _natten(proj, projc, rpb_flat, rb_rows=32):
    n = proj.shape[0]
    nc = projc.shape[0]
    rows = n // GRID_W
    tq = rb_rows * GRID_W
    kcol = NA_W // LANES
    return pl.pallas_call(
        functools.partial(_na_kernel, rb_rows=rb_rows, rows=rows),
        grid=(NA_W // LANES, rows // rb_rows),
        in_specs=[pl.BlockSpec(memory_space=pltpu.SMEM),
                  pl.BlockSpec((tq, LANES), lambda hp, rb: (rb, hp)),
                  pl.BlockSpec((n, LANES), lambda hp, rb: (0, kcol + hp)),
                  pl.BlockSpec((n, LANES), lambda hp, rb: (0, 2 * kcol + hp)),
                  pl.BlockSpec((tq, LANES), lambda hp, rb: (rb, 3 * kcol + hp)),
                  pl.BlockSpec((nc, LANES), lambda hp, rb: (0, kcol + hp)),
                  pl.BlockSpec((nc, LANES), lambda hp, rb: (0, 2 * kcol + hp))],
        out_specs=pl.BlockSpec((tq, LANES), lambda hp, rb: (rb, hp)),
        out_shape=jax.ShapeDtypeStruct((n, NA_W), BF16),
        scratch_shapes=[pltpu.VMEM((2, NA_RO - 1, GRID_W, LANES), F32)],
        compiler_params=pltpu.CompilerParams(
            dimension_semantics=("arbitrary", "arbitrary"), vmem_limit_bytes=VMEM_LIMIT),
        name="natten",
    )(rpb_flat, proj, proj, proj, proj, projc, projc)


GP_TOK = 256
GP_CH = GP_TOK // CHUNK
GP_LOCK = 2
INV_BASE = 8
HALO = 16
CONV_ROWS = 128
XE_ROWS = GP_TOK + CONV_ROWS
SHIFT_TAPS = CONV_K - 1


def _gdnprep_kernel(main_ref, left_ref, right_ref, gates_ref, cw_ref, alog_ref, dtb_ref,
                    rowtab_ref, coltab_ref,
                    u_ref, w_ref, qd_ref, kd_ref, qk_ref, gl_ref,
                    xe_ref, shift_ref, tri_ref, qn_ref, kn_ref, vv_ref, beta_ref, gc_ref, gct_ref,
                    eg_ref, ek_ref, *, n_tiles):
    i = pl.program_id(0)
    tile = jnp.minimum(i, n_tiles - 1)
    qkv_w = 3 * GDN_W
    staged = (qn_ref, kn_ref, vv_ref, beta_ref, gc_ref, gct_ref, eg_ref, ek_ref)

    @pl.when(i == 0)
    def _():
        m = lax.broadcasted_iota(jnp.int32, (SHIFT_TAPS * CONV_ROWS, 2 * CONV_ROWS), 0)
        r = lax.broadcasted_iota(jnp.int32, (SHIFT_TAPS * CONV_ROWS, 2 * CONV_ROWS), 1)
        g, jj, r8 = m // (8 * SHIFT_TAPS), (m // 8) % SHIFT_TAPS, m % 8
        j = jj + jj // (CONV_K // 2)
        hit = r == 8 * g + r8 + j + (HALO - CONV_K // 2)
        shift_ref[...] = jnp.where(hit, 1.0, 0.0).astype(BF16)
        ti = lax.broadcasted_iota(jnp.int32, (2 * GP_TOK, GP_TOK), 0) % GP_TOK
        tj = lax.broadcasted_iota(jnp.int32, (2 * GP_TOK, GP_TOK), 1)
        is_lower = lax.broadcasted_iota(jnp.int32, (2 * GP_TOK, GP_TOK), 0) < GP_TOK
        ordered = (is_lower & (ti >= tj)) | (jnp.logical_not(is_lower) & (ti <= tj))
        keep = ((ti // CHUNK) == (tj // CHUNK)) & ordered
        tri_ref[...] = jnp.where(keep, 1.0, 0.0).astype(BF16)
        xe_ref[HALO + GP_TOK + HALO:, :] = jnp.zeros(
            (XE_ROWS - GP_TOK - 2 * HALO, qkv_w), BF16)
        for ref in staged:
            ref[1] = jnp.zeros(ref.shape[1:], ref.dtype)

    ii = lax.broadcasted_iota(jnp.int32, (CHUNK, LANES), 0)
    jj = lax.broadcasted_iota(jnp.int32, (CHUNK, LANES), 1) % CHUNK
    lo = lax.broadcasted_iota(jnp.int32, (1, LANES), 1) < CHUNK
    lo_wide = lax.broadcasted_iota(jnp.int32, (1, 2 * LANES), 1) < LANES
    eye = jnp.where(ii == jj, 1.0, 0.0).astype(F32)
    bodies = [(d, p) for d in range(2) for p in range(GDN_HEADS // 2)]
    same_block = lambda size: (ii // size) == (jj // size)
    base_blocks = same_block(INV_BASE)
    merge_masks = []
    size = INV_BASE
    while size < CHUNK:
        merge_masks.append(same_block(2 * size) & jnp.logical_not(same_block(size)))
        size *= 2

    def blockdiag(y, first):
        z = jnp.zeros_like(y)
        return jnp.concatenate([jnp.where(first, y, z), jnp.where(first, z, y)], axis=0)

    def prepare(slot):
        lh = left_ref[:, 0:qkv_w]
        rh = right_ref[:, 0:qkv_w]
        xe_ref[0:HALO, :] = jnp.where(tile > 0, lh, jnp.zeros_like(lh))
        xe_ref[HALO:HALO + GP_TOK, :] = main_ref[:, 0:qkv_w]
        xe_ref[HALO + GP_TOK:HALO + GP_TOK + HALO, :] = jnp.where(
            tile < n_tiles - 1, rh, jnp.zeros_like(rh))

        row_lanes = lax.broadcasted_iota(jnp.int32, (1, LANES), 1) < ROPE_AXIS_DIM

        def rope_table(kind):
            return jnp.concatenate(
                [jnp.where(row_lanes, rowtab_ref[kind * GP_CH + c:kind * GP_CH + c + 1, :],
                           coltab_ref[kind]) for c in range(GP_CH)], axis=0)

        cos, sina, sinb = rope_table(0), rope_table(1), rope_table(2)
        pending = {}
        for cb, b in [(cb, b) for cb in range(qkv_w // LANES) for b in range(GP_TOK // CONV_ROWS)]:
            cols = slice(cb * LANES, (cb + 1) * LANES)
            rows = slice(b * CONV_ROWS, (b + 1) * CONV_ROWS)
            if cb % 2 == 0:
                cols2 = slice(cb * LANES, (cb + 2) * LANES)
                taps2 = _dot(shift_ref[...], xe_ref[b * CONV_ROWS:(b + 2) * CONV_ROWS, cols2])
                centre = xe_ref[HALO + b * CONV_ROWS:HALO + (b + 1) * CONV_ROWS, cols2].astype(F32)
                wts = [jnp.broadcast_to(cw_ref[j:j + 1, cols2], (8, 2 * LANES))
                       for j in range(CONV_K)]
                groups = []
                for g in range(CONV_ROWS // 8):
                    acc2 = wts[CONV_K // 2] * centre[8 * g:8 * g + 8]
                    for jj in range(SHIFT_TAPS):
                        r0 = (g * SHIFT_TAPS + jj) * 8
                        acc2 = acc2 + wts[jj + jj // (CONV_K // 2)] * taps2[r0:r0 + 8]
                    groups.append(acc2)
                pending[b] = _silu(jnp.concatenate(groups, axis=0))
            y = pending[b][:, (cb % 2) * LANES:(cb % 2 + 1) * LANES]
            if cb < 2 * GDN_HEADS:
                y = y * lax.rsqrt(jnp.sum(y * y, axis=-1, keepdims=True) + EPS)
                y = (y * cos[rows] + pltpu.roll(y, LANES - ROPE_AXIS_DIM // 2, 1) * sina[rows]
                     + pltpu.roll(y, ROPE_AXIS_DIM // 2, 1) * sinb[rows])
                if cb < GDN_HEADS:
                    qn_ref[slot, rows, cols] = y * (GDN_DK ** -0.5)
                else:
                    kn_ref[slot, rows, (cb - GDN_HEADS) * LANES:(cb - GDN_HEADS + 1) * LANES] = y
            else:
                vv_ref[slot, rows,
                       (cb - 2 * GDN_HEADS) * LANES:(cb - 2 * GDN_HEADS + 1) * LANES] = y
            yield

        gates = gates_ref[...]
        beta_ref[slot] = 0.5 + 0.5 * jnp.tanh(0.5 * gates)
        xa = gates + dtb_ref[...]
        softplus = jnp.maximum(xa, 0.0) + jnp.log1p(jnp.exp(-jnp.abs(xa)))
        g = -jnp.exp(alog_ref[...]) * softplus

        g3 = _split3(g)
        lower = tri_ref[0:GP_TOK, :]
        upper = tri_ref[GP_TOK:2 * GP_TOK, :]
        gc_f = _dot(lower, g3[0]) + (_dot(lower, g3[1]) + _dot(lower, g3[2]))
        gc_b = _dot(upper, g3[0]) + (_dot(upper, g3[1]) + _dot(upper, g3[2]))
        gc_ref[slot, 0] = gc_f
        gc_ref[slot, 1] = gc_b
        yield
        for c in range(GP_CH):
            rows = slice(c * CHUNK, (c + 1) * CHUNK)
            for d, gc in enumerate((gc_f, gc_b)):
                blk = gc[rows]
                gct_ref[slot, c, d] = jnp.concatenate(
                    [blk, pltpu.roll(blk, LANES - 1, 1)], axis=0).T
            ek_ref[slot, 0, rows] = jnp.exp(gc_f[(c + 1) * CHUNK - 1:(c + 1) * CHUNK, :] - gc_f[rows])
            ek_ref[slot, 1, rows] = jnp.exp(gc_b[c * CHUNK:c * CHUNK + 1, :] - gc_b[rows])
            yield
        eg_ref[slot, 0] = jnp.exp(gc_f)
        eg_ref[slot, 1] = jnp.exp(gc_b)

    def solve(slot):
        gl_ref[...] = jnp.zeros_like(gl_ref)
        for c in range(GP_CH):
            gl_ref[c, 0:1, :] = eg_ref[slot, 0, (c + 1) * CHUNK - 1:(c + 1) * CHUNK, :]
            gl_ref[c, 1:2, :] = eg_ref[slot, 1, c * CHUNK:c * CHUNK + 1, :]
        for first_chunk in range(0, GP_CH, GP_LOCK):
            yield from solve_group(slot, range(first_chunk, first_chunk + GP_LOCK))

    def solve_group(slot, chunks):
        a_l, kbeg_l, vb_l, where_l = [], [], [], []
        for c, (d, p) in [(c, b) for c in chunks for b in bodies]:
            tok = slice(c * CHUNK, (c + 1) * CHUNK)
            incl = (ii >= jj) if d == 0 else (ii <= jj)
            strict = (ii > jj) if d == 0 else (ii < jj)
            pl2 = slice(2 * p * LANES, (2 * p + 2) * LANES)
            lb = d * GDN_HEADS + 2 * p
            lg = 2 * GDN_HEADS + lb
            q = qn_ref[slot, tok, pl2]
            k = kn_ref[slot, tok, pl2]
            v = vv_ref[slot, tok, pl2]
            beta = jnp.where(lo_wide, beta_ref[slot, tok, lb:lb + 1],
                             beta_ref[slot, tok, lb + 1:lb + 2])
            gcol = jnp.where(lo, gc_ref[slot, d, tok, lg:lg + 1], gc_ref[slot, d, tok, lg + 1:lg + 2])
            grow = gct_ref[slot, c, d, lg:lg + 1, :]
            eg = jnp.where(lo_wide, eg_ref[slot, d, tok, lg:lg + 1],
                           eg_ref[slot, d, tok, lg + 1:lg + 2])
            ek = jnp.where(lo_wide, ek_ref[slot, d, tok, lg:lg + 1],
                           ek_ref[slot, d, tok, lg + 1:lg + 2])
            dec = jnp.exp(jnp.where(incl, gcol - grow, NEG))
            kb = k * beta
            k_nt = blockdiag(k.astype(BF16), lo_wide)
            kq = _dot_nt(jnp.concatenate([kb.astype(BF16), q.astype(BF16)], axis=0), k_nt)
            a = jnp.where(strict, kq[0:CHUNK] * dec, 0.0)
            qk = kq[CHUNK:2 * CHUNK] * dec
            qk_ref[d, tok, p * LANES:(p + 1) * LANES] = qk.astype(BF16)
            qd_ref[d, tok, pl2] = (q * eg).astype(BF16)
            kd_ref[d, tok, pl2] = (k * ek).astype(BF16)
            a_l.append(a)
            kbeg_l.append((kb * eg).astype(BF16))
            vb_l.append((v * beta).astype(BF16))
            where_l.append((d, tok, pl2))
            if p == GDN_HEADS // 2 - 1:
                yield
        x_l = [jnp.where(base_blocks, -a, 0.0) for a in a_l]
        t_l = [eye + x for x in x_l]
        x_l = [x.astype(BF16) for x in x_l]
        x_l = [_dot(x, blockdiag(x, lo)).astype(BF16) for x in x_l]
        yield
        r_l = [_dot(jnp.concatenate([x, t.astype(BF16)], axis=0), blockdiag(x, lo))
               for x, t in zip(x_l, t_l)]
        x_l = [r[0:CHUNK].astype(BF16) for r in r_l]
        t_l = [t + r[CHUNK:2 * CHUNK] for t, r in zip(t_l, r_l)]
        yield
        t_l = [t + _dot(t.astype(BF16), blockdiag(x, lo)) for t, x in zip(t_l, x_l)]
        yield
        for merged in merge_masks:
            t16_l = [t.astype(BF16) for t in t_l]
            te_l = [_dot(t16, blockdiag(jnp.where(merged, a, 0.0).astype(BF16), lo))
                    for t16, a in zip(t16_l, a_l)]
            yield
            t_l = [t - _dot(te.astype(BF16), blockdiag(t16, lo))
                   for t, te, t16 in zip(t_l, te_l, t16_l)]
            yield
        for n_done, ((d, tok, pl2), t, vb, kbeg) in enumerate(zip(where_l, t_l, vb_l, kbeg_l)):
            t16 = t.astype(BF16)
            uw = _dot(t16, jnp.concatenate([blockdiag(vb, lo_wide), blockdiag(kbeg, lo_wide)],
                                           axis=1))
            u_ref[d, tok, pl2] = uw[:, 0:2 * LANES]
            w_ref[d, tok, pl2] = uw[:, 2 * LANES:4 * LANES].astype(BF16)
            if n_done % len(bodies) == len(bodies) - 1:
                yield

    def interleave(*stages):
        live = list(stages)
        while live:
            for g in list(live):
                if next(g, live) is live:
                    live.remove(g)

    for parity in range(2):
        @pl.when(i % 2 == parity)
        def _():
            interleave(solve(1 - parity), prepare(parity))


def _gdnprep(proj, gates, conv_w, alog, dtb, rowtab, coltab):
    n = proj.shape[0]
    nt = n // GP_TOK
    hb = GP_TOK // HALO
    nhalo = n // HALO
    qkv_w = 3 * GDN_W
    cur = lambda i: jnp.minimum(i, nt - 1)
    done = lambda i: jnp.maximum(i - 1, 0)
    tok_spec = lambda w: pl.BlockSpec((GP_TOK, w), lambda i: (cur(i), 0))
    dir_spec = lambda w: pl.BlockSpec((2, GP_TOK, w), lambda i: (0, done(i), 0))
    stage = lambda *shape: pltpu.VMEM((2,) + shape, F32)
    return pl.pallas_call(
        functools.partial(_gdnprep_kernel, n_tiles=nt),
        grid=(nt + 1,),
        in_specs=[pl.BlockSpec((GP_TOK, PROJ_W // 2), lambda i: (cur(i), 1)),
                  pl.BlockSpec((HALO, PROJ_W // 2),
                               lambda i: (jnp.maximum(cur(i) * hb - 1, 0), 1)),
                  pl.BlockSpec((HALO, PROJ_W // 2),
                               lambda i: (jnp.minimum((cur(i) + 1) * hb, nhalo - 1), 1)),
                  tok_spec(LANES),
                  pl.BlockSpec((8, qkv_w), lambda i: (0, 0)),
                  pl.BlockSpec((1, LANES), lambda i: (0, 0)),
                  pl.BlockSpec((1, LANES), lambda i: (0, 0)),
                  pl.BlockSpec((None, 16, LANES), lambda i: (cur(i), 0, 0)),
                  pl.BlockSpec((3, GRID_W, LANES), lambda i: (0, 0, 0))],
        out_specs=[dir_spec(GDN_W), dir_spec(GDN_W), dir_spec(GDN_W), dir_spec(GDN_W),
                   dir_spec(GDN_HEADS * CHUNK),
                   pl.BlockSpec((GP_CH, 8, LANES), lambda i: (done(i), 0, 0))],
        out_shape=[jax.ShapeDtypeStruct((2, n, GDN_W), F32),
                   jax.ShapeDtypeStruct((2, n, GDN_W), BF16),
                   jax.ShapeDtypeStruct((2, n, GDN_W), BF16),
                   jax.ShapeDtypeStruct((2, n, GDN_W), BF16),
                   jax.ShapeDtypeStruct((2, n, GDN_HEADS * CHUNK), BF16),
                   jax.ShapeDtypeStruct((n // CHUNK, 8, LANES), F32)],
        scratch_shapes=[pltpu.VMEM((XE_ROWS, qkv_w), BF16),
                        pltpu.VMEM((SHIFT_TAPS * CONV_ROWS, 2 * CONV_ROWS), BF16),
                        pltpu.VMEM((2 * GP_TOK, GP_TOK), BF16),
                        stage(GP_TOK, GDN_W), stage(GP_TOK, GDN_W), stage(GP_TOK, GDN_W),
                        stage(GP_TOK, LANES),
                        stage(2, GP_TOK, LANES),
                        stage(GP_CH, 2, LANES, LANES),
                        stage(2, GP_TOK, LANES),
                        stage(2, GP_TOK, LANES)],
        compiler_params=pltpu.CompilerParams(
            dimension_semantics=("arbitrary",), vmem_limit_bytes=VMEM_LIMIT),
        name="gdnprep",
    )(proj, proj, proj, gates, conv_w, alog, dtb, rowtab, coltab)


SC_CH_MAX = 16


def _scan_kernel(gl_ref, s0_ref, uf_ref, wf_ref, qdf_ref, kdf_ref, qkf_ref,
                 ub_ref, wb_ref, qdb_ref, kdb_ref, qkb_ref,
                 of_ref, ob_ref, sfin_ref, s_ref, *, SC_CH):
    n = pl.program_id(0)
    nsteps = pl.num_programs(0)
    nch = nsteps * SC_CH

    @pl.when(n == 0)
    def _():
        s_ref[...] = s0_ref[...]

    streams = ((uf_ref, wf_ref, qdf_ref, kdf_ref, qkf_ref, of_ref),
               (ub_ref, wb_ref, qdb_ref, kdb_ref, qkb_ref, ob_ref))
    lo_wide = lax.broadcasted_iota(jnp.int32, (1, 2 * LANES), 1) < LANES
    chains = [(d, h) for d in range(2) for h in range(GDN_HEADS)]
    hl = lambda h: slice(h * LANES, (h + 1) * LANES)
    for j in range(SC_CH):
        local = (j, SC_CH - 1 - j)
        chunk = (n * SC_CH + j, (nsteps - 1 - n) * SC_CH + SC_CH - 1 - j)
        tok = [slice(c * CHUNK, (c + 1) * CHUNK) for c in local]
        s32 = [s_ref[d, h] for d, h in chains]
        s16 = [s.astype(BF16) for s in s32]
        wq = [_dot(jnp.concatenate([streams[d][1][tok[d], hl(h)], streams[d][2][tok[d], hl(h)]],
                                   axis=0), s)
              for (d, h), s in zip(chains, s16)]
        v16 = [(streams[d][0][tok[d], hl(h)] - x[0:CHUNK]).astype(BF16)
               for (d, h), x in zip(chains, wq)]
        upd = [_dot_tn(streams[d][3][tok[d], hl(h)], v) for (d, h), v in zip(chains, v16)]
        for (d, h), s, x in zip(chains, s32, upd):
            decay = gl_ref[(d * nch + chunk[d]) * GDN_HEADS + h]
            s_ref[d, h] = s * decay + x
        inter = [x[CHUNK:2 * CHUNK] for x in wq]
        for d in range(2):
            qk_ref, o_ref = streams[d][4], streams[d][5]
            for p in range(GDN_HEADS // 2):
                v_pair = jnp.concatenate(
                    v16[d * GDN_HEADS + 2 * p:d * GDN_HEADS + 2 * p + 2], axis=1)
                zero = jnp.zeros_like(v_pair)
                v_bd = jnp.concatenate([jnp.where(lo_wide, v_pair, zero),
                                        jnp.where(lo_wide, zero, v_pair)], axis=0)
                o_intra = _dot(qk_ref[tok[d], p * LANES:(p + 1) * LANES], v_bd)
                for jh in range(2):
                    h = 2 * p + jh
                    o_ref[tok[d], hl(h)] = (inter[d * GDN_HEADS + h]
                                            + o_intra[:, jh * LANES:(jh + 1) * LANES]
                                            ).astype(o_ref.dtype)

    @pl.when(n == nsteps - 1)
    def _():
        sfin_ref[...] = s_ref[...]


def _scan(gl, s0, u, w, qd, kd, qk):
    n = u.shape[1]
    sc_ch = min(SC_CH_MAX, n // CHUNK)
    tok = sc_ch * CHUNK
    nsteps = n // tok
    fwd = lambda wd: pl.BlockSpec((None, tok, wd), lambda i: (0, i, 0))
    bwd = lambda wd: pl.BlockSpec((None, tok, wd), lambda i: (1, nsteps - 1 - i, 0))
    qkw = GDN_HEADS * CHUNK
    state_spec = pl.BlockSpec((2, GDN_HEADS, GDN_DK, GDN_DV), lambda i: (0, 0, 0, 0))
    return pl.pallas_call(
        functools.partial(_scan_kernel, SC_CH=sc_ch),
        grid=(nsteps,),
        in_specs=[pl.BlockSpec(memory_space=pltpu.SMEM), state_spec,
                  fwd(GDN_W), fwd(GDN_W), fwd(GDN_W), fwd(GDN_W), fwd(qkw),
                  bwd(GDN_W), bwd(GDN_W), bwd(GDN_W), bwd(GDN_W), bwd(qkw)],
        out_specs=[pl.BlockSpec((tok, GDN_W), lambda i: (i, 0)),
                   pl.BlockSpec((tok, GDN_W), lambda i: (nsteps - 1 - i, 0)),
                   state_spec],
        out_shape=[jax.ShapeDtypeStruct((n, GDN_W), BF16),
                   jax.ShapeDtypeStruct((n, GDN_W), BF16),
                   jax.ShapeDtypeStruct((2, GDN_HEADS, GDN_DK, GDN_DV), F32)],
        scratch_shapes=[pltpu.VMEM((2, GDN_HEADS, GDN_DK, GDN_DV), F32)],
        compiler_params=pltpu.CompilerParams(dimension_semantics=("arbitrary",)),
        name="scan",
    )(gl, s0, u, w, qd, kd, qk, u, w, qd, kd, qk)


def _outproj_kernel(x_ref, na_ref, of_ref, ob_ref, gz_ref, gnw_ref, wout_ref, gpost_ref, mod_ref,
                    o_ref):
    og = of_ref[...].astype(F32) + ob_ref[...].astype(F32)
    gz = gz_ref[...].astype(F32)
    gnw = gnw_ref[...]
    parts = []
    for h in range(GDN_HEADS):
        seg = og[:, h * LANES:(h + 1) * LANES]
        seg = seg * lax.rsqrt(jnp.mean(seg * seg, axis=-1, keepdims=True) + EPS)
        parts.append(seg * gnw)
    gd = (jnp.concatenate(parts, axis=-1) * _silu(gz)).astype(BF16)
    y = _dot(jnp.concatenate([na_ref[...], gd], axis=-1), wout_ref[...])
    yn = y * lax.rsqrt(jnp.mean(y * y, axis=-1, keepdims=True) + EPS)
    gate = mod_ref[0:1, 2 * D_MODEL:3 * D_MODEL]
    o_ref[...] = x_ref[...] + gate * (yn * gpost_ref[...])


def _outproj(x2, na, o_f, o_b, proj, gnw, w_out, gpost, mod, tm=1024):
    n = x2.shape[0]
    gz_col = (PROJ_W - GDN_W) // GDN_W
    return pl.pallas_call(
        _outproj_kernel,
        grid=(n // tm,),
        in_specs=[pl.BlockSpec((tm, D_MODEL), lambda i: (i, 0)),
                  pl.BlockSpec((tm, NA_W), lambda i: (i, 0)),
                  pl.BlockSpec((tm, GDN_W), lambda i: (i, 0)),
                  pl.BlockSpec((tm, GDN_W), lambda i: (i, 0)),
                  pl.BlockSpec((tm, GDN_W), lambda i: (i, gz_col)),
                  pl.BlockSpec((1, LANES), lambda i: (0, 0)),
                  pl.BlockSpec((NA_W + GDN_W, D_MODEL), lambda i: (0, 0)),
                  pl.BlockSpec((1, D_MODEL), lambda i: (0, 0)),
                  pl.BlockSpec((8, 3 * D_MODEL), lambda i: (0, 0))],
        out_specs=pl.BlockSpec((tm, D_MODEL), lambda i: (i, 0)),
        out_shape=jax.ShapeDtypeStruct((n, D_MODEL), F32),
        compiler_params=pltpu.CompilerParams(
            dimension_semantics=("arbitrary",), vmem_limit_bytes=VMEM_LIMIT),
        name="outproj",
    )(x2, na, o_f, o_b, proj, gnw, w_out, gpost, mod)


def _rope_tables(rows, identity):
    inv_freq = ROPE_BASE ** (-jnp.arange(0, ROPE_AXIS_DIM, 2, dtype=F32) / ROPE_AXIS_DIM)

    def tables(count):
        ang = jnp.arange(count, dtype=F32)[:, None] * inv_freq[None, :]
        if identity:
            ang = jnp.zeros_like(ang)
        c, s = jnp.cos(ang), jnp.sin(ang)
        z = jnp.zeros_like(s)
        return (jnp.concatenate([c, c], -1), jnp.concatenate([-s, z], -1),
                jnp.concatenate([z, s], -1))

    half = jnp.zeros((rows, LANES // 2), F32)
    rowtab = jnp.stack([jnp.concatenate([t, half], -1) for t in tables(rows)], axis=0)
    rowtab = rowtab.reshape(3, rows // GP_CH, GP_CH, LANES).transpose(1, 0, 2, 3)
    rowtab = rowtab.reshape(rows // GP_CH, 3 * GP_CH, LANES)
    rowtab = jnp.pad(rowtab, ((0, 0), (0, 16 - 3 * GP_CH), (0, 0)))
    halfc = jnp.zeros((GRID_W, LANES // 2), F32)
    coltab = jnp.stack([jnp.concatenate([halfc, t], -1) for t in tables(GRID_W)], axis=0)
    return rowtab, coltab


def _lane_row(vals, offset):
    return jnp.zeros((1, LANES), F32).at[0, offset:offset + vals.shape[0]].set(vals)


def kernel(x, c, ctx, c_ctx, w_ada, b_ada, g_pre, g_post, w_in, conv_w, rpb, A_log, dt_bias,
           gdn_norm_w, w_out):
    n = x.shape[1]
    nc = ctx.shape[1]
    x2 = x[0]
    xc2 = ctx[0]

    cc = jnp.zeros((8, D_MODEL), F32).at[0].set(c[0]).at[1].set(c_ctx)
    mod = _ada(cc, w_ada[0], b_ada[0][None, :])

    gpre = g_pre[0][None, :]
    w_in_t = jnp.swapaxes(w_in, 1, 2)
    proj, gates, projc, gatesc = _inproj(x2, xc2, mod, gpre, w_in_t, tm=512)

    na = _natten(proj, projc, rpb[0].reshape(-1))

    cw = jnp.pad(conv_w[0], ((0, 8 - CONV_K), (0, 0)))
    alog = _lane_row(A_log[0].reshape(-1), 2 * GDN_HEADS)
    dtb = _lane_row(dt_bias[0].reshape(-1), 2 * GDN_HEADS)
    rowtab, coltab = _rope_tables(n // GRID_W, identity=False)
    rowtab_c, coltab_c = _rope_tables(nc // GRID_W, identity=True)

    def gl_rows(gl):
        return jnp.stack([gl[:, 0, 2 * GDN_HEADS:3 * GDN_HEADS],
                          gl[:, 1, 3 * GDN_HEADS:4 * GDN_HEADS]], axis=0).reshape(-1)

    uc, wc, qdc, kdc, qkc, glc = _gdnprep(projc, gatesc, cw, alog, dtb, rowtab_c, coltab_c)
    s0 = jnp.zeros((2, GDN_HEADS, GDN_DK, GDN_DV), F32)
    _, _, s_ctx = _scan(gl_rows(glc), s0, uc, wc, qdc, kdc, qkc)

    u, w, qd, kd, qk, gl = _gdnprep(proj, gates, cw, alog, dtb, rowtab, coltab)
    o_f, o_b, _ = _scan(gl_rows(gl), s_ctx, u, w, qd, kd, qk)

    gnw = gdn_norm_w[0][None, :]
    out = _outproj(x2, na, o_f, o_b, proj, gnw, w_out[0].astype(BF16), g_post[0][None, :], mod)
    return out[None]
```

```python
import functools
import math

import jax
import jax.numpy as jnp
from jax import lax
from jax.experimental import pallas as pl
from jax.experimental.pallas import tpu as pltpu

F32 = jnp.float32
BF16 = jnp.bfloat16

D_MODEL = 1024
GRID_W = 64
NA_HEADS = 8
NA_DH = 64
NA_W = NA_HEADS * NA_DH
NA_KH = 8
NA_KW = 16
GDN_HEADS = 4
GDN_DK = 128
GDN_DV = 128
GDN_W = GDN_HEADS * GDN_DV
CHUNK = 64
CONV_K = 5
ROPE_AXIS_DIM = GDN_DK // 2
ROPE_BASE = 10000.0
EPS = 1e-6
PROJ_W = 4 * NA_W + 3 * GDN_W + GDN_W
GATE_W = 2 * 2 * GDN_HEADS
LANES = 128
NEG = -1e30

VMEM_LIMIT = 56 * 1024 * 1024


def _silu(x):
    h = 0.5 * x
    return h + h * jnp.tanh(h)


def _dot(a, b):
    return jnp.dot(a, b, preferred_element_type=F32)


def _dot_nt(a, b):
    return lax.dot_general(a, b, (((1,), (1,)), ((), ())), preferred_element_type=F32)


def _dot_tn(a, b):
    return lax.dot_general(a, b, (((0,), (0,)), ((), ())), preferred_element_type=F32)


def _split2(x):
    hi = x.astype(BF16)
    lo = (x - hi.astype(F32)).astype(BF16)
    return hi, lo


def _split3(x):
    hi = x.astype(BF16)
    r = x - hi.astype(F32)
    mid = r.astype(BF16)
    lo = (r - mid.astype(F32)).astype(BF16)
    return hi, mid, lo


def _mm3(a, b):
    ah, al = _split2(a)
    bh, bl = _split2(b)
    return _dot(ah, bh) + (_dot(al, bh) + _dot(ah, bl))


def _ada_kernel(c_ref, w_ref, b_ref, o_ref):
    s = _silu(c_ref[...])
    o_ref[...] = _mm3(s, w_ref[...]) + b_ref[...]


def _ada(cc, w_ada, b_ada):
    tn = D_MODEL
    n = w_ada.shape[1]
    return pl.pallas_call(
        _ada_kernel,
        grid=(n // tn,),
        in_specs=[pl.BlockSpec((8, D_MODEL), lambda j: (0, 0)),
                  pl.BlockSpec((D_MODEL, tn), lambda j: (0, j)),
                  pl.BlockSpec((1, tn), lambda j: (0, j))],
        out_specs=pl.BlockSpec((8, tn), lambda j: (0, j)),
        out_shape=jax.ShapeDtypeStruct((8, n), F32),
        compiler_params=pltpu.CompilerParams(
            dimension_semantics=("arbitrary",), vmem_limit_bytes=VMEM_LIMIT),
        name="ada",
    )(cc, w_ada, b_ada)


LOG2E = math.log2(math.e)
NA_QSCALE = NA_DH ** -0.5 * LOG2E


def _inproj_kernel(x_ref, xc_ref, mod_ref, gpre_ref, win_ref,
                   proj_ref, gates_ref, projc_ref, gatesc_ref, w_ref):
    def project(x, row, proj_out, gates_out):
        xn = x * lax.rsqrt(jnp.mean(x * x, axis=-1, keepdims=True) + EPS)
        shift = mod_ref[row:row + 1, 0:D_MODEL]
        scale = mod_ref[row:row + 1, D_MODEL:2 * D_MODEL]
        h = (xn * gpre_ref[...]) * (1.0 + scale) + shift
        hb = h.astype(BF16)
        nb = 512
        for j in range(PROJ_W // nb):
            y = _dot_nt(hb, w_ref[j * nb:(j + 1) * nb, :])
            if j * nb < NA_W:
                y = y * NA_QSCALE
            proj_out[:, j * nb:(j + 1) * nb] = y.astype(BF16)
        gates_out[...] = _dot_nt(hb, w_ref[PROJ_W:PROJ_W + LANES, :])

    @pl.when(pl.program_id(0) == 0)
    def _():
        nb = 512
        for j in range(PROJ_W // nb):
            w_ref[j * nb:(j + 1) * nb, :] = win_ref[j * nb:(j + 1) * nb, :].astype(BF16)
        w_ref[PROJ_W:PROJ_W + LANES, :] = jnp.zeros((LANES, D_MODEL), BF16)
        w_ref[PROJ_W:PROJ_W + GATE_W, :] = win_ref[PROJ_W:PROJ_W + GATE_W, :].astype(BF16)
        project(xc_ref[...], 1, projc_ref, gatesc_ref)

    project(x_ref[...], 0, proj_ref, gates_ref)


def _inproj(x2, xc2, mod, gpre, w_in, tm):
    n = x2.shape[0]
    nc = xc2.shape[0]
    return pl.pallas_call(
        _inproj_kernel,
        grid=(n // tm,),
        in_specs=[pl.BlockSpec((tm, D_MODEL), lambda i: (i, 0)),
                  pl.BlockSpec((nc, D_MODEL), lambda i: (0, 0)),
                  pl.BlockSpec((8, 3 * D_MODEL), lambda i: (0, 0)),
                  pl.BlockSpec((1, D_MODEL), lambda i: (0, 0)),
                  pl.BlockSpec((None, PROJ_W + GATE_W, D_MODEL), lambda i: (0, 0, 0),
                               pipeline_mode=pl.Buffered(1))],
        out_specs=[pl.BlockSpec((tm, PROJ_W), lambda i: (i, 0)),
                   pl.BlockSpec((tm, LANES), lambda i: (i, 0)),
                   pl.BlockSpec((nc, PROJ_W), lambda i: (0, 0)),
                   pl.BlockSpec((nc, LANES), lambda i: (0, 0))],
        out_shape=[jax.ShapeDtypeStruct((n, PROJ_W), BF16),
                   jax.ShapeDtypeStruct((n, LANES), F32),
                   jax.ShapeDtypeStruct((nc, PROJ_W), BF16),
                   jax.ShapeDtypeStruct((nc, LANES), F32)],
        scratch_shapes=[pltpu.VMEM((PROJ_W + LANES, D_MODEL), BF16)],
        compiler_params=pltpu.CompilerParams(
            dimension_semantics=("arbitrary",), vmem_limit_bytes=VMEM_LIMIT),
        name="inproj",
    )(x2, xc2, mod, gpre, w_in)


NA_RO = 2 * NA_KH - 1
NA_CO = 2 * NA_KW - 1
NA_ROWS_PER_ITER = 32


def _na_build_tables(rpb_ref, tab_ref, hp):
    qc = lax.broadcasted_iota(jnp.int32, (GRID_W, LANES), 0)
    kcol = lax.broadcasted_iota(jnp.int32, (GRID_W, LANES), 1) % GRID_W
    lo = lax.broadcasted_iota(jnp.int32, (1, LANES), 1) < GRID_W
    diff = kcol - qc + (NA_KW - 1)
    col_start = jnp.clip(qc - NA_KW // 2, 0, GRID_W - NA_KW)
    inwin = (kcol >= col_start) & (kcol < col_start + NA_KW)
    for hh in range(2):
        def ro_body(ro, carry):
            base = ((hp * 2 + hh) * NA_RO + ro) * NA_CO
            acc = jnp.zeros((GRID_W, LANES), F32)
            for j in range(NA_CO):
                val = jnp.where(lo, rpb_ref[base + j], rpb_ref[base + NA_CO + j])
                acc = jnp.where(diff == j, val, acc)
            tab_ref[hh, ro] = jnp.where(inwin, acc * LOG2E, NEG)
            return carry
        lax.fori_loop(0, NA_RO - 1, ro_body, 0)


def _na_kernel(rpb_ref, q_ref, k_ref, v_ref, z_ref, kc_ref, vc_ref, o_ref, tab_ref,
               *, rb_rows, rows):
    hp = pl.program_id(0)
    rb = pl.program_id(1)

    @pl.when(rb == 0)
    def _():
        _na_build_tables(rpb_ref, tab_ref, hp)

    lane = lax.broadcasted_iota(jnp.int32, (1, LANES), 1)
    first = lane < NA_DH
    kc = kc_ref[...]
    vc = vc_ref[...]
    win = NA_KH * GRID_W

    def body(it, carry):
        items = []
        for rr in range(NA_ROWS_PER_ITER):
            i = it * NA_ROWS_PER_ITER + rr
            r = rb * rb_rows + i
            rs = jnp.clip(r - NA_KH // 2, 0, rows - NA_KH)
            ro0 = NA_KH - 1 - (r - rs)
            t0 = pl.multiple_of(i * GRID_W, GRID_W)
            q = q_ref[pl.ds(t0, GRID_W), :]
            k0 = pl.multiple_of(rs * GRID_W, GRID_W)
            kw = k_ref[pl.ds(k0, win), :]
            vw = v_ref[pl.ds(k0, win), :]
            zq = jnp.zeros_like(q)
            qs = jnp.concatenate([jnp.where(first, q, zq), jnp.where(first, zq, q)], axis=0)
            items.append((t0, ro0, qs, kw, vw))
        s_all = [_dot_nt(qs, jnp.concatenate([kw, kc], axis=0)) for (_, _, qs, kw, _) in items]
        s_loc = [s[:, 0:win]
                 + jnp.concatenate(
                     [jnp.concatenate([tab_ref[hh, ro0 + 2 * m] for m in range(NA_KH // 2)], axis=1)
                      for hh in range(2)], axis=0)
                 for s, (_, ro0, _, _, _) in zip(s_all, items)]
        s_ctx = [s[:, win:] for s in s_all]
        mx = [jnp.maximum(jnp.max(a, axis=-1, keepdims=True), jnp.max(b, axis=-1, keepdims=True))
              for a, b in zip(s_loc, s_ctx)]
        p_loc = [jnp.exp2(a - m) for a, m in zip(s_loc, mx)]
        p_ctx = [jnp.exp2(b - m) for b, m in zip(s_ctx, mx)]
        inv = [1.0 / (jnp.sum(a, axis=-1, keepdims=True) + jnp.sum(b, axis=-1, keepdims=True))
               for a, b in zip(p_loc, p_ctx)]
        outs = [_dot(jnp.concatenate([a.astype(BF16), b.astype(BF16)], axis=1),
                     jnp.concatenate([it_[4], vc], axis=0)) * il
                for a, b, il, it_ in zip(p_loc, p_ctx, inv, items)]
        for (t0, _, _, _, _), o2 in zip(items, outs):
            o = jnp.where(first, o2[0:GRID_W], o2[GRID_W:2 * GRID_W])
            z = z_ref[pl.ds(t0, GRID_W), :].astype(F32)
            o_ref[pl.ds(t0, GRID_W), :] = (o * _silu(z)).astype(BF16)
        return carry

    lax.fori_loop(0, rb_rows // NA_ROWS_PER_ITER, body, 0)


def _natten(proj, projc, rpb_flat, rb_rows=32):
    n = proj.shape[0]
    nc = projc.shape[0]
    rows = n // GRID_W
    tq = rb_rows * GRID_W
    kcol = NA_W // LANES
    return pl.pallas_call(
        functools.partial(_na_kernel, rb_rows=rb_rows, rows=rows),
        grid=(NA_W // LANES, rows // rb_rows),
        in_specs=[pl.BlockSpec(memory_space=pltpu.SMEM),
                  pl.BlockSpec((tq, LANES), lambda hp, rb: (rb, hp)),
                  pl.BlockSpec((n, LANES), lambda hp, rb: (0, kcol + hp)),
                  pl.BlockSpec((n, LANES), lambda hp, rb: (0, 2 * kcol + hp)),
                  pl.BlockSpec((tq, LANES), lambda hp, rb: (rb, 3 * kcol + hp)),
                  pl.BlockSpec((nc, LANES), lambda hp, rb: (0, kcol + hp)),
                  pl.BlockSpec((nc, LANES), lambda hp, rb: (0, 2 * kcol + hp))],
        out_specs=pl.BlockSpec((tq, LANES), lambda hp, rb: (rb, hp)),
        out_shape=jax.ShapeDtypeStruct((n, NA_W), BF16),
        scratch_shapes=[pltpu.VMEM((2, NA_RO - 1, GRID_W, LANES), F32)],
        compiler_params=pltpu.CompilerParams(
            dimension_semantics=("arbitrary", "arbitrary"), vmem_limit_bytes=VMEM_LIMIT),
        name="natten",
    )(rpb_flat, proj, proj, proj, proj, projc, projc)


GP_TOK = 256
GP_CH = GP_TOK // CHUNK
GP_LOCK = 2
INV_BASE = 8
HALO = 16
CONV_ROWS = 128
XE_ROWS = GP_TOK + CONV_ROWS


def _gdnprep_kernel(main_ref, left_ref, right_ref, gates_ref, cw_ref, alog_ref, dtb_ref,
                    rowtab_ref, coltab_ref,
                    u_ref, w_ref, qd_ref, kd_ref, qk_ref, gl_ref,
                    xe_ref, shift_ref, tri_ref, qn_ref, kn_ref, vv_ref, beta_ref, gc_ref, gct_ref,
                    eg_ref, ek_ref, *, n_tiles):
    i = pl.program_id(0)
    tile = jnp.minimum(i, n_tiles - 1)
    qkv_w = 3 * GDN_W
    staged = (qn_ref, kn_ref, vv_ref, beta_ref, gc_ref, gct_ref, eg_ref, ek_ref)

    @pl.when(i == 0)
    def _():
        m = lax.broadcasted_iota(jnp.int32, (CONV_K * CONV_ROWS, 2 * CONV_ROWS), 0)
        r = lax.broadcasted_iota(jnp.int32, (CONV_K * CONV_ROWS, 2 * CONV_ROWS), 1)
        g, j, r8 = m // (8 * CONV_K), (m // 8) % CONV_K, m % 8
        hit = r == 8 * g + r8 + j + (HALO - CONV_K // 2)
        shift_ref[...] = jnp.where(hit, 1.0, 0.0).astype(BF16)
        ti = lax.broadcasted_iota(jnp.int32, (2 * GP_TOK, GP_TOK), 0) % GP_TOK
        tj = lax.broadcasted_iota(jnp.int32, (2 * GP_TOK, GP_TOK), 1)
        is_lower = lax.broadcasted_iota(jnp.int32, (2 * GP_TOK, GP_TOK), 0) < GP_TOK
        ordered = (is_lower & (ti >= tj)) | (jnp.logical_not(is_lower) & (ti <= tj))
        keep = ((ti // CHUNK) == (tj // CHUNK)) & ordered
        tri_ref[...] = jnp.where(keep, 1.0, 0.0).astype(BF16)
        xe_ref[HALO + GP_TOK + HALO:, :] = jnp.zeros(
            (XE_ROWS - GP_TOK - 2 * HALO, qkv_w), BF16)
        for ref in staged:
            ref[1] = jnp.zeros(ref.shape[1:], ref.dtype)

    ii = lax.broadcasted_iota(jnp.int32, (CHUNK, LANES), 0)
    jj = lax.broadcasted_iota(jnp.int32, (CHUNK, LANES), 1) % CHUNK
    lo = lax.broadcasted_iota(jnp.int32, (1, LANES), 1) < CHUNK
    lo_wide = lax.broadcasted_iota(jnp.int32, (1, 2 * LANES), 1) < LANES
    eye = jnp.where(ii == jj, 1.0, 0.0).astype(F32)
    bodies = [(d, p) for d in range(2) for p in range(GDN_HEADS // 2)]
    same_block = lambda size: (ii // size) == (jj // size)
    base_blocks = same_block(INV_BASE)
    merge_masks = []
    size = INV_BASE
    while size < CHUNK:
        merge_masks.append(same_block(2 * size) & jnp.logical_not(same_block(size)))
        size *= 2

    def blockdiag(y, first):
        z = jnp.zeros_like(y)
        return jnp.concatenate([jnp.where(first, y, z), jnp.where(first, z, y)], axis=0)

    def prepare(slot):
        lh = left_ref[:, 0:qkv_w]
        rh = right_ref[:, 0:qkv_w]
        xe_ref[0:HALO, :] = jnp.where(tile > 0, lh, jnp.zeros_like(lh))
        xe_ref[HALO:HALO + GP_TOK, :] = main_ref[:, 0:qkv_w]
        xe_ref[HALO + GP_TOK:HALO + GP_TOK + HALO, :] = jnp.where(
            tile < n_tiles - 1, rh, jnp.zeros_like(rh))

        row_lanes = lax.broadcasted_iota(jnp.int32, (1, LANES), 1) < ROPE_AXIS_DIM

        def rope_table(kind):
            return jnp.concatenate(
                [jnp.where(row_lanes, rowtab_ref[kind * GP_CH + c:kind * GP_CH + c + 1, :],
                           coltab_ref[kind]) for c in range(GP_CH)], axis=0)

        cos, sina, sinb = rope_table(0), rope_table(1), rope_table(2)
        pending = {}
        for cb, b in [(cb, b) for cb in range(qkv_w // LANES) for b in range(GP_TOK // CONV_ROWS)]:
            cols = slice(cb * LANES, (cb + 1) * LANES)
            rows = slice(b * CONV_ROWS, (b + 1) * CONV_ROWS)
            if cb % 2 == 0:
                cols2 = slice(cb * LANES, (cb + 2) * LANES)
                taps2 = _dot(shift_ref[...], xe_ref[b * CONV_ROWS:(b + 2) * CONV_ROWS, cols2])
                wts = [jnp.broadcast_to(cw_ref[j:j + 1, cols2], (8, 2 * LANES))
                       for j in range(CONV_K)]
                groups = []
                for g in range(CONV_ROWS // 8):
                    acc2 = None
                    for j in range(CONV_K):
                        r0 = (g * CONV_K + j) * 8
                        term = wts[j] * taps2[r0:r0 + 8]
                        acc2 = term if acc2 is None else acc2 + term
                    groups.append(acc2)
                pending[b] = _silu(jnp.concatenate(groups, axis=0))
            y = pending[b][:, (cb % 2) * LANES:(cb % 2 + 1) * LANES]
            if cb < 2 * GDN_HEADS:
                y = y * lax.rsqrt(jnp.sum(y * y, axis=-1, keepdims=True) + EPS)
                y = (y * cos[rows] + pltpu.roll(y, LANES - ROPE_AXIS_DIM // 2, 1) * sina[rows]
                     + pltpu.roll(y, ROPE_AXIS_DIM // 2, 1) * sinb[rows])
                if cb < GDN_HEADS:
                    qn_ref[slot, rows, cols] = y * (GDN_DK ** -0.5)
                else:
                    kn_ref[slot, rows, (cb - GDN_HEADS) * LANES:(cb - GDN_HEADS + 1) * LANES] = y
            else:
                vv_ref[slot, rows,
                       (cb - 2 * GDN_HEADS) * LANES:(cb - 2 * GDN_HEADS + 1) * LANES] = y
            yield

        gates = gates_ref[...]
        beta_ref[slot] = 0.5 + 0.5 * jnp.tanh(0.5 * gates)
        xa = gates + dtb_ref[...]
        softplus = jnp.maximum(xa, 0.0) + jnp.log1p(jnp.exp(-jnp.abs(xa)))
        g = -jnp.exp(alog_ref[...]) * softplus

        g3 = _split3(g)
        lower = tri_ref[0:GP_TOK, :]
        upper = tri_ref[GP_TOK:2 * GP_TOK, :]
        gc_f = _dot(lower, g3[0]) + (_dot(lower, g3[1]) + _dot(lower, g3[2]))
        gc_b = _dot(upper, g3[0]) + (_dot(upper, g3[1]) + _dot(upper, g3[2]))
        gc_ref[slot, 0] = gc_f
        gc_ref[slot, 1] = gc_b
        yield
        for c in range(GP_CH):
            rows = slice(c * CHUNK, (c + 1) * CHUNK)
            for d, gc in enumerate((gc_f, gc_b)):
                blk = gc[rows]
                gct_ref[slot, c, d] = jnp.concatenate(
                    [blk, pltpu.roll(blk, LANES - 1, 1)], axis=0).T
            ek_ref[slot, 0, rows] = jnp.exp(gc_f[(c + 1) * CHUNK - 1:(c + 1) * CHUNK, :] - gc_f[rows])
            ek_ref[slot, 1, rows] = jnp.exp(gc_b[c * CHUNK:c * CHUNK + 1, :] - gc_b[rows])
            yield
        eg_ref[slot, 0] = jnp.exp(gc_f)
        eg_ref[slot, 1] = jnp.exp(gc_b)

    def solve(slot):
        gl_ref[...] = jnp.zeros_like(gl_ref)
        for c in range(GP_CH):
            gl_ref[c, 0:1, :] = eg_ref[slot, 0, (c + 1) * CHUNK - 1:(c + 1) * CHUNK, :]
            gl_ref[c, 1:2, :] = eg_ref[slot, 1, c * CHUNK:c * CHUNK + 1, :]
        for first_chunk in range(0, GP_CH, GP_LOCK):
            yield from solve_group(slot, range(first_chunk, first_chunk + GP_LOCK))

    def solve_group(slot, chunks):
        a_l, kbeg_l, vb_l, where_l = [], [], [], []
        for c, (d, p) in [(c, b) for c in chunks for b in bodies]:
            tok = slice(c * CHUNK, (c + 1) * CHUNK)
            incl = (ii >= jj) if d == 0 else (ii <= jj)
            strict = (ii > jj) if d == 0 else (ii < jj)
            pl2 = slice(2 * p * LANES, (2 * p + 2) * LANES)
            lb = d * GDN_HEADS + 2 * p
            lg = 2 * GDN_HEADS + lb
            q = qn_ref[slot, tok, pl2]
            k = kn_ref[slot, tok, pl2]
            v = vv_ref[slot, tok, pl2]
            beta = jnp.where(lo_wide, beta_ref[slot, tok, lb:lb + 1],
                             beta_ref[slot, tok, lb + 1:lb + 2])
            gcol = jnp.where(lo, gc_ref[slot, d, tok, lg:lg + 1], gc_ref[slot, d, tok, lg + 1:lg + 2])
            grow = gct_ref[slot, c, d, lg:lg + 1, :]
            eg = jnp.where(lo_wide, eg_ref[slot, d, tok, lg:lg + 1],
                           eg_ref[slot, d, tok, lg + 1:lg + 2])
            ek = jnp.where(lo_wide, ek_ref[slot, d, tok, lg:lg + 1],
                           ek_ref[slot, d, tok, lg + 1:lg + 2])
            dec = jnp.exp(jnp.where(incl, gcol - grow, NEG))
            kb = k * beta
            k_nt = blockdiag(k.astype(BF16), lo_wide)
            kq = _dot_nt(jnp.concatenate([kb.astype(BF16), q.astype(BF16)], axis=0), k_nt)
            a = jnp.where(strict, kq[0:CHUNK] * dec, 0.0)
            qk = kq[CHUNK:2 * CHUNK] * dec
            qk_ref[d, tok, p * LANES:(p + 1) * LANES] = qk.astype(BF16)
            qd_ref[d, tok, pl2] = (q * eg).astype(BF16)
            kd_ref[d, tok, pl2] = (k * ek).astype(BF16)
            a_l.append(a)
            kbeg_l.append((kb * eg).astype(BF16))
            vb_l.append((v * beta).astype(BF16))
            where_l.append((d, tok, pl2))
            if p == GDN_HEADS // 2 - 1:
                yield
        x_l = [jnp.where(base_blocks, -a, 0.0) for a in a_l]
        t_l = [eye + x for x in x_l]
        x_l = [x.astype(BF16) for x in x_l]
        x_l = [_dot(x, blockdiag(x, lo)).astype(BF16) for x in x_l]
        yield
        r_l = [_dot(jnp.concatenate([x, t.astype(BF16)], axis=0), blockdiag(x, lo))
               for x, t in zip(x_l, t_l)]
        x_l = [r[0:CHUNK].astype(BF16) for r in r_l]
        t_l = [t + r[CHUNK:2 * CHUNK] for t, r in zip(t_l, r_l)]
        yield
        t_l = [t + _dot(t.astype(BF16), blockdiag(x, lo)) for t, x in zip(t_l, x_l)]
        yield
        for merged in merge_masks:
            t16_l = [t.astype(BF16) for t in t_l]
            te_l = [_dot(t16, blockdiag(jnp.where(merged, a, 0.0).astype(BF16), lo))
                    for t16, a in zip(t16_l, a_l)]
            yield
            t_l = [t - _dot(te.astype(BF16), blockdiag(t16, lo))
                   for t, te, t16 in zip(t_l, te_l, t16_l)]
            yield
        for n_done, ((d, tok, pl2), t, vb, kbeg) in enumerate(zip(where_l, t_l, vb_l, kbeg_l)):
            t16 = t.astype(BF16)
            uw = _dot(t16, jnp.concatenate([blockdiag(vb, lo_wide), blockdiag(kbeg, lo_wide)],
                                           axis=1))
            u_ref[d, tok, pl2] = uw[:, 0:2 * LANES]
            w_ref[d, tok, pl2] = uw[:, 2 * LANES:4 * LANES].astype(BF16)
            if n_done % len(bodies) == len(bodies) - 1:
                yield

    def interleave(*stages):
        live = list(stages)
        while live:
            for g in list(live):
                if next(g, live) is live:
                    live.remove(g)

    for parity in range(2):
        @pl.when(i % 2 == parity)
        def _():
            interleave(solve(1 - parity), prepare(parity))


def _gdnprep(proj, gates, conv_w, alog, dtb, rowtab, coltab):
    n = proj.shape[0]
    nt = n // GP_TOK
    hb = GP_TOK // HALO
    nhalo = n // HALO
    qkv_w = 3 * GDN_W
    cur = lambda i: jnp.minimum(i, nt - 1)
    done = lambda i: jnp.maximum(i - 1, 0)
    tok_spec = lambda w: pl.BlockSpec((GP_TOK, w), lambda i: (cur(i), 0))
    dir_spec = lambda w: pl.BlockSpec((2, GP_TOK, w), lambda i: (0, done(i), 0))
    stage = lambda *shape: pltpu.VMEM((2,) + shape, F32)
    return pl.pallas_call(
        functools.partial(_gdnprep_kernel, n_tiles=nt),
        grid=(nt + 1,),
        in_specs=[pl.BlockSpec((GP_TOK, PROJ_W // 2), lambda i: (cur(i), 1)),
                  pl.BlockSpec((HALO, PROJ_W // 2),
                               lambda i: (jnp.maximum(cur(i) * hb - 1, 0), 1)),
                  pl.BlockSpec((HALO, PROJ_W // 2),
                               lambda i: (jnp.minimum((cur(i) + 1) * hb, nhalo - 1), 1)),
                  tok_spec(LANES),
                  pl.BlockSpec((8, qkv_w), lambda i: (0, 0)),
                  pl.BlockSpec((1, LANES), lambda i: (0, 0)),
                  pl.BlockSpec((1, LANES), lambda i: (0, 0)),
                  pl.BlockSpec((None, 16, LANES), lambda i: (cur(i), 0, 0)),
                  pl.BlockSpec((3, GRID_W, LANES), lambda i: (0, 0, 0))],
        out_specs=[dir_spec(GDN_W), dir_spec(GDN_W), dir_spec(GDN_W), dir_spec(GDN_W),
                   dir_spec(GDN_HEADS * CHUNK),
                   pl.BlockSpec((GP_CH, 8, LANES), lambda i: (done(i), 0, 0))],
        out_shape=[jax.ShapeDtypeStruct((2, n, GDN_W), F32),
                   jax.ShapeDtypeStruct((2, n, GDN_W), BF16),
                   jax.ShapeDtypeStruct((2, n, GDN_W), BF16),
                   jax.ShapeDtypeStruct((2, n, GDN_W), BF16),
                   jax.ShapeDtypeStruct((2, n, GDN_HEADS * CHUNK), BF16),
                   jax.ShapeDtypeStruct((n // CHUNK, 8, LANES), F32)],
        scratch_shapes=[pltpu.VMEM((XE_ROWS, qkv_w), BF16),
                        pltpu.VMEM((CONV_K * CONV_ROWS, 2 * CONV_ROWS), BF16),
                        pltpu.VMEM((2 * GP_TOK, GP_TOK), BF16),
                        stage(GP_TOK, GDN_W), stage(GP_TOK, GDN_W), stage(GP_TOK, GDN_W),
                        stage(GP_TOK, LANES),
                        stage(2, GP_TOK, LANES),
                        stage(GP_CH, 2, LANES, LANES),
                        stage(2, GP_TOK, LANES),
                        stage(2, GP_TOK, LANES)],
        compiler_params=pltpu.CompilerParams(
            dimension_semantics=("arbitrary",), vmem_limit_bytes=VMEM_LIMIT),
        name="gdnprep",
    )(proj, proj, proj, gates, conv_w, alog, dtb, rowtab, coltab)


SC_CH_MAX = 16


def _scan_kernel(gl_ref, s0_ref, uf_ref, wf_ref, qdf_ref, kdf_ref, qkf_ref,
                 ub_ref, wb_ref, qdb_ref, kdb_ref, qkb_ref,
                 of_ref, ob_ref, sfin_ref, s_ref, *, SC_CH):
    n = pl.program_id(0)
    nsteps = pl.num_programs(0)
    nch = nsteps * SC_CH

    @pl.when(n == 0)
    def _():
        s_ref[...] = s0_ref[...]

    streams = ((uf_ref, wf_ref, qdf_ref, kdf_ref, qkf_ref, of_ref),
               (ub_ref, wb_ref, qdb_ref, kdb_ref, qkb_ref, ob_ref))
    lo_wide = lax.broadcasted_iota(jnp.int32, (1, 2 * LANES), 1) < LANES
    chains = [(d, h) for d in range(2) for h in range(GDN_HEADS)]
    hl = lambda h: slice(h * LANES, (h + 1) * LANES)
    for j in range(SC_CH):
        local = (j, SC_CH - 1 - j)
        chunk = (n * SC_CH + j, (nsteps - 1 - n) * SC_CH + SC_CH - 1 - j)
        tok = [slice(c * CHUNK, (c + 1) * CHUNK) for c in local]
        s32 = [s_ref[d, h] for d, h in chains]
        s16 = [s.astype(BF16) for s in s32]
        wq = [_dot(jnp.concatenate([streams[d][1][tok[d], hl(h)], streams[d][2][tok[d], hl(h)]],
                                   axis=0), s)
              for (d, h), s in zip(chains, s16)]
        v16 = [(streams[d][0][tok[d], hl(h)] - x[0:CHUNK]).astype(BF16)
               for (d, h), x in zip(chains, wq)]
        upd = [_dot_tn(streams[d][3][tok[d], hl(h)], v) for (d, h), v in zip(chains, v16)]
        for (d, h), s, x in zip(chains, s32, upd):
            decay = gl_ref[(d * nch + chunk[d]) * GDN_HEADS + h]
            s_ref[d, h] = s * decay + x
        inter = [x[CHUNK:2 * CHUNK] for x in wq]
        for d in range(2):
            qk_ref, o_ref = streams[d][4], streams[d][5]
            for p in range(GDN_HEADS // 2):
                v_pair = jnp.concatenate(
                    v16[d * GDN_HEADS + 2 * p:d * GDN_HEADS + 2 * p + 2], axis=1)
                zero = jnp.zeros_like(v_pair)
                v_bd = jnp.concatenate([jnp.where(lo_wide, v_pair, zero),
                                        jnp.where(lo_wide, zero, v_pair)], axis=0)
                o_intra = _dot(qk_ref[tok[d], p * LANES:(p + 1) * LANES], v_bd)
                for jh in range(2):
                    h = 2 * p + jh
                    o_ref[tok[d], hl(h)] = (inter[d * GDN_HEADS + h]
                                            + o_intra[:, jh * LANES:(jh + 1) * LANES]
                                            ).astype(o_ref.dtype)

    @pl.when(n == nsteps - 1)
    def _():
        sfin_ref[...] = s_ref[...]


def _scan(gl, s0, u, w, qd, kd, qk):
    n = u.shape[1]
    sc_ch = min(SC_CH_MAX, n // CHUNK)
    tok = sc_ch * CHUNK
    nsteps = n // tok
    fwd = lambda wd: pl.BlockSpec((None, tok, wd), lambda i: (0, i, 0))
    bwd = lambda wd: pl.BlockSpec((None, tok, wd), lambda i: (1, nsteps - 1 - i, 0))
    qkw = GDN_HEADS * CHUNK
    state_spec = pl.BlockSpec((2, GDN_HEADS, GDN_DK, GDN_DV), lambda i: (0, 0, 0, 0))
    return pl.pallas_call(
        functools.partial(_scan_kernel, SC_CH=sc_ch),
        grid=(nsteps,),
        in_specs=[pl.BlockSpec(memory_space=pltpu.SMEM), state_spec,
                  fwd(GDN_W), fwd(GDN_W), fwd(GDN_W), fwd(GDN_W), fwd(qkw),
                  bwd(GDN_W), bwd(GDN_W), bwd(GDN_W), bwd(GDN_W), bwd(qkw)],
        out_specs=[pl.BlockSpec((tok, GDN_W), lambda i: (i, 0)),
                   pl.BlockSpec((tok, GDN_W), lambda i: (nsteps - 1 - i, 0)),
                   state_spec],
        out_shape=[jax.ShapeDtypeStruct((n, GDN_W), BF16),
                   jax.ShapeDtypeStruct((n, GDN_W), BF16),
                   jax.ShapeDtypeStruct((2, GDN_HEADS, GDN_DK, GDN_DV), F32)],
        scratch_shapes=[pltpu.VMEM((2, GDN_HEADS, GDN_DK, GDN_DV), F32)],
        compiler_params=pltpu.CompilerParams(dimension_semantics=("arbitrary",)),
        name="scan",
    )(gl, s0, u, w, qd, kd, qk, u, w, qd, kd, qk)


def _outproj_kernel(x_ref, na_ref, of_ref, ob_ref, gz_ref, gnw_ref, wout_ref, gpost_ref, mod_ref,
                    o_ref):
    og = of_ref[...].astype(F32) + ob_ref[...].astype(F32)
    gz = gz_ref[...].astype(F32)
    gnw = gnw_ref[...]
    parts = []
    for h in range(GDN_HEADS):
        seg = og[:, h * LANES:(h + 1) * LANES]
        seg = seg * lax.rsqrt(jnp.mean(seg * seg, axis=-1, keepdims=True) + EPS)
        parts.append(seg * gnw)
    gd = (jnp.concatenate(parts, axis=-1) * _silu(gz)).astype(BF16)
    y = _dot(jnp.concatenate([na_ref[...], gd], axis=-1), wout_ref[...])
    yn = y * lax.rsqrt(jnp.mean(y * y, axis=-1, keepdims=True) + EPS)
    gate = mod_ref[0:1, 2 * D_MODEL:3 * D_MODEL]
    o_ref[...] = x_ref[...] + gate * (yn * gpost_ref[...])


def _outproj(x2, na, o_f, o_b, proj, gnw, w_out, gpost, mod, tm=1024):
    n = x2.shape[0]
    gz_col = (PROJ_W - GDN_W) // GDN_W
    return pl.pallas_call(
        _outproj_kernel,
        grid=(n // tm,),
        in_specs=[pl.BlockSpec((tm, D_MODEL), lambda i: (i, 0)),
                  pl.BlockSpec((tm, NA_W), lambda i: (i, 0)),
                  pl.BlockSpec((tm, GDN_W), lambda i: (i, 0)),
                  pl.BlockSpec((tm, GDN_W), lambda i: (i, 0)),
                  pl.BlockSpec((tm, GDN_W), lambda i: (i, gz_col)),
                  pl.BlockSpec((1, LANES), lambda i: (0, 0)),
                  pl.BlockSpec((NA_W + GDN_W, D_MODEL), lambda i: (0, 0)),
                  pl.BlockSpec((1, D_MODEL), lambda i: (0, 0)),
                  pl.BlockSpec((8, 3 * D_MODEL), lambda i: (0, 0))],
        out_specs=pl.BlockSpec((tm, D_MODEL), lambda i: (i, 0)),
        out_shape=jax.ShapeDtypeStruct((n, D_MODEL), F32),
        compiler_params=pltpu.CompilerParams(
            dimension_semantics=("arbitrary",), vmem_limit_bytes=VMEM_LIMIT),
        name="outproj",
    )(x2, na, o_f, o_b, proj, gnw, w_out, gpost, mod)


def _rope_tables(rows, identity):
    inv_freq = ROPE_BASE ** (-jnp.arange(0, ROPE_AXIS_DIM, 2, dtype=F32) / ROPE_AXIS_DIM)

    def tables(count):
        ang = jnp.arange(count, dtype=F32)[:, None] * inv_freq[None, :]
        if identity:
            ang = jnp.zeros_like(ang)
        c, s = jnp.cos(ang), jnp.sin(ang)
        z = jnp.zeros_like(s)
        return (jnp.concatenate([c, c], -1), jnp.concatenate([-s, z], -1),
                jnp.concatenate([z, s], -1))

    half = jnp.zeros((rows, LANES // 2), F32)
    rowtab = jnp.stack([jnp.concatenate([t, half], -1) for t in tables(rows)], axis=0)
    rowtab = rowtab.reshape(3, rows // GP_CH, GP_CH, LANES).transpose(1, 0, 2, 3)
    rowtab = rowtab.reshape(rows // GP_CH, 3 * GP_CH, LANES)
    rowtab = jnp.pad(rowtab, ((0, 0), (0, 16 - 3 * GP_CH), (0, 0)))
    halfc = jnp.zeros((GRID_W, LANES // 2), F32)
    coltab = jnp.stack([jnp.concatenate([halfc, t], -1) for t in tables(GRID_W)], axis=0)
    return rowtab, coltab


def _lane_row(vals, offset):
    return jnp.zeros((1, LANES), F32).at[0, offset:offset + vals.shape[0]].set(vals)


def kernel(x, c, ctx, c_ctx, w_ada, b_ada, g_pre, g_post, w_in, conv_w, rpb, A_log, dt_bias,
           gdn_norm_w, w_out):
    n = x.shape[1]
    nc = ctx.shape[1]
    x2 = x[0]
    xc2 = ctx[0]

    cc = jnp.zeros((8, D_MODEL), F32).at[0].set(c[0]).at[1].set(c_ctx)
    mod = _ada(cc, w_ada[0], b_ada[0][None, :])

    gpre = g_pre[0][None, :]
    w_in_t = jnp.swapaxes(w_in, 1, 2)
    proj, gates, projc, gatesc = _inproj(x2, xc2, mod, gpre, w_in_t, tm=512)

    na = _natten(proj, projc, rpb[0].reshape(-1))

    cw = jnp.pad(conv_w[0], ((0, 8 - CONV_K), (0, 0)))
    alog = _lane_row(A_log[0].reshape(-1), 2 * GDN_HEADS)
    dtb = _lane_row(dt_bias[0].reshape(-1), 2 * GDN_HEADS)
    rowtab, coltab = _rope_tables(n // GRID_W, identity=False)
    rowtab_c, coltab_c = _rope_tables(nc // GRID_W, identity=True)

    def gl_rows(gl):
        return jnp.stack([gl[:, 0, 2 * GDN_HEADS:3 * GDN_HEADS],
                          gl[:, 1, 3 * GDN_HEADS:4 * GDN_HEADS]], axis=0).reshape(-1)

    uc, wc, qdc, kdc, qkc, glc = _gdnprep(projc, gatesc, cw, alog, dtb, rowtab_c, coltab_c)
    s0 = jnp.zeros((2, GDN_HEADS, GDN_DK, GDN_DV), F32)
    _, _, s_ctx = _scan(gl_rows(glc), s0, uc, wc, qdc, kdc, qkc)

    u, w, qd, kd, qk, gl = _gdnprep(proj, gates, cw, alog, dtb, rowtab, coltab)
    o_f, o_b, _ = _scan(gl_rows(gl), s_ctx, u, w, qd, kd, qk)

    gnw = gdn_norm_w[0][None, :]
    out = _outproj(x2, na, o_f, o_b, proj, gnw, w_out[0].astype(BF16), g_post[0][None, :], mod)
    return out[None]
```

```python
import functools
import math

import jax
import jax.numpy as jnp
from jax import lax
from jax.experimental import pallas as pl
from jax.experimental.pallas import tpu as pltpu

F32 = jnp.float32
BF16 = jnp.bfloat16

D_MODEL = 1024
GRID_W = 64
NA_HEADS = 8
NA_DH = 64
NA_W = NA_HEADS * NA_DH
NA_KH = 8
NA_KW = 16
GDN_HEADS = 4
GDN_DK = 128
GDN_DV = 128
GDN_W = GDN_HEADS * GDN_DV
CHUNK = 64
CONV_K = 5
ROPE_AXIS_DIM = GDN_DK // 2
ROPE_BASE = 10000.0
EPS = 1e-6
PROJ_W = 4 * NA_W + 3 * GDN_W + GDN_W
GATE_W = 2 * 2 * GDN_HEADS
LANES = 128
NEG = -1e30

VMEM_LIMIT = 56 * 1024 * 1024


def _silu(x):
    h = 0.5 * x
    return h + h * jnp.tanh(h)


def _dot(a, b):
    return jnp.dot(a, b, preferred_element_type=F32)


def _dot_nt(a, b):
    return lax.dot_general(a, b, (((1,), (1,)), ((), ())), preferred_element_type=F32)


def _dot_tn(a, b):
    return lax.dot_general(a, b, (((0,), (0,)), ((), ())), preferred_element_type=F32)


def _split2(x):
    hi = x.astype(BF16)
    lo = (x - hi.astype(F32)).astype(BF16)
    return hi, lo


def _split3(x):
    hi = x.astype(BF16)
    r = x - hi.astype(F32)
    mid = r.astype(BF16)
    lo = (r - mid.astype(F32)).astype(BF16)
    return hi, mid, lo


def _mm3(a, b):
    ah, al = _split2(a)
    bh, bl = _split2(b)
    return _dot(ah, bh) + (_dot(al, bh) + _dot(ah, bl))


def _ada_kernel(c_ref, w_ref, b_ref, o_ref):
    s = _silu(c_ref[...])
    o_ref[...] = _mm3(s, w_ref[...]) + b_ref[...]


def _ada(cc, w_ada, b_ada):
    tn = D_MODEL
    n = w_ada.shape[1]
    return pl.pallas_call(
        _ada_kernel,
        grid=(n // tn,),
        in_specs=[pl.BlockSpec((8, D_MODEL), lambda j: (0, 0)),
                  pl.BlockSpec((D_MODEL, tn), lambda j: (0, j)),
                  pl.BlockSpec((1, tn), lambda j: (0, j))],
        out_specs=pl.BlockSpec((8, tn), lambda j: (0, j)),
        out_shape=jax.ShapeDtypeStruct((8, n), F32),
        compiler_params=pltpu.CompilerParams(
            dimension_semantics=("arbitrary",), vmem_limit_bytes=VMEM_LIMIT),
        name="ada",
    )(cc, w_ada, b_ada)


LOG2E = math.log2(math.e)
NA_QSCALE = NA_DH ** -0.5 * LOG2E


def _inproj_kernel(x_ref, xc_ref, mod_ref, gpre_ref, win_ref,
                   proj_ref, gates_ref, projc_ref, gatesc_ref, w_ref):
    def project(x, row, proj_out, gates_out):
        xn = x * lax.rsqrt(jnp.mean(x * x, axis=-1, keepdims=True) + EPS)
        shift = mod_ref[row:row + 1, 0:D_MODEL]
        scale = mod_ref[row:row + 1, D_MODEL:2 * D_MODEL]
        h = (xn * gpre_ref[...]) * (1.0 + scale) + shift
        hb = h.astype(BF16)
        nb = 512
        for j in range(PROJ_W // nb):
            y = _dot_nt(hb, w_ref[j * nb:(j + 1) * nb, :])
            if j * nb < NA_W:
                y = y * NA_QSCALE
            proj_out[:, j * nb:(j + 1) * nb] = y.astype(BF16)
        gates_out[...] = _dot_nt(hb, w_ref[PROJ_W:PROJ_W + LANES, :])

    @pl.when(pl.program_id(0) == 0)
    def _():
        nb = 512
        for j in range(PROJ_W // nb):
            w_ref[j * nb:(j + 1) * nb, :] = win_ref[j * nb:(j + 1) * nb, :].astype(BF16)
        w_ref[PROJ_W:PROJ_W + LANES, :] = jnp.zeros((LANES, D_MODEL), BF16)
        w_ref[PROJ_W:PROJ_W + GATE_W, :] = win_ref[PROJ_W:PROJ_W + GATE_W, :].astype(BF16)
        project(xc_ref[...], 1, projc_ref, gatesc_ref)

    project(x_ref[...], 0, proj_ref, gates_ref)


def _inproj(x2, xc2, mod, gpre, w_in, tm):
    n = x2.shape[0]
    nc = xc2.shape[0]
    return pl.pallas_call(
        _inproj_kernel,
        grid=(n // tm,),
        in_specs=[pl.BlockSpec((tm, D_MODEL), lambda i: (i, 0)),
                  pl.BlockSpec((nc, D_MODEL), lambda i: (0, 0)),
                  pl.BlockSpec((8, 3 * D_MODEL), lambda i: (0, 0)),
                  pl.BlockSpec((1, D_MODEL), lambda i: (0, 0)),
                  pl.BlockSpec((None, PROJ_W + GATE_W, D_MODEL), lambda i: (0, 0, 0),
                               pipeline_mode=pl.Buffered(1))],
        out_specs=[pl.BlockSpec((tm, PROJ_W), lambda i: (i, 0)),
                   pl.BlockSpec((tm, LANES), lambda i: (i, 0)),
                   pl.BlockSpec((nc, PROJ_W), lambda i: (0, 0)),
                   pl.BlockSpec((nc, LANES), lambda i: (0, 0))],
        out_shape=[jax.ShapeDtypeStruct((n, PROJ_W), BF16),
                   jax.ShapeDtypeStruct((n, LANES), F32),
                   jax.ShapeDtypeStruct((nc, PROJ_W), BF16),
                   jax.ShapeDtypeStruct((nc, LANES), F32)],
        scratch_shapes=[pltpu.VMEM((PROJ_W + LANES, D_MODEL), BF16)],
        compiler_params=pltpu.CompilerParams(
            dimension_semantics=("arbitrary",), vmem_limit_bytes=VMEM_LIMIT),
        name="inproj",
    )(x2, xc2, mod, gpre, w_in)


NA_RO = 2 * NA_KH - 1
NA_CO = 2 * NA_KW - 1
NA_ROWS_PER_ITER = 32


def _na_build_tables(rpb_ref, tab_ref, hp):
    qc = lax.broadcasted_iota(jnp.int32, (GRID_W, LANES), 0)
    kcol = lax.broadcasted_iota(jnp.int32, (GRID_W, LANES), 1) % GRID_W
    lo = lax.broadcasted_iota(jnp.int32, (1, LANES), 1) < GRID_W
    diff = kcol - qc + (NA_KW - 1)
    col_start = jnp.clip(qc - NA_KW // 2, 0, GRID_W - NA_KW)
    inwin = (kcol >= col_start) & (kcol < col_start + NA_KW)
    for hh in range(2):
        def ro_body(ro, carry):
            base = ((hp * 2 + hh) * NA_RO + ro) * NA_CO
            acc = jnp.zeros((GRID_W, LANES), F32)
            for j in range(NA_CO):
                val = jnp.where(lo, rpb_ref[base + j], rpb_ref[base + NA_CO + j])
                acc = jnp.where(diff == j, val, acc)
            tab_ref[hh, ro] = jnp.where(inwin, acc * LOG2E, NEG)
            return carry
        lax.fori_loop(0, NA_RO - 1, ro_body, 0)


def _na_kernel(rpb_ref, q_ref, k_ref, v_ref, z_ref, kc_ref, vc_ref, o_ref, tab_ref,
               *, rb_rows, rows):
    hp = pl.program_id(0)
    rb = pl.program_id(1)

    @pl.when(rb == 0)
    def _():
        _na_build_tables(rpb_ref, tab_ref, hp)

    lane = lax.broadcasted_iota(jnp.int32, (1, LANES), 1)
    first = lane < NA_DH
    kc = kc_ref[...]
    vc = vc_ref[...]
    win = NA_KH * GRID_W

    def body(it, carry):
        items = []
        for rr in range(NA_ROWS_PER_ITER):
            i = it * NA_ROWS_PER_ITER + rr
            r = rb * rb_rows + i
            rs = jnp.clip(r - NA_KH // 2, 0, rows - NA_KH)
            ro0 = NA_KH - 1 - (r - rs)
            t0 = pl.multiple_of(i * GRID_W, GRID_W)
            q = q_ref[pl.ds(t0, GRID_W), :]
            k0 = pl.multiple_of(rs * GRID_W, GRID_W)
            kw = k_ref[pl.ds(k0, win), :]
            vw = v_ref[pl.ds(k0, win), :]
            zq = jnp.zeros_like(q)
            qs = jnp.concatenate([jnp.where(first, q, zq), jnp.where(first, zq, q)], axis=0)
            items.append((t0, ro0, qs, kw, vw))
        s_all = [_dot_nt(qs, jnp.concatenate([kw, kc], axis=0)) for (_, _, qs, kw, _) in items]
        s_loc = [s[:, 0:win]
                 + jnp.concatenate(
                     [jnp.concatenate([tab_ref[hh, ro0 + 2 * m] for m in range(NA_KH // 2)], axis=1)
                      for hh in range(2)], axis=0)
                 for s, (_, ro0, _, _, _) in zip(s_all, items)]
        s_ctx = [s[:, win:] for s in s_all]
        mx = [jnp.maximum(jnp.max(a, axis=-1, keepdims=True), jnp.max(b, axis=-1, keepdims=True))
              for a, b in zip(s_loc, s_ctx)]
        p_loc = [jnp.exp2(a - m) for a, m in zip(s_loc, mx)]
        p_ctx = [jnp.exp2(b - m) for b, m in zip(s_ctx, mx)]
        inv = [1.0 / (jnp.sum(a, axis=-1, keepdims=True) + jnp.sum(b, axis=-1, keepdims=True))
               for a, b in zip(p_loc, p_ctx)]
        outs = [_dot(jnp.concatenate([a.astype(BF16), b.astype(BF16)], axis=1),
                     jnp.concatenate([it_[4], vc], axis=0)) * il
                for a, b, il, it_ in zip(p_loc, p_ctx, inv, items)]
        for (t0, _, _, _, _), o2 in zip(items, outs):
            o = jnp.where(first, o2[0:GRID_W], o2[GRID_W:2 * GRID_W])
            z = z_ref[pl.ds(t0, GRID_W), :].astype(F32)
            o_ref[pl.ds(t0, GRID_W), :] = (o * _silu(z)).astype(BF16)
        return carry

    lax.fori_loop(0, rb_rows // NA_ROWS_PER_ITER, body, 0)


def _natten(proj, projc, rpb_flat, rb_rows=32):
    n = proj.shape[0]
    nc = projc.shape[0]
    rows = n // GRID_W
    tq = rb_rows * GRID_W
    kcol = NA_W // LANES
    return pl.pallas_call(
        functools.partial(_na_kernel, rb_rows=rb_rows, rows=rows),
        grid=(NA_W // LANES, rows // rb_rows),
        in_specs=[pl.BlockSpec(memory_space=pltpu.SMEM),
                  pl.BlockSpec((tq, LANES), lambda hp, rb: (rb, hp)),
                  pl.BlockSpec((n, LANES), lambda hp, rb: (0, kcol + hp)),
                  pl.BlockSpec((n, LANES), lambda hp, rb: (0, 2 * kcol + hp)),
                  pl.BlockSpec((tq, LANES), lambda hp, rb: (rb, 3 * kcol + hp)),
                  pl.BlockSpec((nc, LANES), lambda hp, rb: (0, kcol + hp)),
                  pl.BlockSpec((nc, LANES), lambda hp, rb: (0, 2 * kcol + hp))],
        out_specs=pl.BlockSpec((tq, LANES), lambda hp, rb: (rb, hp)),
        out_shape=jax.ShapeDtypeStruct((n, NA_W), BF16),
        scratch_shapes=[pltpu.VMEM((2, NA_RO - 1, GRID_W, LANES), F32)],
        compiler_params=pltpu.CompilerParams(
            dimension_semantics=("arbitrary", "arbitrary"), vmem_limit_bytes=VMEM_LIMIT),
        name="natten",
    )(rpb_flat, proj, proj, proj, proj, projc, projc)


GP_TOK = 256
GP_CH = GP_TOK // CHUNK
GP_LOCK = 2
INV_BASE = 8
HALO = 16
CONV_ROWS = 128
XE_ROWS = GP_TOK + CONV_ROWS


def _gdnprep_kernel(main_ref, left_ref, right_ref, gates_ref, cw_ref, alog_ref, dtb_ref,
                    rowtab_ref, coltab_ref,
                    u_ref, w_ref, qd_ref, kd_ref, qk_ref, gl_ref,
                    xe_ref, shift_ref, tri_ref, qn_ref, kn_ref, vv_ref, beta_ref, gc_ref, gct_ref,
                    eg_ref, ek_ref, *, n_tiles):
    i = pl.program_id(0)
    tile = jnp.minimum(i, n_tiles - 1)
    qkv_w = 3 * GDN_W
    staged = (qn_ref, kn_ref, vv_ref, beta_ref, gc_ref, gct_ref, eg_ref, ek_ref)

    @pl.when(i == 0)
    def _():
        m = lax.broadcasted_iota(jnp.int32, (CONV_K * CONV_ROWS, 2 * CONV_ROWS), 0)
        r = lax.broadcasted_iota(jnp.int32, (CONV_K * CONV_ROWS, 2 * CONV_ROWS), 1)
        g, j, r8 = m // (8 * CONV_K), (m // 8) % CONV_K, m % 8
        hit = r == 8 * g + r8 + j + (HALO - CONV_K // 2)
        shift_ref[...] = jnp.where(hit, 1.0, 0.0).astype(BF16)
        ti = lax.broadcasted_iota(jnp.int32, (2 * GP_TOK, GP_TOK), 0) % GP_TOK
        tj = lax.broadcasted_iota(jnp.int32, (2 * GP_TOK, GP_TOK), 1)
        is_lower = lax.broadcasted_iota(jnp.int32, (2 * GP_TOK, GP_TOK), 0) < GP_TOK
        ordered = (is_lower & (ti >= tj)) | (jnp.logical_not(is_lower) & (ti <= tj))
        keep = ((ti // CHUNK) == (tj // CHUNK)) & ordered
        tri_ref[...] = jnp.where(keep, 1.0, 0.0).astype(BF16)
        xe_ref[HALO + GP_TOK + HALO:, :] = jnp.zeros(
            (XE_ROWS - GP_TOK - 2 * HALO, qkv_w), BF16)
        for ref in staged:
            ref[1] = jnp.zeros(ref.shape[1:], ref.dtype)

    ii = lax.broadcasted_iota(jnp.int32, (CHUNK, LANES), 0)
    jj = lax.broadcasted_iota(jnp.int32, (CHUNK, LANES), 1) % CHUNK
    lo = lax.broadcasted_iota(jnp.int32, (1, LANES), 1) < CHUNK
    lo_wide = lax.broadcasted_iota(jnp.int32, (1, 2 * LANES), 1) < LANES
    eye = jnp.where(ii == jj, 1.0, 0.0).astype(F32)
    bodies = [(d, p) for d in range(2) for p in range(GDN_HEADS // 2)]
    same_block = lambda size: (ii // size) == (jj // size)
    base_blocks = same_block(INV_BASE)
    merge_masks = []
    size = INV_BASE
    while size < CHUNK:
        merge_masks.append(same_block(2 * size) & jnp.logical_not(same_block(size)))
        size *= 2

    def blockdiag(y, first):
        z = jnp.zeros_like(y)
        return jnp.concatenate([jnp.where(first, y, z), jnp.where(first, z, y)], axis=0)

    def prepare(slot):
        lh = left_ref[:, 0:qkv_w]
        rh = right_ref[:, 0:qkv_w]
        xe_ref[0:HALO, :] = jnp.where(tile > 0, lh, jnp.zeros_like(lh))
        xe_ref[HALO:HALO + GP_TOK, :] = main_ref[:, 0:qkv_w]
        xe_ref[HALO + GP_TOK:HALO + GP_TOK + HALO, :] = jnp.where(
            tile < n_tiles - 1, rh, jnp.zeros_like(rh))

        row_lanes = lax.broadcasted_iota(jnp.int32, (1, LANES), 1) < ROPE_AXIS_DIM

        def rope_table(kind):
            return jnp.concatenate(
                [jnp.where(row_lanes, rowtab_ref[kind * GP_CH + c:kind * GP_CH + c + 1, :],
                           coltab_ref[kind]) for c in range(GP_CH)], axis=0)

        cos, sina, sinb = rope_table(0), rope_table(1), rope_table(2)
        pending = {}
        for cb, b in [(cb, b) for cb in range(qkv_w // LANES) for b in range(GP_TOK // CONV_ROWS)]:
            cols = slice(cb * LANES, (cb + 1) * LANES)
            rows = slice(b * CONV_ROWS, (b + 1) * CONV_ROWS)
            if cb % 2 == 0:
                cols2 = slice(cb * LANES, (cb + 2) * LANES)
                taps2 = _dot(shift_ref[...], xe_ref[b * CONV_ROWS:(b + 2) * CONV_ROWS, cols2])
                wts = [jnp.broadcast_to(cw_ref[j:j + 1, cols2], (8, 2 * LANES))
                       for j in range(CONV_K)]
                groups = []
                for g in range(CONV_ROWS // 8):
                    acc2 = None
                    for j in range(CONV_K):
                        r0 = (g * CONV_K + j) * 8
                        term = wts[j] * taps2[r0:r0 + 8]
                        acc2 = term if acc2 is None else acc2 + term
                    groups.append(acc2)
                pending[b] = _silu(jnp.concatenate(groups, axis=0))
            y = pending[b][:, (cb % 2) * LANES:(cb % 2 + 1) * LANES]
            if cb < 2 * GDN_HEADS:
                y = y * lax.rsqrt(jnp.sum(y * y, axis=-1, keepdims=True) + EPS)
                y = (y * cos[rows] + pltpu.roll(y, LANES - ROPE_AXIS_DIM // 2, 1) * sina[rows]
                     + pltpu.roll(y, ROPE_AXIS_DIM // 2, 1) * sinb[rows])
                if cb < GDN_HEADS:
                    qn_ref[slot, rows, cols] = y * (GDN_DK ** -0.5)
                else:
                    kn_ref[slot, rows, (cb - GDN_HEADS) * LANES:(cb - GDN_HEADS + 1) * LANES] = y
            else:
                vv_ref[slot, rows,
                       (cb - 2 * GDN_HEADS) * LANES:(cb - 2 * GDN_HEADS + 1) * LANES] = y
            yield

        gates = gates_ref[...]
        beta_ref[slot] = 0.5 + 0.5 * jnp.tanh(0.5 * gates)
        xa = gates + dtb_ref[...]
        softplus = jnp.maximum(xa, 0.0) + jnp.log1p(jnp.exp(-jnp.abs(xa)))
        g = -jnp.exp(alog_ref[...]) * softplus

        g3 = _split3(g)
        lower = tri_ref[0:GP_TOK, :]
        upper = tri_ref[GP_TOK:2 * GP_TOK, :]
        gc_f = _dot(lower, g3[0]) + (_dot(lower, g3[1]) + _dot(lower, g3[2]))
        gc_b = _dot(upper, g3[0]) + (_dot(upper, g3[1]) + _dot(upper, g3[2]))
        gc_ref[slot, 0] = gc_f
        gc_ref[slot, 1] = gc_b
        yield
        for c in range(GP_CH):
            rows = slice(c * CHUNK, (c + 1) * CHUNK)
            for d, gc in enumerate((gc_f, gc_b)):
                blk = gc[rows]
                gct_ref[slot, c, d] = jnp.concatenate(
                    [blk, pltpu.roll(blk, LANES - 1, 1)], axis=0).T
            ek_ref[slot, 0, rows] = jnp.exp(gc_f[(c + 1) * CHUNK - 1:(c + 1) * CHUNK, :] - gc_f[rows])
            ek_ref[slot, 1, rows] = jnp.exp(gc_b[c * CHUNK:c * CHUNK + 1, :] - gc_b[rows])
            yield
        eg_ref[slot, 0] = jnp.exp(gc_f)
        eg_ref[slot, 1] = jnp.exp(gc_b)

    def solve(slot):
        gl_ref[...] = jnp.zeros_like(gl_ref)
        for c in range(GP_CH):
            gl_ref[c, 0:1, :] = eg_ref[slot, 0, (c + 1) * CHUNK - 1:(c + 1) * CHUNK, :]
            gl_ref[c, 1:2, :] = eg_ref[slot, 1, c * CHUNK:c * CHUNK + 1, :]
        for first_chunk in range(0, GP_CH, GP_LOCK):
            yield from solve_group(slot, range(first_chunk, first_chunk + GP_LOCK))

    def solve_group(slot, chunks):
        a_l, kbeg_l, vb_l, where_l = [], [], [], []
        for c, (d, p) in [(c, b) for c in chunks for b in bodies]:
            tok = slice(c * CHUNK, (c + 1) * CHUNK)
            incl = (ii >= jj) if d == 0 else (ii <= jj)
            strict = (ii > jj) if d == 0 else (ii < jj)
            pl2 = slice(2 * p * LANES, (2 * p + 2) * LANES)
            lb = d * GDN_HEADS + 2 * p
            lg = 2 * GDN_HEADS + lb
            q = qn_ref[slot, tok, pl2]
            k = kn_ref[slot, tok, pl2]
            v = vv_ref[slot, tok, pl2]
            beta = jnp.where(lo_wide, beta_ref[slot, tok, lb:lb + 1],
                             beta_ref[slot, tok, lb + 1:lb + 2])
            gcol = jnp.where(lo, gc_ref[slot, d, tok, lg:lg + 1], gc_ref[slot, d, tok, lg + 1:lg + 2])
            grow = gct_ref[slot, c, d, lg:lg + 1, :]
            eg = jnp.where(lo_wide, eg_ref[slot, d, tok, lg:lg + 1],
                           eg_ref[slot, d, tok, lg + 1:lg + 2])
            ek = jnp.where(lo_wide, ek_ref[slot, d, tok, lg:lg + 1],
                           ek_ref[slot, d, tok, lg + 1:lg + 2])
            dec = jnp.exp(jnp.where(incl, gcol - grow, NEG))
            kb = k * beta
            k_nt = blockdiag(k.astype(BF16), lo_wide)
            kq = _dot_nt(jnp.concatenate([kb.astype(BF16), q.astype(BF16)], axis=0), k_nt)
            a = jnp.where(strict, kq[0:CHUNK] * dec, 0.0)
            qk = kq[CHUNK:2 * CHUNK] * dec
            qk_ref[d, tok, p * LANES:(p + 1) * LANES] = qk.astype(BF16)
            qd_ref[d, tok, pl2] = (q * eg).astype(BF16)
            kd_ref[d, tok, pl2] = (k * ek).astype(BF16)
            a_l.append(a)
            kbeg_l.append((kb * eg).astype(BF16))
            vb_l.append((v * beta).astype(BF16))
            where_l.append((d, tok, pl2))
            if p == GDN_HEADS // 2 - 1:
                yield
        x_l = [jnp.where(base_blocks, -a, 0.0) for a in a_l]
        t_l = [eye + x for x in x_l]
        x_l = [x.astype(BF16) for x in x_l]
        x_l = [_dot(x, blockdiag(x, lo)).astype(BF16) for x in x_l]
        yield
        r_l = [_dot(jnp.concatenate([x, t.astype(BF16)], axis=0), blockdiag(x, lo))
               for x, t in zip(x_l, t_l)]
        x_l = [r[0:CHUNK].astype(BF16) for r in r_l]
        t_l = [t + r[CHUNK:2 * CHUNK] for t, r in zip(t_l, r_l)]
        yield
        t_l = [t + _dot(t.astype(BF16), blockdiag(x, lo)) for t, x in zip(t_l, x_l)]
        yield
        for merged in merge_masks:
            t16_l = [t.astype(BF16) for t in t_l]
            te_l = [_dot(t16, blockdiag(jnp.where(merged, a, 0.0).astype(BF16), lo))
                    for t16, a in zip(t16_l, a_l)]
            yield
            t_l = [t - _dot(te.astype(BF16), blockdiag(t16, lo))
                   for t, te, t16 in zip(t_l, te_l, t16_l)]
            yield
        for n_done, ((d, tok, pl2), t, vb, kbeg) in enumerate(zip(where_l, t_l, vb_l, kbeg_l)):
            t16 = t.astype(BF16)
            uw = _dot(t16, jnp.concatenate([blockdiag(vb, lo_wide), blockdiag(kbeg, lo_wide)],
                                           axis=1))
            u_ref[d, tok, pl2] = uw[:, 0:2 * LANES]
            w_ref[d, tok, pl2] = uw[:, 2 * LANES:4 * LANES].astype(BF16)
            if n_done % len(bodies) == len(bodies) - 1:
                yield

    def interleave(*stages):
        live = list(stages)
        while live:
            for g in list(live):
                if next(g, live) is live:
                    live.remove(g)

    for parity in range(2):
        @pl.when(i % 2 == parity)
        def _():
            interleave(prepare(parity), solve(1 - parity))


def _gdnprep(proj, gates, conv_w, alog, dtb, rowtab, coltab):
    n = proj.shape[0]
    nt = n // GP_TOK
    hb = GP_TOK // HALO
    nhalo = n // HALO
    qkv_w = 3 * GDN_W
    cur = lambda i: jnp.minimum(i, nt - 1)
    done = lambda i: jnp.maximum(i - 1, 0)
    tok_spec = lambda w: pl.BlockSpec((GP_TOK, w), lambda i: (cur(i), 0))
    dir_spec = lambda w: pl.BlockSpec((2, GP_TOK, w), lambda i: (0, done(i), 0))
    stage = lambda *shape: pltpu.VMEM((2,) + shape, F32)
    return pl.pallas_call(
        functools.partial(_gdnprep_kernel, n_tiles=nt),
        grid=(nt + 1,),
        in_specs=[pl.BlockSpec((GP_TOK, PROJ_W // 2), lambda i: (cur(i), 1)),
                  pl.BlockSpec((HALO, PROJ_W // 2),
                               lambda i: (jnp.maximum(cur(i) * hb - 1, 0), 1)),
                  pl.BlockSpec((HALO, PROJ_W // 2),
                               lambda i: (jnp.minimum((cur(i) + 1) * hb, nhalo - 1), 1)),
                  tok_spec(LANES),
                  pl.BlockSpec((8, qkv_w), lambda i: (0, 0)),
                  pl.BlockSpec((1, LANES), lambda i: (0, 0)),
                  pl.BlockSpec((1, LANES), lambda i: (0, 0)),
                  pl.BlockSpec((None, 16, LANES), lambda i: (cur(i), 0, 0)),
                  pl.BlockSpec((3, GRID_W, LANES), lambda i: (0, 0, 0))],
        out_specs=[dir_spec(GDN_W), dir_spec(GDN_W), dir_spec(GDN_W), dir_spec(GDN_W),
                   dir_spec(GDN_HEADS * CHUNK),
                   pl.BlockSpec((GP_CH, 8, LANES), lambda i: (done(i), 0, 0))],
        out_shape=[jax.ShapeDtypeStruct((2, n, GDN_W), F32),
                   jax.ShapeDtypeStruct((2, n, GDN_W), BF16),
                   jax.ShapeDtypeStruct((2, n, GDN_W), BF16),
                   jax.ShapeDtypeStruct((2, n, GDN_W), BF16),
                   jax.ShapeDtypeStruct((2, n, GDN_HEADS * CHUNK), BF16),
                   jax.ShapeDtypeStruct((n // CHUNK, 8, LANES), F32)],
        scratch_shapes=[pltpu.VMEM((XE_ROWS, qkv_w), BF16),
                        pltpu.VMEM((CONV_K * CONV_ROWS, 2 * CONV_ROWS), BF16),
                        pltpu.VMEM((2 * GP_TOK, GP_TOK), BF16),
                        stage(GP_TOK, GDN_W), stage(GP_TOK, GDN_W), stage(GP_TOK, GDN_W),
                        stage(GP_TOK, LANES),
                        stage(2, GP_TOK, LANES),
                        stage(GP_CH, 2, LANES, LANES),
                        stage(2, GP_TOK, LANES),
                        stage(2, GP_TOK, LANES)],
        compiler_params=pltpu.CompilerParams(
            dimension_semantics=("arbitrary",), vmem_limit_bytes=VMEM_LIMIT),
        name="gdnprep",
    )(proj, proj, proj, gates, conv_w, alog, dtb, rowtab, coltab)


SC_CH_MAX = 16


def _scan_kernel(gl_ref, s0_ref, uf_ref, wf_ref, qdf_ref, kdf_ref, qkf_ref,
                 ub_ref, wb_ref, qdb_ref, kdb_ref, qkb_ref,
                 of_ref, ob_ref, sfin_ref, s_ref, *, SC_CH):
    n = pl.program_id(0)
    nsteps = pl.num_programs(0)
    nch = nsteps * SC_CH

    @pl.when(n == 0)
    def _():
        s_ref[...] = s0_ref[...]

    streams = ((uf_ref, wf_ref, qdf_ref, kdf_ref, qkf_ref, of_ref),
               (ub_ref, wb_ref, qdb_ref, kdb_ref, qkb_ref, ob_ref))
    lo_wide = lax.broadcasted_iota(jnp.int32, (1, 2 * LANES), 1) < LANES
    chains = [(d, h) for d in range(2) for h in range(GDN_HEADS)]
    hl = lambda h: slice(h * LANES, (h + 1) * LANES)
    for j in range(SC_CH):
        local = (j, SC_CH - 1 - j)
        chunk = (n * SC_CH + j, (nsteps - 1 - n) * SC_CH + SC_CH - 1 - j)
        tok = [slice(c * CHUNK, (c + 1) * CHUNK) for c in local]
        s32 = [s_ref[d, h] for d, h in chains]
        s16 = [s.astype(BF16) for s in s32]
        wq = [_dot(jnp.concatenate([streams[d][1][tok[d], hl(h)], streams[d][2][tok[d], hl(h)]],
                                   axis=0), s)
              for (d, h), s in zip(chains, s16)]
        v16 = [(streams[d][0][tok[d], hl(h)] - x[0:CHUNK]).astype(BF16)
               for (d, h), x in zip(chains, wq)]
        upd = [_dot_tn(streams[d][3][tok[d], hl(h)], v) for (d, h), v in zip(chains, v16)]
        for (d, h), s, x in zip(chains, s32, upd):
            decay = gl_ref[(d * nch + chunk[d]) * GDN_HEADS + h]
            s_ref[d, h] = s * decay + x
        inter = [x[CHUNK:2 * CHUNK] for x in wq]
        for d in range(2):
            qk_ref, o_ref = streams[d][4], streams[d][5]
            for p in range(GDN_HEADS // 2):
                v_pair = jnp.concatenate(
                    v16[d * GDN_HEADS + 2 * p:d * GDN_HEADS + 2 * p + 2], axis=1)
                zero = jnp.zeros_like(v_pair)
                v_bd = jnp.concatenate([jnp.where(lo_wide, v_pair, zero),
                                        jnp.where(lo_wide, zero, v_pair)], axis=0)
                o_intra = _dot(qk_ref[tok[d], p * LANES:(p + 1) * LANES], v_bd)
                for jh in range(2):
                    h = 2 * p + jh
                    o_ref[tok[d], hl(h)] = (inter[d * GDN_HEADS + h]
                                            + o_intra[:, jh * LANES:(jh + 1) * LANES]
                                            ).astype(o_ref.dtype)

    @pl.when(n == nsteps - 1)
    def _():
        sfin_ref[...] = s_ref[...]


def _scan(gl, s0, u, w, qd, kd, qk):
    n = u.shape[1]
    sc_ch = min(SC_CH_MAX, n // CHUNK)
    tok = sc_ch * CHUNK
    nsteps = n // tok
    fwd = lambda wd: pl.BlockSpec((None, tok, wd), lambda i: (0, i, 0))
    bwd = lambda wd: pl.BlockSpec((None, tok, wd), lambda i: (1, nsteps - 1 - i, 0))
    qkw = GDN_HEADS * CHUNK
    state_spec = pl.BlockSpec((2, GDN_HEADS, GDN_DK, GDN_DV), lambda i: (0, 0, 0, 0))
    return pl.pallas_call(
        functools.partial(_scan_kernel, SC_CH=sc_ch),
        grid=(nsteps,),
        in_specs=[pl.BlockSpec(memory_space=pltpu.SMEM), state_spec,
                  fwd(GDN_W), fwd(GDN_W), fwd(GDN_W), fwd(GDN_W), fwd(qkw),
                  bwd(GDN_W), bwd(GDN_W), bwd(GDN_W), bwd(GDN_W), bwd(qkw)],
        out_specs=[pl.BlockSpec((tok, GDN_W), lambda i: (i, 0)),
                   pl.BlockSpec((tok, GDN_W), lambda i: (nsteps - 1 - i, 0)),
                   state_spec],
        out_shape=[jax.ShapeDtypeStruct((n, GDN_W), BF16),
                   jax.ShapeDtypeStruct((n, GDN_W), BF16),
                   jax.ShapeDtypeStruct((2, GDN_HEADS, GDN_DK, GDN_DV), F32)],
        scratch_shapes=[pltpu.VMEM((2, GDN_HEADS, GDN_DK, GDN_DV), F32)],
        compiler_params=pltpu.CompilerParams(dimension_semantics=("arbitrary",)),
        name="scan",
    )(gl, s0, u, w, qd, kd, qk, u, w, qd, kd, qk)


def _outproj_kernel(x_ref, na_ref, of_ref, ob_ref, gz_ref, gnw_ref, wout_ref, gpost_ref, mod_ref,
                    o_ref):
    og = of_ref[...].astype(F32) + ob_ref[...].astype(F32)
    gz = gz_ref[...].astype(F32)
    gnw = gnw_ref[...]
    parts = []
    for h in range(GDN_HEADS):
        seg = og[:, h * LANES:(h + 1) * LANES]
        seg = seg * lax.rsqrt(jnp.mean(seg * seg, axis=-1, keepdims=True) + EPS)
        parts.append(seg * gnw)
    gd = (jnp.concatenate(parts, axis=-1) * _silu(gz)).astype(BF16)
    y = _dot(jnp.concatenate([na_ref[...], gd], axis=-1), wout_ref[...])
    yn = y * lax.rsqrt(jnp.mean(y * y, axis=-1, keepdims=True) + EPS)
    gate = mod_ref[0:1, 2 * D_MODEL:3 * D_MODEL]
    o_ref[...] = x_ref[...] + gate * (yn * gpost_ref[...])


def _outproj(x2, na, o_f, o_b, proj, gnw, w_out, gpost, mod, tm=1024):
    n = x2.shape[0]
    gz_col = (PROJ_W - GDN_W) // GDN_W
    return pl.pallas_call(
        _outproj_kernel,
        grid=(n // tm,),
        in_specs=[pl.BlockSpec((tm, D_MODEL), lambda i: (i, 0)),
                  pl.BlockSpec((tm, NA_W), lambda i: (i, 0)),
                  pl.BlockSpec((tm, GDN_W), lambda i: (i, 0)),
                  pl.BlockSpec((tm, GDN_W), lambda i: (i, 0)),
                  pl.BlockSpec((tm, GDN_W), lambda i: (i, gz_col)),
                  pl.BlockSpec((1, LANES), lambda i: (0, 0)),
                  pl.BlockSpec((NA_W + GDN_W, D_MODEL), lambda i: (0, 0)),
                  pl.BlockSpec((1, D_MODEL), lambda i: (0, 0)),
                  pl.BlockSpec((8, 3 * D_MODEL), lambda i: (0, 0))],
        out_specs=pl.BlockSpec((tm, D_MODEL), lambda i: (i, 0)),
        out_shape=jax.ShapeDtypeStruct((n, D_MODEL), F32),
        compiler_params=pltpu.CompilerParams(
            dimension_semantics=("arbitrary",), vmem_limit_bytes=VMEM_LIMIT),
        name="outproj",
    )(x2, na, o_f, o_b, proj, gnw, w_out, gpost, mod)


def _rope_tables(rows, identity):
    inv_freq = ROPE_BASE ** (-jnp.arange(0, ROPE_AXIS_DIM, 2, dtype=F32) / ROPE_AXIS_DIM)

    def tables(count):
        ang = jnp.arange(count, dtype=F32)[:, None] * inv_freq[None, :]
        if identity:
            ang = jnp.zeros_like(ang)
        c, s = jnp.cos(ang), jnp.sin(ang)
        z = jnp.zeros_like(s)
        return (jnp.concatenate([c, c], -1), jnp.concatenate([-s, z], -1),
                jnp.concatenate([z, s], -1))

    half = jnp.zeros((rows, LANES // 2), F32)
    rowtab = jnp.stack([jnp.concatenate([t, half], -1) for t in tables(rows)], axis=0)
    rowtab = rowtab.reshape(3, rows // GP_CH, GP_CH, LANES).transpose(1, 0, 2, 3)
    rowtab = rowtab.reshape(rows // GP_CH, 3 * GP_CH, LANES)
    rowtab = jnp.pad(rowtab, ((0, 0), (0, 16 - 3 * GP_CH), (0, 0)))
    halfc = jnp.zeros((GRID_W, LANES // 2), F32)
    coltab = jnp.stack([jnp.concatenate([halfc, t], -1) for t in tables(GRID_W)], axis=0)
    return rowtab, coltab


def _lane_row(vals, offset):
    return jnp.zeros((1, LANES), F32).at[0, offset:offset + vals.shape[0]].set(vals)


def kernel(x, c, ctx, c_ctx, w_ada, b_ada, g_pre, g_post, w_in, conv_w, rpb, A_log, dt_bias,
           gdn_norm_w, w_out):
    n = x.shape[1]
    nc = ctx.shape[1]
    x2 = x[0]
    xc2 = ctx[0]

    cc = jnp.zeros((8, D_MODEL), F32).at[0].set(c[0]).at[1].set(c_ctx)
    mod = _ada(cc, w_ada[0], b_ada[0][None, :])

    gpre = g_pre[0][None, :]
    w_in_t = jnp.swapaxes(w_in, 1, 2)
    proj, gates, projc, gatesc = _inproj(x2, xc2, mod, gpre, w_in_t, tm=512)

    na = _natten(proj, projc, rpb[0].reshape(-1))

    cw = jnp.pad(conv_w[0], ((0, 8 - CONV_K), (0, 0)))
    alog = _lane_row(A_log[0].reshape(-1), 2 * GDN_HEADS)
    dtb = _lane_row(dt_bias[0].reshape(-1), 2 * GDN_HEADS)
    rowtab, coltab = _rope_tables(n // GRID_W, identity=False)
    rowtab_c, coltab_c = _rope_tables(nc // GRID_W, identity=True)

    def gl_rows(gl):
        return jnp.stack([gl[:, 0, 2 * GDN_HEADS:3 * GDN_HEADS],
                          gl[:, 1, 3 * GDN_HEADS:4 * GDN_HEADS]], axis=0).reshape(-1)

    uc, wc, qdc, kdc, qkc, glc = _gdnprep(projc, gatesc, cw, alog, dtb, rowtab_c, coltab_c)
    s0 = jnp.zeros((2, GDN_HEADS, GDN_DK, GDN_DV), F32)
    _, _, s_ctx = _scan(gl_rows(glc), s0, uc, wc, qdc, kdc, qkc)

    u, w, qd, kd, qk, gl = _gdnprep(proj, gates, cw, alog, dtb, rowtab, coltab)
    o_f, o_b, _ = _scan(gl_rows(gl), s_ctx, u, w, qd, kd, qk)

    gnw = gdn_norm_w[0][None, :]
    out = _outproj(x2, na, o_f, o_b, proj, gnw, w_out[0].astype(BF16), g_post[0][None, :], mod)
    return out[None]
```

```python
import functools
import math

import jax
import jax.numpy as jnp
from jax import lax
from jax.experimental import pallas as pl
from jax.experimental.pallas import tpu as pltpu

F32 = jnp.float32
BF16 = jnp.bfloat16

D_MODEL = 1024
GRID_W = 64
NA_HEADS = 8
NA_DH = 64
NA_W = NA_HEADS * NA_DH
NA_KH = 8
NA_KW = 16
GDN_HEADS = 4
GDN_DK = 128
GDN_DV = 128
GDN_W = GDN_HEADS * GDN_DV
CHUNK = 64
CONV_K = 5
ROPE_AXIS_DIM = GDN_DK // 2
ROPE_BASE = 10000.0
EPS = 1e-6
PROJ_W = 4 * NA_W + 3 * GDN_W + GDN_W
GATE_W = 2 * 2 * GDN_HEADS
LANES = 128
NEG = -1e30

VMEM_LIMIT = 56 * 1024 * 1024


def _silu(x):
    h = 0.5 * x
    return h + h * jnp.tanh(h)


def _dot(a, b):
    return jnp.dot(a, b, preferred_element_type=F32)


def _dot_nt(a, b):
    return lax.dot_general(a, b, (((1,), (1,)), ((), ())), preferred_element_type=F32)


def _dot_tn(a, b):
    return lax.dot_general(a, b, (((0,), (0,)), ((), ())), preferred_element_type=F32)


def _split2(x):
    hi = x.astype(BF16)
    lo = (x - hi.astype(F32)).astype(BF16)
    return hi, lo


def _split3(x):
    hi = x.astype(BF16)
    r = x - hi.astype(F32)
    mid = r.astype(BF16)
    lo = (r - mid.astype(F32)).astype(BF16)
    return hi, mid, lo


def _mm3(a, b):
    ah, al = _split2(a)
    bh, bl = _split2(b)
    return _dot(ah, bh) + (_dot(al, bh) + _dot(ah, bl))


def _ada_kernel(c_ref, w_ref, b_ref, o_ref):
    s = _silu(c_ref[...])
    o_ref[...] = _mm3(s, w_ref[...]) + b_ref[...]


def _ada(cc, w_ada, b_ada):
    tn = D_MODEL
    n = w_ada.shape[1]
    return pl.pallas_call(
        _ada_kernel,
        grid=(n // tn,),
        in_specs=[pl.BlockSpec((8, D_MODEL), lambda j: (0, 0)),
                  pl.BlockSpec((D_MODEL, tn), lambda j: (0, j)),
                  pl.BlockSpec((1, tn), lambda j: (0, j))],
        out_specs=pl.BlockSpec((8, tn), lambda j: (0, j)),
        out_shape=jax.ShapeDtypeStruct((8, n), F32),
        compiler_params=pltpu.CompilerParams(
            dimension_semantics=("arbitrary",), vmem_limit_bytes=VMEM_LIMIT),
        name="ada",
    )(cc, w_ada, b_ada)


LOG2E = math.log2(math.e)
NA_QSCALE = NA_DH ** -0.5 * LOG2E


def _inproj_kernel(x_ref, xc_ref, mod_ref, gpre_ref, win_ref,
                   proj_ref, gates_ref, projc_ref, gatesc_ref, w_ref):
    def project(x, row, proj_out, gates_out):
        xn = x * lax.rsqrt(jnp.mean(x * x, axis=-1, keepdims=True) + EPS)
        shift = mod_ref[row:row + 1, 0:D_MODEL]
        scale = mod_ref[row:row + 1, D_MODEL:2 * D_MODEL]
        h = (xn * gpre_ref[...]) * (1.0 + scale) + shift
        hb = h.astype(BF16)
        nb = 512
        for j in range(PROJ_W // nb):
            y = _dot_nt(hb, w_ref[j * nb:(j + 1) * nb, :])
            if j * nb < NA_W:
                y = y * NA_QSCALE
            proj_out[:, j * nb:(j + 1) * nb] = y.astype(BF16)
        gates_out[...] = _dot_nt(hb, w_ref[PROJ_W:PROJ_W + LANES, :])

    @pl.when(pl.program_id(0) == 0)
    def _():
        nb = 512
        for j in range(PROJ_W // nb):
            w_ref[j * nb:(j + 1) * nb, :] = win_ref[j * nb:(j + 1) * nb, :].astype(BF16)
        w_ref[PROJ_W:PROJ_W + LANES, :] = jnp.zeros((LANES, D_MODEL), BF16)
        w_ref[PROJ_W:PROJ_W + GATE_W, :] = win_ref[PROJ_W:PROJ_W + GATE_W, :].astype(BF16)
        project(xc_ref[...], 1, projc_ref, gatesc_ref)

    project(x_ref[...], 0, proj_ref, gates_ref)


def _inproj(x2, xc2, mod, gpre, w_in, tm):
    n = x2.shape[0]
    nc = xc2.shape[0]
    return pl.pallas_call(
        _inproj_kernel,
        grid=(n // tm,),
        in_specs=[pl.BlockSpec((tm, D_MODEL), lambda i: (i, 0)),
                  pl.BlockSpec((nc, D_MODEL), lambda i: (0, 0)),
                  pl.BlockSpec((8, 3 * D_MODEL), lambda i: (0, 0)),
                  pl.BlockSpec((1, D_MODEL), lambda i: (0, 0)),
                  pl.BlockSpec((None, PROJ_W + GATE_W, D_MODEL), lambda i: (0, 0, 0),
                               pipeline_mode=pl.Buffered(1))],
        out_specs=[pl.BlockSpec((tm, PROJ_W), lambda i: (i, 0)),
                   pl.BlockSpec((tm, LANES), lambda i: (i, 0)),
                   pl.BlockSpec((nc, PROJ_W), lambda i: (0, 0)),
                   pl.BlockSpec((nc, LANES), lambda i: (0, 0))],
        out_shape=[jax.ShapeDtypeStruct((n, PROJ_W), BF16),
                   jax.ShapeDtypeStruct((n, LANES), F32),
                   jax.ShapeDtypeStruct((nc, PROJ_W), BF16),
                   jax.ShapeDtypeStruct((nc, LANES), F32)],
        scratch_shapes=[pltpu.VMEM((PROJ_W + LANES, D_MODEL), BF16)],
        compiler_params=pltpu.CompilerParams(
            dimension_semantics=("arbitrary",), vmem_limit_bytes=VMEM_LIMIT),
        name="inproj",
    )(x2, xc2, mod, gpre, w_in)


NA_RO = 2 * NA_KH - 1
NA_CO = 2 * NA_KW - 1
NA_ROWS_PER_ITER = 32


def _na_build_tables(rpb_ref, tab_ref, hp):
    qc = lax.broadcasted_iota(jnp.int32, (GRID_W, LANES), 0)
    kcol = lax.broadcasted_iota(jnp.int32, (GRID_W, LANES), 1) % GRID_W
    lo = lax.broadcasted_iota(jnp.int32, (1, LANES), 1) < GRID_W
    diff = kcol - qc + (NA_KW - 1)
    col_start = jnp.clip(qc - NA_KW // 2, 0, GRID_W - NA_KW)
    inwin = (kcol >= col_start) & (kcol < col_start + NA_KW)
    for hh in range(2):
        def ro_body(ro, carry):
            base = ((hp * 2 + hh) * NA_RO + ro) * NA_CO
            acc = jnp.zeros((GRID_W, LANES), F32)
            for j in range(NA_CO):
                val = jnp.where(lo, rpb_ref[base + j], rpb_ref[base + NA_CO + j])
                acc = jnp.where(diff == j, val, acc)
            tab_ref[hh, ro] = jnp.where(inwin, acc * LOG2E, NEG)
            return carry
        lax.fori_loop(0, NA_RO - 1, ro_body, 0)


def _na_kernel(rpb_ref, q_ref, k_ref, v_ref, z_ref, kc_ref, vc_ref, o_ref, tab_ref,
               *, rb_rows, rows):
    hp = pl.program_id(0)
    rb = pl.program_id(1)

    @pl.when(rb == 0)
    def _():
        _na_build_tables(rpb_ref, tab_ref, hp)

    lane = lax.broadcasted_iota(jnp.int32, (1, LANES), 1)
    first = lane < NA_DH
    kc = kc_ref[...]
    vc = vc_ref[...]
    win = NA_KH * GRID_W

    def body(it, carry):
        items = []
        for rr in range(NA_ROWS_PER_ITER):
            i = it * NA_ROWS_PER_ITER + rr
            r = rb * rb_rows + i
            rs = jnp.clip(r - NA_KH // 2, 0, rows - NA_KH)
            ro0 = NA_KH - 1 - (r - rs)
            t0 = pl.multiple_of(i * GRID_W, GRID_W)
            q = q_ref[pl.ds(t0, GRID_W), :]
            k0 = pl.multiple_of(rs * GRID_W, GRID_W)
            kw = k_ref[pl.ds(k0, win), :]
            vw = v_ref[pl.ds(k0, win), :]
            zq = jnp.zeros_like(q)
            qs = jnp.concatenate([jnp.where(first, q, zq), jnp.where(first, zq, q)], axis=0)
            items.append((t0, ro0, qs, kw, vw))
        s_all = [_dot_nt(qs, jnp.concatenate([kw, kc], axis=0)) for (_, _, qs, kw, _) in items]
        s_loc = [s[:, 0:win]
                 + jnp.concatenate(
                     [jnp.concatenate([tab_ref[hh, ro0 + 2 * m] for m in range(NA_KH // 2)], axis=1)
                      for hh in range(2)], axis=0)
                 for s, (_, ro0, _, _, _) in zip(s_all, items)]
        s_ctx = [s[:, win:] for s in s_all]
        mx = [jnp.maximum(jnp.max(a, axis=-1, keepdims=True), jnp.max(b, axis=-1, keepdims=True))
              for a, b in zip(s_loc, s_ctx)]
        p_loc = [jnp.exp2(a - m) for a, m in zip(s_loc, mx)]
        p_ctx = [jnp.exp2(b - m) for b, m in zip(s_ctx, mx)]
        inv = [1.0 / (jnp.sum(a, axis=-1, keepdims=True) + jnp.sum(b, axis=-1, keepdims=True))
               for a, b in zip(p_loc, p_ctx)]
        outs = [_dot(jnp.concatenate([a.astype(BF16), b.astype(BF16)], axis=1),
                     jnp.concatenate([it_[4], vc], axis=0)) * il
                for a, b, il, it_ in zip(p_loc, p_ctx, inv, items)]
        for (t0, _, _, _, _), o2 in zip(items, outs):
            o = jnp.where(first, o2[0:GRID_W], o2[GRID_W:2 * GRID_W])
            z = z_ref[pl.ds(t0, GRID_W), :].astype(F32)
            o_ref[pl.ds(t0, GRID_W), :] = (o * _silu(z)).astype(BF16)
        return carry

    lax.fori_loop(0, rb_rows // NA_ROWS_PER_ITER, body, 0)


def _natten(proj, projc, rpb_flat, rb_rows=32):
    n = proj.shape[0]
    nc = projc.shape[0]
    rows = n // GRID_W
    tq = rb_rows * GRID_W
    kcol = NA_W // LANES
    return pl.pallas_call(
        functools.partial(_na_kernel, rb_rows=rb_rows, rows=rows),
        grid=(NA_W // LANES, rows // rb_rows),
        in_specs=[pl.BlockSpec(memory_space=pltpu.SMEM),
                  pl.BlockSpec((tq, LANES), lambda hp, rb: (rb, hp)),
                  pl.BlockSpec((n, LANES), lambda hp, rb: (0, kcol + hp)),
                  pl.BlockSpec((n, LANES), lambda hp, rb: (0, 2 * kcol + hp)),
                  pl.BlockSpec((tq, LANES), lambda hp, rb: (rb, 3 * kcol + hp)),
                  pl.BlockSpec((nc, LANES), lambda hp, rb: (0, kcol + hp)),
                  pl.BlockSpec((nc, LANES), lambda hp, rb: (0, 2 * kcol + hp))],
        out_specs=pl.BlockSpec((tq, LANES), lambda hp, rb: (rb, hp)),
        out_shape=jax.ShapeDtypeStruct((n, NA_W), BF16),
        scratch_shapes=[pltpu.VMEM((2, NA_RO - 1, GRID_W, LANES), F32)],
        compiler_params=pltpu.CompilerParams(
            dimension_semantics=("arbitrary", "arbitrary"), vmem_limit_bytes=VMEM_LIMIT),
        name="natten",
    )(rpb_flat, proj, proj, proj, proj, projc, projc)


GP_TOK = 256
GP_CH = GP_TOK // CHUNK
GP_LOCK = 2
INV_BASE = 8
HALO = 16
CONV_ROWS = 128
XE_ROWS = GP_TOK + CONV_ROWS


def _gdnprep_kernel(main_ref, left_ref, right_ref, gates_ref, cw_ref, alog_ref, dtb_ref,
                    rowtab_ref, coltab_ref,
                    u_ref, w_ref, qd_ref, kd_ref, qk_ref, gl_ref,
                    xe_ref, shift_ref, tri_ref, qn_ref, kn_ref, vv_ref, beta_ref, gc_ref, gct_ref,
                    eg_ref, ek_ref, *, n_tiles):
    i = pl.program_id(0)
    tile = jnp.minimum(i, n_tiles - 1)
    qkv_w = 3 * GDN_W

    @pl.when(i == 0)
    def _():
        m = lax.broadcasted_iota(jnp.int32, (CONV_K * CONV_ROWS, 2 * CONV_ROWS), 0)
        r = lax.broadcasted_iota(jnp.int32, (CONV_K * CONV_ROWS, 2 * CONV_ROWS), 1)
        g, j, r8 = m // (8 * CONV_K), (m // 8) % CONV_K, m % 8
        hit = r == 8 * g + r8 + j + (HALO - CONV_K // 2)
        shift_ref[...] = jnp.where(hit, 1.0, 0.0).astype(BF16)
        ti = lax.broadcasted_iota(jnp.int32, (2 * GP_TOK, GP_TOK), 0) % GP_TOK
        tj = lax.broadcasted_iota(jnp.int32, (2 * GP_TOK, GP_TOK), 1)
        is_lower = lax.broadcasted_iota(jnp.int32, (2 * GP_TOK, GP_TOK), 0) < GP_TOK
        ordered = (is_lower & (ti >= tj)) | (jnp.logical_not(is_lower) & (ti <= tj))
        keep = ((ti // CHUNK) == (tj // CHUNK)) & ordered
        tri_ref[...] = jnp.where(keep, 1.0, 0.0).astype(BF16)
        xe_ref[HALO + GP_TOK + HALO:, :] = jnp.zeros(
            (XE_ROWS - GP_TOK - 2 * HALO, qkv_w), BF16)

    ii = lax.broadcasted_iota(jnp.int32, (CHUNK, LANES), 0)
    jj = lax.broadcasted_iota(jnp.int32, (CHUNK, LANES), 1) % CHUNK
    lo = lax.broadcasted_iota(jnp.int32, (1, LANES), 1) < CHUNK
    lo_wide = lax.broadcasted_iota(jnp.int32, (1, 2 * LANES), 1) < LANES
    eye = jnp.where(ii == jj, 1.0, 0.0).astype(F32)
    bodies = [(d, p) for d in range(2) for p in range(GDN_HEADS // 2)]
    same_block = lambda size: (ii // size) == (jj // size)
    base_blocks = same_block(INV_BASE)
    merge_masks = []
    size = INV_BASE
    while size < CHUNK:
        merge_masks.append(same_block(2 * size) & jnp.logical_not(same_block(size)))
        size *= 2

    def blockdiag(y, first):
        z = jnp.zeros_like(y)
        return jnp.concatenate([jnp.where(first, y, z), jnp.where(first, z, y)], axis=0)

    def prepare(slot):
        lh = left_ref[:, 0:qkv_w]
        rh = right_ref[:, 0:qkv_w]
        xe_ref[0:HALO, :] = jnp.where(tile > 0, lh, jnp.zeros_like(lh))
        xe_ref[HALO:HALO + GP_TOK, :] = main_ref[:, 0:qkv_w]
        xe_ref[HALO + GP_TOK:HALO + GP_TOK + HALO, :] = jnp.where(
            tile < n_tiles - 1, rh, jnp.zeros_like(rh))

        row_lanes = lax.broadcasted_iota(jnp.int32, (1, LANES), 1) < ROPE_AXIS_DIM

        def rope_table(kind):
            return jnp.concatenate(
                [jnp.where(row_lanes, rowtab_ref[kind * GP_CH + c:kind * GP_CH + c + 1, :],
                           coltab_ref[kind]) for c in range(GP_CH)], axis=0)

        cos, sina, sinb = rope_table(0), rope_table(1), rope_table(2)
        pending = {}
        for cb, b in [(cb, b) for cb in range(qkv_w // LANES) for b in range(GP_TOK // CONV_ROWS)]:
            cols = slice(cb * LANES, (cb + 1) * LANES)
            rows = slice(b * CONV_ROWS, (b + 1) * CONV_ROWS)
            if cb % 2 == 0:
                cols2 = slice(cb * LANES, (cb + 2) * LANES)
                taps2 = _dot(shift_ref[...], xe_ref[b * CONV_ROWS:(b + 2) * CONV_ROWS, cols2])
                wts = [jnp.broadcast_to(cw_ref[j:j + 1, cols2], (8, 2 * LANES))
                       for j in range(CONV_K)]
                groups = []
                for g in range(CONV_ROWS // 8):
                    acc2 = None
                    for j in range(CONV_K):
                        r0 = (g * CONV_K + j) * 8
                        term = wts[j] * taps2[r0:r0 + 8]
                        acc2 = term if acc2 is None else acc2 + term
                    groups.append(acc2)
                pending[b] = _silu(jnp.concatenate(groups, axis=0))
            y = pending[b][:, (cb % 2) * LANES:(cb % 2 + 1) * LANES]
            if cb < 2 * GDN_HEADS:
                y = y * lax.rsqrt(jnp.sum(y * y, axis=-1, keepdims=True) + EPS)
                y = (y * cos[rows] + pltpu.roll(y, LANES - ROPE_AXIS_DIM // 2, 1) * sina[rows]
                     + pltpu.roll(y, ROPE_AXIS_DIM // 2, 1) * sinb[rows])
                if cb < GDN_HEADS:
                    qn_ref[slot, rows, cols] = y * (GDN_DK ** -0.5)
                else:
                    kn_ref[slot, rows, (cb - GDN_HEADS) * LANES:(cb - GDN_HEADS + 1) * LANES] = y
            else:
                vv_ref[slot, rows,
                       (cb - 2 * GDN_HEADS) * LANES:(cb - 2 * GDN_HEADS + 1) * LANES] = y
            yield

        gates = gates_ref[...]
        beta_ref[slot] = 0.5 + 0.5 * jnp.tanh(0.5 * gates)
        xa = gates + dtb_ref[...]
        softplus = jnp.maximum(xa, 0.0) + jnp.log1p(jnp.exp(-jnp.abs(xa)))
        g = -jnp.exp(alog_ref[...]) * softplus

        g3 = _split3(g)
        lower = tri_ref[0:GP_TOK, :]
        upper = tri_ref[GP_TOK:2 * GP_TOK, :]
        gc_f = _dot(lower, g3[0]) + (_dot(lower, g3[1]) + _dot(lower, g3[2]))
        gc_b = _dot(upper, g3[0]) + (_dot(upper, g3[1]) + _dot(upper, g3[2]))
        gc_ref[slot, 0] = gc_f
        gc_ref[slot, 1] = gc_b
        yield
        for c in range(GP_CH):
            rows = slice(c * CHUNK, (c + 1) * CHUNK)
            for d, gc in enumerate((gc_f, gc_b)):
                blk = gc[rows]
                gct_ref[slot, c, d] = jnp.concatenate(
                    [blk, pltpu.roll(blk, LANES - 1, 1)], axis=0).T
            ek_ref[slot, 0, rows] = jnp.exp(gc_f[(c + 1) * CHUNK - 1:(c + 1) * CHUNK, :] - gc_f[rows])
            ek_ref[slot, 1, rows] = jnp.exp(gc_b[c * CHUNK:c * CHUNK + 1, :] - gc_b[rows])
            yield
        eg_ref[slot, 0] = jnp.exp(gc_f)
        eg_ref[slot, 1] = jnp.exp(gc_b)

    def solve(slot):
        gl_ref[...] = jnp.zeros_like(gl_ref)
        for c in range(GP_CH):
            gl_ref[c, 0:1, :] = eg_ref[slot, 0, (c + 1) * CHUNK - 1:(c + 1) * CHUNK, :]
            gl_ref[c, 1:2, :] = eg_ref[slot, 1, c * CHUNK:c * CHUNK + 1, :]
        for first_chunk in range(0, GP_CH, GP_LOCK):
            yield from solve_group(slot, range(first_chunk, first_chunk + GP_LOCK))

    def solve_group(slot, chunks):
        a_l, kbeg_l, vb_l, where_l = [], [], [], []
        for c, (d, p) in [(c, b) for c in chunks for b in bodies]:
            tok = slice(c * CHUNK, (c + 1) * CHUNK)
            incl = (ii >= jj) if d == 0 else (ii <= jj)
            strict = (ii > jj) if d == 0 else (ii < jj)
            pl2 = slice(2 * p * LANES, (2 * p + 2) * LANES)
            lb = d * GDN_HEADS + 2 * p
            lg = 2 * GDN_HEADS + lb
            q = qn_ref[slot, tok, pl2]
            k = kn_ref[slot, tok, pl2]
            v = vv_ref[slot, tok, pl2]
            beta = jnp.where(lo_wide, beta_ref[slot, tok, lb:lb + 1],
                             beta_ref[slot, tok, lb + 1:lb + 2])
            gcol = jnp.where(lo, gc_ref[slot, d, tok, lg:lg + 1], gc_ref[slot, d, tok, lg + 1:lg + 2])
            grow = gct_ref[slot, c, d, lg:lg + 1, :]
            eg = jnp.where(lo_wide, eg_ref[slot, d, tok, lg:lg + 1],
                           eg_ref[slot, d, tok, lg + 1:lg + 2])
            ek = jnp.where(lo_wide, ek_ref[slot, d, tok, lg:lg + 1],
                           ek_ref[slot, d, tok, lg + 1:lg + 2])
            dec = jnp.exp(jnp.where(incl, gcol - grow, NEG))
            kb = k * beta
            k_nt = blockdiag(k.astype(BF16), lo_wide)
            kq = _dot_nt(jnp.concatenate([kb.astype(BF16), q.astype(BF16)], axis=0), k_nt)
            a = jnp.where(strict, kq[0:CHUNK] * dec, 0.0)
            qk = kq[CHUNK:2 * CHUNK] * dec
            qk_ref[d, tok, p * LANES:(p + 1) * LANES] = qk.astype(BF16)
            qd_ref[d, tok, pl2] = (q * eg).astype(BF16)
            kd_ref[d, tok, pl2] = (k * ek).astype(BF16)
            a_l.append(a)
            kbeg_l.append((kb * eg).astype(BF16))
            vb_l.append((v * beta).astype(BF16))
            where_l.append((d, tok, pl2))
            if p == GDN_HEADS // 2 - 1:
                yield
        x_l = [jnp.where(base_blocks, -a, 0.0) for a in a_l]
        t_l = [eye + x for x in x_l]
        x_l = [x.astype(BF16) for x in x_l]
        x_l = [_dot(x, blockdiag(x, lo)).astype(BF16) for x in x_l]
        yield
        r_l = [_dot(jnp.concatenate([x, t.astype(BF16)], axis=0), blockdiag(x, lo))
               for x, t in zip(x_l, t_l)]
        x_l = [r[0:CHUNK].astype(BF16) for r in r_l]
        t_l = [t + r[CHUNK:2 * CHUNK] for t, r in zip(t_l, r_l)]
        yield
        t_l = [t + _dot(t.astype(BF16), blockdiag(x, lo)) for t, x in zip(t_l, x_l)]
        yield
        for merged in merge_masks:
            t16_l = [t.astype(BF16) for t in t_l]
            te_l = [_dot(t16, blockdiag(jnp.where(merged, a, 0.0).astype(BF16), lo))
                    for t16, a in zip(t16_l, a_l)]
            yield
            t_l = [t - _dot(te.astype(BF16), blockdiag(t16, lo))
                   for t, te, t16 in zip(t_l, te_l, t16_l)]
            yield
        for n_done, ((d, tok, pl2), t, vb, kbeg) in enumerate(zip(where_l, t_l, vb_l, kbeg_l)):
            t16 = t.astype(BF16)
            uw = _dot(t16, jnp.concatenate([blockdiag(vb, lo_wide), blockdiag(kbeg, lo_wide)],
                                           axis=1))
            u_ref[d, tok, pl2] = uw[:, 0:2 * LANES]
            w_ref[d, tok, pl2] = uw[:, 2 * LANES:4 * LANES].astype(BF16)
            if n_done % len(bodies) == len(bodies) - 1:
                yield

    def interleave(*stages):
        live = list(stages)
        while live:
            for g in list(live):
                if next(g, live) is live:
                    live.remove(g)

    @pl.when(i == 0)
    def _():
        interleave(prepare(0))

    for parity in range(2):
        @pl.when((i % 2 == parity) & (i > 0) & (i < n_tiles))
        def _():
            interleave(prepare(parity), solve(1 - parity))

    @pl.when(i == n_tiles)
    def _():
        interleave(solve((n_tiles - 1) % 2))


def _gdnprep(proj, gates, conv_w, alog, dtb, rowtab, coltab):
    n = proj.shape[0]
    nt = n // GP_TOK
    hb = GP_TOK // HALO
    nhalo = n // HALO
    qkv_w = 3 * GDN_W
    cur = lambda i: jnp.minimum(i, nt - 1)
    done = lambda i: jnp.maximum(i - 1, 0)
    tok_spec = lambda w: pl.BlockSpec((GP_TOK, w), lambda i: (cur(i), 0))
    dir_spec = lambda w: pl.BlockSpec((2, GP_TOK, w), lambda i: (0, done(i), 0))
    stage = lambda *shape: pltpu.VMEM((2,) + shape, F32)
    return pl.pallas_call(
        functools.partial(_gdnprep_kernel, n_tiles=nt),
        grid=(nt + 1,),
        in_specs=[pl.BlockSpec((GP_TOK, PROJ_W // 2), lambda i: (cur(i), 1)),
                  pl.BlockSpec((HALO, PROJ_W // 2),
                               lambda i: (jnp.maximum(cur(i) * hb - 1, 0), 1)),
                  pl.BlockSpec((HALO, PROJ_W // 2),
                               lambda i: (jnp.minimum((cur(i) + 1) * hb, nhalo - 1), 1)),
                  tok_spec(LANES),
                  pl.BlockSpec((8, qkv_w), lambda i: (0, 0)),
                  pl.BlockSpec((1, LANES), lambda i: (0, 0)),
                  pl.BlockSpec((1, LANES), lambda i: (0, 0)),
                  pl.BlockSpec((None, 16, LANES), lambda i: (cur(i), 0, 0)),
                  pl.BlockSpec((3, GRID_W, LANES), lambda i: (0, 0, 0))],
        out_specs=[dir_spec(GDN_W), dir_spec(GDN_W), dir_spec(GDN_W), dir_spec(GDN_W),
                   dir_spec(GDN_HEADS * CHUNK),
                   pl.BlockSpec((GP_CH, 8, LANES), lambda i: (done(i), 0, 0))],
        out_shape=[jax.ShapeDtypeStruct((2, n, GDN_W), F32),
                   jax.ShapeDtypeStruct((2, n, GDN_W), BF16),
                   jax.ShapeDtypeStruct((2, n, GDN_W), BF16),
                   jax.ShapeDtypeStruct((2, n, GDN_W), BF16),
                   jax.ShapeDtypeStruct((2, n, GDN_HEADS * CHUNK), BF16),
                   jax.ShapeDtypeStruct((n // CHUNK, 8, LANES), F32)],
        scratch_shapes=[pltpu.VMEM((XE_ROWS, qkv_w), BF16),
                        pltpu.VMEM((CONV_K * CONV_ROWS, 2 * CONV_ROWS), BF16),
                        pltpu.VMEM((2 * GP_TOK, GP_TOK), BF16),
                        stage(GP_TOK, GDN_W), stage(GP_TOK, GDN_W), stage(GP_TOK, GDN_W),
                        stage(GP_TOK, LANES),
                        stage(2, GP_TOK, LANES),
                        stage(GP_CH, 2, LANES, LANES),
                        stage(2, GP_TOK, LANES),
                        stage(2, GP_TOK, LANES)],
        compiler_params=pltpu.CompilerParams(
            dimension_semantics=("arbitrary",), vmem_limit_bytes=VMEM_LIMIT),
        name="gdnprep",
    )(proj, proj, proj, gates, conv_w, alog, dtb, rowtab, coltab)


SC_CH_MAX = 16


def _scan_kernel(gl_ref, s0_ref, uf_ref, wf_ref, qdf_ref, kdf_ref, qkf_ref,
                 ub_ref, wb_ref, qdb_ref, kdb_ref, qkb_ref,
                 of_ref, ob_ref, sfin_ref, s_ref, *, SC_CH):
    n = pl.program_id(0)
    nsteps = pl.num_programs(0)
    nch = nsteps * SC_CH

    @pl.when(n == 0)
    def _():
        s_ref[...] = s0_ref[...]

    streams = ((uf_ref, wf_ref, qdf_ref, kdf_ref, qkf_ref, of_ref),
               (ub_ref, wb_ref, qdb_ref, kdb_ref, qkb_ref, ob_ref))
    lo_wide = lax.broadcasted_iota(jnp.int32, (1, 2 * LANES), 1) < LANES
    chains = [(d, h) for d in range(2) for h in range(GDN_HEADS)]
    hl = lambda h: slice(h * LANES, (h + 1) * LANES)
    for j in range(SC_CH):
        local = (j, SC_CH - 1 - j)
        chunk = (n * SC_CH + j, (nsteps - 1 - n) * SC_CH + SC_CH - 1 - j)
        tok = [slice(c * CHUNK, (c + 1) * CHUNK) for c in local]
        s32 = [s_ref[d, h] for d, h in chains]
        s16 = [s.astype(BF16) for s in s32]
        wq = [_dot(jnp.concatenate([streams[d][1][tok[d], hl(h)], streams[d][2][tok[d], hl(h)]],
                                   axis=0), s)
              for (d, h), s in zip(chains, s16)]
        v16 = [(streams[d][0][tok[d], hl(h)] - x[0:CHUNK]).astype(BF16)
               for (d, h), x in zip(chains, wq)]
        upd = [_dot_tn(streams[d][3][tok[d], hl(h)], v) for (d, h), v in zip(chains, v16)]
        for (d, h), s, x in zip(chains, s32, upd):
            decay = gl_ref[(d * nch + chunk[d]) * GDN_HEADS + h]
            s_ref[d, h] = s * decay + x
        inter = [x[CHUNK:2 * CHUNK] for x in wq]
        for d in range(2):
            qk_ref, o_ref = streams[d][4], streams[d][5]
            for p in range(GDN_HEADS // 2):
                v_pair = jnp.concatenate(
                    v16[d * GDN_HEADS + 2 * p:d * GDN_HEADS + 2 * p + 2], axis=1)
                zero = jnp.zeros_like(v_pair)
                v_bd = jnp.concatenate([jnp.where(lo_wide, v_pair, zero),
                                        jnp.where(lo_wide, zero, v_pair)], axis=0)
                o_intra = _dot(qk_ref[tok[d], p * LANES:(p + 1) * LANES], v_bd)
                for jh in range(2):
                    h = 2 * p + jh
                    o_ref[tok[d], hl(h)] = (inter[d * GDN_HEADS + h]
                                            + o_intra[:, jh * LANES:(jh + 1) * LANES]
                                            ).astype(o_ref.dtype)

    @pl.when(n == nsteps - 1)
    def _():
        sfin_ref[...] = s_ref[...]


def _scan(gl, s0, u, w, qd, kd, qk):
    n = u.shape[1]
    sc_ch = min(SC_CH_MAX, n // CHUNK)
    tok = sc_ch * CHUNK
    nsteps = n // tok
    fwd = lambda wd: pl.BlockSpec((None, tok, wd), lambda i: (0, i, 0))
    bwd = lambda wd: pl.BlockSpec((None, tok, wd), lambda i: (1, nsteps - 1 - i, 0))
    qkw = GDN_HEADS * CHUNK
    state_spec = pl.BlockSpec((2, GDN_HEADS, GDN_DK, GDN_DV), lambda i: (0, 0, 0, 0))
    return pl.pallas_call(
        functools.partial(_scan_kernel, SC_CH=sc_ch),
        grid=(nsteps,),
        in_specs=[pl.BlockSpec(memory_space=pltpu.SMEM), state_spec,
                  fwd(GDN_W), fwd(GDN_W), fwd(GDN_W), fwd(GDN_W), fwd(qkw),
                  bwd(GDN_W), bwd(GDN_W), bwd(GDN_W), bwd(GDN_W), bwd(qkw)],
        out_specs=[pl.BlockSpec((tok, GDN_W), lambda i: (i, 0)),
                   pl.BlockSpec((tok, GDN_W), lambda i: (nsteps - 1 - i, 0)),
                   state_spec],
        out_shape=[jax.ShapeDtypeStruct((n, GDN_W), BF16),
                   jax.ShapeDtypeStruct((n, GDN_W), BF16),
                   jax.ShapeDtypeStruct((2, GDN_HEADS, GDN_DK, GDN_DV), F32)],
        scratch_shapes=[pltpu.VMEM((2, GDN_HEADS, GDN_DK, GDN_DV), F32)],
        compiler_params=pltpu.CompilerParams(dimension_semantics=("arbitrary",)),
        name="scan",
    )(gl, s0, u, w, qd, kd, qk, u, w, qd, kd, qk)


def _outproj_kernel(x_ref, na_ref, of_ref, ob_ref, gz_ref, gnw_ref, wout_ref, gpost_ref, mod_ref,
                    o_ref):
    og = of_ref[...].astype(F32) + ob_ref[...].astype(F32)
    gz = gz_ref[...].astype(F32)
    gnw = gnw_ref[...]
    parts = []
    for h in range(GDN_HEADS):
        seg = og[:, h * LANES:(h + 1) * LANES]
        seg = seg * lax.rsqrt(jnp.mean(seg * seg, axis=-1, keepdims=True) + EPS)
        parts.append(seg * gnw)
    gd = (jnp.concatenate(parts, axis=-1) * _silu(gz)).astype(BF16)
    y = _dot(jnp.concatenate([na_ref[...], gd], axis=-1), wout_ref[...])
    yn = y * lax.rsqrt(jnp.mean(y * y, axis=-1, keepdims=True) + EPS)
    gate = mod_ref[0:1, 2 * D_MODEL:3 * D_MODEL]
    o_ref[...] = x_ref[...] + gate * (yn * gpost_ref[...])


def _outproj(x2, na, o_f, o_b, proj, gnw, w_out, gpost, mod, tm=1024):
    n = x2.shape[0]
    gz_col = (PROJ_W - GDN_W) // GDN_W
    return pl.pallas_call(
        _outproj_kernel,
        grid=(n // tm,),
        in_specs=[pl.BlockSpec((tm, D_MODEL), lambda i: (i, 0)),
                  pl.BlockSpec((tm, NA_W), lambda i: (i, 0)),
                  pl.BlockSpec((tm, GDN_W), lambda i: (i, 0)),
                  pl.BlockSpec((tm, GDN_W), lambda i: (i, 0)),
                  pl.BlockSpec((tm, GDN_W), lambda i: (i, gz_col)),
                  pl.BlockSpec((1, LANES), lambda i: (0, 0)),
                  pl.BlockSpec((NA_W + GDN_W, D_MODEL), lambda i: (0, 0)),
                  pl.BlockSpec((1, D_MODEL), lambda i: (0, 0)),
                  pl.BlockSpec((8, 3 * D_MODEL), lambda i: (0, 0))],
        out_specs=pl.BlockSpec((tm, D_MODEL), lambda i: (i, 0)),
        out_shape=jax.ShapeDtypeStruct((n, D_MODEL), F32),
        compiler_params=pltpu.CompilerParams(
            dimension_semantics=("arbitrary",), vmem_limit_bytes=VMEM_LIMIT),
        name="outproj",
    )(x2, na, o_f, o_b, proj, gnw, w_out, gpost, mod)


def _rope_tables(rows, identity):
    inv_freq = ROPE_BASE ** (-jnp.arange(0, ROPE_AXIS_DIM, 2, dtype=F32) / ROPE_AXIS_DIM)

    def tables(count):
        ang = jnp.arange(count, dtype=F32)[:, None] * inv_freq[None, :]
        if identity:
            ang = jnp.zeros_like(ang)
        c, s = jnp.cos(ang), jnp.sin(ang)
        z = jnp.zeros_like(s)
        return (jnp.concatenate([c, c], -1), jnp.concatenate([-s, z], -1),
                jnp.concatenate([z, s], -1))

    half = jnp.zeros((rows, LANES // 2), F32)
    rowtab = jnp.stack([jnp.concatenate([t, half], -1) for t in tables(rows)], axis=0)
    rowtab = rowtab.reshape(3, rows // GP_CH, GP_CH, LANES).transpose(1, 0, 2, 3)
    rowtab = rowtab.reshape(rows // GP_CH, 3 * GP_CH, LANES)
    rowtab = jnp.pad(rowtab, ((0, 0), (0, 16 - 3 * GP_CH), (0, 0)))
    halfc = jnp.zeros((GRID_W, LANES // 2), F32)
    coltab = jnp.stack([jnp.concatenate([halfc, t], -1) for t in tables(GRID_W)], axis=0)
    return rowtab, coltab


def _lane_row(vals, offset):
    return jnp.zeros((1, LANES), F32).at[0, offset:offset + vals.shape[0]].set(vals)


def kernel(x, c, ctx, c_ctx, w_ada, b_ada, g_pre, g_post, w_in, conv_w, rpb, A_log, dt_bias,
           gdn_norm_w, w_out):
    n = x.shape[1]
    nc = ctx.shape[1]
    x2 = x[0]
    xc2 = ctx[0]

    cc = jnp.zeros((8, D_MODEL), F32).at[0].set(c[0]).at[1].set(c_ctx)
    mod = _ada(cc, w_ada[0], b_ada[0][None, :])

    gpre = g_pre[0][None, :]
    w_in_t = jnp.swapaxes(w_in, 1, 2)
    proj, gates, projc, gatesc = _inproj(x2, xc2, mod, gpre, w_in_t, tm=512)

    na = _natten(proj, projc, rpb[0].reshape(-1))

    cw = jnp.pad(conv_w[0], ((0, 8 - CONV_K), (0, 0)))
    alog = _lane_row(A_log[0].reshape(-1), 2 * GDN_HEADS)
    dtb = _lane_row(dt_bias[0].reshape(-1), 2 * GDN_HEADS)
    rowtab, coltab = _rope_tables(n // GRID_W, identity=False)
    rowtab_c, coltab_c = _rope_tables(nc // GRID_W, identity=True)

    def gl_rows(gl):
        return jnp.stack([gl[:, 0, 2 * GDN_HEADS:3 * GDN_HEADS],
                          gl[:, 1, 3 * GDN_HEADS:4 * GDN_HEADS]], axis=0).reshape(-1)

    uc, wc, qdc, kdc, qkc, glc = _gdnprep(projc, gatesc, cw, alog, dtb, rowtab_c, coltab_c)
    s0 = jnp.zeros((2, GDN_HEADS, GDN_DK, GDN_DV), F32)
    _, _, s_ctx = _scan(gl_rows(glc), s0, uc, wc, qdc, kdc, qkc)

    u, w, qd, kd, qk, gl = _gdnprep(proj, gates, cw, alog, dtb, rowtab, coltab)
    o_f, o_b, _ = _scan(gl_rows(gl), s_ctx, u, w, qd, kd, qk)

    gnw = gdn_norm_w[0][None, :]
    out = _outproj(x2, na, o_f, o_b, proj, gnw, w_out[0].astype(BF16), g_post[0][None, :], mod)
    return out[None]
```
